```python
import jax, jax.numpy as jnp
from jax import lax
import numpy as np

D_MODEL = 2048
BATCH = 2
SEQ = 4096
DEPTH = 1

GRID_W = 64
CTX_LEN = 256
D_MIX = D_MODEL
N_HEADS = 8
N_KV_HEADS = 2
HEAD_DIM = 128
GQA_GROUP = N_HEADS // N_KV_HEADS
ATTN_W = N_HEADS * HEAD_DIM
KV_W = N_KV_HEADS * HEAD_DIM
ATTN_SCALE = HEAD_DIM ** -0.5
WINDOW = 128
BLOCK = 128
ROPE_THETA = 10000.0
ROPE_AXIS_DIM = HEAD_DIM // 2
LRU_W = D_MIX - ATTN_W
LRU_BLOCKS = 8
LRU_BLOCK_W = LRU_W // LRU_BLOCKS
LRU_C = 8.0
CONV_W = 4
CONV_LEFT = 2
D_IN = ATTN_W + 2 * KV_W + 2 * LRU_W
N_EXPERTS = 64
TOP_K = 8
N_GROUPS = 8
TOPK_GROUPS = 4
EXPERT_FF = 512
SHARED_FF = 512
ROUTED_SCALE = 2.5
EXPERT_CHUNK = 8
LN_EPS = 1e-6
DEEPNORM_ALPHA = (2 * DEPTH) ** 0.25
DEEPNORM_BETA = (8 * DEPTH) ** -0.25

kernel_name = "hybrid_swa_rglru_moe_flow_block"


def layer_norm(x):
    xf = x.astype(jnp.float32)
    mu = jnp.mean(xf, -1, keepdims=True)
    var = jnp.mean(jnp.square(xf - mu), -1, keepdims=True)
    return ((xf - mu) * lax.rsqrt(var + LN_EPS)).astype(x.dtype)


def layer_norm_affine(x, g, b):
    xf = x.astype(jnp.float32)
    mu = jnp.mean(xf, -1, keepdims=True)
    var = jnp.mean(jnp.square(xf - mu), -1, keepdims=True)
    return ((xf - mu) * lax.rsqrt(var + LN_EPS) * g + b).astype(x.dtype)


def rms_norm(x, g):
    xf = x.astype(jnp.float32)
    return (xf * lax.rsqrt(jnp.mean(xf * xf, -1, keepdims=True) + LN_EPS) * g).astype(x.dtype)


def modulate(u, shift, scale):
    return u * (1 + scale) + shift


def axial_rope_tables(n_tokens, dtype):
    rows = n_tokens // GRID_W
    row = jnp.repeat(jnp.arange(rows, dtype=jnp.float32), GRID_W)
    col = jnp.tile(jnp.arange(GRID_W, dtype=jnp.float32), rows)
    inv_freq = jnp.exp(-jnp.log(ROPE_THETA) * jnp.arange(0, ROPE_AXIS_DIM, 2, dtype=jnp.float32) / ROPE_AXIS_DIM)
    ang_r = row[:, None] * inv_freq
    ang_c = col[:, None] * inv_freq
    ex = lambda t: t[None, :, None, :].astype(dtype)
    return (ex(jnp.cos(ang_r)), ex(jnp.sin(ang_r)), ex(jnp.cos(ang_c)), ex(jnp.sin(ang_c)))


def _rotate_half(xh, cos, sin):
    x1, x2 = jnp.split(xh, 2, axis=-1)
    return jnp.concatenate([x1 * cos - x2 * sin, x2 * cos + x1 * sin], axis=-1)


def apply_axial_rope(x, rope):
    cos_r, sin_r, cos_c, sin_c = rope
    xr, xc = jnp.split(x, 2, axis=-1)
    return jnp.concatenate([_rotate_half(xr, cos_r, sin_r), _rotate_half(xc, cos_c, sin_c)], axis=-1)


def window_context_attention(q, k, v, kc, vc, sink):
    B, L = q.shape[:2]
    nb = L // BLOCK
    qb = q.reshape(B, nb, BLOCK, N_KV_HEADS, GQA_GROUP, HEAD_DIM)
    pad = ((0, 0), (BLOCK, BLOCK), (0, 0), (0, 0))
    kp = jnp.pad(k, pad).reshape(B, nb + 2, BLOCK, N_KV_HEADS, HEAD_DIM)
    vp = jnp.pad(v, pad).reshape(B, nb + 2, BLOCK, N_KV_HEADS, HEAD_DIM)
    kb = jnp.concatenate([kp[:, :-2], kp[:, 1:-1], kp[:, 2:]], axis=2)
    vb = jnp.concatenate([vp[:, :-2], vp[:, 1:-1], vp[:, 2:]], axis=2)
    s_win = jnp.einsum('bnqhgd,bnkhd->bhgnqk', qb, kb).astype(jnp.float32) * ATTN_SCALE
    s_ctx = jnp.einsum('bnqhgd,bchd->bhgnqc', qb, kc).astype(jnp.float32) * ATTN_SCALE
    qi = jnp.arange(BLOCK)[None, :, None]
    kj = jnp.arange(3 * BLOCK)[None, None, :]
    kpos = jnp.arange(nb)[:, None, None] * BLOCK + kj - BLOCK
    valid = (jnp.abs(kj - BLOCK - qi) <= WINDOW) & (kpos >= 0) & (kpos < L)
    s_win = jnp.where(valid, s_win, -jnp.inf)
    sink_col = jnp.broadcast_to(sink.astype(jnp.float32).reshape(1, N_KV_HEADS, GQA_GROUP, 1, 1, 1), s_ctx.shape[:-1] + (1,))
    p = jax.nn.softmax(jnp.concatenate([s_win, s_ctx, sink_col], axis=-1), axis=-1)
    n_ctx = kc.shape[1]
    p_win = p[..., :3 * BLOCK].astype(v.dtype)
    p_ctx = p[..., 3 * BLOCK:3 * BLOCK + n_ctx].astype(v.dtype)
    out = jnp.einsum('bhgnqk,bnkhd->bnqhgd', p_win, vb) + jnp.einsum('bhgnqc,bchd->bnqhgd', p_ctx, vc)
    return out.reshape(B, L, ATTN_W)


def context_attention(qc, kc, vc, sink):
    B, C = qc.shape[:2]
    qg = qc.reshape(B, C, N_KV_HEADS, GQA_GROUP, HEAD_DIM)
    s = jnp.einsum('bqhgd,bkhd->bhgqk', qg, kc).astype(jnp.float32) * ATTN_SCALE
    sink_col = jnp.broadcast_to(sink.astype(jnp.float32).reshape(1, N_KV_HEADS, GQA_GROUP, 1, 1), s.shape[:-1] + (1,))
    p = jax.nn.softmax(jnp.concatenate([s, sink_col], axis=-1), axis=-1)[..., :C].astype(vc.dtype)
    return jnp.einsum('bhgqk,bkhd->bqhgd', p, vc).reshape(B, C, ATTN_W)


def dwconv_centered(x, w, b):
    y = lax.conv_general_dilated(x, w[:, None, :], window_strides=(1,),
                                 padding=[(CONV_LEFT, CONV_W - 1 - CONV_LEFT)],
                                 dimension_numbers=('NWC', 'WIO', 'NWC'),
                                 feature_group_count=x.shape[-1])
    return y + b


def block_diag_linear(x, w, b):
    xb = x.reshape(x.shape[:-1] + (LRU_BLOCKS, LRU_BLOCK_W))
    return jnp.einsum('blnc,ncd->blnd', xb, w).reshape(x.shape) + b


def rglru_coeffs(xc, wa, ba, wx, bx, lam):
    r = jax.nn.sigmoid(block_diag_linear(xc, wa, ba).astype(jnp.float32))
    i = jax.nn.sigmoid(block_diag_linear(xc, wx, bx).astype(jnp.float32))
    log_a = -LRU_C * r * jax.nn.softplus(-lam.astype(jnp.float32))
    a = jnp.exp(log_a)
    b = jnp.sqrt(-jnp.expm1(2 * log_a)) * (i * xc.astype(jnp.float32))
    return a, b


def linear_scan(a, b, h0):
    def combine(left, right):
        a_l, b_l = left
        a_r, b_r = right
        return a_l * a_r, a_r * b_l + b_r
    a_cum, h = lax.associative_scan(combine, (a, b), axis=1)
    return h + a_cum * h0[:, None, :]


def rglru_mixer(xr, xrc, conv_w, conv_b, wa, ba, wx, bx, lam):
    xr = dwconv_centered(xr, conv_w, conv_b)
    xrc = dwconv_centered(xrc, conv_w, conv_b)
    B = xr.shape[0]
    h_lat = jnp.zeros(xr.shape, jnp.float32)
    h_ctx = jnp.zeros(xrc.shape, jnp.float32)
    for d in range(2):
        rev = (lambda t: jnp.flip(t, axis=1)) if d == 1 else (lambda t: t)
        a_c, b_c = rglru_coeffs(rev(xrc), wa[d], ba[d], wx[d], bx[d], lam[d])
        hc = linear_scan(a_c, b_c, jnp.zeros((B, LRU_W), jnp.float32))
        a_l, b_l = rglru_coeffs(rev(xr), wa[d], ba[d], wx[d], bx[d], lam[d])
        hl = linear_scan(a_l, b_l, hc[:, -1])
        h_lat = h_lat + rev(hl)
        h_ctx = h_ctx + rev(hc)
    return h_lat.astype(xr.dtype), h_ctx.astype(xrc.dtype)


def head_group_merge(attn, lru_h, gate_branch, g_attn, g_lru, w_out):
    lru_y = lru_h * jax.nn.gelu(gate_branch)
    return jnp.concatenate([rms_norm(attn, g_attn), rms_norm(lru_y, g_lru)], axis=-1) @ w_out


def moe_ffn(u, router_w, router_bias, w_gate, w_up, w_down, ws_gate, ws_up, ws_down):
    shp = u.shape
    t = u.reshape(-1, shp[-1])
    T = t.shape[0]
    scores = jax.nn.sigmoid((t @ router_w).astype(jnp.float32))
    sel = scores + router_bias.astype(jnp.float32)
    grp = sel.reshape(T, N_GROUPS, N_EXPERTS // N_GROUPS)
    grp_score = jnp.sum(lax.top_k(grp, 2)[0], axis=-1)
    _, grp_idx = lax.top_k(grp_score, TOPK_GROUPS)
    grp_mask = jnp.sum(jax.nn.one_hot(grp_idx, N_GROUPS, dtype=jnp.float32), axis=1)
    exp_mask = jnp.repeat(grp_mask, N_EXPERTS // N_GROUPS, axis=1) > 0
    _, eidx = lax.top_k(jnp.where(exp_mask, sel, -jnp.inf), TOP_K)
    w = jnp.take_along_axis(scores, eidx, axis=1)
    w = w / jnp.sum(w, axis=-1, keepdims=True) * ROUTED_SCALE
    gates = jnp.einsum('tk,tke->te', w, jax.nn.one_hot(eidx, N_EXPERTS, dtype=jnp.float32)).astype(u.dtype)
    out = (jax.nn.silu(t @ ws_gate) * (t @ ws_up)) @ ws_down
    for e0 in range(0, N_EXPERTS, EXPERT_CHUNK):
        sl = slice(e0, e0 + EXPERT_CHUNK)
        h = jax.nn.silu(jnp.einsum('td,edf->tef', t, w_gate[sl])) * jnp.einsum('td,edf->tef', t, w_up[sl])
        out = out + jnp.einsum('tef,efd->td', h * gates[:, sl, None], w_down[sl])
    return out.reshape(shp)


def setup_inputs(seed: int = 0) -> dict:
    key = jax.random.key(seed)
    ks = jax.random.split(key, 32)
    f32 = jnp.float32
    nrm = lambda i, shape, s: jax.random.normal(ks[i], shape, f32) * s
    a_pow = jax.random.uniform(ks[14], (DEPTH, 2, LRU_W), f32, 0.9, 0.999)
    a_base = a_pow ** (1.0 / LRU_C)
    lam = jnp.log(a_base) - jnp.log1p(-a_base)
    return {
        'x': nrm(0, (BATCH, SEQ, D_MODEL), 1.0),
        'c': nrm(1, (BATCH, D_MODEL), 1.0),
        'ctx': nrm(2, (BATCH, CTX_LEN, D_MODEL), 1.0),
        'c_ctx': nrm(3, (D_MODEL,), 1.0),
        'w_mod': nrm(4, (DEPTH, D_MODEL, 6 * D_MODEL), 0.5 * D_MODEL ** -0.5),
        'b_mod': nrm(5, (DEPTH, 6 * D_MODEL), 0.02),
        'w_in': nrm(6, (DEPTH, D_MODEL, D_IN), D_MODEL ** -0.5),
        'attn_sink': nrm(7, (DEPTH, N_HEADS), 0.5),
        'conv_w': nrm(8, (DEPTH, CONV_W, LRU_W), CONV_W ** -0.5),
        'conv_b': nrm(9, (DEPTH, LRU_W), 0.02),
        'lru_wa': nrm(10, (DEPTH, 2, LRU_BLOCKS, LRU_BLOCK_W, LRU_BLOCK_W), LRU_BLOCK_W ** -0.5),
        'lru_ba': nrm(11, (DEPTH, 2, LRU_W), 0.02),
        'lru_wx': nrm(12, (DEPTH, 2, LRU_BLOCKS, LRU_BLOCK_W, LRU_BLOCK_W), LRU_BLOCK_W ** -0.5),
        'lru_bx': nrm(13, (DEPTH, 2, LRU_W), 0.02),
        'lru_lam': lam,
        'norm_attn_g': 1.0 + nrm(15, (DEPTH, ATTN_W), 0.02),
        'norm_lru_g': 1.0 + nrm(16, (DEPTH, LRU_W), 0.02),
        'w_out': nrm(17, (DEPTH, D_MIX, D_MODEL), DEEPNORM_BETA * D_MIX ** -0.5),
        'ln1_g': 1.0 + nrm(18, (DEPTH, D_MODEL), 0.02),
        'ln1_b': nrm(19, (DEPTH, D_MODEL), 0.02),
        'router_w': nrm(20, (DEPTH, D_MODEL, N_EXPERTS), D_MODEL ** -0.5),
        'router_bias': nrm(21, (DEPTH, N_EXPERTS), 0.01),
        'exp_w_gate': nrm(22, (DEPTH, N_EXPERTS, D_MODEL, EXPERT_FF), D_MODEL ** -0.5),
        'exp_w_up': nrm(23, (DEPTH, N_EXPERTS, D_MODEL, EXPERT_FF), D_MODEL ** -0.5),
        'exp_w_down': nrm(24, (DEPTH, N_EXPERTS, EXPERT_FF, D_MODEL), DEEPNORM_BETA * EXPERT_FF ** -0.5),
        'sh_w_gate': nrm(25, (DEPTH, D_MODEL, SHARED_FF), D_MODEL ** -0.5),
        'sh_w_up': nrm(26, (DEPTH, D_MODEL, SHARED_FF), D_MODEL ** -0.5),
        'sh_w_down': nrm(27, (DEPTH, SHARED_FF, D_MODEL), DEEPNORM_BETA * SHARED_FF ** -0.5),
        'ln2_g': 1.0 + nrm(28, (DEPTH, D_MODEL), 0.02),
        'ln2_b': nrm(29, (DEPTH, D_MODEL), 0.02),
    }


def reference(x, c, ctx, c_ctx, w_mod, b_mod, w_in, attn_sink, conv_w, conv_b, lru_wa, lru_ba,
              lru_wx, lru_bx, lru_lam, norm_attn_g, norm_lru_g, w_out, ln1_g, ln1_b, router_w,
              router_bias, exp_w_gate, exp_w_up, exp_w_down, sh_w_gate, sh_w_up, sh_w_down,
              ln2_g, ln2_b):
    B, L, _ = x.shape
    rope = axial_rope_tables(L, x.dtype)
    splits = [ATTN_W, ATTN_W + KV_W, ATTN_W + 2 * KV_W, ATTN_W + 2 * KV_W + LRU_W]
    for layer in range(DEPTH):
        last = layer == DEPTH - 1
        mod = jax.nn.silu(c) @ w_mod[layer] + b_mod[layer]
        mod_c = jax.nn.silu(c_ctx) @ w_mod[layer] + b_mod[layer]
        sh1, sc1, g1, sh2, sc2, g2 = jnp.split(mod[:, None, :], 6, axis=-1)
        sh1c, sc1c, g1c, sh2c, sc2c, g2c = jnp.split(mod_c, 6, axis=-1)

        u = modulate(layer_norm(x), sh1, sc1)
        uc = modulate(layer_norm(ctx), sh1c, sc1c)
        q, k, v, xr, yg = jnp.split(u @ w_in[layer], splits, axis=-1)
        qc, kc, vc, xrc, ygc = jnp.split(uc @ w_in[layer], splits, axis=-1)
        C = ctx.shape[1]
        q = apply_axial_rope(q.reshape(B, L, N_HEADS, HEAD_DIM), rope)
        k = apply_axial_rope(k.reshape(B, L, N_KV_HEADS, HEAD_DIM), rope)
        v = v.reshape(B, L, N_KV_HEADS, HEAD_DIM)
        kc = kc.reshape(B, C, N_KV_HEADS, HEAD_DIM)
        vc = vc.reshape(B, C, N_KV_HEADS, HEAD_DIM)
        attn = window_context_attention(q, k, v, kc, vc, attn_sink[layer])
        h_lru, h_lru_c = rglru_mixer(xr, xrc, conv_w[layer], conv_b[layer], lru_wa[layer], lru_ba[layer],
                                     lru_wx[layer], lru_bx[layer], lru_lam[layer])
        y1 = head_group_merge(attn, h_lru, yg, norm_attn_g[layer], norm_lru_g[layer], w_out[layer])
        x_new = layer_norm_affine(DEEPNORM_ALPHA * x + g1 * y1, ln1_g[layer], ln1_b[layer])

        u2 = modulate(layer_norm(x_new), sh2, sc2)
        y2 = moe_ffn(u2, router_w[layer], router_bias[layer], exp_w_gate[layer], exp_w_up[layer],
                     exp_w_down[layer], sh_w_gate[layer], sh_w_up[layer], sh_w_down[layer])
        x_new = layer_norm_affine(DEEPNORM_ALPHA * x_new + g2 * y2, ln2_g[layer], ln2_b[layer])

        if not last:
            qc = qc.reshape(B, C, N_HEADS, HEAD_DIM)
            attn_c = context_attention(qc, kc, vc, attn_sink[layer])
            y1c = head_group_merge(attn_c, h_lru_c, ygc, norm_attn_g[layer], norm_lru_g[layer], w_out[layer])
            ctx_new = layer_norm_affine(DEEPNORM_ALPHA * ctx + g1c * y1c, ln1_g[layer], ln1_b[layer])
            u2c = modulate(layer_norm(ctx_new), sh2c, sc2c)
            y2c = moe_ffn(u2c, router_w[layer], router_bias[layer], exp_w_gate[layer], exp_w_up[layer],
                          exp_w_down[layer], sh_w_gate[layer], sh_w_up[layer], sh_w_down[layer])
            ctx = layer_norm_affine(DEEPNORM_ALPHA * ctx_new + g2c * y2c, ln2_g[layer], ln2_b[layer])
        x = x_new
    return x
```

```python
import functools

import jax
import jax.numpy as jnp
from jax import lax
from jax.experimental import pallas as pl
from jax.experimental.pallas import tpu as pltpu

F32 = jnp.float32
BF16 = jnp.bfloat16

D_MODEL = 2048
GRID_W = 64
N_HEADS = 8
N_KV_HEADS = 2
HEAD_DIM = 128
GQA_GROUP = N_HEADS // N_KV_HEADS
ATTN_W = N_HEADS * HEAD_DIM
KV_W = N_KV_HEADS * HEAD_DIM
ATTN_SCALE = HEAD_DIM ** -0.5
BLOCK = 128
ROPE_THETA = 10000.0
ROPE_AXIS_DIM = HEAD_DIM // 2
LRU_W = D_MODEL - ATTN_W
LRU_BLOCKS = 8
LRU_BLOCK_W = LRU_W // LRU_BLOCKS
LRU_C = 8.0
N_EXPERTS = 64
TOP_K = 8
N_GROUPS = 8
GROUP_SIZE = N_EXPERTS // N_GROUPS
TOPK_GROUPS = 4
EXPERT_FF = 512
ROUTED_SCALE = 2.5
LN_EPS = 1e-6
DEPTH = 1
DEEPNORM_ALPHA = (2 * DEPTH) ** 0.25

SUBLANES = 8
LANES = 128
VMEM_LIMIT = 56 * 1024 * 1024

ROW_TILE = 256
MOD_TN = 1024
MOE_TM = 512


def _params(sem, vmem=VMEM_LIMIT):
    return pltpu.CompilerParams(dimension_semantics=sem, vmem_limit_bytes=vmem)


def _layer_norm(x):
    mu = jnp.mean(x, axis=-1, keepdims=True)
    xc = x - mu
    var = jnp.mean(xc * xc, axis=-1, keepdims=True)
    return xc * lax.rsqrt(var + LN_EPS)


def _rms_norm(x, g):
    return x * lax.rsqrt(jnp.mean(x * x, axis=-1, keepdims=True) + LN_EPS) * g


def _split_bf16(x):
    hi = x.astype(BF16)
    lo = (x - hi.astype(F32)).astype(BF16)
    return hi, lo


def _mod_kernel(c_ref, w_ref, b_ref, o_ref):
    cv = c_ref[...]
    s = cv * jax.nn.sigmoid(cv)
    hi, lo = _split_bf16(s)
    lhs = jnp.concatenate([hi, lo], axis=0)
    r = jnp.dot(lhs, w_ref[...].astype(BF16), preferred_element_type=F32)
    o_ref[...] = r[:SUBLANES] + r[SUBLANES:] + b_ref[...]


def _mod(cvec, w_mod, b_mod):
    n = w_mod.shape[1]
    return pl.pallas_call(
        _mod_kernel,
        grid=(n // MOD_TN,),
        in_specs=[
            pl.BlockSpec((SUBLANES, D_MODEL), lambda j: (0, 0)),
            pl.BlockSpec((D_MODEL, MOD_TN), lambda j: (0, j)),
            pl.BlockSpec((1, MOD_TN), lambda j: (0, j)),
        ],
        out_specs=pl.BlockSpec((SUBLANES, MOD_TN), lambda j: (0, j)),
        out_shape=jax.ShapeDtypeStruct((SUBLANES, n), F32),
        compiler_params=_params(("arbitrary",)),
        name="mod",
    )(cvec, w_mod, b_mod)


def _rope(xh, cos, sin, even_block):
    partner = jnp.where(even_block, pltpu.roll(xh, 96, 1), pltpu.roll(xh, 32, 1))
    return xh * cos + partner * sin


def _inproj_kernel(x_ref, ctx_ref, mod_ref, w_ref, cos_ref, sin_ref,
                   q_ref, k_ref, v_ref, xr_ref, yg_ref):
    b = pl.program_id(0)
    t = pl.program_id(1)
    is_ctx = t == 0
    xin = jnp.where(is_ctx, ctx_ref[0], x_ref[0])
    r = jnp.where(is_ctx, 2, b)
    shift = mod_ref[pl.ds(r, 1), 0:D_MODEL]
    scale = mod_ref[pl.ds(r, 1), D_MODEL:2 * D_MODEL]
    u = (_layer_norm(xin) * (1.0 + scale) + shift).astype(BF16)

    cos = cos_ref[...]
    sin = sin_ref[...]
    lane = lax.broadcasted_iota(jnp.int32, (ROW_TILE, HEAD_DIM), 1)
    even_block = (lane % 64) < 32

    def proj(c0, c1):
        return jnp.dot(u, w_ref[:, c0:c1], preferred_element_type=F32)

    for h in range(N_HEADS):
        qh = proj(h * HEAD_DIM, (h + 1) * HEAD_DIM)
        q_ref[0, :, h * HEAD_DIM:(h + 1) * HEAD_DIM] = (
            _rope(qh, cos, sin, even_block) * ATTN_SCALE).astype(BF16)
    for h in range(N_KV_HEADS):
        c0 = ATTN_W + h * HEAD_DIM
        kh = proj(c0, c0 + HEAD_DIM)
        k_ref[0, :, h * HEAD_DIM:(h + 1) * HEAD_DIM] = _rope(kh, cos, sin, even_block).astype(BF16)
    v_ref[0] = proj(ATTN_W + KV_W, ATTN_W + 2 * KV_W).astype(BF16)
    c0 = ATTN_W + 2 * KV_W
    xr_ref[0] = proj(c0, c0 + LRU_W)
    yg_ref[0] = proj(c0 + LRU_W, c0 + 2 * LRU_W)


def _inproj(x, ctx, mod, w_in_bf, cos_t, sin_t):
    B, L, _ = x.shape
    C = ctx.shape[1]
    assert C == ROW_TILE and L % ROW_TILE == 0
    nt = L // ROW_TILE + 1
    rows = L + C
    d_in = w_in_bf.shape[1]
    out = lambda w, dt: jax.ShapeDtypeStruct((B, rows, w), dt)
    ospec = lambda w: pl.BlockSpec((1, ROW_TILE, w), lambda b, t: (b, t, 0))
    return pl.pallas_call(
        _inproj_kernel,
        grid=(B, nt),
        in_specs=[
            pl.BlockSpec((1, ROW_TILE, D_MODEL), lambda b, t: (b, jnp.maximum(t - 1, 0), 0)),
            pl.BlockSpec((1, ROW_TILE, D_MODEL), lambda b, t: (b, 0, 0)),
            pl.BlockSpec((SUBLANES, 2 * D_MODEL), lambda b, t: (0, 0)),
            pl.BlockSpec((D_MODEL, d_in), lambda b, t: (0, 0)),
            pl.BlockSpec((ROW_TILE, HEAD_DIM), lambda b, t: (t, 0)),
            pl.BlockSpec((ROW_TILE, HEAD_DIM), lambda b, t: (t, 0)),
        ],
        out_specs=[ospec(ATTN_W), ospec(KV_W), ospec(KV_W), ospec(LRU_W), ospec(LRU_W)],
        out_shape=[out(ATTN_W, BF16), out(KV_W, BF16), out(KV_W, BF16), out(LRU_W, F32), out(LRU_W, F32)],
        compiler_params=_params(("arbitrary", "arbitrary")),
        name="inproj",
    )(x, ctx, mod, w_in_bf, cos_t, sin_t)


def _attn_kernel(sink_ref, q_ref, kp_ref, kc_ref, kn_ref, vp_ref, vc_ref, vn_ref, kx_ref, vx_ref, o_ref):
    n = pl.program_id(1)
    nb = pl.num_programs(1)
    rows = GQA_GROUP * BLOCK
    n_ctx = kx_ref.shape[1]
    qi = lax.broadcasted_iota(jnp.int32, (rows, BLOCK), 0) % BLOCK
    kj = lax.broadcasted_iota(jnp.int32, (rows, BLOCK), 1)
    prev_ok = (kj >= qi) & (n > 0)
    next_ok = (kj <= qi) & (n < nb - 1)
    grp = lax.broadcasted_iota(jnp.int32, (rows, 1), 0) // BLOCK
    neg = -jnp.inf
    for h in range(N_KV_HEADS):
        hs = slice(h * HEAD_DIM, (h + 1) * HEAD_DIM)
        q4 = jnp.concatenate(
            [q_ref[0, :, (h * GQA_GROUP + g) * HEAD_DIM:(h * GQA_GROUP + g + 1) * HEAD_DIM]
             for g in range(GQA_GROUP)], axis=0)
        nt = (((1,), (1,)), ((), ()))
        sp = jnp.where(prev_ok, lax.dot_general(q4, kp_ref[0, :, hs], nt, preferred_element_type=F32), neg)
        sc = lax.dot_general(q4, kc_ref[0, :, hs], nt, preferred_element_type=F32)
        sn = jnp.where(next_ok, lax.dot_general(q4, kn_ref[0, :, hs], nt, preferred_element_type=F32), neg)
        sx = lax.dot_general(q4, kx_ref[0, :, hs], nt, preferred_element_type=F32)
        sink = jnp.zeros((rows, 1), F32)
        for g in range(GQA_GROUP):
            sink = jnp.where(grp == g, sink_ref[h * GQA_GROUP + g], sink)
        m = jnp.maximum(jnp.maximum(jnp.max(sp, -1, keepdims=True), jnp.max(sc, -1, keepdims=True)),
                        jnp.maximum(jnp.max(sn, -1, keepdims=True), jnp.max(sx, -1, keepdims=True)))
        m = jnp.maximum(m, sink)
        pp = jnp.exp(sp - m)
        pc = jnp.exp(sc - m)
        pn = jnp.exp(sn - m)
        px = jnp.exp(sx - m)
        denom = (jnp.sum(pp, -1, keepdims=True) + jnp.sum(pc, -1, keepdims=True)
                 + jnp.sum(pn, -1, keepdims=True) + jnp.sum(px, -1, keepdims=True) + jnp.exp(sink - m))
        acc = jnp.dot(pp.astype(BF16), vp_ref[0, :, hs], preferred_element_type=F32)
        acc += jnp.dot(pc.astype(BF16), vc_ref[0, :, hs], preferred_element_type=F32)
        acc += jnp.dot(pn.astype(BF16), vn_ref[0, :, hs], preferred_element_type=F32)
        acc += jnp.dot(px.astype(BF16), vx_ref[0, :, hs], preferred_element_type=F32)
        o = acc / denom
        for g in range(GQA_GROUP):
            c0 = (h * GQA_GROUP + g) * HEAD_DIM
            o_ref[0, :, c0:c0 + HEAD_DIM] = o[g * BLOCK:(g + 1) * BLOCK]


def _attention(sink, q, k, v, L, C):
    B = q.shape[0]
    nb = L // BLOCK
    off = C // BLOCK
    cur = lambda b, n: (b, n + off, 0)
    prv = lambda b, n: (b, jnp.maximum(n - 1, 0) + off, 0)
    nxt = lambda b, n: (b, jnp.minimum(n + 1, nb - 1) + off, 0)
    kv = lambda im: pl.BlockSpec((1, BLOCK, KV_W), im)
    cx = pl.BlockSpec((1, C, KV_W), lambda b, n: (b, 0, 0))
    return pl.pallas_call(
        _attn_kernel,
        grid=(B, nb),
        in_specs=[
            pl.BlockSpec(memory_space=pltpu.SMEM),
            pl.BlockSpec((1, BLOCK, ATTN_W), cur),
            kv(prv), kv(cur), kv(nxt), kv(prv), kv(cur), kv(nxt), cx, cx,
        ],
        out_specs=pl.BlockSpec((1, BLOCK, ATTN_W), lambda b, n: (b, n, 0)),
        out_shape=jax.ShapeDtypeStruct((B, L, ATTN_W), F32),
        compiler_params=_params(("arbitrary", "arbitrary")),
        name="attn",
    )(sink, q, k, k, k, v, v, v, k, v)


def _lru_coeffs(x_ref, p_ref, n_ref, tile, d, cw_ref, cb_ref, w_ref, bias_ref, lam_ref,
                ext_scr, a_scr, b_scr, n_lat_tiles):
    tm = ROW_TILE
    prev_ok = tile >= 2
    next_ok = (tile >= 1) & (tile < n_lat_tiles)
    ext_scr[0:SUBLANES] = jnp.where(prev_ok, p_ref[0], 0.0)
    ext_scr[SUBLANES:SUBLANES + tm] = x_ref[0]
    ext_scr[SUBLANES + tm:2 * SUBLANES + tm] = jnp.where(next_ok, n_ref[0], 0.0)
    xc = cb_ref[...] + cw_ref[2:3] * x_ref[0]
    xc += cw_ref[0:1] * ext_scr[SUBLANES - 2:SUBLANES - 2 + tm]
    xc += cw_ref[1:2] * ext_scr[SUBLANES - 1:SUBLANES - 1 + tm]
    xc += cw_ref[3:4] * ext_scr[SUBLANES + 1:SUBLANES + 1 + tm]
    xcb = xc.astype(BF16)
    lam = lam_ref[d:d + 1]
    sp = jnp.maximum(-lam, 0.0) + jnp.log1p(jnp.exp(-jnp.abs(lam)))
    for n in range(LRU_BLOCKS):
        cs = slice(n * LRU_BLOCK_W, (n + 1) * LRU_BLOCK_W)
        z = jnp.dot(xcb[:, cs], w_ref[d, n], preferred_element_type=F32)
        r = jax.nn.sigmoid(z[:, :LRU_BLOCK_W] + bias_ref[d, 0:1, cs])
        i = jax.nn.sigmoid(z[:, LRU_BLOCK_W:] + bias_ref[d, 1:2, cs])
        log_a = -LRU_C * r * sp[:, cs]
        a = jnp.exp(log_a)
        a_scr[:, cs] = a
        b_scr[:, cs] = jnp.sqrt(1.0 - a * a) * (i * xc[:, cs])


def _lru_scan(a_scr, b_scr, h_ref, state_ref, reset, reverse):
    groups = ROW_TILE // SUBLANES
    row = lax.broadcasted_iota(jnp.int32, (SUBLANES, LRU_W), 0)
    carry0 = jnp.where(reset, 0.0, state_ref[...])

    def body(g, carry):
        gi = (groups - 1 - g) if reverse else g
        r0 = pl.multiple_of(gi * SUBLANES, SUBLANES)
        A = a_scr[pl.ds(r0, SUBLANES), :]
        Bv = b_scr[pl.ds(r0, SUBLANES), :]
        for s in (1, 2, 4):
            if reverse:
                sh, m = SUBLANES - s, row < SUBLANES - s
            else:
                sh, m = s, row >= s
            A_sh = pltpu.roll(A, sh, 0)
            B_sh = pltpu.roll(Bv, sh, 0)
            Bv = jnp.where(m, A * B_sh + Bv, Bv)
            A = jnp.where(m, A * A_sh, A)
        h = Bv + A * carry
        h_ref[0, pl.ds(r0, SUBLANES), :] = h
        last = h[0:1] if reverse else h[SUBLANES - 1:SUBLANES]
        return jnp.broadcast_to(last, (SUBLANES, LRU_W))

    state_ref[...] = lax.fori_loop(0, groups, body, carry0)


def _lru_kernel(xf_ref, xfp_ref, xfn_ref, xb_ref, xbp_ref, xbn_ref, cw_ref, cb_ref, w_ref, bias_ref, lam_ref,
                hf_ref, hb_ref, sf_ref, sb_ref, ext_scr, a_scr, b_scr, *, n_lat_tiles):
    t = pl.program_id(1)
    reset = t == 0
    bt = jnp.where(t == 0, 0, n_lat_tiles + 1 - t)
    _lru_coeffs(xf_ref, xfp_ref, xfn_ref, t, 0, cw_ref, cb_ref, w_ref, bias_ref, lam_ref,
                ext_scr, a_scr, b_scr, n_lat_tiles)
    _lru_scan(a_scr, b_scr, hf_ref, sf_ref, reset, False)
    _lru_coeffs(xb_ref, xbp_ref, xbn_ref, bt, 1, cw_ref, cb_ref, w_ref, bias_ref, lam_ref,
                ext_scr, a_scr, b_scr, n_lat_tiles)
    _lru_scan(a_scr, b_scr, hb_ref, sb_ref, reset, True)


def _lru(xr, conv_w, conv_b, w_gates, bias, lam, L):
    B, rows, _ = xr.shape
    nl = L // ROW_TILE
    nt = nl + 1
    per = ROW_TILE // SUBLANES
    n8 = rows // SUBLANES
    ft = lambda b, t: t
    btile = lambda b, t: jnp.where(t == 0, 0, nl + 1 - t)
    main = lambda f: pl.BlockSpec((1, ROW_TILE, LRU_W), lambda b, t: (b, f(b, t), 0))
    prev = lambda f: pl.BlockSpec((1, SUBLANES, LRU_W), lambda b, t: (b, jnp.maximum(f(b, t) * per - 1, 0), 0))
    nxt = lambda f: pl.BlockSpec((1, SUBLANES, LRU_W),
                                 lambda b, t: (b, jnp.minimum((f(b, t) + 1) * per, n8 - 1), 0))
    full = lambda a: pl.BlockSpec(a.shape, lambda b, t: (0,) * a.ndim)
    return pl.pallas_call(
        functools.partial(_lru_kernel, n_lat_tiles=nl),
        grid=(B, nt),
        in_specs=[main(ft), prev(ft), nxt(ft), main(btile), prev(btile), nxt(btile),
                  full(conv_w), full(conv_b), full(w_gates), full(bias), full(lam)],
        out_specs=[
            pl.BlockSpec((1, ROW_TILE, LRU_W), lambda b, t: (b, jnp.maximum(t - 1, 0), 0)),
            pl.BlockSpec((1, ROW_TILE, LRU_W), lambda b, t: (b, nl - jnp.maximum(t, 1), 0)),
        ],
        out_shape=[jax.ShapeDtypeStruct((B, L, LRU_W), F32)] * 2,
        scratch_shapes=[
            pltpu.VMEM((SUBLANES, LRU_W), F32), pltpu.VMEM((SUBLANES, LRU_W), F32),
            pltpu.VMEM((ROW_TILE + 2 * SUBLANES, LRU_W), F32),
            pltpu.VMEM((ROW_TILE, LRU_W), F32), pltpu.VMEM((ROW_TILE, LRU_W), F32),
        ],
        compiler_params=_params(("arbitrary", "arbitrary")),
        name="lru",
    )(xr, xr, xr, xr, xr, xr, conv_w, conv_b, w_gates, bias, lam)


def _route(scores, sel):
    tm = scores.shape[1]
    neg = -jnp.inf
    iota_g = lax.broadcasted_iota(jnp.int32, (GROUP_SIZE, tm), 0)
    grp_score = []
    for g in range(N_GROUPS):
        sg = sel[g * GROUP_SIZE:(g + 1) * GROUP_SIZE]
        m1 = jnp.max(sg, axis=0, keepdims=True)
        first = jnp.min(jnp.where(sg == m1, iota_g, GROUP_SIZE), axis=0, keepdims=True)
        m2 = jnp.max(jnp.where(iota_g == first, neg, sg), axis=0, keepdims=True)
        grp_score.append(m1 + m2)
    masked = []
    for g in range(N_GROUPS):
        rank = jnp.zeros((1, tm), jnp.int32)
        for o in range(N_GROUPS):
            if o == g:
                continue
            ahead = (grp_score[o] > grp_score[g]) if o > g else (grp_score[o] >= grp_score[g])
            rank += ahead.astype(jnp.int32)
        keep = rank < TOPK_GROUPS
        masked.append(jnp.where(keep, sel[g * GROUP_SIZE:(g + 1) * GROUP_SIZE], neg))
    cand = jnp.concatenate(masked, axis=0)
    iota_e = lax.broadcasted_iota(jnp.int32, (N_EXPERTS, tm), 0)
    chosen = jnp.zeros((N_EXPERTS, tm), jnp.bool_)
    for _ in range(TOP_K):
        cur = jnp.where(chosen, neg, cand)
        m = jnp.max(cur, axis=0, keepdims=True)
        idx = jnp.min(jnp.where((cur == m) & jnp.logical_not(chosen), iota_e, N_EXPERTS), axis=0, keepdims=True)
        chosen = chosen | (iota_e == idx)
    w = jnp.where(chosen, scores, 0.0)
    return w / jnp.sum(w, axis=0, keepdims=True) * ROUTED_SCALE


def _merge_kernel(attn_ref, hf_ref, hb_ref, yg_ref, x_ref, mod_ref, wout_ref, ga_ref, gl_ref, lg_ref, lb_ref,
                  rwh_ref, rwl_ref, rb_ref, xn_ref, u2_ref, gates_ref):
    b = pl.program_id(0)
    D = D_MODEL
    mrow = lambda i: mod_ref[pl.ds(b, 1), i * D:(i + 1) * D]
    lru_y = (hf_ref[0] + hb_ref[0]) * jax.nn.gelu(yg_ref[0])
    na = _rms_norm(attn_ref[0], ga_ref[...]).astype(BF16)
    nl = _rms_norm(lru_y, gl_ref[...]).astype(BF16)
    y1 = jnp.dot(na, wout_ref[0:ATTN_W], preferred_element_type=F32)
    y1 += jnp.dot(nl, wout_ref[ATTN_W:D], preferred_element_type=F32)
    xn = _layer_norm(DEEPNORM_ALPHA * x_ref[0] + mrow(2) * y1) * lg_ref[...] + lb_ref[...]
    xn_ref[0] = xn
    u2 = _layer_norm(xn) * (1.0 + mrow(4)) + mrow(3)
    u_hi, u_lo = _split_bf16(u2)
    u2_ref[0] = u_hi
    nt = (((1,), (1,)), ((), ()))
    logits = lax.dot_general(rwh_ref[...], u_hi, nt, preferred_element_type=F32)
    logits += lax.dot_general(rwh_ref[...], u_lo, nt, preferred_element_type=F32)
    logits += lax.dot_general(rwl_ref[...], u_hi, nt, preferred_element_type=F32)
    scores = jax.nn.sigmoid(logits)
    gates_t = _route(scores, scores + rb_ref[...])
    pad = jnp.concatenate([gates_t, jnp.zeros_like(gates_t)], axis=0)
    gates_ref[0] = pad.T


def _merge(attn, hf, hb, yg, x, mod, w_out_bf, ga, gl, lg, lb, rw_hi, rw_lo, rbias, C):
    B, L, _ = x.shape
    off = C // ROW_TILE
    row = lambda w, o=0: pl.BlockSpec((1, ROW_TILE, w), lambda b, t: (b, t + o, 0))
    full = lambda a: pl.BlockSpec(a.shape, lambda b, t: (0,) * a.ndim)
    return pl.pallas_call(
        _merge_kernel,
        grid=(B, L // ROW_TILE),
        in_specs=[row(ATTN_W), row(LRU_W), row(LRU_W), row(LRU_W, off), row(D_MODEL),
                  full(mod), full(w_out_bf), full(ga), full(gl), full(lg), full(lb),
                  full(rw_hi), full(rw_lo), full(rbias)],
        out_specs=[row(D_MODEL), row(D_MODEL), row(LANES)],
        out_shape=[jax.ShapeDtypeStruct((B, L, D_MODEL), F32),
                   jax.ShapeDtypeStruct((B, L, D_MODEL), BF16),
                   jax.ShapeDtypeStruct((B, L, LANES), F32)],
        compiler_params=_params(("arbitrary", "arbitrary")),
        name="merge",
    )(attn, hf, hb, yg, x, mod, w_out_bf, ga, gl, lg, lb, rw_hi, rw_lo, rbias)


def _swiglu(u, wg, wu):
    hg = jnp.dot(u, wg, preferred_element_type=F32)
    hu = jnp.dot(u, wu, preferred_element_type=F32)
    return hg * jax.nn.sigmoid(hg) * hu


def _moe_kernel(u_ref, gates_ref, wg_ref, wu_ref, wd_ref, o_ref):
    e = pl.program_id(1)
    lane = lax.broadcasted_iota(jnp.int32, gates_ref.shape, 1)
    gcol = jnp.sum(jnp.where(lane == e, gates_ref[...], 0.0), axis=1, keepdims=True)
    h = _swiglu(u_ref[...], wg_ref[0].astype(BF16), wu_ref[0].astype(BF16)) * gcol
    y = jnp.dot(h.astype(BF16), wd_ref[0].astype(BF16), preferred_element_type=F32)

    @pl.when(e == 0)
    def _():
        o_ref[...] = y

    @pl.when(e > 0)
    def _():
        o_ref[...] += y


def _moe(u2, gates, wg, wu, wd):
    T = u2.shape[0]
    tm = MOE_TM
    row = lambda w: pl.BlockSpec((tm, w), lambda i, e: (i, 0))
    return pl.pallas_call(
        _moe_kernel,
        grid=(T // tm, N_EXPERTS),
        in_specs=[row(D_MODEL), row(LANES),
                  pl.BlockSpec((1, D_MODEL, EXPERT_FF), lambda i, e: (e, 0, 0)),
                  pl.BlockSpec((1, D_MODEL, EXPERT_FF), lambda i, e: (e, 0, 0)),
                  pl.BlockSpec((1, EXPERT_FF, D_MODEL), lambda i, e: (e, 0, 0))],
        out_specs=row(D_MODEL),
        out_shape=jax.ShapeDtypeStruct((T, D_MODEL), F32),
        compiler_params=_params(("arbitrary", "arbitrary")),
        name="moe",
    )(u2, gates, wg, wu, wd)


def _finish_kernel(u_ref, y_ref, xn_ref, g2_ref, sg_ref, su_ref, sd_ref, lg_ref, lb_ref, o_ref):
    hs = _swiglu(u_ref[...], sg_ref[...], su_ref[...])
    y2 = jnp.dot(hs.astype(BF16), sd_ref[...], preferred_element_type=F32) + y_ref[...]
    z = DEEPNORM_ALPHA * xn_ref[...] + g2_ref[0] * y2
    o_ref[...] = _layer_norm(z) * lg_ref[...] + lb_ref[...]


def _finish(u2, y, xn, g2, sg, su, sd, lg, lb, tiles_per_batch):
    T = u2.shape[0]
    row = pl.BlockSpec((ROW_TILE, D_MODEL), lambda i: (i, 0))
    full = lambda a: pl.BlockSpec(a.shape, lambda i: (0,) * a.ndim)
    return pl.pallas_call(
        _finish_kernel,
        grid=(T // ROW_TILE,),
        in_specs=[row, row, row,
                  pl.BlockSpec((1, 1, D_MODEL), lambda i: (i // tiles_per_batch, 0, 0)),
                  full(sg), full(su), full(sd), full(lg), full(lb)],
        out_specs=row,
        out_shape=jax.ShapeDtypeStruct((T, D_MODEL), F32),
        compiler_params=_params(("arbitrary",)),
        name="finish",
    )(u2, y, xn, g2, sg, su, sd, lg, lb)


def _rope_tables(L, C):
    rows = L // GRID_W
    row = jnp.repeat(jnp.arange(rows, dtype=F32), GRID_W)
    col = jnp.tile(jnp.arange(GRID_W, dtype=F32), rows)
    inv_freq = jnp.exp(-jnp.log(ROPE_THETA) * jnp.arange(0, ROPE_AXIS_DIM, 2, dtype=F32) / ROPE_AXIS_DIM)
    ang_r = row[:, None] * inv_freq
    ang_c = col[:, None] * inv_freq
    cr, sr, cc, sc = jnp.cos(ang_r), jnp.sin(ang_r), jnp.cos(ang_c), jnp.sin(ang_c)
    cos_t = jnp.concatenate([cr, cr, cc, cc], axis=-1)
    sin_t = jnp.concatenate([-sr, sr, -sc, sc], axis=-1)
    cos_t = jnp.concatenate([jnp.ones((C, HEAD_DIM), F32), cos_t], axis=0)
    sin_t = jnp.concatenate([jnp.zeros((C, HEAD_DIM), F32), sin_t], axis=0)
    return cos_t, sin_t


def kernel(x, c, ctx, c_ctx, w_mod, b_mod, w_in, attn_sink, conv_w, conv_b, lru_wa, lru_ba, lru_wx, lru_bx,
           lru_lam, norm_attn_g, norm_lru_g, w_out, ln1_g, ln1_b, router_w, router_bias, exp_w_gate, exp_w_up,
           exp_w_down, sh_w_gate, sh_w_up, sh_w_down, ln2_g, ln2_b):
    B, L, D = x.shape
    C = ctx.shape[1]
    assert w_mod.shape[0] == DEPTH and D == D_MODEL and B + 1 <= SUBLANES
    row2 = lambda a: a.reshape(1, -1)

    cvec = jnp.concatenate([c, c_ctx[None], jnp.zeros((SUBLANES - B - 1, D), F32)], axis=0)
    mod = _mod(cvec, w_mod[0], row2(b_mod[0]))

    cos_t, sin_t = _rope_tables(L, C)
    q, k, v, xr, yg = _inproj(x, ctx, mod, w_in[0].astype(BF16), cos_t, sin_t)

    attn = _attention(attn_sink[0], q, k, v, L, C)

    w_gates = jnp.concatenate([lru_wa[0], lru_wx[0]], axis=-1).astype(BF16)
    bias = jnp.stack([lru_ba[0], lru_bx[0]], axis=1)
    hf, hb = _lru(xr, conv_w[0], row2(conv_b[0]), w_gates, bias, lru_lam[0], L)

    rw_hi, rw_lo = _split_bf16(router_w[0].T)
    xn, u2, gates = _merge(attn, hf, hb, yg, x, mod, w_out[0].astype(BF16), row2(norm_attn_g[0]),
                           row2(norm_lru_g[0]), row2(ln1_g[0]), row2(ln1_b[0]), rw_hi, rw_lo,
                           router_bias[0].reshape(-1, 1), C)

    T = B * L
    g2 = mod[:B, 5 * D:6 * D].reshape(B, 1, D)
    u2 = u2.reshape(T, D)
    y = _moe(u2, gates.reshape(T, LANES), exp_w_gate[0], exp_w_up[0], exp_w_down[0])
    out = _finish(u2, y, xn.reshape(T, D), g2,
                  sh_w_gate[0].astype(BF16), sh_w_up[0].astype(BF16), sh_w_down[0].astype(BF16),
                  row2(ln2_g[0]), row2(ln2_b[0]), L // ROW_TILE)
    return out.reshape(B, L, D)
```

```python
import functools

import jax
import jax.numpy as jnp
from jax import lax
from jax.experimental import pallas as pl
from jax.experimental.pallas import tpu as pltpu

F32 = jnp.float32
BF16 = jnp.bfloat16

D_MODEL = 2048
GRID_W = 64
N_HEADS = 8
N_KV_HEADS = 2
HEAD_DIM = 128
GQA_GROUP = N_HEADS // N_KV_HEADS
ATTN_W = N_HEADS * HEAD_DIM
KV_W = N_KV_HEADS * HEAD_DIM
ATTN_SCALE = HEAD_DIM ** -0.5
BLOCK = 128
ROPE_THETA = 10000.0
ROPE_AXIS_DIM = HEAD_DIM // 2
LRU_W = D_MODEL - ATTN_W
LRU_BLOCKS = 8
LRU_BLOCK_W = LRU_W // LRU_BLOCKS
LRU_C = 8.0
N_EXPERTS = 64
TOP_K = 8
N_GROUPS = 8
GROUP_SIZE = N_EXPERTS // N_GROUPS
TOPK_GROUPS = 4
EXPERT_FF = 512
ROUTED_SCALE = 2.5
LN_EPS = 1e-6
DEPTH = 1
DEEPNORM_ALPHA = (2 * DEPTH) ** 0.25

SUBLANES = 8
LANES = 128
VMEM_LIMIT = 56 * 1024 * 1024

ROW_TILE = 256
MOD_TN = 1024
TOK_ROWS = D_MODEL // LANES
EXP_TR = 256
COMB_TM = 256
COMB_UNROLL = 4


def _params(sem, vmem=VMEM_LIMIT):
    return pltpu.CompilerParams(dimension_semantics=sem, vmem_limit_bytes=vmem)


def _layer_norm(x):
    mu = jnp.mean(x, axis=-1, keepdims=True)
    xc = x - mu
    var = jnp.mean(xc * xc, axis=-1, keepdims=True)
    return xc * lax.rsqrt(var + LN_EPS)


def _rms_norm(x, g):
    return x * lax.rsqrt(jnp.mean(x * x, axis=-1, keepdims=True) + LN_EPS) * g


def _split_bf16(x):
    hi = x.astype(BF16)
    lo = (x - hi.astype(F32)).astype(BF16)
    return hi, lo


def _mod_kernel(c_ref, w_ref, b_ref, o_ref):
    cv = c_ref[...]
    s = cv * jax.nn.sigmoid(cv)
    hi, lo = _split_bf16(s)
    lhs = jnp.concatenate([hi, lo], axis=0)
    r = jnp.dot(lhs, w_ref[...].astype(BF16), preferred_element_type=F32)
    o_ref[...] = r[:SUBLANES] + r[SUBLANES:] + b_ref[...]


def _mod(cvec, w_mod, b_mod):
    n = w_mod.shape[1]
    return pl.pallas_call(
        _mod_kernel,
        grid=(n // MOD_TN,),
        in_specs=[
            pl.BlockSpec((SUBLANES, D_MODEL), lambda j: (0, 0)),
            pl.BlockSpec((D_MODEL, MOD_TN), lambda j: (0, j)),
            pl.BlockSpec((1, MOD_TN), lambda j: (0, j)),
        ],
        out_specs=pl.BlockSpec((SUBLANES, MOD_TN), lambda j: (0, j)),
        out_shape=jax.ShapeDtypeStruct((SUBLANES, n), F32),
        compiler_params=_params(("arbitrary",)),
        name="mod",
    )(cvec, w_mod, b_mod)


def _rope(xh, cos, sin, even_block):
    partner = jnp.where(even_block, pltpu.roll(xh, 96, 1), pltpu.roll(xh, 32, 1))
    return xh * cos + partner * sin


def _inproj_kernel(x_ref, ctx_ref, mod_ref, w_ref, cos_ref, sin_ref,
                   q_ref, k_ref, v_ref, xr_ref, yg_ref):
    b = pl.program_id(0)
    t = pl.program_id(1)
    is_ctx = t == 0
    xin = jnp.where(is_ctx, ctx_ref[0], x_ref[0])
    r = jnp.where(is_ctx, 2, b)
    shift = mod_ref[pl.ds(r, 1), 0:D_MODEL]
    scale = mod_ref[pl.ds(r, 1), D_MODEL:2 * D_MODEL]
    u = (_layer_norm(xin) * (1.0 + scale) + shift).astype(BF16)

    cos = cos_ref[...]
    sin = sin_ref[...]
    lane = lax.broadcasted_iota(jnp.int32, (ROW_TILE, HEAD_DIM), 1)
    even_block = (lane % 64) < 32

    def proj(c0, c1):
        return jnp.dot(u, w_ref[:, c0:c1], preferred_element_type=F32)

    for h in range(N_HEADS):
        qh = proj(h * HEAD_DIM, (h + 1) * HEAD_DIM)
        q_ref[0, :, h * HEAD_DIM:(h + 1) * HEAD_DIM] = (
            _rope(qh, cos, sin, even_block) * ATTN_SCALE).astype(BF16)
    for h in range(N_KV_HEADS):
        c0 = ATTN_W + h * HEAD_DIM
        kh = proj(c0, c0 + HEAD_DIM)
        k_ref[0, :, h * HEAD_DIM:(h + 1) * HEAD_DIM] = _rope(kh, cos, sin, even_block).astype(BF16)
    v_ref[0] = proj(ATTN_W + KV_W, ATTN_W + 2 * KV_W).astype(BF16)
    c0 = ATTN_W + 2 * KV_W
    xr_ref[0] = proj(c0, c0 + LRU_W)
    yg_ref[0] = proj(c0 + LRU_W, c0 + 2 * LRU_W)


def _inproj(x, ctx, mod, w_in_bf, cos_t, sin_t):
    B, L, _ = x.shape
    C = ctx.shape[1]
    assert C == ROW_TILE and L % ROW_TILE == 0
    nt = L // ROW_TILE + 1
    rows = L + C
    d_in = w_in_bf.shape[1]
    out = lambda w, dt: jax.ShapeDtypeStruct((B, rows, w), dt)
    ospec = lambda w: pl.BlockSpec((1, ROW_TILE, w), lambda b, t: (b, t, 0))
    return pl.pallas_call(
        _inproj_kernel,
        grid=(B, nt),
        in_specs=[
            pl.BlockSpec((1, ROW_TILE, D_MODEL), lambda b, t: (b, jnp.maximum(t - 1, 0), 0)),
            pl.BlockSpec((1, ROW_TILE, D_MODEL), lambda b, t: (b, 0, 0)),
            pl.BlockSpec((SUBLANES, 2 * D_MODEL), lambda b, t: (0, 0)),
            pl.BlockSpec((D_MODEL, d_in), lambda b, t: (0, 0)),
            pl.BlockSpec((ROW_TILE, HEAD_DIM), lambda b, t: (t, 0)),
            pl.BlockSpec((ROW_TILE, HEAD_DIM), lambda b, t: (t, 0)),
        ],
        out_specs=[ospec(ATTN_W), ospec(KV_W), ospec(KV_W), ospec(LRU_W), ospec(LRU_W)],
        out_shape=[out(ATTN_W, BF16), out(KV_W, BF16), out(KV_W, BF16), out(LRU_W, F32), out(LRU_W, F32)],
        compiler_params=_params(("arbitrary", "arbitrary")),
        name="inproj",
    )(x, ctx, mod, w_in_bf, cos_t, sin_t)


def _attn_kernel(sink_ref, q_ref, kp_ref, kc_ref, kn_ref, vp_ref, vc_ref, vn_ref, kx_ref, vx_ref, o_ref):
    n = pl.program_id(1)
    nb = pl.num_programs(1)
    rows = GQA_GROUP * BLOCK
    n_ctx = kx_ref.shape[1]
    qi = lax.broadcasted_iota(jnp.int32, (rows, BLOCK), 0) % BLOCK
    kj = lax.broadcasted_iota(jnp.int32, (rows, BLOCK), 1)
    prev_ok = (kj >= qi) & (n > 0)
    next_ok = (kj <= qi) & (n < nb - 1)
    grp = lax.broadcasted_iota(jnp.int32, (rows, 1), 0) // BLOCK
    neg = -jnp.inf
    for h in range(N_KV_HEADS):
        hs = slice(h * HEAD_DIM, (h + 1) * HEAD_DIM)
        q4 = jnp.concatenate(
            [q_ref[0, :, (h * GQA_GROUP + g) * HEAD_DIM:(h * GQA_GROUP + g + 1) * HEAD_DIM]
             for g in range(GQA_GROUP)], axis=0)
        nt = (((1,), (1,)), ((), ()))
        sp = jnp.where(prev_ok, lax.dot_general(q4, kp_ref[0, :, hs], nt, preferred_element_type=F32), neg)
        sc = lax.dot_general(q4, kc_ref[0, :, hs], nt, preferred_element_type=F32)
        sn = jnp.where(next_ok, lax.dot_general(q4, kn_ref[0, :, hs], nt, preferred_element_type=F32), neg)
        sx = lax.dot_general(q4, kx_ref[0, :, hs], nt, preferred_element_type=F32)
        sink = jnp.zeros((rows, 1), F32)
        for g in range(GQA_GROUP):
            sink = jnp.where(grp == g, sink_ref[h * GQA_GROUP + g], sink)
        m = jnp.maximum(jnp.maximum(jnp.max(sp, -1, keepdims=True), jnp.max(sc, -1, keepdims=True)),
                        jnp.maximum(jnp.max(sn, -1, keepdims=True), jnp.max(sx, -1, keepdims=True)))
        m = jnp.maximum(m, sink)
        pp = jnp.exp(sp - m)
        pc = jnp.exp(sc - m)
        pn = jnp.exp(sn - m)
        px = jnp.exp(sx - m)
        denom = (jnp.sum(pp, -1, keepdims=True) + jnp.sum(pc, -1, keepdims=True)
                 + jnp.sum(pn, -1, keepdims=True) + jnp.sum(px, -1, keepdims=True) + jnp.exp(sink - m))
        acc = jnp.dot(pp.astype(BF16), vp_ref[0, :, hs], preferred_element_type=F32)
        acc += jnp.dot(pc.astype(BF16), vc_ref[0, :, hs], preferred_element_type=F32)
        acc += jnp.dot(pn.astype(BF16), vn_ref[0, :, hs], preferred_element_type=F32)
        acc += jnp.dot(px.astype(BF16), vx_ref[0, :, hs], preferred_element_type=F32)
        o = acc / denom
        for g in range(GQA_GROUP):
            c0 = (h * GQA_GROUP + g) * HEAD_DIM
            o_ref[0, :, c0:c0 + HEAD_DIM] = o[g * BLOCK:(g + 1) * BLOCK]


def _attention(sink, q, k, v, L, C):
    B = q.shape[0]
    nb = L // BLOCK
    off = C // BLOCK
    cur = lambda b, n: (b, n + off, 0)
    prv = lambda b, n: (b, jnp.maximum(n - 1, 0) + off, 0)
    nxt = lambda b, n: (b, jnp.minimum(n + 1, nb - 1) + off, 0)
    kv = lambda im: pl.BlockSpec((1, BLOCK, KV_W), im)
    cx = pl.BlockSpec((1, C, KV_W), lambda b, n: (b, 0, 0))
    return pl.pallas_call(
        _attn_kernel,
        grid=(B, nb),
        in_specs=[
            pl.BlockSpec(memory_space=pltpu.SMEM),
            pl.BlockSpec((1, BLOCK, ATTN_W), cur),
            kv(prv), kv(cur), kv(nxt), kv(prv), kv(cur), kv(nxt), cx, cx,
        ],
        out_specs=pl.BlockSpec((1, BLOCK, ATTN_W), lambda b, n: (b, n, 0)),
        out_shape=jax.ShapeDtypeStruct((B, L, ATTN_W), F32),
        compiler_params=_params(("arbitrary", "arbitrary")),
        name="attn",
    )(sink, q, k, k, k, v, v, v, k, v)


def _lru_coeffs(x_ref, p_ref, n_ref, tile, d, cw_ref, cb_ref, w_ref, bias_ref, lam_ref,
                ext_scr, a_scr, b_scr, n_lat_tiles):
    tm = ROW_TILE
    prev_ok = tile >= 2
    next_ok = (tile >= 1) & (tile < n_lat_tiles)
    ext_scr[0:SUBLANES] = jnp.where(prev_ok, p_ref[0], 0.0)
    ext_scr[SUBLANES:SUBLANES + tm] = x_ref[0]
    ext_scr[SUBLANES + tm:2 * SUBLANES + tm] = jnp.where(next_ok, n_ref[0], 0.0)
    xc = cb_ref[...] + cw_ref[2:3] * x_ref[0]
    xc += cw_ref[0:1] * ext_scr[SUBLANES - 2:SUBLANES - 2 + tm]
    xc += cw_ref[1:2] * ext_scr[SUBLANES - 1:SUBLANES - 1 + tm]
    xc += cw_ref[3:4] * ext_scr[SUBLANES + 1:SUBLANES + 1 + tm]
    xcb = xc.astype(BF16)
    lam = lam_ref[d:d + 1]
    sp = jnp.maximum(-lam, 0.0) + jnp.log1p(jnp.exp(-jnp.abs(lam)))
    for n in range(LRU_BLOCKS):
        cs = slice(n * LRU_BLOCK_W, (n + 1) * LRU_BLOCK_W)
        z = jnp.dot(xcb[:, cs], w_ref[d, n], preferred_element_type=F32)
        r = jax.nn.sigmoid(z[:, :LRU_BLOCK_W] + bias_ref[d, 0:1, cs])
        i = jax.nn.sigmoid(z[:, LRU_BLOCK_W:] + bias_ref[d, 1:2, cs])
        log_a = -LRU_C * r * sp[:, cs]
        a = jnp.exp(log_a)
        a_scr[:, cs] = a
        b_scr[:, cs] = jnp.sqrt(1.0 - a * a) * (i * xc[:, cs])


def _lru_scan(a_scr, b_scr, h_ref, state_ref, reset, reverse):
    groups = ROW_TILE // SUBLANES
    row = lax.broadcasted_iota(jnp.int32, (SUBLANES, LRU_W), 0)
    carry0 = jnp.where(reset, 0.0, state_ref[...])

    def body(g, carry):
        gi = (groups - 1 - g) if reverse else g
        r0 = pl.multiple_of(gi * SUBLANES, SUBLANES)
        A = a_scr[pl.ds(r0, SUBLANES), :]
        Bv = b_scr[pl.ds(r0, SUBLANES), :]
        for s in (1, 2, 4):
            if reverse:
                sh, m = SUBLANES - s, row < SUBLANES - s
            else:
                sh, m = s, row >= s
            A_sh = pltpu.roll(A, sh, 0)
            B_sh = pltpu.roll(Bv, sh, 0)
            Bv = jnp.where(m, A * B_sh + Bv, Bv)
            A = jnp.where(m, A * A_sh, A)
        h = Bv + A * carry
        h_ref[0, pl.ds(r0, SUBLANES), :] = h
        last = h[0:1] if reverse else h[SUBLANES - 1:SUBLANES]
        return jnp.broadcast_to(last, (SUBLANES, LRU_W))

    state_ref[...] = lax.fori_loop(0, groups, body, carry0)


def _lru_kernel(xf_ref, xfp_ref, xfn_ref, xb_ref, xbp_ref, xbn_ref, cw_ref, cb_ref, w_ref, bias_ref, lam_ref,
                hf_ref, hb_ref, sf_ref, sb_ref, ext_scr, a_scr, b_scr, *, n_lat_tiles):
    t = pl.program_id(1)
    reset = t == 0
    bt = jnp.where(t == 0, 0, n_lat_tiles + 1 - t)
    _lru_coeffs(xf_ref, xfp_ref, xfn_ref, t, 0, cw_ref, cb_ref, w_ref, bias_ref, lam_ref,
                ext_scr, a_scr, b_scr, n_lat_tiles)
    _lru_scan(a_scr, b_scr, hf_ref, sf_ref, reset, False)
    _lru_coeffs(xb_ref, xbp_ref, xbn_ref, bt, 1, cw_ref, cb_ref, w_ref, bias_ref, lam_ref,
                ext_scr, a_scr, b_scr, n_lat_tiles)
    _lru_scan(a_scr, b_scr, hb_ref, sb_ref, reset, True)


def _lru(xr, conv_w, conv_b, w_gates, bias, lam, L):
    B, rows, _ = xr.shape
    nl = L // ROW_TILE
    nt = nl + 1
    per = ROW_TILE // SUBLANES
    n8 = rows // SUBLANES
    ft = lambda b, t: t
    btile = lambda b, t: jnp.where(t == 0, 0, nl + 1 - t)
    main = lambda f: pl.BlockSpec((1, ROW_TILE, LRU_W), lambda b, t: (b, f(b, t), 0))
    prev = lambda f: pl.BlockSpec((1, SUBLANES, LRU_W), lambda b, t: (b, jnp.maximum(f(b, t) * per - 1, 0), 0))
    nxt = lambda f: pl.BlockSpec((1, SUBLANES, LRU_W),
                                 lambda b, t: (b, jnp.minimum((f(b, t) + 1) * per, n8 - 1), 0))
    full = lambda a: pl.BlockSpec(a.shape, lambda b, t: (0,) * a.ndim)
    return pl.pallas_call(
        functools.partial(_lru_kernel, n_lat_tiles=nl),
        grid=(B, nt),
        in_specs=[main(ft), prev(ft), nxt(ft), main(btile), prev(btile), nxt(btile),
                  full(conv_w), full(conv_b), full(w_gates), full(bias), full(lam)],
        out_specs=[
            pl.BlockSpec((1, ROW_TILE, LRU_W), lambda b, t: (b, jnp.maximum(t - 1, 0), 0)),
            pl.BlockSpec((1, ROW_TILE, LRU_W), lambda b, t: (b, nl - jnp.maximum(t, 1), 0)),
        ],
        out_shape=[jax.ShapeDtypeStruct((B, L, LRU_W), F32)] * 2,
        scratch_shapes=[
            pltpu.VMEM((SUBLANES, LRU_W), F32), pltpu.VMEM((SUBLANES, LRU_W), F32),
            pltpu.VMEM((ROW_TILE + 2 * SUBLANES, LRU_W), F32),
            pltpu.VMEM((ROW_TILE, LRU_W), F32), pltpu.VMEM((ROW_TILE, LRU_W), F32),
        ],
        compiler_params=_params(("arbitrary", "arbitrary")),
        name="lru",
    )(xr, xr, xr, xr, xr, xr, conv_w, conv_b, w_gates, bias, lam)


def _route(scores, sel):
    tm = scores.shape[1]
    neg = -jnp.inf
    iota_g = lax.broadcasted_iota(jnp.int32, (GROUP_SIZE, tm), 0)
    grp_score = []
    for g in range(N_GROUPS):
        sg = sel[g * GROUP_SIZE:(g + 1) * GROUP_SIZE]
        m1 = jnp.max(sg, axis=0, keepdims=True)
        first = jnp.min(jnp.where(sg == m1, iota_g, GROUP_SIZE), axis=0, keepdims=True)
        m2 = jnp.max(jnp.where(iota_g == first, neg, sg), axis=0, keepdims=True)
        grp_score.append(m1 + m2)
    masked = []
    for g in range(N_GROUPS):
        rank = jnp.zeros((1, tm), jnp.int32)
        for o in range(N_GROUPS):
            if o == g:
                continue
            ahead = (grp_score[o] > grp_score[g]) if o > g else (grp_score[o] >= grp_score[g])
            rank += ahead.astype(jnp.int32)
        keep = rank < TOPK_GROUPS
        masked.append(jnp.where(keep, sel[g * GROUP_SIZE:(g + 1) * GROUP_SIZE], neg))
    cand = jnp.concatenate(masked, axis=0)
    iota_e = lax.broadcasted_iota(jnp.int32, (N_EXPERTS, tm), 0)
    chosen = jnp.zeros((N_EXPERTS, tm), jnp.bool_)
    picks = []
    for _ in range(TOP_K):
        cur = jnp.where(chosen, neg, cand)
        m = jnp.max(cur, axis=0, keepdims=True)
        idx = jnp.min(jnp.where((cur == m) & jnp.logical_not(chosen), iota_e, N_EXPERTS), axis=0, keepdims=True)
        chosen = chosen | (iota_e == idx)
        picks.append(idx)
    w = jnp.where(chosen, scores, 0.0)
    gates = w / jnp.sum(w, axis=0, keepdims=True) * ROUTED_SCALE
    return gates, chosen, picks


def _to_token_major(y, scr):
    rows = y.shape[0]
    for s in range(TOK_ROWS):
        scr[pl.ds(s, rows, stride=TOK_ROWS), :] = y[:, s * LANES:(s + 1) * LANES]


def _from_token_major(scr, rows):
    return [scr[pl.ds(s, rows, stride=TOK_ROWS), :] for s in range(TOK_ROWS)]


def _merge_kernel(attn_ref, hf_ref, hb_ref, yg_ref, x_ref, mod_ref, wout_ref, ga_ref, gl_ref, lg_ref, lb_ref,
                  rwh_ref, rwl_ref, rb_ref, tri_ref, xn_ref, u2_ref, up_ref, eidx_ref, wk_ref, pos_ref, cnt_ref,
                  carry_ref, tok_scr):
    b = pl.program_id(0)

    @pl.when((b == 0) & (pl.program_id(1) == 0))
    def _():
        carry_ref[...] = jnp.zeros_like(carry_ref)

    D = D_MODEL
    mrow = lambda i: mod_ref[pl.ds(b, 1), i * D:(i + 1) * D]
    lru_y = (hf_ref[0] + hb_ref[0]) * jax.nn.gelu(yg_ref[0])
    na = _rms_norm(attn_ref[0], ga_ref[...]).astype(BF16)
    nl = _rms_norm(lru_y, gl_ref[...]).astype(BF16)
    y1 = jnp.dot(na, wout_ref[0:ATTN_W], preferred_element_type=F32)
    y1 += jnp.dot(nl, wout_ref[ATTN_W:D], preferred_element_type=F32)
    xn = _layer_norm(DEEPNORM_ALPHA * x_ref[0] + mrow(2) * y1) * lg_ref[...] + lb_ref[...]
    xn_ref[0] = xn
    u2 = _layer_norm(xn) * (1.0 + mrow(4)) + mrow(3)
    u_hi, u_lo = _split_bf16(u2)
    u2_ref[0] = u_hi
    nt = (((1,), (1,)), ((), ()))
    logits = lax.dot_general(rwh_ref[...], u_hi, nt, preferred_element_type=F32)
    logits += lax.dot_general(rwh_ref[...], u_lo, nt, preferred_element_type=F32)
    logits += lax.dot_general(rwl_ref[...], u_hi, nt, preferred_element_type=F32)
    _to_token_major(u2, tok_scr)
    up_ref[...] = tok_scr[...].astype(BF16)
    scores = jax.nn.sigmoid(logits)
    gates, chosen, picks = _route(scores, scores + rb_ref[...])
    sel01 = jnp.where(chosen, 1.0, 0.0)
    incl = jnp.dot(sel01.astype(BF16), tri_ref[...], preferred_element_type=F32)
    rank = carry_ref[:, 0:1] + incl - sel01
    carry_ref[...] = carry_ref[...] + incl[:, ROW_TILE - 1:ROW_TILE]
    cnt_ref[...] = carry_ref[...]
    iota_e = lax.broadcasted_iota(jnp.int32, (N_EXPERTS, ROW_TILE), 0)
    for kk, idx in enumerate(picks):
        hit = iota_e == idx
        eidx_ref[kk:kk + 1, :] = idx
        wk_ref[kk:kk + 1, :] = jnp.sum(jnp.where(hit, gates, 0.0), axis=0, keepdims=True)
        pos_ref[kk:kk + 1, :] = jnp.sum(jnp.where(hit, rank, 0.0), axis=0, keepdims=True).astype(jnp.int32)


def _merge(attn, hf, hb, yg, x, mod, w_out_bf, ga, gl, lg, lb, rw_hi, rw_lo, rbias, tri, C):
    B, L, _ = x.shape
    T = B * L
    nt = L // ROW_TILE
    off = C // ROW_TILE
    row = lambda w, o=0: pl.BlockSpec((1, ROW_TILE, w), lambda b, t: (b, t + o, 0))
    full = lambda a: pl.BlockSpec(a.shape, lambda b, t: (0,) * a.ndim)
    tok = pl.BlockSpec((TOP_K, ROW_TILE), lambda b, t: (0, b * nt + t))
    return pl.pallas_call(
        _merge_kernel,
        grid=(B, nt),
        in_specs=[row(ATTN_W), row(LRU_W), row(LRU_W), row(LRU_W, off), row(D_MODEL),
                  full(mod), full(w_out_bf), full(ga), full(gl), full(lg), full(lb),
                  full(rw_hi), full(rw_lo), full(rbias), full(tri)],
        out_specs=[row(D_MODEL), row(D_MODEL),
                   pl.BlockSpec((ROW_TILE * TOK_ROWS, LANES), lambda b, t: (b * nt + t, 0)),
                   tok, tok, tok,
                   pl.BlockSpec((N_EXPERTS, LANES), lambda b, t: (0, 0))],
        out_shape=[jax.ShapeDtypeStruct((B, L, D_MODEL), F32),
                   jax.ShapeDtypeStruct((B, L, D_MODEL), BF16),
                   jax.ShapeDtypeStruct((T * TOK_ROWS, LANES), BF16),
                   jax.ShapeDtypeStruct((TOP_K, T), jnp.int32),
                   jax.ShapeDtypeStruct((TOP_K, T), F32),
                   jax.ShapeDtypeStruct((TOP_K, T), jnp.int32),
                   jax.ShapeDtypeStruct((N_EXPERTS, LANES), F32)],
        scratch_shapes=[pltpu.VMEM((N_EXPERTS, LANES), F32), pltpu.VMEM((ROW_TILE * TOK_ROWS, LANES), F32)],
        compiler_params=_params(("arbitrary", "arbitrary")),
        name="merge",
    )(attn, hf, hb, yg, x, mod, w_out_bf, ga, gl, lg, lb, rw_hi, rw_lo, rbias, tri)


def _swiglu(u, wg, wu):
    hg = jnp.dot(u, wg, preferred_element_type=F32)
    hu = jnp.dot(u, wu, preferred_element_type=F32)
    return hg * jax.nn.sigmoid(hg) * hu


def _token_rows(ref, row0, n):
    return ref.at[pl.ds(pl.multiple_of(row0, TOK_ROWS), n * TOK_ROWS), :]


def _dispatch_kernel(dest_ref, up_ref, xs_zero_ref, xs_ref, sem):
    del xs_zero_ref

    def issue(j, carry):
        src = _token_rows(up_ref, j * TOK_ROWS, 1)
        for k in range(TOP_K):
            pltpu.make_async_copy(src, _token_rows(xs_ref, dest_ref[k, j], 1), sem).start()
        return carry

    lax.fori_loop(0, ROW_TILE, issue, 0)
    for k in range(TOP_K):
        pltpu.make_async_copy(up_ref, _token_rows(xs_ref, 0, ROW_TILE), sem).wait()


def _dispatch(dest_rows, up, n_sorted_rows):
    T = dest_rows.shape[1]
    xs0 = jnp.zeros((n_sorted_rows * TOK_ROWS, LANES), BF16)
    return pl.pallas_call(
        _dispatch_kernel,
        grid=(T // ROW_TILE,),
        in_specs=[pl.BlockSpec((TOP_K, ROW_TILE), lambda i: (0, i), memory_space=pltpu.SMEM),
                  pl.BlockSpec((ROW_TILE * TOK_ROWS, LANES), lambda i: (i, 0)),
                  pl.BlockSpec(memory_space=pl.ANY)],
        out_specs=pl.BlockSpec(memory_space=pl.ANY),
        out_shape=jax.ShapeDtypeStruct(xs0.shape, BF16),
        scratch_shapes=[pltpu.SemaphoreType.DMA(())],
        input_output_aliases={2: 0},
        compiler_params=_params(("arbitrary",)),
        name="dispatch",
    )(dest_rows, up, xs0)


def _expert_kernel(te_ref, nu_ref, xs_ref, wg_ref, wu_ref, wd_ref, ys_ref, wgb_ref, wub_ref, wdb_ref, tok_scr):
    i = pl.program_id(0)

    @pl.when(i < nu_ref[0])
    def _():
        new_expert = (i == 0) | (te_ref[i] != te_ref[jnp.maximum(i - 1, 0)])

        @pl.when(new_expert)
        def _():
            wgb_ref[...] = wg_ref[0].astype(BF16)
            wub_ref[...] = wu_ref[0].astype(BF16)
            wdb_ref[...] = wd_ref[0].astype(BF16)

        tok_scr[...] = xs_ref[...].astype(F32)
        x = jnp.concatenate([c.astype(BF16) for c in _from_token_major(tok_scr, EXP_TR)], axis=1)
        h = _swiglu(x, wgb_ref[...], wub_ref[...])
        y = jnp.dot(h.astype(BF16), wdb_ref[...], preferred_element_type=F32)
        _to_token_major(y, tok_scr)
        ys_ref[...] = tok_scr[...].astype(BF16)

    @pl.when(i >= nu_ref[0])
    def _():
        ys_ref[...] = jnp.zeros_like(ys_ref)


def _experts(tile_expert, n_used, xs, wg, wu, wd):
    n_tiles = tile_expert.shape[0]
    blk = (EXP_TR * TOK_ROWS, LANES)
    wspec = lambda a: pl.BlockSpec((1,) + a.shape[1:], lambda i, te, nu: (te[i], 0, 0))
    return pl.pallas_call(
        _expert_kernel,
        grid_spec=pltpu.PrefetchScalarGridSpec(
            num_scalar_prefetch=2,
            grid=(n_tiles,),
            in_specs=[pl.BlockSpec(blk, lambda i, te, nu: (jnp.minimum(i, nu[0] - 1), 0)),
                      wspec(wg), wspec(wu), wspec(wd)],
            out_specs=pl.BlockSpec(blk, lambda i, te, nu: (i, 0)),
            scratch_shapes=[pltpu.VMEM(wg.shape[1:], BF16), pltpu.VMEM(wu.shape[1:], BF16),
                            pltpu.VMEM(wd.shape[1:], BF16), pltpu.VMEM(blk, F32)],
        ),
        out_shape=jax.ShapeDtypeStruct(xs.shape, BF16),
        compiler_params=_params(("arbitrary",)),
        name="experts",
    )(tile_expert, n_used, xs, wg, wu, wd)


def _combine_kernel(dcur_ref, dnext_ref, wk_ref, u_ref, xn_ref, g2_ref, sg_ref, su_ref, sd_ref, lg_ref, lb_ref,
                    ys_ref, o_ref, buf_ref, y_scr, sem):
    i = pl.program_id(0)
    n = pl.num_programs(0)
    tm = COMB_TM
    slot = i % 2

    def gather(d_ref, sl):
        def issue(j, carry):
            for k in range(TOP_K):
                pltpu.make_async_copy(_token_rows(ys_ref, d_ref[k, j], 1),
                                      _token_rows(buf_ref.at[sl], (k * tm + j) * TOK_ROWS, 1),
                                      sem.at[sl]).start()
            return carry
        lax.fori_loop(0, tm, issue, 0)

    @pl.when(i == 0)
    def _():
        gather(dcur_ref, 0)

    @pl.when(i + 1 < n)
    def _():
        gather(dnext_ref, 1 - slot)

    pltpu.make_async_copy(_token_rows(ys_ref, 0, TOP_K * tm), buf_ref.at[slot], sem.at[slot]).wait()

    def accumulate(jo, carry):
        for u in range(COMB_UNROLL):
            j = jo * COMB_UNROLL + u
            acc = jnp.zeros((TOK_ROWS, LANES), F32)
            for k in range(TOP_K):
                rows = _token_rows(buf_ref.at[slot], (k * tm + j) * TOK_ROWS, 1)
                acc += wk_ref[k, j] * rows[...].astype(F32)
            _token_rows(y_scr, j * TOK_ROWS, 1)[...] = acc
        return carry

    lax.fori_loop(0, tm // COMB_UNROLL, accumulate, 0)
    y_routed = jnp.concatenate(_from_token_major(y_scr, tm), axis=1)

    hs = _swiglu(u_ref[...], sg_ref[...], su_ref[...])
    y2 = jnp.dot(hs.astype(BF16), sd_ref[...], preferred_element_type=F32) + y_routed
    z = DEEPNORM_ALPHA * xn_ref[...] + g2_ref[0] * y2
    o_ref[...] = _layer_norm(z) * lg_ref[...] + lb_ref[...]


def _combine(dest_rows, wk, u2, xn, g2, sg, su, sd, lg, lb, ys, tiles_per_batch):
    T = u2.shape[0]
    tm = COMB_TM
    n = T // tm
    row = pl.BlockSpec((tm, D_MODEL), lambda i: (i, 0))
    full = lambda a: pl.BlockSpec(a.shape, lambda i: (0,) * a.ndim)
    tok = lambda f: pl.BlockSpec((TOP_K, tm), lambda i: (0, f(i)), memory_space=pltpu.SMEM)
    return pl.pallas_call(
        _combine_kernel,
        grid=(n,),
        in_specs=[tok(lambda i: i), tok(lambda i: jnp.minimum(i + 1, n - 1)), tok(lambda i: i), row, row,
                  pl.BlockSpec((1, 1, D_MODEL), lambda i: (i // tiles_per_batch, 0, 0)),
                  full(sg), full(su), full(sd), full(lg), full(lb),
                  pl.BlockSpec(memory_space=pl.ANY)],
        out_specs=row,
        out_shape=jax.ShapeDtypeStruct((T, D_MODEL), F32),
        scratch_shapes=[pltpu.VMEM((2, TOP_K * tm * TOK_ROWS, LANES), BF16),
                        pltpu.VMEM((tm * TOK_ROWS, LANES), F32),
                        pltpu.SemaphoreType.DMA((2,))],
        compiler_params=_params(("arbitrary",)),
        name="combine",
    )(dest_rows, dest_rows, wk, u2, xn, g2, sg, su, sd, lg, lb, ys)


def _rope_tables(L, C):
    rows = L // GRID_W
    row = jnp.repeat(jnp.arange(rows, dtype=F32), GRID_W)
    col = jnp.tile(jnp.arange(GRID_W, dtype=F32), rows)
    inv_freq = jnp.exp(-jnp.log(ROPE_THETA) * jnp.arange(0, ROPE_AXIS_DIM, 2, dtype=F32) / ROPE_AXIS_DIM)
    ang_r = row[:, None] * inv_freq
    ang_c = col[:, None] * inv_freq
    cr, sr, cc, sc = jnp.cos(ang_r), jnp.sin(ang_r), jnp.cos(ang_c), jnp.sin(ang_c)
    cos_t = jnp.concatenate([cr, cr, cc, cc], axis=-1)
    sin_t = jnp.concatenate([-sr, sr, -sc, sc], axis=-1)
    cos_t = jnp.concatenate([jnp.ones((C, HEAD_DIM), F32), cos_t], axis=0)
    sin_t = jnp.concatenate([jnp.zeros((C, HEAD_DIM), F32), sin_t], axis=0)
    return cos_t, sin_t


def kernel(x, c, ctx, c_ctx, w_mod, b_mod, w_in, attn_sink, conv_w, conv_b, lru_wa, lru_ba, lru_wx, lru_bx,
           lru_lam, norm_attn_g, norm_lru_g, w_out, ln1_g, ln1_b, router_w, router_bias, exp_w_gate, exp_w_up,
           exp_w_down, sh_w_gate, sh_w_up, sh_w_down, ln2_g, ln2_b):
    B, L, D = x.shape
    C = ctx.shape[1]
    assert w_mod.shape[0] == DEPTH and D == D_MODEL and B + 1 <= SUBLANES
    row2 = lambda a: a.reshape(1, -1)

    cvec = jnp.concatenate([c, c_ctx[None], jnp.zeros((SUBLANES - B - 1, D), F32)], axis=0)
    mod = _mod(cvec, w_mod[0], row2(b_mod[0]))

    cos_t, sin_t = _rope_tables(L, C)
    q, k, v, xr, yg = _inproj(x, ctx, mod, w_in[0].astype(BF16), cos_t, sin_t)

    attn = _attention(attn_sink[0], q, k, v, L, C)

    w_gates = jnp.concatenate([lru_wa[0], lru_wx[0]], axis=-1).astype(BF16)
    bias = jnp.stack([lru_ba[0], lru_bx[0]], axis=1)
    hf, hb = _lru(xr, conv_w[0], row2(conv_b[0]), w_gates, bias, lru_lam[0], L)

    rw_hi, rw_lo = _split_bf16(router_w[0].T)
    tri = jnp.triu(jnp.ones((ROW_TILE, ROW_TILE), BF16))
    xn, u2, up, eidx, wk, pos, cnt = _merge(
        attn, hf, hb, yg, x, mod, w_out[0].astype(BF16), row2(norm_attn_g[0]), row2(norm_lru_g[0]),
        row2(ln1_g[0]), row2(ln1_b[0]), rw_hi, rw_lo, router_bias[0].reshape(-1, 1), tri, C)

    T = B * L
    n_tiles = T * TOP_K // EXP_TR + N_EXPERTS
    counts = cnt[:, 0].astype(jnp.int32)
    tiles_e = (counts + EXP_TR - 1) // EXP_TR
    tile_end = jnp.cumsum(tiles_e)
    n_used = tile_end[-1:]
    row_start = (tile_end - tiles_e) * EXP_TR
    experts = jnp.arange(N_EXPERTS, dtype=jnp.int32)
    start_of = jnp.sum(jnp.where(eidx[None] == experts[:, None, None], row_start[:, None, None], 0), axis=0)
    dest_rows = (start_of + pos) * TOK_ROWS
    tile_id = jnp.minimum(jnp.arange(n_tiles, dtype=jnp.int32), n_used - 1)
    tile_expert = jnp.sum((tile_end[None, :] <= tile_id[:, None]).astype(jnp.int32), axis=1)

    xs = _dispatch(dest_rows, up, n_tiles * EXP_TR)
    ys = _experts(tile_expert, n_used, xs, exp_w_gate[0], exp_w_up[0], exp_w_down[0])
    g2 = mod[:B, 5 * D:6 * D].reshape(B, 1, D)
    out = _combine(dest_rows, wk, u2.reshape(T, D), xn.reshape(T, D), g2,
                   sh_w_gate[0].astype(BF16), sh_w_up[0].astype(BF16), sh_w_down[0].astype(BF16),
                   row2(ln2_g[0]), row2(ln2_b[0]), ys, L // COMB_TM)
    return out.reshape(B, L, D)
```

```python
import functools

import jax
import jax.numpy as jnp
from jax import lax
from jax.experimental import pallas as pl
from jax.experimental.pallas import tpu as pltpu

F32 = jnp.float32
BF16 = jnp.bfloat16

D_MODEL = 2048
GRID_W = 64
N_HEADS = 8
N_KV_HEADS = 2
HEAD_DIM = 128
GQA_GROUP = N_HEADS // N_KV_HEADS
ATTN_W = N_HEADS * HEAD_DIM
KV_W = N_KV_HEADS * HEAD_DIM
ATTN_SCALE = HEAD_DIM ** -0.5
BLOCK = 128
ROPE_THETA = 10000.0
ROPE_AXIS_DIM = HEAD_DIM // 2
LRU_W = D_MODEL - ATTN_W
LRU_BLOCKS = 8
LRU_BLOCK_W = LRU_W // LRU_BLOCKS
LRU_C = 8.0
N_EXPERTS = 64
TOP_K = 8
N_GROUPS = 8
GROUP_SIZE = N_EXPERTS // N_GROUPS
TOPK_GROUPS = 4
EXPERT_FF = 512
ROUTED_SCALE = 2.5
LN_EPS = 1e-6
DEPTH = 1
DEEPNORM_ALPHA = (2 * DEPTH) ** 0.25

SUBLANES = 8
LANES = 128
VMEM_LIMIT = 56 * 1024 * 1024

ROW_TILE = 256
MOD_TN = 1024
TOK_ROWS = D_MODEL // LANES
TOK_PITCH = TOK_ROWS + SUBLANES
EXP_TR = 256
COMB_TM = 256
COMB_UNROLL = 4


def _params(sem, vmem=VMEM_LIMIT):
    return pltpu.CompilerParams(dimension_semantics=sem, vmem_limit_bytes=vmem)


def _layer_norm(x):
    mu = jnp.mean(x, axis=-1, keepdims=True)
    xc = x - mu
    var = jnp.mean(xc * xc, axis=-1, keepdims=True)
    return xc * lax.rsqrt(var + LN_EPS)


def _rms_norm(x, g):
    return x * lax.rsqrt(jnp.mean(x * x, axis=-1, keepdims=True) + LN_EPS) * g


def _split_bf16(x):
    hi = x.astype(BF16)
    lo = (x - hi.astype(F32)).astype(BF16)
    return hi, lo


def _mod_kernel(c_ref, w_ref, b_ref, o_ref):
    cv = c_ref[...]
    s = cv * jax.nn.sigmoid(cv)
    hi, lo = _split_bf16(s)
    lhs = jnp.concatenate([hi, lo], axis=0)
    r = jnp.dot(lhs, w_ref[...].astype(BF16), preferred_element_type=F32)
    o_ref[...] = r[:SUBLANES] + r[SUBLANES:] + b_ref[...]


def _mod(cvec, w_mod, b_mod):
    n = w_mod.shape[1]
    return pl.pallas_call(
        _mod_kernel,
        grid=(n // MOD_TN,),
        in_specs=[
            pl.BlockSpec((SUBLANES, D_MODEL), lambda j: (0, 0)),
            pl.BlockSpec((D_MODEL, MOD_TN), lambda j: (0, j)),
            pl.BlockSpec((1, MOD_TN), lambda j: (0, j)),
        ],
        out_specs=pl.BlockSpec((SUBLANES, MOD_TN), lambda j: (0, j)),
        out_shape=jax.ShapeDtypeStruct((SUBLANES, n), F32),
        compiler_params=_params(("arbitrary",)),
        name="mod",
    )(cvec, w_mod, b_mod)


def _rope(xh, cos, sin, even_block):
    partner = jnp.where(even_block, pltpu.roll(xh, 96, 1), pltpu.roll(xh, 32, 1))
    return xh * cos + partner * sin


def _inproj_kernel(x_ref, ctx_ref, mod_ref, w_ref, cos_ref, sin_ref,
                   q_ref, k_ref, v_ref, xr_ref, yg_ref):
    b = pl.program_id(0)
    t = pl.program_id(1)
    is_ctx = t == 0
    xin = jnp.where(is_ctx, ctx_ref[0], x_ref[0])
    r = jnp.where(is_ctx, 2, b)
    shift = mod_ref[pl.ds(r, 1), 0:D_MODEL]
    scale = mod_ref[pl.ds(r, 1), D_MODEL:2 * D_MODEL]
    u = (_layer_norm(xin) * (1.0 + scale) + shift).astype(BF16)

    cos = cos_ref[...]
    sin = sin_ref[...]
    lane = lax.broadcasted_iota(jnp.int32, (ROW_TILE, HEAD_DIM), 1)
    even_block = (lane % 64) < 32

    def proj(c0, c1):
        return jnp.dot(u, w_ref[:, c0:c1], preferred_element_type=F32)

    for h in range(N_HEADS):
        qh = proj(h * HEAD_DIM, (h + 1) * HEAD_DIM)
        q_ref[0, :, h * HEAD_DIM:(h + 1) * HEAD_DIM] = (
            _rope(qh, cos, sin, even_block) * ATTN_SCALE).astype(BF16)
    for h in range(N_KV_HEADS):
        c0 = ATTN_W + h * HEAD_DIM
        kh = proj(c0, c0 + HEAD_DIM)
        k_ref[0, :, h * HEAD_DIM:(h + 1) * HEAD_DIM] = _rope(kh, cos, sin, even_block).astype(BF16)
    v_ref[0] = proj(ATTN_W + KV_W, ATTN_W + 2 * KV_W).astype(BF16)
    c0 = ATTN_W + 2 * KV_W
    xr_ref[0] = proj(c0, c0 + LRU_W)
    yg_ref[0] = proj(c0 + LRU_W, c0 + 2 * LRU_W)


def _inproj(x, ctx, mod, w_in_bf, cos_t, sin_t):
    B, L, _ = x.shape
    C = ctx.shape[1]
    assert C == ROW_TILE and L % ROW_TILE == 0
    nt = L // ROW_TILE + 1
    rows = L + C
    d_in = w_in_bf.shape[1]
    out = lambda w, dt: jax.ShapeDtypeStruct((B, rows, w), dt)
    ospec = lambda w: pl.BlockSpec((1, ROW_TILE, w), lambda b, t: (b, t, 0))
    return pl.pallas_call(
        _inproj_kernel,
        grid=(B, nt),
        in_specs=[
            pl.BlockSpec((1, ROW_TILE, D_MODEL), lambda b, t: (b, jnp.maximum(t - 1, 0), 0)),
            pl.BlockSpec((1, ROW_TILE, D_MODEL), lambda b, t: (b, 0, 0)),
            pl.BlockSpec((SUBLANES, 2 * D_MODEL), lambda b, t: (0, 0)),
            pl.BlockSpec((D_MODEL, d_in), lambda b, t: (0, 0)),
            pl.BlockSpec((ROW_TILE, HEAD_DIM), lambda b, t: (t, 0)),
            pl.BlockSpec((ROW_TILE, HEAD_DIM), lambda b, t: (t, 0)),
        ],
        out_specs=[ospec(ATTN_W), ospec(KV_W), ospec(KV_W), ospec(LRU_W), ospec(LRU_W)],
        out_shape=[out(ATTN_W, BF16), out(KV_W, BF16), out(KV_W, BF16), out(LRU_W, F32), out(LRU_W, F32)],
        compiler_params=_params(("arbitrary", "arbitrary")),
        name="inproj",
    )(x, ctx, mod, w_in_bf, cos_t, sin_t)


def _attn_kernel(sink_ref, q_ref, kp_ref, kc_ref, kn_ref, vp_ref, vc_ref, vn_ref, kx_ref, vx_ref, o_ref):
    n = pl.program_id(1)
    nb = pl.num_programs(1)
    rows = GQA_GROUP * BLOCK
    n_ctx = kx_ref.shape[1]
    qi = lax.broadcasted_iota(jnp.int32, (rows, BLOCK), 0) % BLOCK
    kj = lax.broadcasted_iota(jnp.int32, (rows, BLOCK), 1)
    prev_ok = (kj >= qi) & (n > 0)
    next_ok = (kj <= qi) & (n < nb - 1)
    grp = lax.broadcasted_iota(jnp.int32, (rows, 1), 0) // BLOCK
    neg = -jnp.inf
    for h in range(N_KV_HEADS):
        hs = slice(h * HEAD_DIM, (h + 1) * HEAD_DIM)
        q4 = jnp.concatenate(
            [q_ref[0, :, (h * GQA_GROUP + g) * HEAD_DIM:(h * GQA_GROUP + g + 1) * HEAD_DIM]
             for g in range(GQA_GROUP)], axis=0)
        nt = (((1,), (1,)), ((), ()))
        sp = jnp.where(prev_ok, lax.dot_general(q4, kp_ref[0, :, hs], nt, preferred_element_type=F32), neg)
        sc = lax.dot_general(q4, kc_ref[0, :, hs], nt, preferred_element_type=F32)
        sn = jnp.where(next_ok, lax.dot_general(q4, kn_ref[0, :, hs], nt, preferred_element_type=F32), neg)
        sx = lax.dot_general(q4, kx_ref[0, :, hs], nt, preferred_element_type=F32)
        sink = jnp.zeros((rows, 1), F32)
        for g in range(GQA_GROUP):
            sink = jnp.where(grp == g, sink_ref[h * GQA_GROUP + g], sink)
        m = jnp.maximum(jnp.maximum(jnp.max(sp, -1, keepdims=True), jnp.max(sc, -1, keepdims=True)),
                        jnp.maximum(jnp.max(sn, -1, keepdims=True), jnp.max(sx, -1, keepdims=True)))
        m = jnp.maximum(m, sink)
        pp = jnp.exp(sp - m)
        pc = jnp.exp(sc - m)
        pn = jnp.exp(sn - m)
        px = jnp.exp(sx - m)
        denom = (jnp.sum(pp, -1, keepdims=True) + jnp.sum(pc, -1, keepdims=True)
                 + jnp.sum(pn, -1, keepdims=True) + jnp.sum(px, -1, keepdims=True) + jnp.exp(sink - m))
        acc = jnp.dot(pp.astype(BF16), vp_ref[0, :, hs], preferred_element_type=F32)
        acc += jnp.dot(pc.astype(BF16), vc_ref[0, :, hs], preferred_element_type=F32)
        acc += jnp.dot(pn.astype(BF16), vn_ref[0, :, hs], preferred_element_type=F32)
        acc += jnp.dot(px.astype(BF16), vx_ref[0, :, hs], preferred_element_type=F32)
        o = acc / denom
        for g in range(GQA_GROUP):
            c0 = (h * GQA_GROUP + g) * HEAD_DIM
            o_ref[0, :, c0:c0 + HEAD_DIM] = o[g * BLOCK:(g + 1) * BLOCK]


def _attention(sink, q, k, v, L, C):
    B = q.shape[0]
    nb = L // BLOCK
    off = C // BLOCK
    cur = lambda b, n: (b, n + off, 0)
    prv = lambda b, n: (b, jnp.maximum(n - 1, 0) + off, 0)
    nxt = lambda b, n: (b, jnp.minimum(n + 1, nb - 1) + off, 0)
    kv = lambda im: pl.BlockSpec((1, BLOCK, KV_W), im)
    cx = pl.BlockSpec((1, C, KV_W), lambda b, n: (b, 0, 0))
    return pl.pallas_call(
        _attn_kernel,
        grid=(B, nb),
        in_specs=[
            pl.BlockSpec(memory_space=pltpu.SMEM),
            pl.BlockSpec((1, BLOCK, ATTN_W), cur),
            kv(prv), kv(cur), kv(nxt), kv(prv), kv(cur), kv(nxt), cx, cx,
        ],
        out_specs=pl.BlockSpec((1, BLOCK, ATTN_W), lambda b, n: (b, n, 0)),
        out_shape=jax.ShapeDtypeStruct((B, L, ATTN_W), F32),
        compiler_params=_params(("arbitrary", "arbitrary")),
        name="attn",
    )(sink, q, k, k, k, v, v, v, k, v)


def _lru_coeffs(x_ref, p_ref, n_ref, tile, d, cw_ref, cb_ref, w_ref, bias_ref, lam_ref,
                ext_scr, a_scr, b_scr, n_lat_tiles):
    tm = ROW_TILE
    prev_ok = tile >= 2
    next_ok = (tile >= 1) & (tile < n_lat_tiles)
    ext_scr[0:SUBLANES] = jnp.where(prev_ok, p_ref[0], 0.0)
    ext_scr[SUBLANES:SUBLANES + tm] = x_ref[0]
    ext_scr[SUBLANES + tm:2 * SUBLANES + tm] = jnp.where(next_ok, n_ref[0], 0.0)
    xc = cb_ref[...] + cw_ref[2:3] * x_ref[0]
    xc += cw_ref[0:1] * ext_scr[SUBLANES - 2:SUBLANES - 2 + tm]
    xc += cw_ref[1:2] * ext_scr[SUBLANES - 1:SUBLANES - 1 + tm]
    xc += cw_ref[3:4] * ext_scr[SUBLANES + 1:SUBLANES + 1 + tm]
    xcb = xc.astype(BF16)
    lam = lam_ref[d:d + 1]
    sp = jnp.maximum(-lam, 0.0) + jnp.log1p(jnp.exp(-jnp.abs(lam)))
    for n in range(LRU_BLOCKS):
        cs = slice(n * LRU_BLOCK_W, (n + 1) * LRU_BLOCK_W)
        z = jnp.dot(xcb[:, cs], w_ref[d, n], preferred_element_type=F32)
        r = jax.nn.sigmoid(z[:, :LRU_BLOCK_W] + bias_ref[d, 0:1, cs])
        i = jax.nn.sigmoid(z[:, LRU_BLOCK_W:] + bias_ref[d, 1:2, cs])
        log_a = -LRU_C * r * sp[:, cs]
        a = jnp.exp(log_a)
        a_scr[:, cs] = a
        b_scr[:, cs] = jnp.sqrt(1.0 - a * a) * (i * xc[:, cs])


def _lru_scan(a_scr, b_scr, h_ref, state_ref, reset, reverse):
    groups = ROW_TILE // SUBLANES
    row = lax.broadcasted_iota(jnp.int32, (SUBLANES, LRU_W), 0)
    carry0 = jnp.where(reset, 0.0, state_ref[...])

    def body(g, carry):
        gi = (groups - 1 - g) if reverse else g
        r0 = pl.multiple_of(gi * SUBLANES, SUBLANES)
        A = a_scr[pl.ds(r0, SUBLANES), :]
        Bv = b_scr[pl.ds(r0, SUBLANES), :]
        for s in (1, 2, 4):
            if reverse:
                sh, m = SUBLANES - s, row < SUBLANES - s
            else:
                sh, m = s, row >= s
            A_sh = pltpu.roll(A, sh, 0)
            B_sh = pltpu.roll(Bv, sh, 0)
            Bv = jnp.where(m, A * B_sh + Bv, Bv)
            A = jnp.where(m, A * A_sh, A)
        h = Bv + A * carry
        h_ref[0, pl.ds(r0, SUBLANES), :] = h
        last = h[0:1] if reverse else h[SUBLANES - 1:SUBLANES]
        return jnp.broadcast_to(last, (SUBLANES, LRU_W))

    state_ref[...] = lax.fori_loop(0, groups, body, carry0)


def _lru_kernel(xf_ref, xfp_ref, xfn_ref, xb_ref, xbp_ref, xbn_ref, cw_ref, cb_ref, w_ref, bias_ref, lam_ref,
                hf_ref, hb_ref, sf_ref, sb_ref, ext_scr, a_scr, b_scr, *, n_lat_tiles):
    t = pl.program_id(1)
    reset = t == 0
    bt = jnp.where(t == 0, 0, n_lat_tiles + 1 - t)
    _lru_coeffs(xf_ref, xfp_ref, xfn_ref, t, 0, cw_ref, cb_ref, w_ref, bias_ref, lam_ref,
                ext_scr, a_scr, b_scr, n_lat_tiles)
    _lru_scan(a_scr, b_scr, hf_ref, sf_ref, reset, False)
    _lru_coeffs(xb_ref, xbp_ref, xbn_ref, bt, 1, cw_ref, cb_ref, w_ref, bias_ref, lam_ref,
                ext_scr, a_scr, b_scr, n_lat_tiles)
    _lru_scan(a_scr, b_scr, hb_ref, sb_ref, reset, True)


def _lru(xr, conv_w, conv_b, w_gates, bias, lam, L):
    B, rows, _ = xr.shape
    nl = L // ROW_TILE
    nt = nl + 1
    per = ROW_TILE // SUBLANES
    n8 = rows // SUBLANES
    ft = lambda b, t: t
    btile = lambda b, t: jnp.where(t == 0, 0, nl + 1 - t)
    main = lambda f: pl.BlockSpec((1, ROW_TILE, LRU_W), lambda b, t: (b, f(b, t), 0))
    prev = lambda f: pl.BlockSpec((1, SUBLANES, LRU_W), lambda b, t: (b, jnp.maximum(f(b, t) * per - 1, 0), 0))
    nxt = lambda f: pl.BlockSpec((1, SUBLANES, LRU_W),
                                 lambda b, t: (b, jnp.minimum((f(b, t) + 1) * per, n8 - 1), 0))
    full = lambda a: pl.BlockSpec(a.shape, lambda b, t: (0,) * a.ndim)
    return pl.pallas_call(
        functools.partial(_lru_kernel, n_lat_tiles=nl),
        grid=(B, nt),
        in_specs=[main(ft), prev(ft), nxt(ft), main(btile), prev(btile), nxt(btile),
                  full(conv_w), full(conv_b), full(w_gates), full(bias), full(lam)],
        out_specs=[
            pl.BlockSpec((1, ROW_TILE, LRU_W), lambda b, t: (b, jnp.maximum(t - 1, 0), 0)),
            pl.BlockSpec((1, ROW_TILE, LRU_W), lambda b, t: (b, nl - jnp.maximum(t, 1), 0)),
        ],
        out_shape=[jax.ShapeDtypeStruct((B, L, LRU_W), F32)] * 2,
        scratch_shapes=[
            pltpu.VMEM((SUBLANES, LRU_W), F32), pltpu.VMEM((SUBLANES, LRU_W), F32),
            pltpu.VMEM((ROW_TILE + 2 * SUBLANES, LRU_W), F32),
            pltpu.VMEM((ROW_TILE, LRU_W), F32), pltpu.VMEM((ROW_TILE, LRU_W), F32),
        ],
        compiler_params=_params(("arbitrary", "arbitrary")),
        name="lru",
    )(xr, xr, xr, xr, xr, xr, conv_w, conv_b, w_gates, bias, lam)


def _route(scores, sel):
    tm = scores.shape[1]
    neg = -jnp.inf
    iota_g = lax.broadcasted_iota(jnp.int32, (GROUP_SIZE, tm), 0)
    grp_score = []
    for g in range(N_GROUPS):
        sg = sel[g * GROUP_SIZE:(g + 1) * GROUP_SIZE]
        m1 = jnp.max(sg, axis=0, keepdims=True)
        first = jnp.min(jnp.where(sg == m1, iota_g, GROUP_SIZE), axis=0, keepdims=True)
        m2 = jnp.max(jnp.where(iota_g == first, neg, sg), axis=0, keepdims=True)
        grp_score.append(m1 + m2)
    masked = []
    for g in range(N_GROUPS):
        rank = jnp.zeros((1, tm), jnp.int32)
        for o in range(N_GROUPS):
            if o == g:
                continue
            ahead = (grp_score[o] > grp_score[g]) if o > g else (grp_score[o] >= grp_score[g])
            rank += ahead.astype(jnp.int32)
        keep = rank < TOPK_GROUPS
        masked.append(jnp.where(keep, sel[g * GROUP_SIZE:(g + 1) * GROUP_SIZE], neg))
    cand = jnp.concatenate(masked, axis=0)
    iota_e = lax.broadcasted_iota(jnp.int32, (N_EXPERTS, tm), 0)
    chosen = jnp.zeros((N_EXPERTS, tm), jnp.bool_)
    picks = []
    for _ in range(TOP_K):
        cur = jnp.where(chosen, neg, cand)
        m = jnp.max(cur, axis=0, keepdims=True)
        idx = jnp.min(jnp.where((cur == m) & jnp.logical_not(chosen), iota_e, N_EXPERTS), axis=0, keepdims=True)
        chosen = chosen | (iota_e == idx)
        picks.append(idx)
    w = jnp.where(chosen, scores, 0.0)
    gates = w / jnp.sum(w, axis=0, keepdims=True) * ROUTED_SCALE
    return gates, chosen, picks


def _tok_scratch(rows):
    return pltpu.VMEM((rows, TOK_PITCH, LANES), F32)


def _to_token_major(y, scr):
    rows = scr.shape[0]
    flat = scr.reshape(rows * TOK_PITCH, LANES)
    for s in range(TOK_ROWS):
        flat[pl.ds(s, rows, stride=TOK_PITCH), :] = y[:, s * LANES:(s + 1) * LANES]


def _from_token_major(scr):
    rows = scr.shape[0]
    flat = scr.reshape(rows * TOK_PITCH, LANES)
    return [flat[pl.ds(s, rows, stride=TOK_PITCH), :] for s in range(TOK_ROWS)]


def _load_tokens(scr):
    return scr[:, 0:TOK_ROWS, :].reshape(scr.shape[0] * TOK_ROWS, LANES)


def _store_tokens(scr, tiles):
    scr[:, 0:TOK_ROWS, :] = tiles.reshape(scr.shape[0], TOK_ROWS, LANES)


def _merge_kernel(attn_ref, hf_ref, hb_ref, yg_ref, x_ref, mod_ref, wout_ref, ga_ref, gl_ref, lg_ref, lb_ref,
                  rwh_ref, rwl_ref, rb_ref, tri_ref, xn_ref, u2_ref, up_ref, eidx_ref, wk_ref, pos_ref, cnt_ref,
                  carry_ref, tok_scr):
    b = pl.program_id(0)

    @pl.when((b == 0) & (pl.program_id(1) == 0))
    def _():
        carry_ref[...] = jnp.zeros_like(carry_ref)

    D = D_MODEL
    mrow = lambda i: mod_ref[pl.ds(b, 1), i * D:(i + 1) * D]
    lru_y = (hf_ref[0] + hb_ref[0]) * jax.nn.gelu(yg_ref[0])
    na = _rms_norm(attn_ref[0], ga_ref[...]).astype(BF16)
    nl = _rms_norm(lru_y, gl_ref[...]).astype(BF16)
    y1 = jnp.dot(na, wout_ref[0:ATTN_W], preferred_element_type=F32)
    y1 += jnp.dot(nl, wout_ref[ATTN_W:D], preferred_element_type=F32)
    xn = _layer_norm(DEEPNORM_ALPHA * x_ref[0] + mrow(2) * y1) * lg_ref[...] + lb_ref[...]
    xn_ref[0] = xn
    u2 = _layer_norm(xn) * (1.0 + mrow(4)) + mrow(3)
    u_hi, u_lo = _split_bf16(u2)
    u2_ref[0] = u_hi
    nt = (((1,), (1,)), ((), ()))
    logits = lax.dot_general(rwh_ref[...], u_hi, nt, preferred_element_type=F32)
    logits += lax.dot_general(rwh_ref[...], u_lo, nt, preferred_element_type=F32)
    logits += lax.dot_general(rwl_ref[...], u_hi, nt, preferred_element_type=F32)
    _to_token_major(u2, tok_scr)
    up_ref[...] = _load_tokens(tok_scr).astype(BF16)
    scores = jax.nn.sigmoid(logits)
    gates, chosen, picks = _route(scores, scores + rb_ref[...])
    sel01 = jnp.where(chosen, 1.0, 0.0)
    incl = jnp.dot(sel01.astype(BF16), tri_ref[...], preferred_element_type=F32)
    rank = carry_ref[:, 0:1] + incl - sel01
    carry_ref[...] = carry_ref[...] + incl[:, ROW_TILE - 1:ROW_TILE]
    cnt_ref[...] = carry_ref[...]
    iota_e = lax.broadcasted_iota(jnp.int32, (N_EXPERTS, ROW_TILE), 0)
    for kk, idx in enumerate(picks):
        hit = iota_e == idx
        eidx_ref[kk:kk + 1, :] = idx
        wk_ref[kk:kk + 1, :] = jnp.sum(jnp.where(hit, gates, 0.0), axis=0, keepdims=True)
        pos_ref[kk:kk + 1, :] = jnp.sum(jnp.where(hit, rank, 0.0), axis=0, keepdims=True).astype(jnp.int32)


def _merge(attn, hf, hb, yg, x, mod, w_out_bf, ga, gl, lg, lb, rw_hi, rw_lo, rbias, tri, C):
    B, L, _ = x.shape
    T = B * L
    nt = L // ROW_TILE
    off = C // ROW_TILE
    row = lambda w, o=0: pl.BlockSpec((1, ROW_TILE, w), lambda b, t: (b, t + o, 0))
    full = lambda a: pl.BlockSpec(a.shape, lambda b, t: (0,) * a.ndim)
    tok = pl.BlockSpec((TOP_K, ROW_TILE), lambda b, t: (0, b * nt + t))
    return pl.pallas_call(
        _merge_kernel,
        grid=(B, nt),
        in_specs=[row(ATTN_W), row(LRU_W), row(LRU_W), row(LRU_W, off), row(D_MODEL),
                  full(mod), full(w_out_bf), full(ga), full(gl), full(lg), full(lb),
                  full(rw_hi), full(rw_lo), full(rbias), full(tri)],
        out_specs=[row(D_MODEL), row(D_MODEL),
                   pl.BlockSpec((ROW_TILE * TOK_ROWS, LANES), lambda b, t: (b * nt + t, 0)),
                   tok, tok, tok,
                   pl.BlockSpec((N_EXPERTS, LANES), lambda b, t: (0, 0))],
        out_shape=[jax.ShapeDtypeStruct((B, L, D_MODEL), F32),
                   jax.ShapeDtypeStruct((B, L, D_MODEL), BF16),
                   jax.ShapeDtypeStruct((T * TOK_ROWS, LANES), BF16),
                   jax.ShapeDtypeStruct((TOP_K, T), jnp.int32),
                   jax.ShapeDtypeStruct((TOP_K, T), F32),
                   jax.ShapeDtypeStruct((TOP_K, T), jnp.int32),
                   jax.ShapeDtypeStruct((N_EXPERTS, LANES), F32)],
        scratch_shapes=[pltpu.VMEM((N_EXPERTS, LANES), F32), _tok_scratch(ROW_TILE)],
        compiler_params=_params(("arbitrary", "arbitrary")),
        name="merge",
    )(attn, hf, hb, yg, x, mod, w_out_bf, ga, gl, lg, lb, rw_hi, rw_lo, rbias, tri)


def _swiglu(u, wg, wu):
    hg = jnp.dot(u, wg, preferred_element_type=F32)
    hu = jnp.dot(u, wu, preferred_element_type=F32)
    return hg * jax.nn.sigmoid(hg) * hu


def _token_rows(ref, row0, n):
    return ref.at[pl.ds(pl.multiple_of(row0, TOK_ROWS), n * TOK_ROWS), :]


def _dispatch_kernel(zt_ref, dest_ref, up_ref, xs_ref, zero_scr, sem, zsem):
    @pl.when(pl.program_id(0) == 0)
    def _():
        zero_scr[...] = jnp.zeros_like(zero_scr)
        n_tiles = zt_ref.shape[0]

        def zero_tile(z):
            return pltpu.make_async_copy(zero_scr, _token_rows(xs_ref, z * (EXP_TR * TOK_ROWS), EXP_TR), zsem)

        def start(z, carry):
            @pl.when(zt_ref[z] != 0)
            def _():
                zero_tile(z).start()
            return carry

        def wait(z, carry):
            @pl.when(zt_ref[z] != 0)
            def _():
                zero_tile(z).wait()
            return carry

        lax.fori_loop(0, n_tiles, start, 0)
        lax.fori_loop(0, n_tiles, wait, 0)

    def issue(j, carry):
        src = _token_rows(up_ref, j * TOK_ROWS, 1)
        for k in range(TOP_K):
            pltpu.make_async_copy(src, _token_rows(xs_ref, dest_ref[k, j], 1), sem).start(priority=k % 2)
        return carry

    lax.fori_loop(0, ROW_TILE, issue, 0)
    for k in range(TOP_K):
        pltpu.make_async_copy(up_ref, _token_rows(xs_ref, 0, ROW_TILE), sem).wait()


def _dispatch(zero_tiles, dest_rows, up, n_sorted_rows):
    T = dest_rows.shape[1]
    return pl.pallas_call(
        _dispatch_kernel,
        grid_spec=pltpu.PrefetchScalarGridSpec(
            num_scalar_prefetch=1,
            grid=(T // ROW_TILE,),
            in_specs=[pl.BlockSpec((TOP_K, ROW_TILE), lambda i, zt: (0, i), memory_space=pltpu.SMEM),
                      pl.BlockSpec((ROW_TILE * TOK_ROWS, LANES), lambda i, zt: (i, 0))],
            out_specs=pl.BlockSpec(memory_space=pl.ANY),
            scratch_shapes=[pltpu.VMEM((EXP_TR * TOK_ROWS, LANES), BF16),
                            pltpu.SemaphoreType.DMA(()), pltpu.SemaphoreType.DMA(())],
        ),
        out_shape=jax.ShapeDtypeStruct((n_sorted_rows * TOK_ROWS, LANES), BF16),
        compiler_params=_params(("arbitrary",)),
        name="dispatch",
    )(zero_tiles, dest_rows, up)


def _expert_kernel(te_ref, nu_ref, xs_ref, wg_ref, wu_ref, wd_ref, ys_ref, wgb_ref, wub_ref, wdb_ref, tok_scr):
    i = pl.program_id(0)

    @pl.when(i < nu_ref[0])
    def _():
        new_expert = (i == 0) | (te_ref[i] != te_ref[jnp.maximum(i - 1, 0)])

        @pl.when(new_expert)
        def _():
            wgb_ref[...] = wg_ref[0].astype(BF16)
            wub_ref[...] = wu_ref[0].astype(BF16)
            wdb_ref[...] = wd_ref[0].astype(BF16)

        _store_tokens(tok_scr, xs_ref[...].astype(F32))
        x = jnp.concatenate([c.astype(BF16) for c in _from_token_major(tok_scr)], axis=1)
        h = _swiglu(x, wgb_ref[...], wub_ref[...])
        y = jnp.dot(h.astype(BF16), wdb_ref[...], preferred_element_type=F32)
        _to_token_major(y, tok_scr)
        ys_ref[...] = _load_tokens(tok_scr).astype(BF16)

    @pl.when(i >= nu_ref[0])
    def _():
        ys_ref[...] = jnp.zeros_like(ys_ref)


def _experts(tile_expert, n_used, xs, wg, wu, wd):
    n_tiles = tile_expert.shape[0]
    blk = (EXP_TR * TOK_ROWS, LANES)
    wspec = lambda a: pl.BlockSpec((1,) + a.shape[1:], lambda i, te, nu: (te[i], 0, 0))
    return pl.pallas_call(
        _expert_kernel,
        grid_spec=pltpu.PrefetchScalarGridSpec(
            num_scalar_prefetch=2,
            grid=(n_tiles,),
            in_specs=[pl.BlockSpec(blk, lambda i, te, nu: (jnp.minimum(i, nu[0] - 1), 0)),
                      wspec(wg), wspec(wu), wspec(wd)],
            out_specs=pl.BlockSpec(blk, lambda i, te, nu: (i, 0)),
            scratch_shapes=[pltpu.VMEM(wg.shape[1:], BF16), pltpu.VMEM(wu.shape[1:], BF16),
                            pltpu.VMEM(wd.shape[1:], BF16), _tok_scratch(EXP_TR)],
        ),
        out_shape=jax.ShapeDtypeStruct(xs.shape, BF16),
        compiler_params=_params(("arbitrary",)),
        name="experts",
    )(tile_expert, n_used, xs, wg, wu, wd)


def _combine_kernel(dcur_ref, dnext_ref, wk_ref, u_ref, xn_ref, g2_ref, sg_ref, su_ref, sd_ref, lg_ref, lb_ref,
                    ys_ref, o_ref, buf_ref, y_scr, sem):
    i = pl.program_id(0)
    n = pl.num_programs(0)
    tm = COMB_TM
    slot = i % 2

    def gather(d_ref, sl):
        def issue(j, carry):
            for k in range(TOP_K):
                pltpu.make_async_copy(_token_rows(ys_ref, d_ref[k, j], 1),
                                      _token_rows(buf_ref.at[sl], (k * tm + j) * TOK_ROWS, 1),
                                      sem.at[sl]).start(priority=k % 2)
            return carry
        lax.fori_loop(0, tm, issue, 0)

    @pl.when(i == 0)
    def _():
        gather(dcur_ref, 0)

    @pl.when(i + 1 < n)
    def _():
        gather(dnext_ref, 1 - slot)

    pltpu.make_async_copy(_token_rows(ys_ref, 0, TOP_K * tm), buf_ref.at[slot], sem.at[slot]).wait()

    def accumulate(jo, carry):
        for u in range(COMB_UNROLL):
            j = jo * COMB_UNROLL + u
            acc = jnp.zeros((TOK_ROWS, LANES), F32)
            for k in range(TOP_K):
                rows = _token_rows(buf_ref.at[slot], (k * tm + j) * TOK_ROWS, 1)
                acc += wk_ref[k, j] * rows[...].astype(F32)
            y_scr[j, 0:TOK_ROWS, :] = acc
        return carry

    lax.fori_loop(0, tm // COMB_UNROLL, accumulate, 0)
    y_routed = jnp.concatenate(_from_token_major(y_scr), axis=1)

    hs = _swiglu(u_ref[...], sg_ref[...], su_ref[...])
    y2 = jnp.dot(hs.astype(BF16), sd_ref[...], preferred_element_type=F32) + y_routed
    z = DEEPNORM_ALPHA * xn_ref[...] + g2_ref[0] * y2
    o_ref[...] = _layer_norm(z) * lg_ref[...] + lb_ref[...]


def _combine(dest_rows, wk, u2, xn, g2, sg, su, sd, lg, lb, ys, tiles_per_batch):
    T = u2.shape[0]
    tm = COMB_TM
    n = T // tm
    row = pl.BlockSpec((tm, D_MODEL), lambda i: (i, 0))
    full = lambda a: pl.BlockSpec(a.shape, lambda i: (0,) * a.ndim)
    tok = lambda f: pl.BlockSpec((TOP_K, tm), lambda i: (0, f(i)), memory_space=pltpu.SMEM)
    return pl.pallas_call(
        _combine_kernel,
        grid=(n,),
        in_specs=[tok(lambda i: i), tok(lambda i: jnp.minimum(i + 1, n - 1)), tok(lambda i: i), row, row,
                  pl.BlockSpec((1, 1, D_MODEL), lambda i: (i // tiles_per_batch, 0, 0)),
                  full(sg), full(su), full(sd), full(lg), full(lb),
                  pl.BlockSpec(memory_space=pl.ANY)],
        out_specs=row,
        out_shape=jax.ShapeDtypeStruct((T, D_MODEL), F32),
        scratch_shapes=[pltpu.VMEM((2, TOP_K * tm * TOK_ROWS, LANES), BF16),
                        _tok_scratch(tm),
                        pltpu.SemaphoreType.DMA((2,))],
        compiler_params=_params(("arbitrary",)),
        name="combine",
    )(dest_rows, dest_rows, wk, u2, xn, g2, sg, su, sd, lg, lb, ys)


def _rope_tables(L, C):
    rows = L // GRID_W
    row = jnp.repeat(jnp.arange(rows, dtype=F32), GRID_W)
    col = jnp.tile(jnp.arange(GRID_W, dtype=F32), rows)
    inv_freq = jnp.exp(-jnp.log(ROPE_THETA) * jnp.arange(0, ROPE_AXIS_DIM, 2, dtype=F32) / ROPE_AXIS_DIM)
    ang_r = row[:, None] * inv_freq
    ang_c = col[:, None] * inv_freq
    cr, sr, cc, sc = jnp.cos(ang_r), jnp.sin(ang_r), jnp.cos(ang_c), jnp.sin(ang_c)
    cos_t = jnp.concatenate([cr, cr, cc, cc], axis=-1)
    sin_t = jnp.concatenate([-sr, sr, -sc, sc], axis=-1)
    cos_t = jnp.concatenate([jnp.ones((C, HEAD_DIM), F32), cos_t], axis=0)
    sin_t = jnp.concatenate([jnp.zeros((C, HEAD_DIM), F32), sin_t], axis=0)
    return cos_t, sin_t


def kernel(x, c, ctx, c_ctx, w_mod, b_mod, w_in, attn_sink, conv_w, conv_b, lru_wa, lru_ba, lru_wx, lru_bx,
           lru_lam, norm_attn_g, norm_lru_g, w_out, ln1_g, ln1_b, router_w, router_bias, exp_w_gate, exp_w_up,
           exp_w_down, sh_w_gate, sh_w_up, sh_w_down, ln2_g, ln2_b):
    B, L, D = x.shape
    C = ctx.shape[1]
    assert w_mod.shape[0] == DEPTH and D == D_MODEL and B + 1 <= SUBLANES
    row2 = lambda a: a.reshape(1, -1)

    cvec = jnp.concatenate([c, c_ctx[None], jnp.zeros((SUBLANES - B - 1, D), F32)], axis=0)
    mod = _mod(cvec, w_mod[0], row2(b_mod[0]))

    cos_t, sin_t = _rope_tables(L, C)
    q, k, v, xr, yg = _inproj(x, ctx, mod, w_in[0].astype(BF16), cos_t, sin_t)

    attn = _attention(attn_sink[0], q, k, v, L, C)

    w_gates = jnp.concatenate([lru_wa[0], lru_wx[0]], axis=-1).astype(BF16)
    bias = jnp.stack([lru_ba[0], lru_bx[0]], axis=1)
    hf, hb = _lru(xr, conv_w[0], row2(conv_b[0]), w_gates, bias, lru_lam[0], L)

    rw_hi, rw_lo = _split_bf16(router_w[0].T)
    tri = jnp.triu(jnp.ones((ROW_TILE, ROW_TILE), BF16))
    xn, u2, up, eidx, wk, pos, cnt = _merge(
        attn, hf, hb, yg, x, mod, w_out[0].astype(BF16), row2(norm_attn_g[0]), row2(norm_lru_g[0]),
        row2(ln1_g[0]), row2(ln1_b[0]), rw_hi, rw_lo, router_bias[0].reshape(-1, 1), tri, C)

    T = B * L
    n_tiles = T * TOP_K // EXP_TR + N_EXPERTS
    counts = cnt[:, 0].astype(jnp.int32)
    tiles_e = (counts + EXP_TR - 1) // EXP_TR
    tile_end = jnp.cumsum(tiles_e)
    n_used = tile_end[-1:]
    row_start = (tile_end - tiles_e) * EXP_TR
    experts = jnp.arange(N_EXPERTS, dtype=jnp.int32)
    start_of = jnp.sum(jnp.where(eidx[None] == experts[:, None, None], row_start[:, None, None], 0), axis=0)
    dest_rows = (start_of + pos) * TOK_ROWS
    tile_id = jnp.minimum(jnp.arange(n_tiles, dtype=jnp.int32), n_used - 1)
    tile_expert = jnp.sum((tile_end[None, :] <= tile_id[:, None]).astype(jnp.int32), axis=1)

    all_tiles = jnp.arange(n_tiles, dtype=jnp.int32)
    ends_run = jnp.any((tile_end[None, :] - 1 == all_tiles[:, None]) & (tiles_e[None, :] > 0), axis=1)
    zero_tiles = (ends_run | (all_tiles >= n_used)).astype(jnp.int32)
    xs = _dispatch(zero_tiles, dest_rows, up, n_tiles * EXP_TR)
    ys = _experts(tile_expert, n_used, xs, exp_w_gate[0], exp_w_up[0], exp_w_down[0])
    g2 = mod[:B, 5 * D:6 * D].reshape(B, 1, D)
    out = _combine(dest_rows, wk, u2.reshape(T, D), xn.reshape(T, D), g2,
                   sh_w_gate[0].astype(BF16), sh_w_up[0].astype(BF16), sh_w_down[0].astype(BF16),
                   row2(ln2_g[0]), row2(ln2_b[0]), ys, L // COMB_TM)
    return out.reshape(B, L, D)
```

```python
import functools

import jax
import jax.numpy as jnp
from jax import lax
from jax.experimental import pallas as pl
from jax.experimental.pallas import tpu as pltpu

F32 = jnp.float32
BF16 = jnp.bfloat16

D_MODEL = 2048
GRID_W = 64
N_HEADS = 8
N_KV_HEADS = 2
HEAD_DIM = 128
GQA_GROUP = N_HEADS // N_KV_HEADS
ATTN_W = N_HEADS * HEAD_DIM
KV_W = N_KV_HEADS * HEAD_DIM
ATTN_SCALE = HEAD_DIM ** -0.5
BLOCK = 128
ROPE_THETA = 10000.0
ROPE_AXIS_DIM = HEAD_DIM // 2
LRU_W = D_MODEL - ATTN_W
LRU_BLOCKS = 8
LRU_BLOCK_W = LRU_W // LRU_BLOCKS
LRU_C = 8.0
N_EXPERTS = 64
TOP_K = 8
N_GROUPS = 8
GROUP_SIZE = N_EXPERTS // N_GROUPS
TOPK_GROUPS = 4
EXPERT_FF = 512
ROUTED_SCALE = 2.5
LN_EPS = 1e-6
DEPTH = 1
DEEPNORM_ALPHA = (2 * DEPTH) ** 0.25

SUBLANES = 8
LANES = 128
VMEM_LIMIT = 56 * 1024 * 1024

ROW_TILE = 256
MOD_TN = 1024
TOK_ROWS = D_MODEL // LANES
TOK_PITCH = TOK_ROWS + SUBLANES
EXP_TR = 256
COMB_TM = 256
COMB_UNROLL = 4


def _params(sem, vmem=VMEM_LIMIT):
    return pltpu.CompilerParams(dimension_semantics=sem, vmem_limit_bytes=vmem)


def _layer_norm(x):
    mu = jnp.mean(x, axis=-1, keepdims=True)
    xc = x - mu
    var = jnp.mean(xc * xc, axis=-1, keepdims=True)
    return xc * lax.rsqrt(var + LN_EPS)


def _rms_norm(x, g):
    return x * lax.rsqrt(jnp.mean(x * x, axis=-1, keepdims=True) + LN_EPS) * g


def _split_bf16(x):
    hi = x.astype(BF16)
    lo = (x - hi.astype(F32)).astype(BF16)
    return hi, lo


def _mod_kernel(c_ref, w_ref, b_ref, o_ref):
    cv = c_ref[...]
    s = cv * jax.nn.sigmoid(cv)
    hi, lo = _split_bf16(s)
    lhs = jnp.concatenate([hi, lo], axis=0)
    r = jnp.dot(lhs, w_ref[...].astype(BF16), preferred_element_type=F32)
    o_ref[...] = r[:SUBLANES] + r[SUBLANES:] + b_ref[...]


def _mod(cvec, w_mod, b_mod):
    n = w_mod.shape[1]
    return pl.pallas_call(
        _mod_kernel,
        grid=(n // MOD_TN,),
        in_specs=[
            pl.BlockSpec((SUBLANES, D_MODEL), lambda j: (0, 0)),
            pl.BlockSpec((D_MODEL, MOD_TN), lambda j: (0, j)),
            pl.BlockSpec((1, MOD_TN), lambda j: (0, j)),
        ],
        out_specs=pl.BlockSpec((SUBLANES, MOD_TN), lambda j: (0, j)),
        out_shape=jax.ShapeDtypeStruct((SUBLANES, n), F32),
        compiler_params=_params(("arbitrary",)),
        name="mod",
    )(cvec, w_mod, b_mod)


def _rope(xh, cos, sin, even_block):
    partner = jnp.where(even_block, pltpu.roll(xh, 96, 1), pltpu.roll(xh, 32, 1))
    return xh * cos + partner * sin


def _inproj_kernel(x_ref, ctx_ref, mod_ref, w_ref, cos_ref, sin_ref,
                   q_ref, k_ref, v_ref, xr_ref, yg_ref):
    b = pl.program_id(0)
    t = pl.program_id(1)
    is_ctx = t == 0
    xin = jnp.where(is_ctx, ctx_ref[0], x_ref[0])
    r = jnp.where(is_ctx, 2, b)
    shift = mod_ref[pl.ds(r, 1), 0:D_MODEL]
    scale = mod_ref[pl.ds(r, 1), D_MODEL:2 * D_MODEL]
    u = (_layer_norm(xin) * (1.0 + scale) + shift).astype(BF16)

    cos = cos_ref[...]
    sin = sin_ref[...]
    lane = lax.broadcasted_iota(jnp.int32, (ROW_TILE, HEAD_DIM), 1)
    even_block = (lane % 64) < 32

    def proj(c0, c1):
        return jnp.dot(u, w_ref[:, c0:c1], preferred_element_type=F32)

    for h in range(N_HEADS):
        qh = proj(h * HEAD_DIM, (h + 1) * HEAD_DIM)
        q_ref[0, :, h * HEAD_DIM:(h + 1) * HEAD_DIM] = (
            _rope(qh, cos, sin, even_block) * ATTN_SCALE).astype(BF16)
    for h in range(N_KV_HEADS):
        c0 = ATTN_W + h * HEAD_DIM
        kh = proj(c0, c0 + HEAD_DIM)
        k_ref[0, :, h * HEAD_DIM:(h + 1) * HEAD_DIM] = _rope(kh, cos, sin, even_block).astype(BF16)
    v_ref[0] = proj(ATTN_W + KV_W, ATTN_W + 2 * KV_W).astype(BF16)
    c0 = ATTN_W + 2 * KV_W
    xr_ref[0] = proj(c0, c0 + LRU_W)
    yg_ref[0] = proj(c0 + LRU_W, c0 + 2 * LRU_W)


def _inproj(x, ctx, mod, w_in_bf, cos_t, sin_t):
    B, L, _ = x.shape
    C = ctx.shape[1]
    assert C == ROW_TILE and L % ROW_TILE == 0
    nt = L // ROW_TILE + 1
    rows = L + C
    d_in = w_in_bf.shape[1]
    out = lambda w, dt: jax.ShapeDtypeStruct((B, rows, w), dt)
    ospec = lambda w: pl.BlockSpec((1, ROW_TILE, w), lambda b, t: (b, t, 0))
    return pl.pallas_call(
        _inproj_kernel,
        grid=(B, nt),
        in_specs=[
            pl.BlockSpec((1, ROW_TILE, D_MODEL), lambda b, t: (b, jnp.maximum(t - 1, 0), 0)),
            pl.BlockSpec((1, ROW_TILE, D_MODEL), lambda b, t: (b, 0, 0)),
            pl.BlockSpec((SUBLANES, 2 * D_MODEL), lambda b, t: (0, 0)),
            pl.BlockSpec((D_MODEL, d_in), lambda b, t: (0, 0)),
            pl.BlockSpec((ROW_TILE, HEAD_DIM), lambda b, t: (t, 0)),
            pl.BlockSpec((ROW_TILE, HEAD_DIM), lambda b, t: (t, 0)),
        ],
        out_specs=[ospec(ATTN_W), ospec(KV_W), ospec(KV_W), ospec(LRU_W), ospec(LRU_W)],
        out_shape=[out(ATTN_W, BF16), out(KV_W, BF16), out(KV_W, BF16), out(LRU_W, F32), out(LRU_W, F32)],
        compiler_params=_params(("arbitrary", "arbitrary")),
        name="inproj",
    )(x, ctx, mod, w_in_bf, cos_t, sin_t)


def _attn_kernel(sink_ref, q_ref, kp_ref, kc_ref, kn_ref, vp_ref, vc_ref, vn_ref, kx_ref, vx_ref, o_ref):
    n = pl.program_id(1)
    nb = pl.num_programs(1)
    rows = GQA_GROUP * BLOCK
    n_ctx = kx_ref.shape[1]
    qi = lax.broadcasted_iota(jnp.int32, (rows, BLOCK), 0) % BLOCK
    kj = lax.broadcasted_iota(jnp.int32, (rows, BLOCK), 1)
    prev_ok = (kj >= qi) & (n > 0)
    next_ok = (kj <= qi) & (n < nb - 1)
    grp = lax.broadcasted_iota(jnp.int32, (rows, 1), 0) // BLOCK
    neg = -jnp.inf
    for h in range(N_KV_HEADS):
        hs = slice(h * HEAD_DIM, (h + 1) * HEAD_DIM)
        q4 = jnp.concatenate(
            [q_ref[0, :, (h * GQA_GROUP + g) * HEAD_DIM:(h * GQA_GROUP + g + 1) * HEAD_DIM]
             for g in range(GQA_GROUP)], axis=0)
        nt = (((1,), (1,)), ((), ()))
        sp = jnp.where(prev_ok, lax.dot_general(q4, kp_ref[0, :, hs], nt, preferred_element_type=F32), neg)
        sc = lax.dot_general(q4, kc_ref[0, :, hs], nt, preferred_element_type=F32)
        sn = jnp.where(next_ok, lax.dot_general(q4, kn_ref[0, :, hs], nt, preferred_element_type=F32), neg)
        sx = lax.dot_general(q4, kx_ref[0, :, hs], nt, preferred_element_type=F32)
        sink = jnp.zeros((rows, 1), F32)
        for g in range(GQA_GROUP):
            sink = jnp.where(grp == g, sink_ref[h * GQA_GROUP + g], sink)
        m = jnp.maximum(jnp.maximum(jnp.max(sp, -1, keepdims=True), jnp.max(sc, -1, keepdims=True)),
                        jnp.maximum(jnp.max(sn, -1, keepdims=True), jnp.max(sx, -1, keepdims=True)))
        m = jnp.maximum(m, sink)
        pp = jnp.exp(sp - m)
        pc = jnp.exp(sc - m)
        pn = jnp.exp(sn - m)
        px = jnp.exp(sx - m)
        denom = (jnp.sum(pp, -1, keepdims=True) + jnp.sum(pc, -1, keepdims=True)
                 + jnp.sum(pn, -1, keepdims=True) + jnp.sum(px, -1, keepdims=True) + jnp.exp(sink - m))
        acc = jnp.dot(pp.astype(BF16), vp_ref[0, :, hs], preferred_element_type=F32)
        acc += jnp.dot(pc.astype(BF16), vc_ref[0, :, hs], preferred_element_type=F32)
        acc += jnp.dot(pn.astype(BF16), vn_ref[0, :, hs], preferred_element_type=F32)
        acc += jnp.dot(px.astype(BF16), vx_ref[0, :, hs], preferred_element_type=F32)
        o = acc / denom
        for g in range(GQA_GROUP):
            c0 = (h * GQA_GROUP + g) * HEAD_DIM
            o_ref[0, :, c0:c0 + HEAD_DIM] = o[g * BLOCK:(g + 1) * BLOCK]


def _attention(sink, q, k, v, L, C):
    B = q.shape[0]
    nb = L // BLOCK
    off = C // BLOCK
    cur = lambda b, n: (b, n + off, 0)
    prv = lambda b, n: (b, jnp.maximum(n - 1, 0) + off, 0)
    nxt = lambda b, n: (b, jnp.minimum(n + 1, nb - 1) + off, 0)
    kv = lambda im: pl.BlockSpec((1, BLOCK, KV_W), im)
    cx = pl.BlockSpec((1, C, KV_W), lambda b, n: (b, 0, 0))
    return pl.pallas_call(
        _attn_kernel,
        grid=(B, nb),
        in_specs=[
            pl.BlockSpec(memory_space=pltpu.SMEM),
            pl.BlockSpec((1, BLOCK, ATTN_W), cur),
            kv(prv), kv(cur), kv(nxt), kv(prv), kv(cur), kv(nxt), cx, cx,
        ],
        out_specs=pl.BlockSpec((1, BLOCK, ATTN_W), lambda b, n: (b, n, 0)),
        out_shape=jax.ShapeDtypeStruct((B, L, ATTN_W), F32),
        compiler_params=_params(("arbitrary", "arbitrary")),
        name="attn",
    )(sink, q, k, k, k, v, v, v, k, v)


def _lru_coeffs(x_ref, p_ref, n_ref, tile, d, cw_ref, cb_ref, w_ref, bias_ref, lam_ref,
                ext_scr, a_scr, b_scr, n_lat_tiles):
    tm = ROW_TILE
    prev_ok = tile >= 2
    next_ok = (tile >= 1) & (tile < n_lat_tiles)
    ext_scr[0:SUBLANES] = jnp.where(prev_ok, p_ref[0], 0.0)
    ext_scr[SUBLANES:SUBLANES + tm] = x_ref[0]
    ext_scr[SUBLANES + tm:2 * SUBLANES + tm] = jnp.where(next_ok, n_ref[0], 0.0)
    xc = cb_ref[...] + cw_ref[2:3] * x_ref[0]
    xc += cw_ref[0:1] * ext_scr[SUBLANES - 2:SUBLANES - 2 + tm]
    xc += cw_ref[1:2] * ext_scr[SUBLANES - 1:SUBLANES - 1 + tm]
    xc += cw_ref[3:4] * ext_scr[SUBLANES + 1:SUBLANES + 1 + tm]
    xcb = xc.astype(BF16)
    lam = lam_ref[d:d + 1]
    sp = jnp.maximum(-lam, 0.0) + jnp.log1p(jnp.exp(-jnp.abs(lam)))
    for n in range(LRU_BLOCKS):
        cs = slice(n * LRU_BLOCK_W, (n + 1) * LRU_BLOCK_W)
        z = jnp.dot(xcb[:, cs], w_ref[d, n], preferred_element_type=F32)
        r = jax.nn.sigmoid(z[:, :LRU_BLOCK_W] + bias_ref[d, 0:1, cs])
        i = jax.nn.sigmoid(z[:, LRU_BLOCK_W:] + bias_ref[d, 1:2, cs])
        log_a = -LRU_C * r * sp[:, cs]
        a = jnp.exp(log_a)
        a_scr[:, cs] = a
        b_scr[:, cs] = jnp.sqrt(1.0 - a * a) * (i * xc[:, cs])


def _lru_scan(a_scr, b_scr, h_ref, state_ref, reset, reverse):
    groups = ROW_TILE // SUBLANES
    row = lax.broadcasted_iota(jnp.int32, (SUBLANES, LRU_W), 0)
    carry0 = jnp.where(reset, 0.0, state_ref[...])

    def body(g, carry):
        gi = (groups - 1 - g) if reverse else g
        r0 = pl.multiple_of(gi * SUBLANES, SUBLANES)
        A = a_scr[pl.ds(r0, SUBLANES), :]
        Bv = b_scr[pl.ds(r0, SUBLANES), :]
        for s in (1, 2, 4):
            if reverse:
                sh, m = SUBLANES - s, row < SUBLANES - s
            else:
                sh, m = s, row >= s
            A_sh = pltpu.roll(A, sh, 0)
            B_sh = pltpu.roll(Bv, sh, 0)
            Bv = jnp.where(m, A * B_sh + Bv, Bv)
            A = jnp.where(m, A * A_sh, A)
        h = Bv + A * carry
        h_ref[0, pl.ds(r0, SUBLANES), :] = h
        last = h[0:1] if reverse else h[SUBLANES - 1:SUBLANES]
        return jnp.broadcast_to(last, (SUBLANES, LRU_W))

    state_ref[...] = lax.fori_loop(0, groups, body, carry0)


def _lru_kernel(xf_ref, xfp_ref, xfn_ref, xb_ref, xbp_ref, xbn_ref, cw_ref, cb_ref, w_ref, bias_ref, lam_ref,
                hf_ref, hb_ref, sf_ref, sb_ref, ext_scr, a_scr, b_scr, *, n_lat_tiles):
    t = pl.program_id(1)
    reset = t == 0
    bt = jnp.where(t == 0, 0, n_lat_tiles + 1 - t)
    _lru_coeffs(xf_ref, xfp_ref, xfn_ref, t, 0, cw_ref, cb_ref, w_ref, bias_ref, lam_ref,
                ext_scr, a_scr, b_scr, n_lat_tiles)
    _lru_scan(a_scr, b_scr, hf_ref, sf_ref, reset, False)
    _lru_coeffs(xb_ref, xbp_ref, xbn_ref, bt, 1, cw_ref, cb_ref, w_ref, bias_ref, lam_ref,
                ext_scr, a_scr, b_scr, n_lat_tiles)
    _lru_scan(a_scr, b_scr, hb_ref, sb_ref, reset, True)


def _lru(xr, conv_w, conv_b, w_gates, bias, lam, L):
    B, rows, _ = xr.shape
    nl = L // ROW_TILE
    nt = nl + 1
    per = ROW_TILE // SUBLANES
    n8 = rows // SUBLANES
    ft = lambda b, t: t
    btile = lambda b, t: jnp.where(t == 0, 0, nl + 1 - t)
    main = lambda f: pl.BlockSpec((1, ROW_TILE, LRU_W), lambda b, t: (b, f(b, t), 0))
    prev = lambda f: pl.BlockSpec((1, SUBLANES, LRU_W), lambda b, t: (b, jnp.maximum(f(b, t) * per - 1, 0), 0))
    nxt = lambda f: pl.BlockSpec((1, SUBLANES, LRU_W),
                                 lambda b, t: (b, jnp.minimum((f(b, t) + 1) * per, n8 - 1), 0))
    full = lambda a: pl.BlockSpec(a.shape, lambda b, t: (0,) * a.ndim)
    return pl.pallas_call(
        functools.partial(_lru_kernel, n_lat_tiles=nl),
        grid=(B, nt),
        in_specs=[main(ft), prev(ft), nxt(ft), main(btile), prev(btile), nxt(btile),
                  full(conv_w), full(conv_b), full(w_gates), full(bias), full(lam)],
        out_specs=[
            pl.BlockSpec((1, ROW_TILE, LRU_W), lambda b, t: (b, jnp.maximum(t - 1, 0), 0)),
            pl.BlockSpec((1, ROW_TILE, LRU_W), lambda b, t: (b, nl - jnp.maximum(t, 1), 0)),
        ],
        out_shape=[jax.ShapeDtypeStruct((B, L, LRU_W), F32)] * 2,
        scratch_shapes=[
            pltpu.VMEM((SUBLANES, LRU_W), F32), pltpu.VMEM((SUBLANES, LRU_W), F32),
            pltpu.VMEM((ROW_TILE + 2 * SUBLANES, LRU_W), F32),
            pltpu.VMEM((ROW_TILE, LRU_W), F32), pltpu.VMEM((ROW_TILE, LRU_W), F32),
        ],
        compiler_params=_params(("arbitrary", "arbitrary")),
        name="lru",
    )(xr, xr, xr, xr, xr, xr, conv_w, conv_b, w_gates, bias, lam)


def _route(scores, sel):
    tm = scores.shape[1]
    neg = -jnp.inf
    iota_g = lax.broadcasted_iota(jnp.int32, (GROUP_SIZE, tm), 0)
    grp_score = []
    for g in range(N_GROUPS):
        sg = sel[g * GROUP_SIZE:(g + 1) * GROUP_SIZE]
        m1 = jnp.max(sg, axis=0, keepdims=True)
        first = jnp.min(jnp.where(sg == m1, iota_g, GROUP_SIZE), axis=0, keepdims=True)
        m2 = jnp.max(jnp.where(iota_g == first, neg, sg), axis=0, keepdims=True)
        grp_score.append(m1 + m2)
    masked = []
    for g in range(N_GROUPS):
        rank = jnp.zeros((1, tm), jnp.int32)
        for o in range(N_GROUPS):
            if o == g:
                continue
            ahead = (grp_score[o] > grp_score[g]) if o > g else (grp_score[o] >= grp_score[g])
            rank += ahead.astype(jnp.int32)
        keep = rank < TOPK_GROUPS
        masked.append(jnp.where(keep, sel[g * GROUP_SIZE:(g + 1) * GROUP_SIZE], neg))
    cand = jnp.concatenate(masked, axis=0)
    iota_e = lax.broadcasted_iota(jnp.int32, (N_EXPERTS, tm), 0)
    chosen = jnp.zeros((N_EXPERTS, tm), jnp.bool_)
    picks = []
    for _ in range(TOP_K):
        cur = jnp.where(chosen, neg, cand)
        m = jnp.max(cur, axis=0, keepdims=True)
        idx = jnp.min(jnp.where((cur == m) & jnp.logical_not(chosen), iota_e, N_EXPERTS), axis=0, keepdims=True)
        chosen = chosen | (iota_e == idx)
        picks.append(idx)
    w = jnp.where(chosen, scores, 0.0)
    gates = w / jnp.sum(w, axis=0, keepdims=True) * ROUTED_SCALE
    return gates, chosen, picks


def _tok_scratch(rows):
    return pltpu.VMEM((rows, TOK_PITCH, LANES), F32)


def _to_token_major(y, scr):
    rows = scr.shape[0]
    flat = scr.reshape(rows * TOK_PITCH, LANES)
    for s in range(TOK_ROWS):
        flat[pl.ds(s, rows, stride=TOK_PITCH), :] = y[:, s * LANES:(s + 1) * LANES]


def _from_token_major(scr):
    rows = scr.shape[0]
    flat = scr.reshape(rows * TOK_PITCH, LANES)
    return [flat[pl.ds(s, rows, stride=TOK_PITCH), :] for s in range(TOK_ROWS)]


def _token_tiles(scr):
    return scr.at[:, 0:TOK_ROWS, :]


def _merge_kernel(attn_ref, hf_ref, hb_ref, yg_ref, x_ref, mod_ref, wout_ref, ga_ref, gl_ref, lg_ref, lb_ref,
                  rwh_ref, rwl_ref, rb_ref, tri_ref, xn_ref, u2_ref, up_ref, eidx_ref, wk_ref, pos_ref, cnt_ref,
                  carry_ref, tok_scr):
    b = pl.program_id(0)

    @pl.when((b == 0) & (pl.program_id(1) == 0))
    def _():
        carry_ref[...] = jnp.zeros_like(carry_ref)

    D = D_MODEL
    mrow = lambda i: mod_ref[pl.ds(b, 1), i * D:(i + 1) * D]
    lru_y = (hf_ref[0] + hb_ref[0]) * jax.nn.gelu(yg_ref[0])
    na = _rms_norm(attn_ref[0], ga_ref[...]).astype(BF16)
    nl = _rms_norm(lru_y, gl_ref[...]).astype(BF16)
    y1 = jnp.dot(na, wout_ref[0:ATTN_W], preferred_element_type=F32)
    y1 += jnp.dot(nl, wout_ref[ATTN_W:D], preferred_element_type=F32)
    xn = _layer_norm(DEEPNORM_ALPHA * x_ref[0] + mrow(2) * y1) * lg_ref[...] + lb_ref[...]
    xn_ref[0] = xn
    u2 = _layer_norm(xn) * (1.0 + mrow(4)) + mrow(3)
    u_hi, u_lo = _split_bf16(u2)
    u2_ref[0] = u_hi
    nt = (((1,), (1,)), ((), ()))
    logits = lax.dot_general(rwh_ref[...], u_hi, nt, preferred_element_type=F32)
    logits += lax.dot_general(rwh_ref[...], u_lo, nt, preferred_element_type=F32)
    logits += lax.dot_general(rwl_ref[...], u_hi, nt, preferred_element_type=F32)
    _to_token_major(u2, tok_scr)
    up_ref[...] = _token_tiles(tok_scr)[...].astype(BF16)
    scores = jax.nn.sigmoid(logits)
    gates, chosen, picks = _route(scores, scores + rb_ref[...])
    sel01 = jnp.where(chosen, 1.0, 0.0)
    incl = jnp.dot(sel01.astype(BF16), tri_ref[...], preferred_element_type=F32)
    rank = carry_ref[:, 0:1] + incl - sel01
    carry_ref[...] = carry_ref[...] + incl[:, ROW_TILE - 1:ROW_TILE]
    cnt_ref[...] = carry_ref[...]
    iota_e = lax.broadcasted_iota(jnp.int32, (N_EXPERTS, ROW_TILE), 0)
    for kk, idx in enumerate(picks):
        hit = iota_e == idx
        eidx_ref[kk:kk + 1, :] = idx
        wk_ref[kk:kk + 1, :] = jnp.sum(jnp.where(hit, gates, 0.0), axis=0, keepdims=True)
        pos_ref[kk:kk + 1, :] = jnp.sum(jnp.where(hit, rank, 0.0), axis=0, keepdims=True).astype(jnp.int32)


def _merge(attn, hf, hb, yg, x, mod, w_out_bf, ga, gl, lg, lb, rw_hi, rw_lo, rbias, tri, C):
    B, L, _ = x.shape
    T = B * L
    nt = L // ROW_TILE
    off = C // ROW_TILE
    row = lambda w, o=0: pl.BlockSpec((1, ROW_TILE, w), lambda b, t: (b, t + o, 0))
    full = lambda a: pl.BlockSpec(a.shape, lambda b, t: (0,) * a.ndim)
    tok = pl.BlockSpec((TOP_K, ROW_TILE), lambda b, t: (0, b * nt + t))
    return pl.pallas_call(
        _merge_kernel,
        grid=(B, nt),
        in_specs=[row(ATTN_W), row(LRU_W), row(LRU_W), row(LRU_W, off), row(D_MODEL),
                  full(mod), full(w_out_bf), full(ga), full(gl), full(lg), full(lb),
                  full(rw_hi), full(rw_lo), full(rbias), full(tri)],
        out_specs=[row(D_MODEL), row(D_MODEL),
                   pl.BlockSpec((ROW_TILE, TOK_ROWS, LANES), lambda b, t: (b * nt + t, 0, 0)),
                   tok, tok, tok,
                   pl.BlockSpec((N_EXPERTS, LANES), lambda b, t: (0, 0))],
        out_shape=[jax.ShapeDtypeStruct((B, L, D_MODEL), F32),
                   jax.ShapeDtypeStruct((B, L, D_MODEL), BF16),
                   jax.ShapeDtypeStruct((T, TOK_ROWS, LANES), BF16),
                   jax.ShapeDtypeStruct((TOP_K, T), jnp.int32),
                   jax.ShapeDtypeStruct((TOP_K, T), F32),
                   jax.ShapeDtypeStruct((TOP_K, T), jnp.int32),
                   jax.ShapeDtypeStruct((N_EXPERTS, LANES), F32)],
        scratch_shapes=[pltpu.VMEM((N_EXPERTS, LANES), F32), _tok_scratch(ROW_TILE)],
        compiler_params=_params(("arbitrary", "arbitrary")),
        name="merge",
    )(attn, hf, hb, yg, x, mod, w_out_bf, ga, gl, lg, lb, rw_hi, rw_lo, rbias, tri)


def _swiglu(u, wg, wu):
    hg = jnp.dot(u, wg, preferred_element_type=F32)
    hu = jnp.dot(u, wu, preferred_element_type=F32)
    return hg * jax.nn.sigmoid(hg) * hu


def _dispatch_kernel(zt_ref, dest_ref, up_ref, xs_ref, zero_scr, sem, zsem):
    @pl.when(pl.program_id(0) == 0)
    def _():
        zero_scr[...] = jnp.zeros_like(zero_scr)
        n_tiles = zt_ref.shape[0]

        def zero_tile(z):
            return pltpu.make_async_copy(zero_scr, xs_ref.at[pl.ds(z * EXP_TR, EXP_TR)], zsem)

        def start(z, carry):
            @pl.when(zt_ref[z] != 0)
            def _():
                zero_tile(z).start()
            return carry

        def wait(z, carry):
            @pl.when(zt_ref[z] != 0)
            def _():
                zero_tile(z).wait()
            return carry

        lax.fori_loop(0, n_tiles, start, 0)
        lax.fori_loop(0, n_tiles, wait, 0)

    def issue(j, carry):
        for k in range(TOP_K):
            pltpu.make_async_copy(up_ref.at[j], xs_ref.at[dest_ref[j * TOP_K + k]], sem).start(priority=k % 2)
        return carry

    lax.fori_loop(0, ROW_TILE, issue, 0)
    for k in range(TOP_K):
        pltpu.make_async_copy(up_ref, xs_ref.at[pl.ds(0, ROW_TILE)], sem).wait()


def _dispatch(zero_tiles, slots, up, n_sorted_rows):
    T = up.shape[0]
    return pl.pallas_call(
        _dispatch_kernel,
        grid_spec=pltpu.PrefetchScalarGridSpec(
            num_scalar_prefetch=1,
            grid=(T // ROW_TILE,),
            in_specs=[pl.BlockSpec((ROW_TILE * TOP_K,), lambda i, zt: (i,), memory_space=pltpu.SMEM),
                      pl.BlockSpec((ROW_TILE, TOK_ROWS, LANES), lambda i, zt: (i, 0, 0))],
            out_specs=pl.BlockSpec(memory_space=pl.ANY),
            scratch_shapes=[pltpu.VMEM((EXP_TR, TOK_ROWS, LANES), BF16),
                            pltpu.SemaphoreType.DMA(()), pltpu.SemaphoreType.DMA(())],
        ),
        out_shape=jax.ShapeDtypeStruct((n_sorted_rows, TOK_ROWS, LANES), BF16),
        compiler_params=_params(("arbitrary",)),
        name="dispatch",
    )(zero_tiles, slots, up)


def _expert_kernel(te_ref, nu_ref, xs_ref, wg_ref, wu_ref, wd_ref, ys_ref, wgb_ref, wub_ref, wdb_ref, tok_scr):
    i = pl.program_id(0)

    @pl.when(i < nu_ref[0])
    def _():
        new_expert = (i == 0) | (te_ref[i] != te_ref[jnp.maximum(i - 1, 0)])

        @pl.when(new_expert)
        def _():
            wgb_ref[...] = wg_ref[0].astype(BF16)
            wub_ref[...] = wu_ref[0].astype(BF16)
            wdb_ref[...] = wd_ref[0].astype(BF16)

        _token_tiles(tok_scr)[...] = xs_ref[...].astype(F32)
        x = jnp.concatenate([c.astype(BF16) for c in _from_token_major(tok_scr)], axis=1)
        h = _swiglu(x, wgb_ref[...], wub_ref[...])
        y = jnp.dot(h.astype(BF16), wdb_ref[...], preferred_element_type=F32)
        _to_token_major(y, tok_scr)
        ys_ref[...] = _token_tiles(tok_scr)[...].astype(BF16)

    @pl.when(i >= nu_ref[0])
    def _():
        ys_ref[...] = jnp.zeros_like(ys_ref)


def _experts(tile_expert, n_used, xs, wg, wu, wd):
    n_tiles = tile_expert.shape[0]
    blk = (EXP_TR, TOK_ROWS, LANES)
    wspec = lambda a: pl.BlockSpec((1,) + a.shape[1:], lambda i, te, nu: (te[i], 0, 0))
    return pl.pallas_call(
        _expert_kernel,
        grid_spec=pltpu.PrefetchScalarGridSpec(
            num_scalar_prefetch=2,
            grid=(n_tiles,),
            in_specs=[pl.BlockSpec(blk, lambda i, te, nu: (jnp.minimum(i, nu[0] - 1), 0, 0)),
                      wspec(wg), wspec(wu), wspec(wd)],
            out_specs=pl.BlockSpec(blk, lambda i, te, nu: (i, 0, 0)),
            scratch_shapes=[pltpu.VMEM(wg.shape[1:], BF16), pltpu.VMEM(wu.shape[1:], BF16),
                            pltpu.VMEM(wd.shape[1:], BF16), _tok_scratch(EXP_TR)],
        ),
        out_shape=jax.ShapeDtypeStruct(xs.shape, BF16),
        compiler_params=_params(("arbitrary",)),
        name="experts",
    )(tile_expert, n_used, xs, wg, wu, wd)


def _combine_kernel(dcur_ref, dnext_ref, wv_ref, u_ref, xn_ref, g2_ref, sg_ref, su_ref, sd_ref, lg_ref, lb_ref,
                    ys_ref, o_ref, buf_ref, y_scr, sem):
    i = pl.program_id(0)
    n = pl.num_programs(0)
    tm = COMB_TM
    slot = i % 2

    def gather(d_ref, sl):
        def issue(j, carry):
            for k in range(TOP_K):
                pltpu.make_async_copy(ys_ref.at[d_ref[j * TOP_K + k]], buf_ref.at[sl, k, j],
                                      sem.at[sl]).start(priority=k % 2)
            return carry
        lax.fori_loop(0, tm, issue, 0)

    @pl.when(i == 0)
    def _():
        gather(dcur_ref, 0)

    @pl.when(i + 1 < n)
    def _():
        gather(dnext_ref, 1 - slot)

    for k in range(TOP_K):
        pltpu.make_async_copy(ys_ref.at[pl.ds(0, tm)], buf_ref.at[slot, k], sem.at[slot]).wait()

    half = TOK_ROWS // 2

    def accumulate(jo, carry):
        for u in range(COMB_UNROLL):
            j = jo * COMB_UNROLL + u
            lo = jnp.zeros((half, LANES), F32)
            hi = jnp.zeros((half, LANES), F32)
            for k in range(TOP_K):
                gate = wv_ref[pl.ds(j * TOP_K + k, half, stride=0), :]
                rows = buf_ref[slot, k, j].astype(F32)
                lo += gate * rows[:half]
                hi += gate * rows[half:]
            y_scr[j, 0:half, :] = lo
            y_scr[j, half:TOK_ROWS, :] = hi
        return carry

    lax.fori_loop(0, tm // COMB_UNROLL, accumulate, 0)
    y_routed = jnp.concatenate(_from_token_major(y_scr), axis=1)

    hs = _swiglu(u_ref[...], sg_ref[...], su_ref[...])
    y2 = jnp.dot(hs.astype(BF16), sd_ref[...], preferred_element_type=F32) + y_routed
    z = DEEPNORM_ALPHA * xn_ref[...] + g2_ref[0] * y2
    o_ref[...] = _layer_norm(z) * lg_ref[...] + lb_ref[...]


def _combine(slots, gates, u2, xn, g2, sg, su, sd, lg, lb, ys, tiles_per_batch):
    T = u2.shape[0]
    tm = COMB_TM
    n = T // tm
    row = pl.BlockSpec((tm, D_MODEL), lambda i: (i, 0))
    full = lambda a: pl.BlockSpec(a.shape, lambda i: (0,) * a.ndim)
    tok = lambda f: pl.BlockSpec((tm * TOP_K,), lambda i: (f(i),), memory_space=pltpu.SMEM)
    return pl.pallas_call(
        _combine_kernel,
        grid=(n,),
        in_specs=[tok(lambda i: i), tok(lambda i: jnp.minimum(i + 1, n - 1)),
                  pl.BlockSpec((tm * TOP_K, LANES), lambda i: (i, 0)), row, row,
                  pl.BlockSpec((1, 1, D_MODEL), lambda i: (i // tiles_per_batch, 0, 0)),
                  full(sg), full(su), full(sd), full(lg), full(lb),
                  pl.BlockSpec(memory_space=pl.ANY)],
        out_specs=row,
        out_shape=jax.ShapeDtypeStruct((T, D_MODEL), F32),
        scratch_shapes=[pltpu.VMEM((2, TOP_K, tm, TOK_ROWS, LANES), BF16),
                        _tok_scratch(tm),
                        pltpu.SemaphoreType.DMA((2,))],
        compiler_params=_params(("arbitrary",)),
        name="combine",
    )(slots, slots, gates, u2, xn, g2, sg, su, sd, lg, lb, ys)


def _rope_tables(L, C):
    rows = L // GRID_W
    row = jnp.repeat(jnp.arange(rows, dtype=F32), GRID_W)
    col = jnp.tile(jnp.arange(GRID_W, dtype=F32), rows)
    inv_freq = jnp.exp(-jnp.log(ROPE_THETA) * jnp.arange(0, ROPE_AXIS_DIM, 2, dtype=F32) / ROPE_AXIS_DIM)
    ang_r = row[:, None] * inv_freq
    ang_c = col[:, None] * inv_freq
    cr, sr, cc, sc = jnp.cos(ang_r), jnp.sin(ang_r), jnp.cos(ang_c), jnp.sin(ang_c)
    cos_t = jnp.concatenate([cr, cr, cc, cc], axis=-1)
    sin_t = jnp.concatenate([-sr, sr, -sc, sc], axis=-1)
    cos_t = jnp.concatenate([jnp.ones((C, HEAD_DIM), F32), cos_t], axis=0)
    sin_t = jnp.concatenate([jnp.zeros((C, HEAD_DIM), F32), sin_t], axis=0)
    return cos_t, sin_t


def kernel(x, c, ctx, c_ctx, w_mod, b_mod, w_in, attn_sink, conv_w, conv_b, lru_wa, lru_ba, lru_wx, lru_bx,
           lru_lam, norm_attn_g, norm_lru_g, w_out, ln1_g, ln1_b, router_w, router_bias, exp_w_gate, exp_w_up,
           exp_w_down, sh_w_gate, sh_w_up, sh_w_down, ln2_g, ln2_b):
    B, L, D = x.shape
    C = ctx.shape[1]
    assert w_mod.shape[0] == DEPTH and D == D_MODEL and B + 1 <= SUBLANES
    row2 = lambda a: a.reshape(1, -1)

    cvec = jnp.concatenate([c, c_ctx[None], jnp.zeros((SUBLANES - B - 1, D), F32)], axis=0)
    mod = _mod(cvec, w_mod[0], row2(b_mod[0]))

    cos_t, sin_t = _rope_tables(L, C)
    q, k, v, xr, yg = _inproj(x, ctx, mod, w_in[0].astype(BF16), cos_t, sin_t)

    attn = _attention(attn_sink[0], q, k, v, L, C)

    w_gates = jnp.concatenate([lru_wa[0], lru_wx[0]], axis=-1).astype(BF16)
    bias = jnp.stack([lru_ba[0], lru_bx[0]], axis=1)
    hf, hb = _lru(xr, conv_w[0], row2(conv_b[0]), w_gates, bias, lru_lam[0], L)

    rw_hi, rw_lo = _split_bf16(router_w[0].T)
    tri = jnp.triu(jnp.ones((ROW_TILE, ROW_TILE), BF16))
    xn, u2, up, eidx, wk, pos, cnt = _merge(
        attn, hf, hb, yg, x, mod, w_out[0].astype(BF16), row2(norm_attn_g[0]), row2(norm_lru_g[0]),
        row2(ln1_g[0]), row2(ln1_b[0]), rw_hi, rw_lo, router_bias[0].reshape(-1, 1), tri, C)

    T = B * L
    n_tiles = T * TOP_K // EXP_TR + N_EXPERTS
    counts = cnt[:, 0].astype(jnp.int32)
    tiles_e = (counts + EXP_TR - 1) // EXP_TR
    tile_end = jnp.cumsum(tiles_e)
    n_used = tile_end[-1:]
    row_start = (tile_end - tiles_e) * EXP_TR
    experts = jnp.arange(N_EXPERTS, dtype=jnp.int32)
    start_of = jnp.sum(jnp.where(eidx[None] == experts[:, None, None], row_start[:, None, None], 0), axis=0)
    slots = (start_of + pos).T.reshape(-1)
    gates = jnp.broadcast_to(wk.T.reshape(-1, 1), (T * TOP_K, LANES))
    tile_id = jnp.minimum(jnp.arange(n_tiles, dtype=jnp.int32), n_used - 1)
    tile_expert = jnp.sum((tile_end[None, :] <= tile_id[:, None]).astype(jnp.int32), axis=1)

    all_tiles = jnp.arange(n_tiles, dtype=jnp.int32)
    ends_run = jnp.any((tile_end[None, :] - 1 == all_tiles[:, None]) & (tiles_e[None, :] > 0), axis=1)
    zero_tiles = (ends_run | (all_tiles >= n_used)).astype(jnp.int32)
    xs = _dispatch(zero_tiles, slots, up, n_tiles * EXP_TR)
    ys = _experts(tile_expert, n_used, xs, exp_w_gate[0], exp_w_up[0], exp_w_down[0])
    g2 = mod[:B, 5 * D:6 * D].reshape(B, 1, D)
    out = _combine(slots, gates, u2.reshape(T, D), xn.reshape(T, D), g2,
                   sh_w_gate[0].astype(BF16), sh_w_up[0].astype(BF16), sh_w_down[0].astype(BF16),
                   row2(ln2_g[0]), row2(ln2_b[0]), ys, L // COMB_TM)
    return out.reshape(B, L, D)
```

```python
import functools

import jax
import jax.numpy as jnp
from jax import lax
from jax.experimental import pallas as pl
from jax.experimental.pallas import tpu as pltpu

F32 = jnp.float32
BF16 = jnp.bfloat16

D_MODEL = 2048
GRID_W = 64
N_HEADS = 8
N_KV_HEADS = 2
HEAD_DIM = 128
GQA_GROUP = N_HEADS // N_KV_HEADS
ATTN_W = N_HEADS * HEAD_DIM
KV_W = N_KV_HEADS * HEAD_DIM
ATTN_SCALE = HEAD_DIM ** -0.5
BLOCK = 128
ROPE_THETA = 10000.0
ROPE_AXIS_DIM = HEAD_DIM // 2
LRU_W = D_MODEL - ATTN_W
LRU_BLOCKS = 8
LRU_BLOCK_W = LRU_W // LRU_BLOCKS
LRU_C = 8.0
N_EXPERTS = 64
TOP_K = 8
N_GROUPS = 8
GROUP_SIZE = N_EXPERTS // N_GROUPS
TOPK_GROUPS = 4
EXPERT_FF = 512
ROUTED_SCALE = 2.5
LN_EPS = 1e-6
DEPTH = 1
DEEPNORM_ALPHA = (2 * DEPTH) ** 0.25

SUBLANES = 8
LANES = 128
VMEM_LIMIT = 56 * 1024 * 1024

ROW_TILE = 256
MOD_TN = 1024
TOK_ROWS = D_MODEL // LANES
TOK_PITCH = TOK_ROWS + SUBLANES
EXP_TR = 256
CAST_CHUNKS = 16
COMB_TM = 256
COMB_UNROLL = 4


def _params(sem, vmem=VMEM_LIMIT):
    return pltpu.CompilerParams(dimension_semantics=sem, vmem_limit_bytes=vmem)


def _layer_norm(x):
    mu = jnp.mean(x, axis=-1, keepdims=True)
    xc = x - mu
    var = jnp.mean(xc * xc, axis=-1, keepdims=True)
    return xc * lax.rsqrt(var + LN_EPS)


def _rms_norm(x, g):
    return x * lax.rsqrt(jnp.mean(x * x, axis=-1, keepdims=True) + LN_EPS) * g


def _split_bf16(x):
    hi = x.astype(BF16)
    lo = (x - hi.astype(F32)).astype(BF16)
    return hi, lo


def _mod_kernel(c_ref, w_ref, b_ref, o_ref):
    cv = c_ref[...]
    s = cv * jax.nn.sigmoid(cv)
    hi, lo = _split_bf16(s)
    lhs = jnp.concatenate([hi, lo], axis=0)
    r = jnp.dot(lhs, w_ref[...].astype(BF16), preferred_element_type=F32)
    o_ref[...] = r[:SUBLANES] + r[SUBLANES:] + b_ref[...]


def _mod(cvec, w_mod, b_mod):
    n = w_mod.shape[1]
    return pl.pallas_call(
        _mod_kernel,
        grid=(n // MOD_TN,),
        in_specs=[
            pl.BlockSpec((SUBLANES, D_MODEL), lambda j: (0, 0)),
            pl.BlockSpec((D_MODEL, MOD_TN), lambda j: (0, j)),
            pl.BlockSpec((1, MOD_TN), lambda j: (0, j)),
        ],
        out_specs=pl.BlockSpec((SUBLANES, MOD_TN), lambda j: (0, j)),
        out_shape=jax.ShapeDtypeStruct((SUBLANES, n), F32),
        compiler_params=_params(("arbitrary",)),
        name="mod",
    )(cvec, w_mod, b_mod)


def _rope(xh, cos, sin, even_block):
    partner = jnp.where(even_block, pltpu.roll(xh, 96, 1), pltpu.roll(xh, 32, 1))
    return xh * cos + partner * sin


def _inproj_kernel(x_ref, ctx_ref, mod_ref, w_ref, cos_ref, sin_ref,
                   q_ref, k_ref, v_ref, xr_ref, yg_ref):
    b = pl.program_id(0)
    t = pl.program_id(1)
    is_ctx = t == 0
    xin = jnp.where(is_ctx, ctx_ref[0], x_ref[0])
    r = jnp.where(is_ctx, 2, b)
    shift = mod_ref[pl.ds(r, 1), 0:D_MODEL]
    scale = mod_ref[pl.ds(r, 1), D_MODEL:2 * D_MODEL]
    u = (_layer_norm(xin) * (1.0 + scale) + shift).astype(BF16)

    cos = cos_ref[...]
    sin = sin_ref[...]
    lane = lax.broadcasted_iota(jnp.int32, (ROW_TILE, HEAD_DIM), 1)
    even_block = (lane % 64) < 32

    def proj(c0, c1):
        return jnp.dot(u, w_ref[:, c0:c1], preferred_element_type=F32)

    for h in range(N_HEADS):
        qh = proj(h * HEAD_DIM, (h + 1) * HEAD_DIM)
        q_ref[0, :, h * HEAD_DIM:(h + 1) * HEAD_DIM] = (
            _rope(qh, cos, sin, even_block) * ATTN_SCALE).astype(BF16)
    for h in range(N_KV_HEADS):
        c0 = ATTN_W + h * HEAD_DIM
        kh = proj(c0, c0 + HEAD_DIM)
        k_ref[0, :, h * HEAD_DIM:(h + 1) * HEAD_DIM] = _rope(kh, cos, sin, even_block).astype(BF16)
    v_ref[0] = proj(ATTN_W + KV_W, ATTN_W + 2 * KV_W).astype(BF16)
    c0 = ATTN_W + 2 * KV_W
    xr_ref[0] = proj(c0, c0 + LRU_W)
    yg_ref[0] = proj(c0 + LRU_W, c0 + 2 * LRU_W)


def _inproj(x, ctx, mod, w_in_bf, cos_t, sin_t):
    B, L, _ = x.shape
    C = ctx.shape[1]
    assert C == ROW_TILE and L % ROW_TILE == 0
    nt = L // ROW_TILE + 1
    rows = L + C
    d_in = w_in_bf.shape[1]
    out = lambda w, dt: jax.ShapeDtypeStruct((B, rows, w), dt)
    ospec = lambda w: pl.BlockSpec((1, ROW_TILE, w), lambda b, t: (b, t, 0))
    return pl.pallas_call(
        _inproj_kernel,
        grid=(B, nt),
        in_specs=[
            pl.BlockSpec((1, ROW_TILE, D_MODEL), lambda b, t: (b, jnp.maximum(t - 1, 0), 0)),
            pl.BlockSpec((1, ROW_TILE, D_MODEL), lambda b, t: (b, 0, 0)),
            pl.BlockSpec((SUBLANES, 2 * D_MODEL), lambda b, t: (0, 0)),
            pl.BlockSpec((D_MODEL, d_in), lambda b, t: (0, 0)),
            pl.BlockSpec((ROW_TILE, HEAD_DIM), lambda b, t: (t, 0)),
            pl.BlockSpec((ROW_TILE, HEAD_DIM), lambda b, t: (t, 0)),
        ],
        out_specs=[ospec(ATTN_W), ospec(KV_W), ospec(KV_W), ospec(LRU_W), ospec(LRU_W)],
        out_shape=[out(ATTN_W, BF16), out(KV_W, BF16), out(KV_W, BF16), out(LRU_W, F32), out(LRU_W, F32)],
        compiler_params=_params(("arbitrary", "arbitrary")),
        name="inproj",
    )(x, ctx, mod, w_in_bf, cos_t, sin_t)


def _attn_kernel(sink_ref, q_ref, kp_ref, kc_ref, kn_ref, vp_ref, vc_ref, vn_ref, kx_ref, vx_ref, o_ref):
    n = pl.program_id(1)
    nb = pl.num_programs(1)
    rows = GQA_GROUP * BLOCK
    n_ctx = kx_ref.shape[1]
    qi = lax.broadcasted_iota(jnp.int32, (rows, BLOCK), 0) % BLOCK
    kj = lax.broadcasted_iota(jnp.int32, (rows, BLOCK), 1)
    prev_ok = (kj >= qi) & (n > 0)
    next_ok = (kj <= qi) & (n < nb - 1)
    grp = lax.broadcasted_iota(jnp.int32, (rows, 1), 0) // BLOCK
    neg = -jnp.inf
    for h in range(N_KV_HEADS):
        hs = slice(h * HEAD_DIM, (h + 1) * HEAD_DIM)
        q4 = jnp.concatenate(
            [q_ref[0, :, (h * GQA_GROUP + g) * HEAD_DIM:(h * GQA_GROUP + g + 1) * HEAD_DIM]
             for g in range(GQA_GROUP)], axis=0)
        nt = (((1,), (1,)), ((), ()))
        sp = jnp.where(prev_ok, lax.dot_general(q4, kp_ref[0, :, hs], nt, preferred_element_type=F32), neg)
        sc = lax.dot_general(q4, kc_ref[0, :, hs], nt, preferred_element_type=F32)
        sn = jnp.where(next_ok, lax.dot_general(q4, kn_ref[0, :, hs], nt, preferred_element_type=F32), neg)
        sx = lax.dot_general(q4, kx_ref[0, :, hs], nt, preferred_element_type=F32)
        sink = jnp.zeros((rows, 1), F32)
        for g in range(GQA_GROUP):
            sink = jnp.where(grp == g, sink_ref[h * GQA_GROUP + g], sink)
        m = jnp.maximum(jnp.maximum(jnp.max(sp, -1, keepdims=True), jnp.max(sc, -1, keepdims=True)),
                        jnp.maximum(jnp.max(sn, -1, keepdims=True), jnp.max(sx, -1, keepdims=True)))
        m = jnp.maximum(m, sink)
        pp = jnp.exp(sp - m)
        pc = jnp.exp(sc - m)
        pn = jnp.exp(sn - m)
        px = jnp.exp(sx - m)
        denom = (jnp.sum(pp, -1, keepdims=True) + jnp.sum(pc, -1, keepdims=True)
                 + jnp.sum(pn, -1, keepdims=True) + jnp.sum(px, -1, keepdims=True) + jnp.exp(sink - m))
        acc = jnp.dot(pp.astype(BF16), vp_ref[0, :, hs], preferred_element_type=F32)
        acc += jnp.dot(pc.astype(BF16), vc_ref[0, :, hs], preferred_element_type=F32)
        acc += jnp.dot(pn.astype(BF16), vn_ref[0, :, hs], preferred_element_type=F32)
        acc += jnp.dot(px.astype(BF16), vx_ref[0, :, hs], preferred_element_type=F32)
        o = acc / denom
        for g in range(GQA_GROUP):
            c0 = (h * GQA_GROUP + g) * HEAD_DIM
            o_ref[0, :, c0:c0 + HEAD_DIM] = o[g * BLOCK:(g + 1) * BLOCK]


def _attention(sink, q, k, v, L, C):
    B = q.shape[0]
    nb = L // BLOCK
    off = C // BLOCK
    cur = lambda b, n: (b, n + off, 0)
    prv = lambda b, n: (b, jnp.maximum(n - 1, 0) + off, 0)
    nxt = lambda b, n: (b, jnp.minimum(n + 1, nb - 1) + off, 0)
    kv = lambda im: pl.BlockSpec((1, BLOCK, KV_W), im)
    cx = pl.BlockSpec((1, C, KV_W), lambda b, n: (b, 0, 0))
    return pl.pallas_call(
        _attn_kernel,
        grid=(B, nb),
        in_specs=[
            pl.BlockSpec(memory_space=pltpu.SMEM),
            pl.BlockSpec((1, BLOCK, ATTN_W), cur),
            kv(prv), kv(cur), kv(nxt), kv(prv), kv(cur), kv(nxt), cx, cx,
        ],
        out_specs=pl.BlockSpec((1, BLOCK, ATTN_W), lambda b, n: (b, n, 0)),
        out_shape=jax.ShapeDtypeStruct((B, L, ATTN_W), F32),
        compiler_params=_params(("arbitrary", "arbitrary")),
        name="attn",
    )(sink, q, k, k, k, v, v, v, k, v)


def _lru_coeffs(x_ref, p_ref, n_ref, tile, d, cw_ref, cb_ref, w_ref, bias_ref, lam_ref,
                ext_scr, a_scr, b_scr, n_lat_tiles):
    tm = ROW_TILE
    prev_ok = tile >= 2
    next_ok = (tile >= 1) & (tile < n_lat_tiles)
    ext_scr[0:SUBLANES] = jnp.where(prev_ok, p_ref[0], 0.0)
    ext_scr[SUBLANES:SUBLANES + tm] = x_ref[0]
    ext_scr[SUBLANES + tm:2 * SUBLANES + tm] = jnp.where(next_ok, n_ref[0], 0.0)
    xc = cb_ref[...] + cw_ref[2:3] * x_ref[0]
    xc += cw_ref[0:1] * ext_scr[SUBLANES - 2:SUBLANES - 2 + tm]
    xc += cw_ref[1:2] * ext_scr[SUBLANES - 1:SUBLANES - 1 + tm]
    xc += cw_ref[3:4] * ext_scr[SUBLANES + 1:SUBLANES + 1 + tm]
    xcb = xc.astype(BF16)
    lam = lam_ref[d:d + 1]
    sp = jnp.maximum(-lam, 0.0) + jnp.log1p(jnp.exp(-jnp.abs(lam)))
    for n in range(LRU_BLOCKS):
        cs = slice(n * LRU_BLOCK_W, (n + 1) * LRU_BLOCK_W)
        z = jnp.dot(xcb[:, cs], w_ref[d, n], preferred_element_type=F32)
        r = jax.nn.sigmoid(z[:, :LRU_BLOCK_W] + bias_ref[d, 0:1, cs])
        i = jax.nn.sigmoid(z[:, LRU_BLOCK_W:] + bias_ref[d, 1:2, cs])
        log_a = -LRU_C * r * sp[:, cs]
        a = jnp.exp(log_a)
        a_scr[:, cs] = a
        b_scr[:, cs] = jnp.sqrt(1.0 - a * a) * (i * xc[:, cs])


def _lru_scan(a_scr, b_scr, h_ref, state_ref, reset, reverse):
    groups = ROW_TILE // SUBLANES
    row = lax.broadcasted_iota(jnp.int32, (SUBLANES, LRU_W), 0)
    carry0 = jnp.where(reset, 0.0, state_ref[...])

    def body(g, carry):
        gi = (groups - 1 - g) if reverse else g
        r0 = pl.multiple_of(gi * SUBLANES, SUBLANES)
        A = a_scr[pl.ds(r0, SUBLANES), :]
        Bv = b_scr[pl.ds(r0, SUBLANES), :]
        for s in (1, 2, 4):
            if reverse:
                sh, m = SUBLANES - s, row < SUBLANES - s
            else:
                sh, m = s, row >= s
            A_sh = pltpu.roll(A, sh, 0)
            B_sh = pltpu.roll(Bv, sh, 0)
            Bv = jnp.where(m, A * B_sh + Bv, Bv)
            A = jnp.where(m, A * A_sh, A)
        h = Bv + A * carry
        h_ref[0, pl.ds(r0, SUBLANES), :] = h
        last = h[0:1] if reverse else h[SUBLANES - 1:SUBLANES]
        return jnp.broadcast_to(last, (SUBLANES, LRU_W))

    state_ref[...] = lax.fori_loop(0, groups, body, carry0)


def _lru_kernel(xf_ref, xfp_ref, xfn_ref, xb_ref, xbp_ref, xbn_ref, cw_ref, cb_ref, w_ref, bias_ref, lam_ref,
                hf_ref, hb_ref, sf_ref, sb_ref, ext_scr, a_scr, b_scr, *, n_lat_tiles):
    t = pl.program_id(1)
    reset = t == 0
    bt = jnp.where(t == 0, 0, n_lat_tiles + 1 - t)
    _lru_coeffs(xf_ref, xfp_ref, xfn_ref, t, 0, cw_ref, cb_ref, w_ref, bias_ref, lam_ref,
                ext_scr, a_scr, b_scr, n_lat_tiles)
    _lru_scan(a_scr, b_scr, hf_ref, sf_ref, reset, False)
    _lru_coeffs(xb_ref, xbp_ref, xbn_ref, bt, 1, cw_ref, cb_ref, w_ref, bias_ref, lam_ref,
                ext_scr, a_scr, b_scr, n_lat_tiles)
    _lru_scan(a_scr, b_scr, hb_ref, sb_ref, reset, True)


def _lru(xr, conv_w, conv_b, w_gates, bias, lam, L):
    B, rows, _ = xr.shape
    nl = L // ROW_TILE
    nt = nl + 1
    per = ROW_TILE // SUBLANES
    n8 = rows // SUBLANES
    ft = lambda b, t: t
    btile = lambda b, t: jnp.where(t == 0, 0, nl + 1 - t)
    main = lambda f: pl.BlockSpec((1, ROW_TILE, LRU_W), lambda b, t: (b, f(b, t), 0))
    prev = lambda f: pl.BlockSpec((1, SUBLANES, LRU_W), lambda b, t: (b, jnp.maximum(f(b, t) * per - 1, 0), 0))
    nxt = lambda f: pl.BlockSpec((1, SUBLANES, LRU_W),
                                 lambda b, t: (b, jnp.minimum((f(b, t) + 1) * per, n8 - 1), 0))
    full = lambda a: pl.BlockSpec(a.shape, lambda b, t: (0,) * a.ndim)
    return pl.pallas_call(
        functools.partial(_lru_kernel, n_lat_tiles=nl),
        grid=(B, nt),
        in_specs=[main(ft), prev(ft), nxt(ft), main(btile), prev(btile), nxt(btile),
                  full(conv_w), full(conv_b), full(w_gates), full(bias), full(lam)],
        out_specs=[
            pl.BlockSpec((1, ROW_TILE, LRU_W), lambda b, t: (b, jnp.maximum(t - 1, 0), 0)),
            pl.BlockSpec((1, ROW_TILE, LRU_W), lambda b, t: (b, nl - jnp.maximum(t, 1), 0)),
        ],
        out_shape=[jax.ShapeDtypeStruct((B, L, LRU_W), F32)] * 2,
        scratch_shapes=[
            pltpu.VMEM((SUBLANES, LRU_W), F32), pltpu.VMEM((SUBLANES, LRU_W), F32),
            pltpu.VMEM((ROW_TILE + 2 * SUBLANES, LRU_W), F32),
            pltpu.VMEM((ROW_TILE, LRU_W), F32), pltpu.VMEM((ROW_TILE, LRU_W), F32),
        ],
        compiler_params=_params(("arbitrary", "arbitrary")),
        name="lru",
    )(xr, xr, xr, xr, xr, xr, conv_w, conv_b, w_gates, bias, lam)


def _route(scores, sel):
    tm = scores.shape[1]
    neg = -jnp.inf
    iota_g = lax.broadcasted_iota(jnp.int32, (GROUP_SIZE, tm), 0)
    grp_score = []
    for g in range(N_GROUPS):
        sg = sel[g * GROUP_SIZE:(g + 1) * GROUP_SIZE]
        m1 = jnp.max(sg, axis=0, keepdims=True)
        first = jnp.min(jnp.where(sg == m1, iota_g, GROUP_SIZE), axis=0, keepdims=True)
        m2 = jnp.max(jnp.where(iota_g == first, neg, sg), axis=0, keepdims=True)
        grp_score.append(m1 + m2)
    masked = []
    for g in range(N_GROUPS):
        rank = jnp.zeros((1, tm), jnp.int32)
        for o in range(N_GROUPS):
            if o == g:
                continue
            ahead = (grp_score[o] > grp_score[g]) if o > g else (grp_score[o] >= grp_score[g])
            rank += ahead.astype(jnp.int32)
        keep = rank < TOPK_GROUPS
        masked.append(jnp.where(keep, sel[g * GROUP_SIZE:(g + 1) * GROUP_SIZE], neg))
    cand = jnp.concatenate(masked, axis=0)
    iota_e = lax.broadcasted_iota(jnp.int32, (N_EXPERTS, tm), 0)
    chosen = jnp.zeros((N_EXPERTS, tm), jnp.bool_)
    picks = []
    for _ in range(TOP_K):
        cur = jnp.where(chosen, neg, cand)
        m = jnp.max(cur, axis=0, keepdims=True)
        idx = jnp.min(jnp.where((cur == m) & jnp.logical_not(chosen), iota_e, N_EXPERTS), axis=0, keepdims=True)
        chosen = chosen | (iota_e == idx)
        picks.append(idx)
    w = jnp.where(chosen, scores, 0.0)
    gates = w / jnp.sum(w, axis=0, keepdims=True) * ROUTED_SCALE
    return gates, chosen, picks


def _tok_scratch(rows):
    return pltpu.VMEM((rows, TOK_PITCH, LANES), F32)


def _to_token_major(y, scr):
    rows = scr.shape[0]
    flat = scr.reshape(rows * TOK_PITCH, LANES)
    for s in range(TOK_ROWS):
        flat[pl.ds(s, rows, stride=TOK_PITCH), :] = y[:, s * LANES:(s + 1) * LANES]


def _from_token_major(scr):
    rows = scr.shape[0]
    flat = scr.reshape(rows * TOK_PITCH, LANES)
    return [flat[pl.ds(s, rows, stride=TOK_PITCH), :] for s in range(TOK_ROWS)]


def _token_tiles(scr):
    return scr.at[:, 0:TOK_ROWS, :]


def _merge_kernel(attn_ref, hf_ref, hb_ref, yg_ref, x_ref, mod_ref, wout_ref, ga_ref, gl_ref, lg_ref, lb_ref,
                  rwh_ref, rwl_ref, rb_ref, tri_ref, xn_ref, u2_ref, up_ref, eidx_ref, wk_ref, pos_ref, cnt_ref,
                  carry_ref, tok_scr):
    b = pl.program_id(0)

    @pl.when((b == 0) & (pl.program_id(1) == 0))
    def _():
        carry_ref[...] = jnp.zeros_like(carry_ref)

    D = D_MODEL
    mrow = lambda i: mod_ref[pl.ds(b, 1), i * D:(i + 1) * D]
    lru_y = (hf_ref[0] + hb_ref[0]) * jax.nn.gelu(yg_ref[0])
    na = _rms_norm(attn_ref[0], ga_ref[...]).astype(BF16)
    nl = _rms_norm(lru_y, gl_ref[...]).astype(BF16)
    y1 = jnp.dot(na, wout_ref[0:ATTN_W], preferred_element_type=F32)
    y1 += jnp.dot(nl, wout_ref[ATTN_W:D], preferred_element_type=F32)
    xn = _layer_norm(DEEPNORM_ALPHA * x_ref[0] + mrow(2) * y1) * lg_ref[...] + lb_ref[...]
    xn_ref[0] = xn
    u2 = _layer_norm(xn) * (1.0 + mrow(4)) + mrow(3)
    u_hi, u_lo = _split_bf16(u2)
    u2_ref[0] = u_hi
    nt = (((1,), (1,)), ((), ()))
    logits = lax.dot_general(rwh_ref[...], u_hi, nt, preferred_element_type=F32)
    logits += lax.dot_general(rwh_ref[...], u_lo, nt, preferred_element_type=F32)
    logits += lax.dot_general(rwl_ref[...], u_hi, nt, preferred_element_type=F32)
    _to_token_major(u2, tok_scr)
    up_ref[...] = _token_tiles(tok_scr)[...].astype(BF16)
    scores = jax.nn.sigmoid(logits)
    gates, chosen, picks = _route(scores, scores + rb_ref[...])
    sel01 = jnp.where(chosen, 1.0, 0.0)
    incl = jnp.dot(sel01.astype(BF16), tri_ref[...], preferred_element_type=F32)
    rank = carry_ref[:, 0:1] + incl - sel01
    carry_ref[...] = carry_ref[...] + incl[:, ROW_TILE - 1:ROW_TILE]
    cnt_ref[...] = carry_ref[...]
    iota_e = lax.broadcasted_iota(jnp.int32, (N_EXPERTS, ROW_TILE), 0)
    for kk, idx in enumerate(picks):
        hit = iota_e == idx
        eidx_ref[kk:kk + 1, :] = idx
        wk_ref[kk:kk + 1, :] = jnp.sum(jnp.where(hit, gates, 0.0), axis=0, keepdims=True)
        pos_ref[kk:kk + 1, :] = jnp.sum(jnp.where(hit, rank, 0.0), axis=0, keepdims=True).astype(jnp.int32)


def _merge(attn, hf, hb, yg, x, mod, w_out_bf, ga, gl, lg, lb, rw_hi, rw_lo, rbias, tri, C):
    B, L, _ = x.shape
    T = B * L
    nt = L // ROW_TILE
    off = C // ROW_TILE
    row = lambda w, o=0: pl.BlockSpec((1, ROW_TILE, w), lambda b, t: (b, t + o, 0))
    full = lambda a: pl.BlockSpec(a.shape, lambda b, t: (0,) * a.ndim)
    tok = pl.BlockSpec((TOP_K, ROW_TILE), lambda b, t: (0, b * nt + t))
    return pl.pallas_call(
        _merge_kernel,
        grid=(B, nt),
        in_specs=[row(ATTN_W), row(LRU_W), row(LRU_W), row(LRU_W, off), row(D_MODEL),
                  full(mod), full(w_out_bf), full(ga), full(gl), full(lg), full(lb),
                  full(rw_hi), full(rw_lo), full(rbias), full(tri)],
        out_specs=[row(D_MODEL), row(D_MODEL),
                   pl.BlockSpec((ROW_TILE, TOK_ROWS, LANES), lambda b, t: (b * nt + t, 0, 0)),
                   tok, tok, tok,
                   pl.BlockSpec((N_EXPERTS, LANES), lambda b, t: (0, 0))],
        out_shape=[jax.ShapeDtypeStruct((B, L, D_MODEL), F32),
                   jax.ShapeDtypeStruct((B, L, D_MODEL), BF16),
                   jax.ShapeDtypeStruct((T, TOK_ROWS, LANES), BF16),
                   jax.ShapeDtypeStruct((TOP_K, T), jnp.int32),
                   jax.ShapeDtypeStruct((TOP_K, T), F32),
                   jax.ShapeDtypeStruct((TOP_K, T), jnp.int32),
                   jax.ShapeDtypeStruct((N_EXPERTS, LANES), F32)],
        scratch_shapes=[pltpu.VMEM((N_EXPERTS, LANES), F32), _tok_scratch(ROW_TILE)],
        compiler_params=_params(("arbitrary", "arbitrary")),
        name="merge",
    )(attn, hf, hb, yg, x, mod, w_out_bf, ga, gl, lg, lb, rw_hi, rw_lo, rbias, tri)


def _swiglu(u, wg, wu):
    hg = jnp.dot(u, wg, preferred_element_type=F32)
    hu = jnp.dot(u, wu, preferred_element_type=F32)
    return hg * jax.nn.sigmoid(hg) * hu


def _dispatch_kernel(zt_ref, dest_ref, up_ref, xs_ref, zero_scr, sem, zsem):
    @pl.when(pl.program_id(0) == 0)
    def _():
        zero_scr[...] = jnp.zeros_like(zero_scr)
        n_tiles = zt_ref.shape[0]

        def zero_tile(z):
            return pltpu.make_async_copy(zero_scr, xs_ref.at[pl.ds(z * EXP_TR, EXP_TR)], zsem)

        def start(z, carry):
            @pl.when(zt_ref[z] != 0)
            def _():
                zero_tile(z).start()
            return carry

        def wait(z, carry):
            @pl.when(zt_ref[z] != 0)
            def _():
                zero_tile(z).wait()
            return carry

        lax.fori_loop(0, n_tiles, start, 0)
        lax.fori_loop(0, n_tiles, wait, 0)

    def issue(j, carry):
        for k in range(TOP_K):
            pltpu.make_async_copy(up_ref.at[j], xs_ref.at[dest_ref[j * TOP_K + k]], sem).start(priority=k % 2)
        return carry

    lax.fori_loop(0, ROW_TILE, issue, 0)
    for k in range(TOP_K):
        pltpu.make_async_copy(up_ref, xs_ref.at[pl.ds(0, ROW_TILE)], sem).wait()


def _dispatch(zero_tiles, slots, up, n_sorted_rows):
    T = up.shape[0]
    return pl.pallas_call(
        _dispatch_kernel,
        grid_spec=pltpu.PrefetchScalarGridSpec(
            num_scalar_prefetch=1,
            grid=(T // ROW_TILE,),
            in_specs=[pl.BlockSpec((ROW_TILE * TOP_K,), lambda i, zt: (i,), memory_space=pltpu.SMEM),
                      pl.BlockSpec((ROW_TILE, TOK_ROWS, LANES), lambda i, zt: (i, 0, 0))],
            out_specs=pl.BlockSpec(memory_space=pl.ANY),
            scratch_shapes=[pltpu.VMEM((EXP_TR, TOK_ROWS, LANES), BF16),
                            pltpu.SemaphoreType.DMA(()), pltpu.SemaphoreType.DMA(())],
        ),
        out_shape=jax.ShapeDtypeStruct((n_sorted_rows, TOK_ROWS, LANES), BF16),
        compiler_params=_params(("arbitrary",)),
        name="dispatch",
    )(zero_tiles, slots, up)


def _expert_kernel(te_ref, nu_ref, nxt_ref, par_ref, xs_ref, wg_hbm, wu_hbm, wd_hbm, ys_ref,
                   wg32, wu32, wd32, wgb_ref, wub_ref, wdb_ref, tok_scr, wsem):
    i = pl.program_id(0)
    n_used = nu_ref[0]

    def weight_copies(e, s):
        return [pltpu.make_async_copy(hbm.at[e], stage.at[s], wsem.at[s])
                for hbm, stage in ((wg_hbm, wg32), (wu_hbm, wu32), (wd_hbm, wd32))]

    @pl.when(i == 0)
    def _():
        for cp in weight_copies(te_ref[0], 0):
            cp.start()

    @pl.when(i < n_used)
    def _():
        e = te_ref[i]

        first_of_expert = (i == 0) | (e != te_ref[jnp.maximum(i - 1, 0)])
        for s in range(2):
            @pl.when(first_of_expert & (par_ref[i] == s))
            def _():
                for cp in weight_copies(e, s):
                    cp.wait()
                for src, dst in ((wg32, wgb_ref), (wu32, wub_ref), (wd32, wdb_ref)):
                    rows = dst.shape[0] // CAST_CHUNKS

                    def cast_chunk(r, carry, src=src, dst=dst, rows=rows):
                        sl = pl.ds(pl.multiple_of(r * rows, rows), rows)
                        dst[sl, :] = src[s, sl, :].astype(BF16)
                        return carry

                    lax.fori_loop(0, CAST_CHUNKS, cast_chunk, 0)

                @pl.when(nxt_ref[i] >= 0)
                def _():
                    for cp in weight_copies(nxt_ref[i], 1 - s):
                        cp.start()

        _token_tiles(tok_scr)[...] = xs_ref[...].astype(F32)
        x = jnp.concatenate([c.astype(BF16) for c in _from_token_major(tok_scr)], axis=1)
        h = _swiglu(x, wgb_ref[...], wub_ref[...])
        y = jnp.dot(h.astype(BF16), wdb_ref[...], preferred_element_type=F32)
        _to_token_major(y, tok_scr)
        ys_ref[...] = _token_tiles(tok_scr)[...].astype(BF16)

    @pl.when(i >= n_used)
    def _():
        ys_ref[...] = jnp.zeros_like(ys_ref)


def _experts(tile_expert, n_used, next_expert, run_parity, xs, wg, wu, wd):
    n_tiles = xs.shape[0] // EXP_TR
    blk = (EXP_TR, TOK_ROWS, LANES)
    stage = lambda a: pltpu.VMEM((2,) + a.shape[1:], F32)
    cast = lambda a: pltpu.VMEM(a.shape[1:], BF16)
    hbm = pl.BlockSpec(memory_space=pl.ANY)
    return pl.pallas_call(
        _expert_kernel,
        grid_spec=pltpu.PrefetchScalarGridSpec(
            num_scalar_prefetch=4,
            grid=(n_tiles,),
            in_specs=[pl.BlockSpec(blk, lambda i, te, nu, nx, pr: (jnp.minimum(i, nu[0] - 1), 0, 0)),
                      hbm, hbm, hbm],
            out_specs=pl.BlockSpec(blk, lambda i, te, nu, nx, pr: (i, 0, 0)),
            scratch_shapes=[stage(wg), stage(wu), stage(wd), cast(wg), cast(wu), cast(wd),
                            _tok_scratch(EXP_TR), pltpu.SemaphoreType.DMA((2,))],
        ),
        out_shape=jax.ShapeDtypeStruct(xs.shape, BF16),
        compiler_params=_params(("arbitrary",)),
        name="experts",
    )(tile_expert, n_used, next_expert, run_parity, xs, wg, wu, wd)


def _combine_kernel(dcur_ref, dnext_ref, wv_ref, u_ref, xn_ref, g2_ref, sg_ref, su_ref, sd_ref, lg_ref, lb_ref,
                    ys_ref, o_ref, buf_ref, y_scr, sem):
    i = pl.program_id(0)
    n = pl.num_programs(0)
    tm = COMB_TM
    slot = i % 2

    def gather(d_ref, sl):
        def issue(j, carry):
            for k in range(TOP_K):
                pltpu.make_async_copy(ys_ref.at[d_ref[j * TOP_K + k]], buf_ref.at[sl, k, j],
                                      sem.at[sl]).start(priority=k % 2)
            return carry
        lax.fori_loop(0, tm, issue, 0)

    @pl.when(i == 0)
    def _():
        gather(dcur_ref, 0)

    @pl.when(i + 1 < n)
    def _():
        gather(dnext_ref, 1 - slot)

    for k in range(TOP_K):
        pltpu.make_async_copy(ys_ref.at[pl.ds(0, tm)], buf_ref.at[slot, k], sem.at[slot]).wait()

    half = TOK_ROWS // 2

    def accumulate(jo, carry):
        for u in range(COMB_UNROLL):
            j = jo * COMB_UNROLL + u
            lo = jnp.zeros((half, LANES), F32)
            hi = jnp.zeros((half, LANES), F32)
            for k in range(TOP_K):
                gate = wv_ref[pl.ds(j * TOP_K + k, half, stride=0), :]
                rows = buf_ref[slot, k, j].astype(F32)
                lo += gate * rows[:half]
                hi += gate * rows[half:]
            y_scr[j, 0:half, :] = lo
            y_scr[j, half:TOK_ROWS, :] = hi
        return carry

    lax.fori_loop(0, tm // COMB_UNROLL, accumulate, 0)
    y_routed = jnp.concatenate(_from_token_major(y_scr), axis=1)

    hs = _swiglu(u_ref[...], sg_ref[...], su_ref[...])
    y2 = jnp.dot(hs.astype(BF16), sd_ref[...], preferred_element_type=F32) + y_routed
    z = DEEPNORM_ALPHA * xn_ref[...] + g2_ref[0] * y2
    o_ref[...] = _layer_norm(z) * lg_ref[...] + lb_ref[...]


def _combine(slots, gates, u2, xn, g2, sg, su, sd, lg, lb, ys, tiles_per_batch):
    T = u2.shape[0]
    tm = COMB_TM
    n = T // tm
    row = pl.BlockSpec((tm, D_MODEL), lambda i: (i, 0))
    full = lambda a: pl.BlockSpec(a.shape, lambda i: (0,) * a.ndim)
    tok = lambda f: pl.BlockSpec((tm * TOP_K,), lambda i: (f(i),), memory_space=pltpu.SMEM)
    return pl.pallas_call(
        _combine_kernel,
        grid=(n,),
        in_specs=[tok(lambda i: i), tok(lambda i: jnp.minimum(i + 1, n - 1)),
                  pl.BlockSpec((tm * TOP_K, LANES), lambda i: (i, 0)), row, row,
                  pl.BlockSpec((1, 1, D_MODEL), lambda i: (i // tiles_per_batch, 0, 0)),
                  full(sg), full(su), full(sd), full(lg), full(lb),
                  pl.BlockSpec(memory_space=pl.ANY)],
        out_specs=row,
        out_shape=jax.ShapeDtypeStruct((T, D_MODEL), F32),
        scratch_shapes=[pltpu.VMEM((2, TOP_K, tm, TOK_ROWS, LANES), BF16),
                        _tok_scratch(tm),
                        pltpu.SemaphoreType.DMA((2,))],
        compiler_params=_params(("arbitrary",)),
        name="combine",
    )(slots, slots, gates, u2, xn, g2, sg, su, sd, lg, lb, ys)


def _rope_tables(L, C):
    rows = L // GRID_W
    row = jnp.repeat(jnp.arange(rows, dtype=F32), GRID_W)
    col = jnp.tile(jnp.arange(GRID_W, dtype=F32), rows)
    inv_freq = jnp.exp(-jnp.log(ROPE_THETA) * jnp.arange(0, ROPE_AXIS_DIM, 2, dtype=F32) / ROPE_AXIS_DIM)
    ang_r = row[:, None] * inv_freq
    ang_c = col[:, None] * inv_freq
    cr, sr, cc, sc = jnp.cos(ang_r), jnp.sin(ang_r), jnp.cos(ang_c), jnp.sin(ang_c)
    cos_t = jnp.concatenate([cr, cr, cc, cc], axis=-1)
    sin_t = jnp.concatenate([-sr, sr, -sc, sc], axis=-1)
    cos_t = jnp.concatenate([jnp.ones((C, HEAD_DIM), F32), cos_t], axis=0)
    sin_t = jnp.concatenate([jnp.zeros((C, HEAD_DIM), F32), sin_t], axis=0)
    return cos_t, sin_t


def kernel(x, c, ctx, c_ctx, w_mod, b_mod, w_in, attn_sink, conv_w, conv_b, lru_wa, lru_ba, lru_wx, lru_bx,
           lru_lam, norm_attn_g, norm_lru_g, w_out, ln1_g, ln1_b, router_w, router_bias, exp_w_gate, exp_w_up,
           exp_w_down, sh_w_gate, sh_w_up, sh_w_down, ln2_g, ln2_b):
    B, L, D = x.shape
    C = ctx.shape[1]
    assert w_mod.shape[0] == DEPTH and D == D_MODEL and B + 1 <= SUBLANES
    row2 = lambda a: a.reshape(1, -1)

    cvec = jnp.concatenate([c, c_ctx[None], jnp.zeros((SUBLANES - B - 1, D), F32)], axis=0)
    mod = _mod(cvec, w_mod[0], row2(b_mod[0]))

    cos_t, sin_t = _rope_tables(L, C)
    q, k, v, xr, yg = _inproj(x, ctx, mod, w_in[0].astype(BF16), cos_t, sin_t)

    attn = _attention(attn_sink[0], q, k, v, L, C)

    w_gates = jnp.concatenate([lru_wa[0], lru_wx[0]], axis=-1).astype(BF16)
    bias = jnp.stack([lru_ba[0], lru_bx[0]], axis=1)
    hf, hb = _lru(xr, conv_w[0], row2(conv_b[0]), w_gates, bias, lru_lam[0], L)

    rw_hi, rw_lo = _split_bf16(router_w[0].T)
    tri = jnp.triu(jnp.ones((ROW_TILE, ROW_TILE), BF16))
    xn, u2, up, eidx, wk, pos, cnt = _merge(
        attn, hf, hb, yg, x, mod, w_out[0].astype(BF16), row2(norm_attn_g[0]), row2(norm_lru_g[0]),
        row2(ln1_g[0]), row2(ln1_b[0]), rw_hi, rw_lo, router_bias[0].reshape(-1, 1), tri, C)

    T = B * L
    n_tiles = T * TOP_K // EXP_TR + N_EXPERTS
    counts = cnt[:, 0].astype(jnp.int32)
    tiles_e = (counts + EXP_TR - 1) // EXP_TR
    tile_end = jnp.cumsum(tiles_e)
    n_used = tile_end[-1:]
    row_start = (tile_end - tiles_e) * EXP_TR
    experts = jnp.arange(N_EXPERTS, dtype=jnp.int32)
    start_of = jnp.sum(jnp.where(eidx[None] == experts[:, None, None], row_start[:, None, None], 0), axis=0)
    slots = (start_of + pos).T.reshape(-1)
    gates = jnp.broadcast_to(wk.T.reshape(-1, 1), (T * TOP_K, LANES))
    tile_id = jnp.minimum(jnp.arange(n_tiles, dtype=jnp.int32), n_used - 1)
    tile_expert = jnp.sum((tile_end[None, :] <= tile_id[:, None]).astype(jnp.int32), axis=1)

    all_tiles = jnp.arange(n_tiles, dtype=jnp.int32)
    ends_run = jnp.any((tile_end[None, :] - 1 == all_tiles[:, None]) & (tiles_e[None, :] > 0), axis=1)
    zero_tiles = (ends_run | (all_tiles >= n_used)).astype(jnp.int32)
    xs = _dispatch(zero_tiles, slots, up, n_tiles * EXP_TR)
    has_rows = tiles_e > 0
    run_parity_e = (jnp.cumsum(has_rows) - has_rows) % 2
    later = has_rows[None, :] & (experts[None, :] > experts[:, None])
    next_e = jnp.min(jnp.where(later, experts[None, :], N_EXPERTS), axis=1)
    next_e = jnp.where(next_e == N_EXPERTS, -1, next_e)
    ys = _experts(tile_expert, n_used, next_e[tile_expert].astype(jnp.int32),
                  run_parity_e[tile_expert].astype(jnp.int32), xs, exp_w_gate[0], exp_w_up[0], exp_w_down[0])
    g2 = mod[:B, 5 * D:6 * D].reshape(B, 1, D)
    out = _combine(slots, gates, u2.reshape(T, D), xn.reshape(T, D), g2,
                   sh_w_gate[0].astype(BF16), sh_w_up[0].astype(BF16), sh_w_down[0].astype(BF16),
                   row2(ln2_g[0]), row2(ln2_b[0]), ys, L // COMB_TM)
    return out.reshape(B, L, D)
```

```python
import functools

import jax
import jax.numpy as jnp
from jax import lax
from jax.experimental import pallas as pl
from jax.experimental.pallas import tpu as pltpu

F32 = jnp.float32
BF16 = jnp.bfloat16

D_MODEL = 2048
GRID_W = 64
N_HEADS = 8
N_KV_HEADS = 2
HEAD_DIM = 128
GQA_GROUP = N_HEADS // N_KV_HEADS
ATTN_W = N_HEADS * HEAD_DIM
KV_W = N_KV_HEADS * HEAD_DIM
ATTN_SCALE = HEAD_DIM ** -0.5
BLOCK = 128
ROPE_THETA = 10000.0
ROPE_AXIS_DIM = HEAD_DIM // 2
LRU_W = D_MODEL - ATTN_W
LRU_BLOCKS = 8
LRU_BLOCK_W = LRU_W // LRU_BLOCKS
LRU_C = 8.0
N_EXPERTS = 64
TOP_K = 8
N_GROUPS = 8
GROUP_SIZE = N_EXPERTS // N_GROUPS
TOPK_GROUPS = 4
EXPERT_FF = 512
ROUTED_SCALE = 2.5
LN_EPS = 1e-6
SQRT_GUARD = 1e-30
DEPTH = 1
DEEPNORM_ALPHA = (2 * DEPTH) ** 0.25

SUBLANES = 8
LANES = 128
VMEM_LIMIT = 56 * 1024 * 1024

ROW_TILE = 256
MOD_TN = 1024
MERGE_SPLIT = 2
TOK_ROWS = D_MODEL // LANES
TOK_PITCH = TOK_ROWS + SUBLANES
EXP_TR = 256
CAST_CHUNKS = 16
COMB_TM = 256
COMB_UNROLL = 4


def _params(sem, vmem=VMEM_LIMIT):
    return pltpu.CompilerParams(dimension_semantics=sem, vmem_limit_bytes=vmem)


def _layer_norm(x):
    mu = jnp.mean(x, axis=-1, keepdims=True)
    xc = x - mu
    var = jnp.mean(xc * xc, axis=-1, keepdims=True)
    return xc * lax.rsqrt(var + LN_EPS)


def _rms_norm(x, g):
    return x * lax.rsqrt(jnp.mean(x * x, axis=-1, keepdims=True) + LN_EPS) * g


def _sigmoid(x):
    return 0.5 * jnp.tanh(0.5 * x) + 0.5


def _split_bf16(x):
    hi = x.astype(BF16)
    lo = (x - hi.astype(F32)).astype(BF16)
    return hi, lo


def _mod_kernel(c_ref, w_ref, b_ref, o_ref):
    cv = c_ref[...]
    s = cv * jax.nn.sigmoid(cv)
    hi, lo = _split_bf16(s)
    lhs = jnp.concatenate([hi, lo], axis=0)
    r = jnp.dot(lhs, w_ref[...].astype(BF16), preferred_element_type=F32)
    o_ref[...] = r[:SUBLANES] + r[SUBLANES:] + b_ref[...]


def _mod(cvec, w_mod, b_mod):
    n = w_mod.shape[1]
    return pl.pallas_call(
        _mod_kernel,
        grid=(n // MOD_TN,),
        in_specs=[
            pl.BlockSpec((SUBLANES, D_MODEL), lambda j: (0, 0)),
            pl.BlockSpec((D_MODEL, MOD_TN), lambda j: (0, j)),
            pl.BlockSpec((1, MOD_TN), lambda j: (0, j)),
        ],
        out_specs=pl.BlockSpec((SUBLANES, MOD_TN), lambda j: (0, j)),
        out_shape=jax.ShapeDtypeStruct((SUBLANES, n), F32),
        compiler_params=_params(("arbitrary",)),
        name="mod",
    )(cvec, w_mod, b_mod)


def _rope(xh, cos, sin, even_block):
    partner = jnp.where(even_block, pltpu.roll(xh, 96, 1), pltpu.roll(xh, 32, 1))
    return xh * cos + partner * sin


def _inproj_kernel(x_ref, ctx_ref, mod_ref, w_ref, cos_ref, sin_ref,
                   q_ref, k_ref, v_ref, xr_ref, yg_ref):
    b = pl.program_id(0)
    t = pl.program_id(1)
    is_ctx = t == 0
    xin = jnp.where(is_ctx, ctx_ref[0], x_ref[0])
    r = jnp.where(is_ctx, 2, b)
    shift = mod_ref[pl.ds(r, 1), 0:D_MODEL]
    scale = mod_ref[pl.ds(r, 1), D_MODEL:2 * D_MODEL]
    u = (_layer_norm(xin) * (1.0 + scale) + shift).astype(BF16)

    cos = cos_ref[...]
    sin = sin_ref[...]
    lane = lax.broadcasted_iota(jnp.int32, (ROW_TILE, HEAD_DIM), 1)
    even_block = (lane % 64) < 32

    def proj(c0, c1):
        return jnp.dot(u, w_ref[:, c0:c1], preferred_element_type=F32)

    head = lambda y, h: y[:, h * HEAD_DIM:(h + 1) * HEAD_DIM]
    for p in range(N_HEADS // 2):
        qq = proj(2 * p * HEAD_DIM, (2 * p + 2) * HEAD_DIM)
        for h in range(2):
            q_ref[0, :, (2 * p + h) * HEAD_DIM:(2 * p + h + 1) * HEAD_DIM] = (
                _rope(head(qq, h), cos, sin, even_block) * ATTN_SCALE).astype(BF16)
    kk = proj(ATTN_W, ATTN_W + KV_W)
    for h in range(N_KV_HEADS):
        k_ref[0, :, h * HEAD_DIM:(h + 1) * HEAD_DIM] = _rope(head(kk, h), cos, sin, even_block).astype(BF16)
    v_ref[0] = proj(ATTN_W + KV_W, ATTN_W + 2 * KV_W).astype(BF16)
    c0 = ATTN_W + 2 * KV_W
    xr_ref[0] = proj(c0, c0 + LRU_W)
    yg_ref[0] = proj(c0 + LRU_W, c0 + 2 * LRU_W)


def _inproj(x, ctx, mod, w_in_bf, cos_t, sin_t):
    B, L, _ = x.shape
    C = ctx.shape[1]
    assert C == ROW_TILE and L % ROW_TILE == 0
    nt = L // ROW_TILE + 1
    rows = L + C
    d_in = w_in_bf.shape[1]
    out = lambda w, dt: jax.ShapeDtypeStruct((B, rows, w), dt)
    ospec = lambda w: pl.BlockSpec((1, ROW_TILE, w), lambda b, t: (b, t, 0))
    return pl.pallas_call(
        _inproj_kernel,
        grid=(B, nt),
        in_specs=[
            pl.BlockSpec((1, ROW_TILE, D_MODEL), lambda b, t: (b, jnp.maximum(t - 1, 0), 0)),
            pl.BlockSpec((1, ROW_TILE, D_MODEL), lambda b, t: (b, 0, 0)),
            pl.BlockSpec((SUBLANES, 2 * D_MODEL), lambda b, t: (0, 0)),
            pl.BlockSpec((D_MODEL, d_in), lambda b, t: (0, 0)),
            pl.BlockSpec((ROW_TILE, HEAD_DIM), lambda b, t: (t, 0)),
            pl.BlockSpec((ROW_TILE, HEAD_DIM), lambda b, t: (t, 0)),
        ],
        out_specs=[ospec(ATTN_W), ospec(KV_W), ospec(KV_W), ospec(LRU_W), ospec(LRU_W)],
        out_shape=[out(ATTN_W, BF16), out(KV_W, BF16), out(KV_W, BF16), out(LRU_W, F32), out(LRU_W, F32)],
        compiler_params=_params(("arbitrary", "arbitrary")),
        name="inproj",
    )(x, ctx, mod, w_in_bf, cos_t, sin_t)


def _attn_kernel(sink_ref, q_ref, kp_ref, kc_ref, kn_ref, vp_ref, vc_ref, vn_ref, kx_ref, vx_ref, o_ref):
    n = pl.program_id(1)
    nb = pl.num_programs(1)
    rows = GQA_GROUP * BLOCK
    n_ctx = kx_ref.shape[1]
    qi = lax.broadcasted_iota(jnp.int32, (rows, BLOCK), 0) % BLOCK
    kj = lax.broadcasted_iota(jnp.int32, (rows, BLOCK), 1)
    prev_ok = (kj >= qi) & (n > 0)
    next_ok = (kj <= qi) & (n < nb - 1)
    grp = lax.broadcasted_iota(jnp.int32, (rows, 1), 0) // BLOCK
    neg = -jnp.inf
    for h in range(N_KV_HEADS):
        hs = slice(h * HEAD_DIM, (h + 1) * HEAD_DIM)
        q4 = jnp.concatenate(
            [q_ref[0, :, (h * GQA_GROUP + g) * HEAD_DIM:(h * GQA_GROUP + g + 1) * HEAD_DIM]
             for g in range(GQA_GROUP)], axis=0)
        nt = (((1,), (1,)), ((), ()))
        sp = jnp.where(prev_ok, lax.dot_general(q4, kp_ref[0, :, hs], nt, preferred_element_type=F32), neg)
        sc = lax.dot_general(q4, kc_ref[0, :, hs], nt, preferred_element_type=F32)
        sn = jnp.where(next_ok, lax.dot_general(q4, kn_ref[0, :, hs], nt, preferred_element_type=F32), neg)
        sx = lax.dot_general(q4, kx_ref[0, :, hs], nt, preferred_element_type=F32)
        sink = jnp.zeros((rows, 1), F32)
        for g in range(GQA_GROUP):
            sink = jnp.where(grp == g, sink_ref[h * GQA_GROUP + g], sink)
        lane_chunks = lambda a: [a[:, c * BLOCK:(c + 1) * BLOCK] for c in range(a.shape[1] // BLOCK)]
        fold = lambda op, parts: functools.reduce(op, parts)
        m = jnp.max(fold(jnp.maximum, [sp, sc, sn] + lane_chunks(sx)), -1, keepdims=True)
        m = jnp.maximum(m, sink)
        pp = jnp.exp(sp - m)
        pc = jnp.exp(sc - m)
        pn = jnp.exp(sn - m)
        px = jnp.exp(sx - m)
        denom = jnp.sum(fold(jnp.add, [pp, pc, pn] + lane_chunks(px)), -1, keepdims=True) + jnp.exp(sink - m)
        acc = jnp.dot(pp.astype(BF16), vp_ref[0, :, hs], preferred_element_type=F32)
        acc += jnp.dot(pc.astype(BF16), vc_ref[0, :, hs], preferred_element_type=F32)
        acc += jnp.dot(pn.astype(BF16), vn_ref[0, :, hs], preferred_element_type=F32)
        acc += jnp.dot(px.astype(BF16), vx_ref[0, :, hs], preferred_element_type=F32)
        o = acc / denom
        for g in range(GQA_GROUP):
            c0 = (h * GQA_GROUP + g) * HEAD_DIM
            o_ref[0, :, c0:c0 + HEAD_DIM] = o[g * BLOCK:(g + 1) * BLOCK]


def _attention(sink, q, k, v, L, C):
    B = q.shape[0]
    nb = L // BLOCK
    off = C // BLOCK
    cur = lambda b, n: (b, n + off, 0)
    prv = lambda b, n: (b, jnp.maximum(n - 1, 0) + off, 0)
    nxt = lambda b, n: (b, jnp.minimum(n + 1, nb - 1) + off, 0)
    kv = lambda im: pl.BlockSpec((1, BLOCK, KV_W), im)
    cx = pl.BlockSpec((1, C, KV_W), lambda b, n: (b, 0, 0))
    return pl.pallas_call(
        _attn_kernel,
        grid=(B, nb),
        in_specs=[
            pl.BlockSpec(memory_space=pltpu.SMEM),
            pl.BlockSpec((1, BLOCK, ATTN_W), cur),
            kv(prv), kv(cur), kv(nxt), kv(prv), kv(cur), kv(nxt), cx, cx,
        ],
        out_specs=pl.BlockSpec((1, BLOCK, ATTN_W), lambda b, n: (b, n, 0)),
        out_shape=jax.ShapeDtypeStruct((B, L, ATTN_W), F32),
        compiler_params=_params(("arbitrary", "arbitrary")),
        name="attn",
    )(sink, q, k, k, k, v, v, v, k, v)


def _lru_coeffs(x_ref, p_ref, n_ref, tile, d, cw_ref, cb_ref, w_ref, bias_ref, lam_ref,
                ext_scr, a_scr, b_scr, n_lat_tiles):
    tm = ROW_TILE
    prev_ok = tile >= 2
    next_ok = (tile >= 1) & (tile < n_lat_tiles)
    ext_scr[0:SUBLANES] = jnp.where(prev_ok, p_ref[0], 0.0)
    ext_scr[SUBLANES:SUBLANES + tm] = x_ref[0]
    ext_scr[SUBLANES + tm:2 * SUBLANES + tm] = jnp.where(next_ok, n_ref[0], 0.0)
    xc = cb_ref[...] + cw_ref[2:3] * x_ref[0]
    xc += cw_ref[0:1] * ext_scr[SUBLANES - 2:SUBLANES - 2 + tm]
    xc += cw_ref[1:2] * ext_scr[SUBLANES - 1:SUBLANES - 1 + tm]
    xc += cw_ref[3:4] * ext_scr[SUBLANES + 1:SUBLANES + 1 + tm]
    xcb = xc.astype(BF16)
    lam = lam_ref[d:d + 1]
    sp = jnp.maximum(-lam, 0.0) + jnp.log1p(jnp.exp(-jnp.abs(lam)))
    for n in range(LRU_BLOCKS):
        cs = slice(n * LRU_BLOCK_W, (n + 1) * LRU_BLOCK_W)
        z = jnp.dot(xcb[:, cs], w_ref[d, n], preferred_element_type=F32)
        r = _sigmoid(z[:, :LRU_BLOCK_W] + bias_ref[d, 0:1, cs])
        i = _sigmoid(z[:, LRU_BLOCK_W:] + bias_ref[d, 1:2, cs])
        log_a = -LRU_C * r * sp[:, cs]
        a = jnp.exp(log_a)
        a_scr[:, cs] = a
        v = 1.0 - a * a
        b_scr[:, cs] = v * lax.rsqrt(jnp.maximum(v, SQRT_GUARD)) * (i * xc[:, cs])


def _lru_scan(a_scr, b_scr, h_ref, state_ref, reset, reverse):
    groups = ROW_TILE // SUBLANES
    row = lax.broadcasted_iota(jnp.int32, (SUBLANES, LRU_W), 0)
    carry0 = jnp.where(reset, 0.0, state_ref[...])

    def body(g, carry):
        gi = (groups - 1 - g) if reverse else g
        r0 = pl.multiple_of(gi * SUBLANES, SUBLANES)
        A = a_scr[pl.ds(r0, SUBLANES), :]
        Bv = b_scr[pl.ds(r0, SUBLANES), :]
        for s in (1, 2, 4):
            if reverse:
                sh, m = SUBLANES - s, row < SUBLANES - s
            else:
                sh, m = s, row >= s
            A_sh = pltpu.roll(A, sh, 0)
            B_sh = pltpu.roll(Bv, sh, 0)
            Bv = jnp.where(m, A * B_sh + Bv, Bv)
            A = jnp.where(m, A * A_sh, A)
        h = Bv + A * carry
        h_ref[0, pl.ds(r0, SUBLANES), :] = h
        last = h[0:1] if reverse else h[SUBLANES - 1:SUBLANES]
        return jnp.broadcast_to(last, (SUBLANES, LRU_W))

    state_ref[...] = lax.fori_loop(0, groups, body, carry0)


def _lru_kernel(xf_ref, xfp_ref, xfn_ref, xb_ref, xbp_ref, xbn_ref, cw_ref, cb_ref, w_ref, bias_ref, lam_ref,
                hf_ref, hb_ref, sf_ref, sb_ref, ext_scr, a_scr, b_scr, *, n_lat_tiles):
    t = pl.program_id(1)
    reset = t == 0
    bt = jnp.where(t == 0, 0, n_lat_tiles + 1 - t)
    _lru_coeffs(xf_ref, xfp_ref, xfn_ref, t, 0, cw_ref, cb_ref, w_ref, bias_ref, lam_ref,
                ext_scr, a_scr, b_scr, n_lat_tiles)
    _lru_scan(a_scr, b_scr, hf_ref, sf_ref, reset, False)
    _lru_coeffs(xb_ref, xbp_ref, xbn_ref, bt, 1, cw_ref, cb_ref, w_ref, bias_ref, lam_ref,
                ext_scr, a_scr, b_scr, n_lat_tiles)
    _lru_scan(a_scr, b_scr, hb_ref, sb_ref, reset, True)


def _lru(xr, conv_w, conv_b, w_gates, bias, lam, L):
    B, rows, _ = xr.shape
    nl = L // ROW_TILE
    nt = nl + 1
    per = ROW_TILE // SUBLANES
    n8 = rows // SUBLANES
    ft = lambda b, t: t
    btile = lambda b, t: jnp.where(t == 0, 0, nl + 1 - t)
    main = lambda f: pl.BlockSpec((1, ROW_TILE, LRU_W), lambda b, t: (b, f(b, t), 0))
    prev = lambda f: pl.BlockSpec((1, SUBLANES, LRU_W), lambda b, t: (b, jnp.maximum(f(b, t) * per - 1, 0), 0))
    nxt = lambda f: pl.BlockSpec((1, SUBLANES, LRU_W),
                                 lambda b, t: (b, jnp.minimum((f(b, t) + 1) * per, n8 - 1), 0))
    full = lambda a: pl.BlockSpec(a.shape, lambda b, t: (0,) * a.ndim)
    return pl.pallas_call(
        functools.partial(_lru_kernel, n_lat_tiles=nl),
        grid=(B, nt),
        in_specs=[main(ft), prev(ft), nxt(ft), main(btile), prev(btile), nxt(btile),
                  full(conv_w), full(conv_b), full(w_gates), full(bias), full(lam)],
        out_specs=[
            pl.BlockSpec((1, ROW_TILE, LRU_W), lambda b, t: (b, jnp.maximum(t - 1, 0), 0)),
            pl.BlockSpec((1, ROW_TILE, LRU_W), lambda b, t: (b, nl - jnp.maximum(t, 1), 0)),
        ],
        out_shape=[jax.ShapeDtypeStruct((B, L, LRU_W), F32)] * 2,
        scratch_shapes=[
            pltpu.VMEM((SUBLANES, LRU_W), F32), pltpu.VMEM((SUBLANES, LRU_W), F32),
            pltpu.VMEM((ROW_TILE + 2 * SUBLANES, LRU_W), F32),
            pltpu.VMEM((ROW_TILE, LRU_W), F32), pltpu.VMEM((ROW_TILE, LRU_W), F32),
        ],
        compiler_params=_params(("arbitrary", "arbitrary")),
        name="lru",
    )(xr, xr, xr, xr, xr, xr, conv_w, conv_b, w_gates, bias, lam)


def _route(scores, sel):
    tm = scores.shape[1]
    neg = -jnp.inf
    iota_g = lax.broadcasted_iota(jnp.int32, (GROUP_SIZE, tm), 0)
    grp_score = []
    for g in range(N_GROUPS):
        sg = sel[g * GROUP_SIZE:(g + 1) * GROUP_SIZE]
        m1 = jnp.max(sg, axis=0, keepdims=True)
        first = jnp.min(jnp.where(sg == m1, iota_g, GROUP_SIZE), axis=0, keepdims=True)
        m2 = jnp.max(jnp.where(iota_g == first, neg, sg), axis=0, keepdims=True)
        grp_score.append(m1 + m2)
    masked = []
    for g in range(N_GROUPS):
        rank = jnp.zeros((1, tm), jnp.int32)
        for o in range(N_GROUPS):
            if o == g:
                continue
            ahead = (grp_score[o] > grp_score[g]) if o > g else (grp_score[o] >= grp_score[g])
            rank += ahead.astype(jnp.int32)
        keep = rank < TOPK_GROUPS
        masked.append(jnp.where(keep, sel[g * GROUP_SIZE:(g + 1) * GROUP_SIZE], neg))
    cand = jnp.concatenate(masked, axis=0)
    iota_e = lax.broadcasted_iota(jnp.int32, (N_EXPERTS, tm), 0)
    chosen = jnp.zeros((N_EXPERTS, tm), jnp.bool_)
    picks = []
    for _ in range(TOP_K):
        cur = jnp.where(chosen, neg, cand)
        m = jnp.max(cur, axis=0, keepdims=True)
        idx = jnp.min(jnp.where((cur == m) & jnp.logical_not(chosen), iota_e, N_EXPERTS), axis=0, keepdims=True)
        chosen = chosen | (iota_e == idx)
        picks.append(idx)
    w = jnp.where(chosen, scores, 0.0)
    gates = w / jnp.sum(w, axis=0, keepdims=True) * ROUTED_SCALE
    return gates, chosen, picks


def _tok_scratch(rows):
    return pltpu.VMEM((rows, TOK_PITCH, LANES), F32)


def _to_token_major(y, scr):
    rows = scr.shape[0]
    flat = scr.reshape(rows * TOK_PITCH, LANES)
    for s in range(TOK_ROWS):
        flat[pl.ds(s, rows, stride=TOK_PITCH), :] = y[:, s * LANES:(s + 1) * LANES]


def _from_token_major(scr):
    rows = scr.shape[0]
    flat = scr.reshape(rows * TOK_PITCH, LANES)
    return [flat[pl.ds(s, rows, stride=TOK_PITCH), :] for s in range(TOK_ROWS)]


def _token_tiles(scr):
    return scr.at[:, 0:TOK_ROWS, :]


def _merge_kernel(attn_ref, hf_ref, hb_ref, yg_ref, x_ref, mod_ref, wout_ref, ga_ref, gl_ref, lg_ref, lb_ref,
                  rwh_ref, rwl_ref, rb_ref, tri_ref, xn_ref, u2_ref, up_ref, eidx_ref, wk_ref, pos_ref, cnt_ref,
                  carry_ref, tok_scr):
    b = pl.program_id(0)

    @pl.when((b == 0) & (pl.program_id(1) == 0))
    def _():
        carry_ref[...] = jnp.zeros_like(carry_ref)

    D = D_MODEL
    mrow = lambda i: mod_ref[pl.ds(b, 1), i * D:(i + 1) * D]
    nt = (((1,), (1,)), ((), ()))
    u2_parts, logit_parts = [], []
    sub = ROW_TILE // MERGE_SPLIT
    for p in range(MERGE_SPLIT):
        rs = slice(p * sub, (p + 1) * sub)
        lru_y = (hf_ref[0, rs] + hb_ref[0, rs]) * jax.nn.gelu(yg_ref[0, rs])
        na = _rms_norm(attn_ref[0, rs], ga_ref[...]).astype(BF16)
        nl = _rms_norm(lru_y, gl_ref[...]).astype(BF16)
        y1 = jnp.dot(na, wout_ref[0:ATTN_W], preferred_element_type=F32)
        y1 += jnp.dot(nl, wout_ref[ATTN_W:D], preferred_element_type=F32)
        xn = _layer_norm(DEEPNORM_ALPHA * x_ref[0, rs] + mrow(2) * y1) * lg_ref[...] + lb_ref[...]
        xn_ref[0, rs] = xn
        u2_p = _layer_norm(xn) * (1.0 + mrow(4)) + mrow(3)
        u_hi, u_lo = _split_bf16(u2_p)
        u2_ref[0, rs] = u_hi
        lg_p = lax.dot_general(rwh_ref[...], u_hi, nt, preferred_element_type=F32)
        lg_p += lax.dot_general(rwh_ref[...], u_lo, nt, preferred_element_type=F32)
        lg_p += lax.dot_general(rwl_ref[...], u_hi, nt, preferred_element_type=F32)
        u2_parts.append(u2_p)
        logit_parts.append(lg_p)
    u2 = jnp.concatenate(u2_parts, axis=0)
    logits = jnp.concatenate(logit_parts, axis=1)
    _to_token_major(u2, tok_scr)
    up_ref[...] = _token_tiles(tok_scr)[...].astype(BF16)
    scores = jax.nn.sigmoid(logits)
    gates, chosen, picks = _route(scores, scores + rb_ref[...])
    sel01 = jnp.where(chosen, 1.0, 0.0)
    incl = jnp.dot(sel01.astype(BF16), tri_ref[...], preferred_element_type=F32)
    rank = carry_ref[:, 0:1] + incl - sel01
    carry_ref[...] = carry_ref[...] + incl[:, ROW_TILE - 1:ROW_TILE]
    cnt_ref[...] = carry_ref[...]
    iota_e = lax.broadcasted_iota(jnp.int32, (N_EXPERTS, ROW_TILE), 0)
    for kk, idx in enumerate(picks):
        hit = iota_e == idx
        eidx_ref[kk:kk + 1, :] = idx
        wk_ref[kk:kk + 1, :] = jnp.sum(jnp.where(hit, gates, 0.0), axis=0, keepdims=True)
        pos_ref[kk:kk + 1, :] = jnp.sum(jnp.where(hit, rank, 0.0), axis=0, keepdims=True).astype(jnp.int32)


def _merge(attn, hf, hb, yg, x, mod, w_out_bf, ga, gl, lg, lb, rw_hi, rw_lo, rbias, tri, C):
    B, L, _ = x.shape
    T = B * L
    nt = L // ROW_TILE
    off = C // ROW_TILE
    row = lambda w, o=0: pl.BlockSpec((1, ROW_TILE, w), lambda b, t: (b, t + o, 0))
    full = lambda a: pl.BlockSpec(a.shape, lambda b, t: (0,) * a.ndim)
    tok = pl.BlockSpec((TOP_K, ROW_TILE), lambda b, t: (0, b * nt + t))
    return pl.pallas_call(
        _merge_kernel,
        grid=(B, nt),
        in_specs=[row(ATTN_W), row(LRU_W), row(LRU_W), row(LRU_W, off), row(D_MODEL),
                  full(mod), full(w_out_bf), full(ga), full(gl), full(lg), full(lb),
                  full(rw_hi), full(rw_lo), full(rbias), full(tri)],
        out_specs=[row(D_MODEL), row(D_MODEL),
                   pl.BlockSpec((ROW_TILE, TOK_ROWS, LANES), lambda b, t: (b * nt + t, 0, 0)),
                   tok, tok, tok,
                   pl.BlockSpec((N_EXPERTS, LANES), lambda b, t: (0, 0))],
        out_shape=[jax.ShapeDtypeStruct((B, L, D_MODEL), F32),
                   jax.ShapeDtypeStruct((B, L, D_MODEL), BF16),
                   jax.ShapeDtypeStruct((T, TOK_ROWS, LANES), BF16),
                   jax.ShapeDtypeStruct((TOP_K, T), jnp.int32),
                   jax.ShapeDtypeStruct((TOP_K, T), F32),
                   jax.ShapeDtypeStruct((TOP_K, T), jnp.int32),
                   jax.ShapeDtypeStruct((N_EXPERTS, LANES), F32)],
        scratch_shapes=[pltpu.VMEM((N_EXPERTS, LANES), F32), _tok_scratch(ROW_TILE)],
        compiler_params=_params(("arbitrary", "arbitrary")),
        name="merge",
    )(attn, hf, hb, yg, x, mod, w_out_bf, ga, gl, lg, lb, rw_hi, rw_lo, rbias, tri)


def _swiglu(u, wg, wu):
    hg = jnp.dot(u, wg, preferred_element_type=F32)
    hu = jnp.dot(u, wu, preferred_element_type=F32)
    return hg * jax.nn.sigmoid(hg) * hu


def _dispatch_kernel(zt_ref, dest_ref, up_ref, xs_ref, zero_scr, sem, zsem):
    @pl.when(pl.program_id(0) == 0)
    def _():
        zero_scr[...] = jnp.zeros_like(zero_scr)
        n_tiles = zt_ref.shape[0]

        def zero_tile(z):
            return pltpu.make_async_copy(zero_scr, xs_ref.at[pl.ds(z * EXP_TR, EXP_TR)], zsem)

        def start(z, carry):
            @pl.when(zt_ref[z] != 0)
            def _():
                zero_tile(z).start()
            return carry

        def wait(z, carry):
            @pl.when(zt_ref[z] != 0)
            def _():
                zero_tile(z).wait()
            return carry

        lax.fori_loop(0, n_tiles, start, 0)
        lax.fori_loop(0, n_tiles, wait, 0)

    def issue(j, carry):
        for k in range(TOP_K):
            pltpu.make_async_copy(up_ref.at[j], xs_ref.at[dest_ref[j * TOP_K + k]], sem).start(priority=k % 2)
        return carry

    lax.fori_loop(0, ROW_TILE, issue, 0)
    for k in range(TOP_K):
        pltpu.make_async_copy(up_ref, xs_ref.at[pl.ds(0, ROW_TILE)], sem).wait()


def _dispatch(zero_tiles, slots, up, n_sorted_rows):
    T = up.shape[0]
    return pl.pallas_call(
        _dispatch_kernel,
        grid_spec=pltpu.PrefetchScalarGridSpec(
            num_scalar_prefetch=1,
            grid=(T // ROW_TILE,),
            in_specs=[pl.BlockSpec((ROW_TILE * TOP_K,), lambda i, zt: (i,), memory_space=pltpu.SMEM),
                      pl.BlockSpec((ROW_TILE, TOK_ROWS, LANES), lambda i, zt: (i, 0, 0))],
            out_specs=pl.BlockSpec(memory_space=pl.ANY),
            scratch_shapes=[pltpu.VMEM((EXP_TR, TOK_ROWS, LANES), BF16),
                            pltpu.SemaphoreType.DMA(()), pltpu.SemaphoreType.DMA(())],
        ),
        out_shape=jax.ShapeDtypeStruct((n_sorted_rows, TOK_ROWS, LANES), BF16),
        compiler_params=_params(("arbitrary",)),
        name="dispatch",
    )(zero_tiles, slots, up)


def _expert_kernel(te_ref, nu_ref, nxt_ref, par_ref, xs_ref, wg_hbm, wu_hbm, wd_hbm, ys_ref,
                   wg32, wu32, wd32, wgb_ref, wub_ref, wdb_ref, tok_scr, wsem):
    i = pl.program_id(0)
    n_used = nu_ref[0]

    def weight_copies(e, s):
        return [pltpu.make_async_copy(hbm.at[e], stage.at[s], wsem.at[s])
                for hbm, stage in ((wg_hbm, wg32), (wu_hbm, wu32), (wd_hbm, wd32))]

    @pl.when(i == 0)
    def _():
        for cp in weight_copies(te_ref[0], 0):
            cp.start()

    @pl.when(i < n_used)
    def _():
        e = te_ref[i]

        first_of_expert = (i == 0) | (e != te_ref[jnp.maximum(i - 1, 0)])
        for s in range(2):
            @pl.when(first_of_expert & (par_ref[i] == s))
            def _():
                for cp in weight_copies(e, s):
                    cp.wait()
                for src, dst in ((wg32, wgb_ref), (wu32, wub_ref), (wd32, wdb_ref)):
                    rows = dst.shape[0] // CAST_CHUNKS

                    def cast_chunk(r, carry, src=src, dst=dst, rows=rows):
                        sl = pl.ds(pl.multiple_of(r * rows, rows), rows)
                        dst[sl, :] = src[s, sl, :].astype(BF16)
                        return carry

                    lax.fori_loop(0, CAST_CHUNKS, cast_chunk, 0)

                @pl.when(nxt_ref[i] >= 0)
                def _():
                    for cp in weight_copies(nxt_ref[i], 1 - s):
                        cp.start()

        _token_tiles(tok_scr)[...] = xs_ref[...].astype(F32)
        x = jnp.concatenate([c.astype(BF16) for c in _from_token_major(tok_scr)], axis=1)
        h = _swiglu(x, wgb_ref[...], wub_ref[...])
        y = jnp.dot(h.astype(BF16), wdb_ref[...], preferred_element_type=F32)
        _to_token_major(y, tok_scr)
        ys_ref[...] = _token_tiles(tok_scr)[...].astype(BF16)

    @pl.when(i >= n_used)
    def _():
        ys_ref[...] = jnp.zeros_like(ys_ref)


def _experts(tile_expert, n_used, next_expert, run_parity, xs, wg, wu, wd):
    n_tiles = xs.shape[0] // EXP_TR
    blk = (EXP_TR, TOK_ROWS, LANES)
    stage = lambda a: pltpu.VMEM((2,) + a.shape[1:], F32)
    cast = lambda a: pltpu.VMEM(a.shape[1:], BF16)
    hbm = pl.BlockSpec(memory_space=pl.ANY)
    return pl.pallas_call(
        _expert_kernel,
        grid_spec=pltpu.PrefetchScalarGridSpec(
            num_scalar_prefetch=4,
            grid=(n_tiles,),
            in_specs=[pl.BlockSpec(blk, lambda i, te, nu, nx, pr: (jnp.minimum(i, nu[0] - 1), 0, 0)),
                      hbm, hbm, hbm],
            out_specs=pl.BlockSpec(blk, lambda i, te, nu, nx, pr: (i, 0, 0)),
            scratch_shapes=[stage(wg), stage(wu), stage(wd), cast(wg), cast(wu), cast(wd),
                            _tok_scratch(EXP_TR), pltpu.SemaphoreType.DMA((2,))],
        ),
        out_shape=jax.ShapeDtypeStruct(xs.shape, BF16),
        compiler_params=_params(("arbitrary",)),
        name="experts",
    )(tile_expert, n_used, next_expert, run_parity, xs, wg, wu, wd)


def _combine_kernel(dcur_ref, dnext_ref, wv_ref, u_ref, xn_ref, g2_ref, sg_ref, su_ref, sd_ref, lg_ref, lb_ref,
                    ys_ref, o_ref, buf_ref, y_scr, sem):
    i = pl.program_id(0)
    n = pl.num_programs(0)
    tm = COMB_TM
    slot = i % 2

    def gather(d_ref, sl):
        def issue(j, carry):
            for k in range(TOP_K):
                pltpu.make_async_copy(ys_ref.at[d_ref[j * TOP_K + k]], buf_ref.at[sl, k, j],
                                      sem.at[sl]).start(priority=k % 2)
            return carry
        lax.fori_loop(0, tm, issue, 0)

    @pl.when(i == 0)
    def _():
        gather(dcur_ref, 0)

    @pl.when(i + 1 < n)
    def _():
        gather(dnext_ref, 1 - slot)

    for k in range(TOP_K):
        pltpu.make_async_copy(ys_ref.at[pl.ds(0, tm)], buf_ref.at[slot, k], sem.at[slot]).wait()

    half = TOK_ROWS // 2

    def accumulate(jo, carry):
        for u in range(COMB_UNROLL):
            j = jo * COMB_UNROLL + u
            lo = jnp.zeros((half, LANES), F32)
            hi = jnp.zeros((half, LANES), F32)
            for k in range(TOP_K):
                gate = wv_ref[pl.ds(j * TOP_K + k, half, stride=0), :]
                rows = buf_ref[slot, k, j].astype(F32)
                lo += gate * rows[:half]
                hi += gate * rows[half:]
            y_scr[j, 0:half, :] = lo
            y_scr[j, half:TOK_ROWS, :] = hi
        return carry

    lax.fori_loop(0, tm // COMB_UNROLL, accumulate, 0)
    y_routed = jnp.concatenate(_from_token_major(y_scr), axis=1)

    hs = _swiglu(u_ref[...], sg_ref[...], su_ref[...])
    y2 = jnp.dot(hs.astype(BF16), sd_ref[...], preferred_element_type=F32) + y_routed
    z = DEEPNORM_ALPHA * xn_ref[...] + g2_ref[0] * y2
    o_ref[...] = _layer_norm(z) * lg_ref[...] + lb_ref[...]


def _combine(slots, gates, u2, xn, g2, sg, su, sd, lg, lb, ys, tiles_per_batch):
    T = u2.shape[0]
    tm = COMB_TM
    n = T // tm
    row = pl.BlockSpec((tm, D_MODEL), lambda i: (i, 0))
    full = lambda a: pl.BlockSpec(a.shape, lambda i: (0,) * a.ndim)
    tok = lambda f: pl.BlockSpec((tm * TOP_K,), lambda i: (f(i),), memory_space=pltpu.SMEM)
    return pl.pallas_call(
        _combine_kernel,
        grid=(n,),
        in_specs=[tok(lambda i: i), tok(lambda i: jnp.minimum(i + 1, n - 1)),
                  pl.BlockSpec((tm * TOP_K, LANES), lambda i: (i, 0)), row, row,
                  pl.BlockSpec((1, 1, D_MODEL), lambda i: (i // tiles_per_batch, 0, 0)),
                  full(sg), full(su), full(sd), full(lg), full(lb),
                  pl.BlockSpec(memory_space=pl.ANY)],
        out_specs=row,
        out_shape=jax.ShapeDtypeStruct((T, D_MODEL), F32),
        scratch_shapes=[pltpu.VMEM((2, TOP_K, tm, TOK_ROWS, LANES), BF16),
                        _tok_scratch(tm),
                        pltpu.SemaphoreType.DMA((2,))],
        compiler_params=_params(("arbitrary",)),
        name="combine",
    )(slots, slots, gates, u2, xn, g2, sg, su, sd, lg, lb, ys)


def _rope_tables(L, C):
    rows = L // GRID_W
    row = jnp.repeat(jnp.arange(rows, dtype=F32), GRID_W)
    col = jnp.tile(jnp.arange(GRID_W, dtype=F32), rows)
    inv_freq = jnp.exp(-jnp.log(ROPE_THETA) * jnp.arange(0, ROPE_AXIS_DIM, 2, dtype=F32) / ROPE_AXIS_DIM)
    ang_r = row[:, None] * inv_freq
    ang_c = col[:, None] * inv_freq
    cr, sr, cc, sc = jnp.cos(ang_r), jnp.sin(ang_r), jnp.cos(ang_c), jnp.sin(ang_c)
    cos_t = jnp.concatenate([cr, cr, cc, cc], axis=-1)
    sin_t = jnp.concatenate([-sr, sr, -sc, sc], axis=-1)
    cos_t = jnp.concatenate([jnp.ones((C, HEAD_DIM), F32), cos_t], axis=0)
    sin_t = jnp.concatenate([jnp.zeros((C, HEAD_DIM), F32), sin_t], axis=0)
    return cos_t, sin_t


def kernel(x, c, ctx, c_ctx, w_mod, b_mod, w_in, attn_sink, conv_w, conv_b, lru_wa, lru_ba, lru_wx, lru_bx,
           lru_lam, norm_attn_g, norm_lru_g, w_out, ln1_g, ln1_b, router_w, router_bias, exp_w_gate, exp_w_up,
           exp_w_down, sh_w_gate, sh_w_up, sh_w_down, ln2_g, ln2_b):
    B, L, D = x.shape
    C = ctx.shape[1]
    assert w_mod.shape[0] == DEPTH and D == D_MODEL and B + 1 <= SUBLANES
    row2 = lambda a: a.reshape(1, -1)

    cvec = jnp.concatenate([c, c_ctx[None], jnp.zeros((SUBLANES - B - 1, D), F32)], axis=0)
    mod = _mod(cvec, w_mod[0], row2(b_mod[0]))

    cos_t, sin_t = _rope_tables(L, C)
    q, k, v, xr, yg = _inproj(x, ctx, mod, w_in[0].astype(BF16), cos_t, sin_t)

    attn = _attention(attn_sink[0], q, k, v, L, C)

    w_gates = jnp.concatenate([lru_wa[0], lru_wx[0]], axis=-1).astype(BF16)
    bias = jnp.stack([lru_ba[0], lru_bx[0]], axis=1)
    hf, hb = _lru(xr, conv_w[0], row2(conv_b[0]), w_gates, bias, lru_lam[0], L)

    rw_hi, rw_lo = _split_bf16(router_w[0].T)
    tri = jnp.triu(jnp.ones((ROW_TILE, ROW_TILE), BF16))
    xn, u2, up, eidx, wk, pos, cnt = _merge(
        attn, hf, hb, yg, x, mod, w_out[0].astype(BF16), row2(norm_attn_g[0]), row2(norm_lru_g[0]),
        row2(ln1_g[0]), row2(ln1_b[0]), rw_hi, rw_lo, router_bias[0].reshape(-1, 1), tri, C)

    T = B * L
    n_tiles = T * TOP_K // EXP_TR + N_EXPERTS
    counts = cnt[:, 0].astype(jnp.int32)
    tiles_e = (counts + EXP_TR - 1) // EXP_TR
    tile_end = jnp.cumsum(tiles_e)
    n_used = tile_end[-1:]
    row_start = (tile_end - tiles_e) * EXP_TR
    experts = jnp.arange(N_EXPERTS, dtype=jnp.int32)
    start_of = jnp.sum(jnp.where(eidx[None] == experts[:, None, None], row_start[:, None, None], 0), axis=0)
    slots = (start_of + pos).T.reshape(-1)
    gates = jnp.broadcast_to(wk.T.reshape(-1, 1), (T * TOP_K, LANES))
    tile_id = jnp.minimum(jnp.arange(n_tiles, dtype=jnp.int32), n_used - 1)
    tile_expert = jnp.sum((tile_end[None, :] <= tile_id[:, None]).astype(jnp.int32), axis=1)

    all_tiles = jnp.arange(n_tiles, dtype=jnp.int32)
    ends_run = jnp.any((tile_end[None, :] - 1 == all_tiles[:, None]) & (tiles_e[None, :] > 0), axis=1)
    zero_tiles = (ends_run | (all_tiles >= n_used)).astype(jnp.int32)
    xs = _dispatch(zero_tiles, slots, up, n_tiles * EXP_TR)
    has_rows = tiles_e > 0
    run_parity_e = (jnp.cumsum(has_rows) - has_rows) % 2
    later = has_rows[None, :] & (experts[None, :] > experts[:, None])
    next_e = jnp.min(jnp.where(later, experts[None, :], N_EXPERTS), axis=1)
    next_e = jnp.where(next_e == N_EXPERTS, -1, next_e)
    of_tile = tile_expert[:, None] == experts[None, :]
    per_tile = lambda v: jnp.sum(jnp.where(of_tile, v[None, :], 0), axis=1).astype(jnp.int32)
    ys = _experts(tile_expert, n_used, per_tile(next_e), per_tile(run_parity_e), xs,
                  exp_w_gate[0], exp_w_up[0], exp_w_down[0])
    g2 = mod[:B, 5 * D:6 * D].reshape(B, 1, D)
    out = _combine(slots, gates, u2.reshape(T, D), xn.reshape(T, D), g2,
                   sh_w_gate[0].astype(BF16), sh_w_up[0].astype(BF16), sh_w_down[0].astype(BF16),
                   row2(ln2_g[0]), row2(ln2_b[0]), ys, L // COMB_TM)
    return out.reshape(B, L, D)
```

```python
import functools

import jax
import jax.numpy as jnp
import numpy as np
from jax import lax
from jax.experimental import pallas as pl
from jax.experimental.pallas import tpu as pltpu

F32 = jnp.float32
BF16 = jnp.bfloat16

D_MODEL = 2048
GRID_W = 64
N_HEADS = 8
N_KV_HEADS = 2
HEAD_DIM = 128
GQA_GROUP = N_HEADS // N_KV_HEADS
ATTN_W = N_HEADS * HEAD_DIM
KV_W = N_KV_HEADS * HEAD_DIM
ATTN_SCALE = HEAD_DIM ** -0.5
BLOCK = 128
ROPE_THETA = 10000.0
ROPE_AXIS_DIM = HEAD_DIM // 2
LRU_W = D_MODEL - ATTN_W
LRU_BLOCKS = 8
LRU_BLOCK_W = LRU_W // LRU_BLOCKS
LRU_C = 8.0
N_EXPERTS = 64
TOP_K = 8
N_GROUPS = 8
GROUP_SIZE = N_EXPERTS // N_GROUPS
TOPK_GROUPS = 4
EXPERT_FF = 512
ROUTED_SCALE = 2.5
LN_EPS = 1e-6
SQRT_GUARD = 1e-30
DEPTH = 1
DEEPNORM_ALPHA = (2 * DEPTH) ** 0.25

SUBLANES = 8
LANES = 128
VMEM_LIMIT = 56 * 1024 * 1024

ROW_TILE = 256
MOD_TN = 1024
MERGE_SPLIT = 2
TOK_ROWS = D_MODEL // LANES
TOK_PITCH = TOK_ROWS + SUBLANES
EXP_TR = 256
CAST_CHUNKS = 16
COMB_TM = 256
COMB_UNROLL = 4


def _params(sem, vmem=VMEM_LIMIT):
    return pltpu.CompilerParams(dimension_semantics=sem, vmem_limit_bytes=vmem)


def _layer_norm(x):
    mu = jnp.mean(x, axis=-1, keepdims=True)
    xc = x - mu
    var = jnp.mean(xc * xc, axis=-1, keepdims=True)
    return xc * lax.rsqrt(var + LN_EPS)


def _rms_norm(x, g):
    return x * lax.rsqrt(jnp.mean(x * x, axis=-1, keepdims=True) + LN_EPS) * g


def _sigmoid(x):
    return 0.5 * jnp.tanh(0.5 * x) + 0.5


def _split_bf16(x):
    hi = x.astype(BF16)
    lo = (x - hi.astype(F32)).astype(BF16)
    return hi, lo


def _mod_kernel(c_ref, w_ref, b_ref, o_ref):
    cv = c_ref[...]
    s = cv * jax.nn.sigmoid(cv)
    hi, lo = _split_bf16(s)
    lhs = jnp.concatenate([hi, lo], axis=0)
    r = jnp.dot(lhs, w_ref[...].astype(BF16), preferred_element_type=F32)
    o_ref[...] = r[:SUBLANES] + r[SUBLANES:] + b_ref[...]


def _mod(cvec, w_mod, b_mod):
    n = w_mod.shape[1]
    return pl.pallas_call(
        _mod_kernel,
        grid=(n // MOD_TN,),
        in_specs=[
            pl.BlockSpec((SUBLANES, D_MODEL), lambda j: (0, 0)),
            pl.BlockSpec((D_MODEL, MOD_TN), lambda j: (0, j)),
            pl.BlockSpec((1, MOD_TN), lambda j: (0, j)),
        ],
        out_specs=pl.BlockSpec((SUBLANES, MOD_TN), lambda j: (0, j)),
        out_shape=jax.ShapeDtypeStruct((SUBLANES, n), F32),
        compiler_params=_params(("arbitrary",)),
        name="mod",
    )(cvec, w_mod, b_mod)


def _rope(xh, cos, sin, even_block):
    partner = jnp.where(even_block, pltpu.roll(xh, 96, 1), pltpu.roll(xh, 32, 1))
    return xh * cos + partner * sin


def _inproj_kernel(x_ref, ctx_ref, mod_ref, w_ref, cos_ref, sin_ref,
                   q_ref, k_ref, v_ref, xr_ref, yg_ref):
    b = pl.program_id(0)
    t = pl.program_id(1)
    is_ctx = t == 0
    xin = jnp.where(is_ctx, ctx_ref[0], x_ref[0])
    r = jnp.where(is_ctx, 2, b)
    shift = mod_ref[pl.ds(r, 1), 0:D_MODEL]
    scale = mod_ref[pl.ds(r, 1), D_MODEL:2 * D_MODEL]
    u = (_layer_norm(xin) * (1.0 + scale) + shift).astype(BF16)

    cos = cos_ref[...]
    sin = sin_ref[...]
    lane = lax.broadcasted_iota(jnp.int32, (ROW_TILE, HEAD_DIM), 1)
    even_block = (lane % 64) < 32

    def proj(c0, c1):
        return jnp.dot(u, w_ref[:, c0:c1], preferred_element_type=F32)

    head = lambda y, h: y[:, h * HEAD_DIM:(h + 1) * HEAD_DIM]
    for p in range(N_HEADS // 2):
        qq = proj(2 * p * HEAD_DIM, (2 * p + 2) * HEAD_DIM)
        for h in range(2):
            q_ref[0, :, (2 * p + h) * HEAD_DIM:(2 * p + h + 1) * HEAD_DIM] = (
                _rope(head(qq, h), cos, sin, even_block) * ATTN_SCALE).astype(BF16)
    kk = proj(ATTN_W, ATTN_W + KV_W)
    for h in range(N_KV_HEADS):
        k_ref[0, :, h * HEAD_DIM:(h + 1) * HEAD_DIM] = _rope(head(kk, h), cos, sin, even_block).astype(BF16)
    v_ref[0] = proj(ATTN_W + KV_W, ATTN_W + 2 * KV_W).astype(BF16)
    c0 = ATTN_W + 2 * KV_W
    xr_ref[0] = proj(c0, c0 + LRU_W)
    yg_ref[0] = proj(c0 + LRU_W, c0 + 2 * LRU_W)


def _inproj(x, ctx, mod, w_in_bf, cos_t, sin_t):
    B, L, _ = x.shape
    C = ctx.shape[1]
    assert C == ROW_TILE and L % ROW_TILE == 0
    nt = L // ROW_TILE + 1
    rows = L + C
    d_in = w_in_bf.shape[1]
    out = lambda w, dt: jax.ShapeDtypeStruct((B, rows, w), dt)
    ospec = lambda w: pl.BlockSpec((1, ROW_TILE, w), lambda b, t: (b, t, 0))
    return pl.pallas_call(
        _inproj_kernel,
        grid=(B, nt),
        in_specs=[
            pl.BlockSpec((1, ROW_TILE, D_MODEL), lambda b, t: (b, jnp.maximum(t - 1, 0), 0)),
            pl.BlockSpec((1, ROW_TILE, D_MODEL), lambda b, t: (b, 0, 0)),
            pl.BlockSpec((SUBLANES, 2 * D_MODEL), lambda b, t: (0, 0)),
            pl.BlockSpec((D_MODEL, d_in), lambda b, t: (0, 0)),
            pl.BlockSpec((ROW_TILE, HEAD_DIM), lambda b, t: (t, 0)),
            pl.BlockSpec((ROW_TILE, HEAD_DIM), lambda b, t: (t, 0)),
        ],
        out_specs=[ospec(ATTN_W), ospec(KV_W), ospec(KV_W), ospec(LRU_W), ospec(LRU_W)],
        out_shape=[out(ATTN_W, BF16), out(KV_W, BF16), out(KV_W, BF16), out(LRU_W, F32), out(LRU_W, F32)],
        compiler_params=_params(("arbitrary", "arbitrary")),
        name="inproj",
    )(x, ctx, mod, w_in_bf, cos_t, sin_t)


def _attn_kernel(sink_ref, q_ref, kp_ref, kc_ref, kn_ref, vp_ref, vc_ref, vn_ref, kx_ref, vx_ref, o_ref):
    n = pl.program_id(1)
    nb = pl.num_programs(1)
    rows = GQA_GROUP * BLOCK
    n_ctx = kx_ref.shape[1]
    qi = lax.broadcasted_iota(jnp.int32, (rows, BLOCK), 0) % BLOCK
    kj = lax.broadcasted_iota(jnp.int32, (rows, BLOCK), 1)
    prev_ok = (kj >= qi) & (n > 0)
    next_ok = (kj <= qi) & (n < nb - 1)
    grp = lax.broadcasted_iota(jnp.int32, (rows, 1), 0) // BLOCK
    neg = -jnp.inf
    for h in range(N_KV_HEADS):
        hs = slice(h * HEAD_DIM, (h + 1) * HEAD_DIM)
        q4 = jnp.concatenate(
            [q_ref[0, :, (h * GQA_GROUP + g) * HEAD_DIM:(h * GQA_GROUP + g + 1) * HEAD_DIM]
             for g in range(GQA_GROUP)], axis=0)
        nt = (((1,), (1,)), ((), ()))
        sp = jnp.where(prev_ok, lax.dot_general(q4, kp_ref[0, :, hs], nt, preferred_element_type=F32), neg)
        sc = lax.dot_general(q4, kc_ref[0, :, hs], nt, preferred_element_type=F32)
        sn = jnp.where(next_ok, lax.dot_general(q4, kn_ref[0, :, hs], nt, preferred_element_type=F32), neg)
        sx = lax.dot_general(q4, kx_ref[0, :, hs], nt, preferred_element_type=F32)
        sink = jnp.zeros((rows, 1), F32)
        for g in range(GQA_GROUP):
            sink = jnp.where(grp == g, sink_ref[h * GQA_GROUP + g], sink)
        lane_chunks = lambda a: [a[:, c * BLOCK:(c + 1) * BLOCK] for c in range(a.shape[1] // BLOCK)]
        fold = lambda op, parts: functools.reduce(op, parts)
        m = jnp.max(fold(jnp.maximum, [sp, sc, sn] + lane_chunks(sx)), -1, keepdims=True)
        m = jnp.maximum(m, sink)
        pp = jnp.exp(sp - m)
        pc = jnp.exp(sc - m)
        pn = jnp.exp(sn - m)
        px = jnp.exp(sx - m)
        denom = jnp.sum(fold(jnp.add, [pp, pc, pn] + lane_chunks(px)), -1, keepdims=True) + jnp.exp(sink - m)
        acc = jnp.dot(pp.astype(BF16), vp_ref[0, :, hs], preferred_element_type=F32)
        acc += jnp.dot(pc.astype(BF16), vc_ref[0, :, hs], preferred_element_type=F32)
        acc += jnp.dot(pn.astype(BF16), vn_ref[0, :, hs], preferred_element_type=F32)
        acc += jnp.dot(px.astype(BF16), vx_ref[0, :, hs], preferred_element_type=F32)
        o = acc / denom
        for g in range(GQA_GROUP):
            c0 = (h * GQA_GROUP + g) * HEAD_DIM
            o_ref[0, :, c0:c0 + HEAD_DIM] = o[g * BLOCK:(g + 1) * BLOCK]


def _attention(sink, q, k, v, L, C):
    B = q.shape[0]
    nb = L // BLOCK
    off = C // BLOCK
    cur = lambda b, n: (b, n + off, 0)
    prv = lambda b, n: (b, jnp.maximum(n - 1, 0) + off, 0)
    nxt = lambda b, n: (b, jnp.minimum(n + 1, nb - 1) + off, 0)
    kv = lambda im: pl.BlockSpec((1, BLOCK, KV_W), im)
    cx = pl.BlockSpec((1, C, KV_W), lambda b, n: (b, 0, 0))
    return pl.pallas_call(
        _attn_kernel,
        grid=(B, nb),
        in_specs=[
            pl.BlockSpec(memory_space=pltpu.SMEM),
            pl.BlockSpec((1, BLOCK, ATTN_W), cur),
            kv(prv), kv(cur), kv(nxt), kv(prv), kv(cur), kv(nxt), cx, cx,
        ],
        out_specs=pl.BlockSpec((1, BLOCK, ATTN_W), lambda b, n: (b, n, 0)),
        out_shape=jax.ShapeDtypeStruct((B, L, ATTN_W), F32),
        compiler_params=_params(("arbitrary", "arbitrary")),
        name="attn",
    )(sink, q, k, k, k, v, v, v, k, v)


def _lru_coeffs(x_ref, p_ref, n_ref, tile, d, cw_ref, cb_ref, w_ref, bias_ref, lam_ref,
                ext_scr, a_scr, b_scr, n_lat_tiles):
    tm = ROW_TILE
    prev_ok = tile >= 2
    next_ok = (tile >= 1) & (tile < n_lat_tiles)
    ext_scr[0:SUBLANES] = jnp.where(prev_ok, p_ref[0], 0.0)
    ext_scr[SUBLANES:SUBLANES + tm] = x_ref[0]
    ext_scr[SUBLANES + tm:2 * SUBLANES + tm] = jnp.where(next_ok, n_ref[0], 0.0)
    xc = cb_ref[...] + cw_ref[2:3] * x_ref[0]
    xc += cw_ref[0:1] * ext_scr[SUBLANES - 2:SUBLANES - 2 + tm]
    xc += cw_ref[1:2] * ext_scr[SUBLANES - 1:SUBLANES - 1 + tm]
    xc += cw_ref[3:4] * ext_scr[SUBLANES + 1:SUBLANES + 1 + tm]
    xcb = xc.astype(BF16)
    lam = lam_ref[d:d + 1]
    sp = jnp.maximum(-lam, 0.0) + jnp.log1p(jnp.exp(-jnp.abs(lam)))
    for n in range(LRU_BLOCKS):
        cs = slice(n * LRU_BLOCK_W, (n + 1) * LRU_BLOCK_W)
        z = jnp.dot(xcb[:, cs], w_ref[d, n], preferred_element_type=F32)
        r = _sigmoid(z[:, :LRU_BLOCK_W] + bias_ref[d, 0:1, cs])
        i = _sigmoid(z[:, LRU_BLOCK_W:] + bias_ref[d, 1:2, cs])
        log_a = -LRU_C * r * sp[:, cs]
        a = jnp.exp(log_a)
        a_scr[:, cs] = a
        v = 1.0 - a * a
        b_scr[:, cs] = v * lax.rsqrt(jnp.maximum(v, SQRT_GUARD)) * (i * xc[:, cs])


def _lru_scan(a_scr, b_scr, h_ref, state_ref, reset, reverse):
    groups = ROW_TILE // SUBLANES
    row = lax.broadcasted_iota(jnp.int32, (SUBLANES, LRU_W), 0)
    carry0 = jnp.where(reset, 0.0, state_ref[...])

    def body(g, carry):
        gi = (groups - 1 - g) if reverse else g
        r0 = pl.multiple_of(gi * SUBLANES, SUBLANES)
        A = a_scr[pl.ds(r0, SUBLANES), :]
        Bv = b_scr[pl.ds(r0, SUBLANES), :]
        for s in (1, 2, 4):
            if reverse:
                sh, m = SUBLANES - s, row < SUBLANES - s
            else:
                sh, m = s, row >= s
            A_sh = pltpu.roll(A, sh, 0)
            B_sh = pltpu.roll(Bv, sh, 0)
            Bv = jnp.where(m, A * B_sh + Bv, Bv)
            A = jnp.where(m, A * A_sh, A)
        h = Bv + A * carry
        h_ref[0, pl.ds(r0, SUBLANES), :] = h
        last = h[0:1] if reverse else h[SUBLANES - 1:SUBLANES]
        return jnp.broadcast_to(last, (SUBLANES, LRU_W))

    state_ref[...] = lax.fori_loop(0, groups, body, carry0)


def _lru_kernel(xf_ref, xfp_ref, xfn_ref, xb_ref, xbp_ref, xbn_ref, cw_ref, cb_ref, w_ref, bias_ref, lam_ref,
                hf_ref, hb_ref, sf_ref, sb_ref, ext_scr, a_scr, b_scr, *, n_lat_tiles):
    t = pl.program_id(1)
    reset = t == 0
    bt = jnp.where(t == 0, 0, n_lat_tiles + 1 - t)
    _lru_coeffs(xf_ref, xfp_ref, xfn_ref, t, 0, cw_ref, cb_ref, w_ref, bias_ref, lam_ref,
                ext_scr, a_scr, b_scr, n_lat_tiles)
    _lru_scan(a_scr, b_scr, hf_ref, sf_ref, reset, False)
    _lru_coeffs(xb_ref, xbp_ref, xbn_ref, bt, 1, cw_ref, cb_ref, w_ref, bias_ref, lam_ref,
                ext_scr, a_scr, b_scr, n_lat_tiles)
    _lru_scan(a_scr, b_scr, hb_ref, sb_ref, reset, True)


def _lru(xr, conv_w, conv_b, w_gates, bias, lam, L):
    B, rows, _ = xr.shape
    nl = L // ROW_TILE
    nt = nl + 1
    per = ROW_TILE // SUBLANES
    n8 = rows // SUBLANES
    ft = lambda b, t: t
    btile = lambda b, t: jnp.where(t == 0, 0, nl + 1 - t)
    main = lambda f: pl.BlockSpec((1, ROW_TILE, LRU_W), lambda b, t: (b, f(b, t), 0))
    prev = lambda f: pl.BlockSpec((1, SUBLANES, LRU_W), lambda b, t: (b, jnp.maximum(f(b, t) * per - 1, 0), 0))
    nxt = lambda f: pl.BlockSpec((1, SUBLANES, LRU_W),
                                 lambda b, t: (b, jnp.minimum((f(b, t) + 1) * per, n8 - 1), 0))
    full = lambda a: pl.BlockSpec(a.shape, lambda b, t: (0,) * a.ndim)
    return pl.pallas_call(
        functools.partial(_lru_kernel, n_lat_tiles=nl),
        grid=(B, nt),
        in_specs=[main(ft), prev(ft), nxt(ft), main(btile), prev(btile), nxt(btile),
                  full(conv_w), full(conv_b), full(w_gates), full(bias), full(lam)],
        out_specs=[
            pl.BlockSpec((1, ROW_TILE, LRU_W), lambda b, t: (b, jnp.maximum(t - 1, 0), 0)),
            pl.BlockSpec((1, ROW_TILE, LRU_W), lambda b, t: (b, nl - jnp.maximum(t, 1), 0)),
        ],
        out_shape=[jax.ShapeDtypeStruct((B, L, LRU_W), F32)] * 2,
        scratch_shapes=[
            pltpu.VMEM((SUBLANES, LRU_W), F32), pltpu.VMEM((SUBLANES, LRU_W), F32),
            pltpu.VMEM((ROW_TILE + 2 * SUBLANES, LRU_W), F32),
            pltpu.VMEM((ROW_TILE, LRU_W), F32), pltpu.VMEM((ROW_TILE, LRU_W), F32),
        ],
        compiler_params=_params(("arbitrary", "arbitrary")),
        name="lru",
    )(xr, xr, xr, xr, xr, xr, conv_w, conv_b, w_gates, bias, lam)


def _route(scores, sel):
    tm = scores.shape[1]
    neg = -jnp.inf
    iota_g = lax.broadcasted_iota(jnp.int32, (GROUP_SIZE, tm), 0)
    grp_score = []
    for g in range(N_GROUPS):
        sg = sel[g * GROUP_SIZE:(g + 1) * GROUP_SIZE]
        m1 = jnp.max(sg, axis=0, keepdims=True)
        first = jnp.min(jnp.where(sg == m1, iota_g, GROUP_SIZE), axis=0, keepdims=True)
        m2 = jnp.max(jnp.where(iota_g == first, neg, sg), axis=0, keepdims=True)
        grp_score.append(m1 + m2)
    masked = []
    for g in range(N_GROUPS):
        rank = jnp.zeros((1, tm), jnp.int32)
        for o in range(N_GROUPS):
            if o == g:
                continue
            ahead = (grp_score[o] > grp_score[g]) if o > g else (grp_score[o] >= grp_score[g])
            rank += ahead.astype(jnp.int32)
        keep = rank < TOPK_GROUPS
        masked.append(jnp.where(keep, sel[g * GROUP_SIZE:(g + 1) * GROUP_SIZE], neg))
    cand = jnp.concatenate(masked, axis=0)
    iota_e = lax.broadcasted_iota(jnp.int32, (N_EXPERTS, tm), 0)
    chosen = jnp.zeros((N_EXPERTS, tm), jnp.bool_)
    picks = []
    for _ in range(TOP_K):
        cur = jnp.where(chosen, neg, cand)
        m = jnp.max(cur, axis=0, keepdims=True)
        idx = jnp.min(jnp.where((cur == m) & jnp.logical_not(chosen), iota_e, N_EXPERTS), axis=0, keepdims=True)
        chosen = chosen | (iota_e == idx)
        picks.append(idx)
    w = jnp.where(chosen, scores, 0.0)
    gates = w / jnp.sum(w, axis=0, keepdims=True) * ROUTED_SCALE
    return gates, chosen, picks


def _tok_scratch(rows):
    return pltpu.VMEM((rows, TOK_PITCH, LANES), F32)


def _to_token_major(y, scr):
    rows = scr.shape[0]
    flat = scr.reshape(rows * TOK_PITCH, LANES)
    for s in range(TOK_ROWS):
        flat[pl.ds(s, rows, stride=TOK_PITCH), :] = y[:, s * LANES:(s + 1) * LANES]


def _from_token_major(scr):
    rows = scr.shape[0]
    flat = scr.reshape(rows * TOK_PITCH, LANES)
    return [flat[pl.ds(s, rows, stride=TOK_PITCH), :] for s in range(TOK_ROWS)]


def _token_tiles(scr):
    return scr.at[:, 0:TOK_ROWS, :]


def _merge_kernel(attn_ref, hf_ref, hb_ref, yg_ref, x_ref, mod_ref, wout_ref, ga_ref, gl_ref, lg_ref, lb_ref,
                  rwh_ref, rwl_ref, rb_ref, tri_ref, xn_ref, u2_ref, up_ref, eidx_ref, wk_ref, pos_ref, cnt_ref,
                  carry_ref, tok_scr):
    b = pl.program_id(0)

    @pl.when((b == 0) & (pl.program_id(1) == 0))
    def _():
        carry_ref[...] = jnp.zeros_like(carry_ref)

    D = D_MODEL
    mrow = lambda i: mod_ref[pl.ds(b, 1), i * D:(i + 1) * D]
    nt = (((1,), (1,)), ((), ()))
    u2_parts, logit_parts = [], []
    sub = ROW_TILE // MERGE_SPLIT
    for p in range(MERGE_SPLIT):
        rs = slice(p * sub, (p + 1) * sub)
        lru_y = (hf_ref[0, rs] + hb_ref[0, rs]) * jax.nn.gelu(yg_ref[0, rs])
        na = _rms_norm(attn_ref[0, rs], ga_ref[...]).astype(BF16)
        nl = _rms_norm(lru_y, gl_ref[...]).astype(BF16)
        y1 = jnp.dot(na, wout_ref[0:ATTN_W], preferred_element_type=F32)
        y1 += jnp.dot(nl, wout_ref[ATTN_W:D], preferred_element_type=F32)
        xn = _layer_norm(DEEPNORM_ALPHA * x_ref[0, rs] + mrow(2) * y1) * lg_ref[...] + lb_ref[...]
        xn_ref[0, rs] = xn
        u2_p = _layer_norm(xn) * (1.0 + mrow(4)) + mrow(3)
        u_hi, u_lo = _split_bf16(u2_p)
        u2_ref[0, rs] = u_hi
        lg_p = lax.dot_general(rwh_ref[...], u_hi, nt, preferred_element_type=F32)
        lg_p += lax.dot_general(rwh_ref[...], u_lo, nt, preferred_element_type=F32)
        lg_p += lax.dot_general(rwl_ref[...], u_hi, nt, preferred_element_type=F32)
        u2_parts.append(u2_p)
        logit_parts.append(lg_p)
    u2 = jnp.concatenate(u2_parts, axis=0)
    logits = jnp.concatenate(logit_parts, axis=1)
    _to_token_major(u2, tok_scr)
    up_ref[...] = _token_tiles(tok_scr)[...].astype(BF16)
    scores = jax.nn.sigmoid(logits)
    gates, chosen, picks = _route(scores, scores + rb_ref[...])
    sel01 = jnp.where(chosen, 1.0, 0.0)
    incl = jnp.dot(sel01.astype(BF16), tri_ref[...], preferred_element_type=F32)
    rank = carry_ref[:, 0:1] + incl - sel01
    carry_ref[...] = carry_ref[...] + incl[:, ROW_TILE - 1:ROW_TILE]
    cnt_ref[...] = carry_ref[...]
    iota_e = lax.broadcasted_iota(jnp.int32, (N_EXPERTS, ROW_TILE), 0)
    for kk, idx in enumerate(picks):
        hit = iota_e == idx
        eidx_ref[kk:kk + 1, :] = idx
        wk_ref[kk:kk + 1, :] = jnp.sum(jnp.where(hit, gates, 0.0), axis=0, keepdims=True)
        pos_ref[kk:kk + 1, :] = jnp.sum(jnp.where(hit, rank, 0.0), axis=0, keepdims=True).astype(jnp.int32)


def _merge(attn, hf, hb, yg, x, mod, w_out_bf, ga, gl, lg, lb, rw_hi, rw_lo, rbias, tri, C):
    B, L, _ = x.shape
    T = B * L
    nt = L // ROW_TILE
    off = C // ROW_TILE
    row = lambda w, o=0: pl.BlockSpec((1, ROW_TILE, w), lambda b, t: (b, t + o, 0))
    full = lambda a: pl.BlockSpec(a.shape, lambda b, t: (0,) * a.ndim)
    tok = pl.BlockSpec((TOP_K, ROW_TILE), lambda b, t: (0, b * nt + t))
    return pl.pallas_call(
        _merge_kernel,
        grid=(B, nt),
        in_specs=[row(ATTN_W), row(LRU_W), row(LRU_W), row(LRU_W, off), row(D_MODEL),
                  full(mod), full(w_out_bf), full(ga), full(gl), full(lg), full(lb),
                  full(rw_hi), full(rw_lo), full(rbias), full(tri)],
        out_specs=[row(D_MODEL), row(D_MODEL),
                   pl.BlockSpec((ROW_TILE, TOK_ROWS, LANES), lambda b, t: (b * nt + t, 0, 0)),
                   tok, tok, tok,
                   pl.BlockSpec((N_EXPERTS, LANES), lambda b, t: (0, 0))],
        out_shape=[jax.ShapeDtypeStruct((B, L, D_MODEL), F32),
                   jax.ShapeDtypeStruct((B, L, D_MODEL), BF16),
                   jax.ShapeDtypeStruct((T, TOK_ROWS, LANES), BF16),
                   jax.ShapeDtypeStruct((TOP_K, T), jnp.int32),
                   jax.ShapeDtypeStruct((TOP_K, T), F32),
                   jax.ShapeDtypeStruct((TOP_K, T), jnp.int32),
                   jax.ShapeDtypeStruct((N_EXPERTS, LANES), F32)],
        scratch_shapes=[pltpu.VMEM((N_EXPERTS, LANES), F32), _tok_scratch(ROW_TILE)],
        compiler_params=_params(("arbitrary", "arbitrary")),
        name="merge",
    )(attn, hf, hb, yg, x, mod, w_out_bf, ga, gl, lg, lb, rw_hi, rw_lo, rbias, tri)


def _swiglu(u, wg, wu):
    hg = jnp.dot(u, wg, preferred_element_type=F32)
    hu = jnp.dot(u, wu, preferred_element_type=F32)
    return hg * jax.nn.sigmoid(hg) * hu


def _dispatch_kernel(zt_ref, dest_ref, up_ref, xs_ref, zero_scr, sem, zsem):
    @pl.when(pl.program_id(0) == 0)
    def _():
        zero_scr[...] = jnp.zeros_like(zero_scr)
        n_tiles = zt_ref.shape[0]

        def zero_tile(z):
            return pltpu.make_async_copy(zero_scr, xs_ref.at[pl.ds(z * EXP_TR, EXP_TR)], zsem)

        def start(z, carry):
            @pl.when(zt_ref[z] != 0)
            def _():
                zero_tile(z).start()
            return carry

        def wait(z, carry):
            @pl.when(zt_ref[z] != 0)
            def _():
                zero_tile(z).wait()
            return carry

        lax.fori_loop(0, n_tiles, start, 0)
        lax.fori_loop(0, n_tiles, wait, 0)

    def issue(j, carry):
        for k in range(TOP_K):
            pltpu.make_async_copy(up_ref.at[j], xs_ref.at[dest_ref[j * TOP_K + k]], sem).start(priority=k % 2)
        return carry

    lax.fori_loop(0, ROW_TILE, issue, 0)
    for k in range(TOP_K):
        pltpu.make_async_copy(up_ref, xs_ref.at[pl.ds(0, ROW_TILE)], sem).wait()


def _dispatch(zero_tiles, slots, up, n_sorted_rows):
    T = up.shape[0]
    return pl.pallas_call(
        _dispatch_kernel,
        grid_spec=pltpu.PrefetchScalarGridSpec(
            num_scalar_prefetch=1,
            grid=(T // ROW_TILE,),
            in_specs=[pl.BlockSpec((ROW_TILE * TOP_K,), lambda i, zt: (i,), memory_space=pltpu.SMEM),
                      pl.BlockSpec((ROW_TILE, TOK_ROWS, LANES), lambda i, zt: (i, 0, 0))],
            out_specs=pl.BlockSpec(memory_space=pl.ANY),
            scratch_shapes=[pltpu.VMEM((EXP_TR, TOK_ROWS, LANES), BF16),
                            pltpu.SemaphoreType.DMA(()), pltpu.SemaphoreType.DMA(())],
        ),
        out_shape=jax.ShapeDtypeStruct((n_sorted_rows, TOK_ROWS, LANES), BF16),
        compiler_params=_params(("arbitrary",)),
        name="dispatch",
    )(zero_tiles, slots, up)


def _expert_kernel(te_ref, nu_ref, nxt_ref, par_ref, xs_ref, wg_hbm, wu_hbm, wd_hbm, ys_ref,
                   wg32, wu32, wd32, wgb_ref, wub_ref, wdb_ref, tok_scr, wsem):
    i = pl.program_id(0)
    n_used = nu_ref[0]

    def weight_copies(e, s):
        return [pltpu.make_async_copy(hbm.at[e], stage.at[s], wsem.at[s])
                for hbm, stage in ((wg_hbm, wg32), (wu_hbm, wu32), (wd_hbm, wd32))]

    @pl.when(i == 0)
    def _():
        for cp in weight_copies(te_ref[0], 0):
            cp.start()

    @pl.when(i < n_used)
    def _():
        e = te_ref[i]

        first_of_expert = (i == 0) | (e != te_ref[jnp.maximum(i - 1, 0)])
        for s in range(2):
            @pl.when(first_of_expert & (par_ref[i] == s))
            def _():
                for cp in weight_copies(e, s):
                    cp.wait()
                for src, dst in ((wg32, wgb_ref), (wu32, wub_ref), (wd32, wdb_ref)):
                    rows = dst.shape[0] // CAST_CHUNKS

                    def cast_chunk(r, carry, src=src, dst=dst, rows=rows):
                        sl = pl.ds(pl.multiple_of(r * rows, rows), rows)
                        dst[sl, :] = src[s, sl, :].astype(BF16)
                        return carry

                    lax.fori_loop(0, CAST_CHUNKS, cast_chunk, 0)

                @pl.when(nxt_ref[i] >= 0)
                def _():
                    for cp in weight_copies(nxt_ref[i], 1 - s):
                        cp.start()

        _token_tiles(tok_scr)[...] = xs_ref[...].astype(F32)
        x = jnp.concatenate([c.astype(BF16) for c in _from_token_major(tok_scr)], axis=1)
        h = _swiglu(x, wgb_ref[...], wub_ref[...])
        y = jnp.dot(h.astype(BF16), wdb_ref[...], preferred_element_type=F32)
        _to_token_major(y, tok_scr)
        ys_ref[...] = _token_tiles(tok_scr)[...].astype(BF16)

    @pl.when(i >= n_used)
    def _():
        ys_ref[...] = jnp.zeros_like(ys_ref)


def _experts(tile_expert, n_used, next_expert, run_parity, xs, wg, wu, wd):
    n_tiles = xs.shape[0] // EXP_TR
    blk = (EXP_TR, TOK_ROWS, LANES)
    stage = lambda a: pltpu.VMEM((2,) + a.shape[1:], F32)
    cast = lambda a: pltpu.VMEM(a.shape[1:], BF16)
    hbm = pl.BlockSpec(memory_space=pl.ANY)
    return pl.pallas_call(
        _expert_kernel,
        grid_spec=pltpu.PrefetchScalarGridSpec(
            num_scalar_prefetch=4,
            grid=(n_tiles,),
            in_specs=[pl.BlockSpec(blk, lambda i, te, nu, nx, pr: (jnp.minimum(i, nu[0] - 1), 0, 0)),
                      hbm, hbm, hbm],
            out_specs=pl.BlockSpec(blk, lambda i, te, nu, nx, pr: (i, 0, 0)),
            scratch_shapes=[stage(wg), stage(wu), stage(wd), cast(wg), cast(wu), cast(wd),
                            _tok_scratch(EXP_TR), pltpu.SemaphoreType.DMA((2,))],
        ),
        out_shape=jax.ShapeDtypeStruct(xs.shape, BF16),
        compiler_params=_params(("arbitrary",)),
        name="experts",
    )(tile_expert, n_used, next_expert, run_parity, xs, wg, wu, wd)


def _combine_kernel(dcur_ref, dnext_ref, wv_ref, u_ref, xn_ref, g2_ref, sg_ref, su_ref, sd_ref, lg_ref, lb_ref,
                    ys_ref, o_ref, buf_ref, y_scr, sem):
    i = pl.program_id(0)
    n = pl.num_programs(0)
    tm = COMB_TM
    slot = i % 2

    def issue(d_ref, sl, j):
        for k in range(TOP_K):
            pltpu.make_async_copy(ys_ref.at[d_ref[j * TOP_K + k]], buf_ref.at[sl, k, j],
                                  sem.at[sl]).start(priority=k % 2)

    @pl.when(i == 0)
    def _():
        def first(j, carry):
            issue(dcur_ref, 0, j)
            return carry
        lax.fori_loop(0, tm, first, 0)

    for k in range(TOP_K):
        pltpu.make_async_copy(ys_ref.at[pl.ds(0, tm)], buf_ref.at[slot, k], sem.at[slot]).wait()

    half = TOK_ROWS // 2

    def accumulate(j):
        lo = jnp.zeros((half, LANES), F32)
        hi = jnp.zeros((half, LANES), F32)
        for k in range(TOP_K):
            gate = wv_ref[pl.ds(j * TOP_K + k, half, stride=0), :]
            rows = buf_ref[slot, k, j].astype(F32)
            lo += gate * rows[:half]
            hi += gate * rows[half:]
        y_scr[j, 0:half, :] = lo
        y_scr[j, half:TOK_ROWS, :] = hi

    @pl.when(i + 1 < n)
    def _():
        def both(jo, carry):
            for u in range(COMB_UNROLL):
                j = jo * COMB_UNROLL + u
                issue(dnext_ref, 1 - slot, j)
                accumulate(j)
            return carry
        lax.fori_loop(0, tm // COMB_UNROLL, both, 0)

    @pl.when(i + 1 == n)
    def _():
        def last(jo, carry):
            for u in range(COMB_UNROLL):
                accumulate(jo * COMB_UNROLL + u)
            return carry
        lax.fori_loop(0, tm // COMB_UNROLL, last, 0)
    y_routed = jnp.concatenate(_from_token_major(y_scr), axis=1)

    hs = _swiglu(u_ref[...], sg_ref[...], su_ref[...])
    y2 = jnp.dot(hs.astype(BF16), sd_ref[...], preferred_element_type=F32) + y_routed
    z = DEEPNORM_ALPHA * xn_ref[...] + g2_ref[0] * y2
    o_ref[...] = _layer_norm(z) * lg_ref[...] + lb_ref[...]


def _combine(slots, gates, u2, xn, g2, sg, su, sd, lg, lb, ys, tiles_per_batch):
    T = u2.shape[0]
    tm = COMB_TM
    n = T // tm
    row = pl.BlockSpec((tm, D_MODEL), lambda i: (i, 0))
    full = lambda a: pl.BlockSpec(a.shape, lambda i: (0,) * a.ndim)
    tok = lambda f: pl.BlockSpec((tm * TOP_K,), lambda i: (f(i),), memory_space=pltpu.SMEM)
    return pl.pallas_call(
        _combine_kernel,
        grid=(n,),
        in_specs=[tok(lambda i: i), tok(lambda i: jnp.minimum(i + 1, n - 1)),
                  pl.BlockSpec((tm * TOP_K, LANES), lambda i: (i, 0)), row, row,
                  pl.BlockSpec((1, 1, D_MODEL), lambda i: (i // tiles_per_batch, 0, 0)),
                  full(sg), full(su), full(sd), full(lg), full(lb),
                  pl.BlockSpec(memory_space=pl.ANY)],
        out_specs=row,
        out_shape=jax.ShapeDtypeStruct((T, D_MODEL), F32),
        scratch_shapes=[pltpu.VMEM((2, TOP_K, tm, TOK_ROWS, LANES), BF16),
                        _tok_scratch(tm),
                        pltpu.SemaphoreType.DMA((2,))],
        compiler_params=_params(("arbitrary",)),
        name="combine",
    )(slots, slots, gates, u2, xn, g2, sg, su, sd, lg, lb, ys)


def _rope_tables(L, C):
    rows = L // GRID_W
    row = np.repeat(np.arange(rows, dtype=np.float64), GRID_W)
    col = np.tile(np.arange(GRID_W, dtype=np.float64), rows)
    inv_freq = np.exp(-np.log(ROPE_THETA) * np.arange(0, ROPE_AXIS_DIM, 2, dtype=np.float64) / ROPE_AXIS_DIM)
    ang_r = row[:, None] * inv_freq
    ang_c = col[:, None] * inv_freq
    cr, sr, cc, sc = np.cos(ang_r), np.sin(ang_r), np.cos(ang_c), np.sin(ang_c)
    cos_t = np.concatenate([cr, cr, cc, cc], axis=-1)
    sin_t = np.concatenate([-sr, sr, -sc, sc], axis=-1)
    cos_t = np.concatenate([np.ones((C, HEAD_DIM)), cos_t], axis=0)
    sin_t = np.concatenate([np.zeros((C, HEAD_DIM)), sin_t], axis=0)
    return jnp.asarray(cos_t, F32), jnp.asarray(sin_t, F32)


def kernel(x, c, ctx, c_ctx, w_mod, b_mod, w_in, attn_sink, conv_w, conv_b, lru_wa, lru_ba, lru_wx, lru_bx,
           lru_lam, norm_attn_g, norm_lru_g, w_out, ln1_g, ln1_b, router_w, router_bias, exp_w_gate, exp_w_up,
           exp_w_down, sh_w_gate, sh_w_up, sh_w_down, ln2_g, ln2_b):
    B, L, D = x.shape
    C = ctx.shape[1]
    assert w_mod.shape[0] == DEPTH and D == D_MODEL and B + 1 <= SUBLANES
    row2 = lambda a: a.reshape(1, -1)

    cvec = jnp.concatenate([c, c_ctx[None], jnp.zeros((SUBLANES - B - 1, D), F32)], axis=0)
    mod = _mod(cvec, w_mod[0], row2(b_mod[0]))

    cos_t, sin_t = _rope_tables(L, C)
    q, k, v, xr, yg = _inproj(x, ctx, mod, w_in[0].astype(BF16), cos_t, sin_t)

    attn = _attention(attn_sink[0], q, k, v, L, C)

    w_gates = jnp.concatenate([lru_wa[0], lru_wx[0]], axis=-1).astype(BF16)
    bias = jnp.stack([lru_ba[0], lru_bx[0]], axis=1)
    hf, hb = _lru(xr, conv_w[0], row2(conv_b[0]), w_gates, bias, lru_lam[0], L)

    rw_hi, rw_lo = _split_bf16(router_w[0].T)
    tri = jnp.triu(jnp.ones((ROW_TILE, ROW_TILE), BF16))
    xn, u2, up, eidx, wk, pos, cnt = _merge(
        attn, hf, hb, yg, x, mod, w_out[0].astype(BF16), row2(norm_attn_g[0]), row2(norm_lru_g[0]),
        row2(ln1_g[0]), row2(ln1_b[0]), rw_hi, rw_lo, router_bias[0].reshape(-1, 1), tri, C)

    T = B * L
    n_tiles = T * TOP_K // EXP_TR + N_EXPERTS
    counts = cnt[:, 0].astype(jnp.int32)
    tiles_e = (counts + EXP_TR - 1) // EXP_TR
    tile_end = jnp.cumsum(tiles_e)
    n_used = tile_end[-1:]
    row_start = (tile_end - tiles_e) * EXP_TR
    experts = jnp.arange(N_EXPERTS, dtype=jnp.int32)
    start_of = jnp.sum(jnp.where(eidx[None] == experts[:, None, None], row_start[:, None, None], 0), axis=0)
    slots = (start_of + pos).T.reshape(-1)
    gates = jnp.broadcast_to(wk.T.reshape(-1, 1), (T * TOP_K, LANES))
    tile_id = jnp.minimum(jnp.arange(n_tiles, dtype=jnp.int32), n_used - 1)
    tile_expert = jnp.sum((tile_end[None, :] <= tile_id[:, None]).astype(jnp.int32), axis=1)

    all_tiles = jnp.arange(n_tiles, dtype=jnp.int32)
    ends_run = jnp.any((tile_end[None, :] - 1 == all_tiles[:, None]) & (tiles_e[None, :] > 0), axis=1)
    zero_tiles = (ends_run | (all_tiles >= n_used)).astype(jnp.int32)
    xs = _dispatch(zero_tiles, slots, up, n_tiles * EXP_TR)
    has_rows = tiles_e > 0
    run_parity_e = (jnp.cumsum(has_rows) - has_rows) % 2
    later = has_rows[None, :] & (experts[None, :] > experts[:, None])
    next_e = jnp.min(jnp.where(later, experts[None, :], N_EXPERTS), axis=1)
    next_e = jnp.where(next_e == N_EXPERTS, -1, next_e)
    of_tile = tile_expert[:, None] == experts[None, :]
    per_tile = lambda v: jnp.sum(jnp.where(of_tile, v[None, :], 0), axis=1).astype(jnp.int32)
    ys = _experts(tile_expert, n_used, per_tile(next_e), per_tile(run_parity_e), xs,
                  exp_w_gate[0], exp_w_up[0], exp_w_down[0])
    g2 = mod[:B, 5 * D:6 * D].reshape(B, 1, D)
    out = _combine(slots, gates, u2.reshape(T, D), xn.reshape(T, D), g2,
                   sh_w_gate[0].astype(BF16), sh_w_up[0].astype(BF16), sh_w_down[0].astype(BF16),
                   row2(ln2_g[0]), row2(ln2_b[0]), ys, L // COMB_TM)
    return out.reshape(B, L, D)
```

```python
import functools

import jax
import jax.numpy as jnp
import numpy as np
from jax import lax
from jax.experimental import pallas as pl
from jax.experimental.pallas import tpu as pltpu

F32 = jnp.float32
BF16 = jnp.bfloat16

D_MODEL = 2048
GRID_W = 64
N_HEADS = 8
N_KV_HEADS = 2
HEAD_DIM = 128
GQA_GROUP = N_HEADS // N_KV_HEADS
ATTN_W = N_HEADS * HEAD_DIM
KV_W = N_KV_HEADS * HEAD_DIM
ATTN_SCALE = HEAD_DIM ** -0.5
BLOCK = 128
ROPE_THETA = 10000.0
ROPE_AXIS_DIM = HEAD_DIM // 2
LRU_W = D_MODEL - ATTN_W
LRU_BLOCKS = 8
LRU_BLOCK_W = LRU_W // LRU_BLOCKS
LRU_C = 8.0
N_EXPERTS = 64
TOP_K = 8
N_GROUPS = 8
GROUP_SIZE = N_EXPERTS // N_GROUPS
TOPK_GROUPS = 4
EXPERT_FF = 512
ROUTED_SCALE = 2.5
LN_EPS = 1e-6
SQRT_GUARD = 1e-30
DEPTH = 1
DEEPNORM_ALPHA = (2 * DEPTH) ** 0.25

SUBLANES = 8
LANES = 128
VMEM_LIMIT = 56 * 1024 * 1024

ROW_TILE = 256
MOD_TN = 1024
MERGE_SPLIT = 2
TOK_ROWS = D_MODEL // LANES
TOK_PITCH = TOK_ROWS + SUBLANES
EXP_TR = 256
CAST_CHUNKS = 16
COMB_TM = 256
COMB_UNROLL = 4


def _params(sem, vmem=VMEM_LIMIT):
    return pltpu.CompilerParams(dimension_semantics=sem, vmem_limit_bytes=vmem)


def _layer_norm(x):
    mu = jnp.mean(x, axis=-1, keepdims=True)
    xc = x - mu
    var = jnp.mean(xc * xc, axis=-1, keepdims=True)
    return xc * lax.rsqrt(var + LN_EPS)


def _rms_norm(x, g):
    return x * lax.rsqrt(jnp.mean(x * x, axis=-1, keepdims=True) + LN_EPS) * g


def _sigmoid(x):
    return 0.5 * jnp.tanh(0.5 * x) + 0.5


def _split_bf16(x):
    hi = x.astype(BF16)
    lo = (x - hi.astype(F32)).astype(BF16)
    return hi, lo


def _mod_kernel(c_ref, w_ref, b_ref, o_ref):
    cv = c_ref[...]
    s = cv * jax.nn.sigmoid(cv)
    hi, lo = _split_bf16(s)
    lhs = jnp.concatenate([hi, lo], axis=0)
    r = jnp.dot(lhs, w_ref[...].astype(BF16), preferred_element_type=F32)
    o_ref[...] = r[:SUBLANES] + r[SUBLANES:] + b_ref[...]


def _mod(cvec, w_mod, b_mod):
    n = w_mod.shape[1]
    return pl.pallas_call(
        _mod_kernel,
        grid=(n // MOD_TN,),
        in_specs=[
            pl.BlockSpec((SUBLANES, D_MODEL), lambda j: (0, 0)),
            pl.BlockSpec((D_MODEL, MOD_TN), lambda j: (0, j)),
            pl.BlockSpec((1, MOD_TN), lambda j: (0, j)),
        ],
        out_specs=pl.BlockSpec((SUBLANES, MOD_TN), lambda j: (0, j)),
        out_shape=jax.ShapeDtypeStruct((SUBLANES, n), F32),
        compiler_params=_params(("arbitrary",)),
        name="mod",
    )(cvec, w_mod, b_mod)


def _rope(xh, cos, sin, even_block):
    partner = jnp.where(even_block, pltpu.roll(xh, 96, 1), pltpu.roll(xh, 32, 1))
    return xh * cos + partner * sin


def _inproj_kernel(x_ref, ctx_ref, mod_ref, w_ref, cos_ref, sin_ref,
                   q_ref, k_ref, v_ref, xr_ref, yg_ref):
    b = pl.program_id(0)
    t = pl.program_id(1)
    is_ctx = t == 0
    xin = jnp.where(is_ctx, ctx_ref[0], x_ref[0])
    r = jnp.where(is_ctx, 2, b)
    shift = mod_ref[pl.ds(r, 1), 0:D_MODEL]
    scale = mod_ref[pl.ds(r, 1), D_MODEL:2 * D_MODEL]
    u = (_layer_norm(xin) * (1.0 + scale) + shift).astype(BF16)

    cos = cos_ref[...]
    sin = sin_ref[...]
    lane = lax.broadcasted_iota(jnp.int32, (ROW_TILE, HEAD_DIM), 1)
    even_block = (lane % 64) < 32

    def proj(c0, c1):
        return jnp.dot(u, w_ref[:, c0:c1], preferred_element_type=F32)

    head = lambda y, h: y[:, h * HEAD_DIM:(h + 1) * HEAD_DIM]
    for p in range(N_HEADS // 2):
        qq = proj(2 * p * HEAD_DIM, (2 * p + 2) * HEAD_DIM)
        for h in range(2):
            q_ref[0, :, (2 * p + h) * HEAD_DIM:(2 * p + h + 1) * HEAD_DIM] = (
                _rope(head(qq, h), cos, sin, even_block) * ATTN_SCALE).astype(BF16)
    kk = proj(ATTN_W, ATTN_W + KV_W)
    for h in range(N_KV_HEADS):
        k_ref[0, :, h * HEAD_DIM:(h + 1) * HEAD_DIM] = _rope(head(kk, h), cos, sin, even_block).astype(BF16)
    v_ref[0] = proj(ATTN_W + KV_W, ATTN_W + 2 * KV_W).astype(BF16)
    c0 = ATTN_W + 2 * KV_W
    xr_ref[0] = proj(c0, c0 + LRU_W)
    yg_ref[0] = proj(c0 + LRU_W, c0 + 2 * LRU_W)


def _inproj(x, ctx, mod, w_in_bf, cos_t, sin_t):
    B, L, _ = x.shape
    C = ctx.shape[1]
    assert C == ROW_TILE and L % ROW_TILE == 0
    nt = L // ROW_TILE + 1
    rows = L + C
    d_in = w_in_bf.shape[1]
    out = lambda w, dt: jax.ShapeDtypeStruct((B, rows, w), dt)
    ospec = lambda w: pl.BlockSpec((1, ROW_TILE, w), lambda b, t: (b, t, 0))
    return pl.pallas_call(
        _inproj_kernel,
        grid=(B, nt),
        in_specs=[
            pl.BlockSpec((1, ROW_TILE, D_MODEL), lambda b, t: (b, jnp.maximum(t - 1, 0), 0)),
            pl.BlockSpec((1, ROW_TILE, D_MODEL), lambda b, t: (b, 0, 0)),
            pl.BlockSpec((SUBLANES, 2 * D_MODEL), lambda b, t: (0, 0)),
            pl.BlockSpec((D_MODEL, d_in), lambda b, t: (0, 0)),
            pl.BlockSpec((ROW_TILE, HEAD_DIM), lambda b, t: (t, 0)),
            pl.BlockSpec((ROW_TILE, HEAD_DIM), lambda b, t: (t, 0)),
        ],
        out_specs=[ospec(ATTN_W), ospec(KV_W), ospec(KV_W), ospec(LRU_W), ospec(LRU_W)],
        out_shape=[out(ATTN_W, BF16), out(KV_W, BF16), out(KV_W, BF16), out(LRU_W, F32), out(LRU_W, F32)],
        compiler_params=_params(("arbitrary", "arbitrary")),
        name="inproj",
    )(x, ctx, mod, w_in_bf, cos_t, sin_t)


def _attn_kernel(sink_ref, q_ref, kp_ref, kc_ref, kn_ref, vp_ref, vc_ref, vn_ref, kx_ref, vx_ref, o_ref):
    n = pl.program_id(1)
    nb = pl.num_programs(1)
    rows = GQA_GROUP * BLOCK
    n_ctx = kx_ref.shape[1]
    qi = lax.broadcasted_iota(jnp.int32, (rows, BLOCK), 0) % BLOCK
    kj = lax.broadcasted_iota(jnp.int32, (rows, BLOCK), 1)
    prev_ok = (kj >= qi) & (n > 0)
    next_ok = (kj <= qi) & (n < nb - 1)
    grp = lax.broadcasted_iota(jnp.int32, (rows, 1), 0) // BLOCK
    neg = -jnp.inf
    for h in range(N_KV_HEADS):
        hs = slice(h * HEAD_DIM, (h + 1) * HEAD_DIM)
        q4 = jnp.concatenate(
            [q_ref[0, :, (h * GQA_GROUP + g) * HEAD_DIM:(h * GQA_GROUP + g + 1) * HEAD_DIM]
             for g in range(GQA_GROUP)], axis=0)
        nt = (((1,), (1,)), ((), ()))
        sp = jnp.where(prev_ok, lax.dot_general(q4, kp_ref[0, :, hs], nt, preferred_element_type=F32), neg)
        sc = lax.dot_general(q4, kc_ref[0, :, hs], nt, preferred_element_type=F32)
        sn = jnp.where(next_ok, lax.dot_general(q4, kn_ref[0, :, hs], nt, preferred_element_type=F32), neg)
        sx = lax.dot_general(q4, kx_ref[0, :, hs], nt, preferred_element_type=F32)
        sink = jnp.zeros((rows, 1), F32)
        for g in range(GQA_GROUP):
            sink = jnp.where(grp == g, sink_ref[h * GQA_GROUP + g], sink)
        lane_chunks = lambda a: [a[:, c * BLOCK:(c + 1) * BLOCK] for c in range(a.shape[1] // BLOCK)]
        fold = lambda op, parts: functools.reduce(op, parts)
        m = jnp.max(fold(jnp.maximum, [sp, sc, sn] + lane_chunks(sx)), -1, keepdims=True)
        m = jnp.maximum(m, sink)
        pp = jnp.exp(sp - m)
        pc = jnp.exp(sc - m)
        pn = jnp.exp(sn - m)
        px = jnp.exp(sx - m)
        denom = jnp.sum(fold(jnp.add, [pp, pc, pn] + lane_chunks(px)), -1, keepdims=True) + jnp.exp(sink - m)
        acc = jnp.dot(pp.astype(BF16), vp_ref[0, :, hs], preferred_element_type=F32)
        acc += jnp.dot(pc.astype(BF16), vc_ref[0, :, hs], preferred_element_type=F32)
        acc += jnp.dot(pn.astype(BF16), vn_ref[0, :, hs], preferred_element_type=F32)
        acc += jnp.dot(px.astype(BF16), vx_ref[0, :, hs], preferred_element_type=F32)
        o = acc / denom
        for g in range(GQA_GROUP):
            c0 = (h * GQA_GROUP + g) * HEAD_DIM
            o_ref[0, :, c0:c0 + HEAD_DIM] = o[g * BLOCK:(g + 1) * BLOCK]


def _attention(sink, q, k, v, L, C):
    B = q.shape[0]
    nb = L // BLOCK
    off = C // BLOCK
    cur = lambda b, n: (b, n + off, 0)
    prv = lambda b, n: (b, jnp.maximum(n - 1, 0) + off, 0)
    nxt = lambda b, n: (b, jnp.minimum(n + 1, nb - 1) + off, 0)
    kv = lambda im: pl.BlockSpec((1, BLOCK, KV_W), im)
    cx = pl.BlockSpec((1, C, KV_W), lambda b, n: (b, 0, 0))
    return pl.pallas_call(
        _attn_kernel,
        grid=(B, nb),
        in_specs=[
            pl.BlockSpec(memory_space=pltpu.SMEM),
            pl.BlockSpec((1, BLOCK, ATTN_W), cur),
            kv(prv), kv(cur), kv(nxt), kv(prv), kv(cur), kv(nxt), cx, cx,
        ],
        out_specs=pl.BlockSpec((1, BLOCK, ATTN_W), lambda b, n: (b, n, 0)),
        out_shape=jax.ShapeDtypeStruct((B, L, ATTN_W), F32),
        compiler_params=_params(("arbitrary", "arbitrary")),
        name="attn",
    )(sink, q, k, k, k, v, v, v, k, v)


def _lru_coeffs(x_ref, p_ref, n_ref, tile, d, cw_ref, cb_ref, w_ref, bias_ref, lam_ref,
                ext_scr, a_scr, b_scr, n_lat_tiles):
    tm = ROW_TILE
    prev_ok = tile >= 2
    next_ok = (tile >= 1) & (tile < n_lat_tiles)
    ext_scr[0:SUBLANES] = jnp.where(prev_ok, p_ref[0], 0.0)
    ext_scr[SUBLANES:SUBLANES + tm] = x_ref[0]
    ext_scr[SUBLANES + tm:2 * SUBLANES + tm] = jnp.where(next_ok, n_ref[0], 0.0)
    xc = cb_ref[...] + cw_ref[2:3] * x_ref[0]
    xc += cw_ref[0:1] * ext_scr[SUBLANES - 2:SUBLANES - 2 + tm]
    xc += cw_ref[1:2] * ext_scr[SUBLANES - 1:SUBLANES - 1 + tm]
    xc += cw_ref[3:4] * ext_scr[SUBLANES + 1:SUBLANES + 1 + tm]
    xcb = xc.astype(BF16)
    lam = lam_ref[d:d + 1]
    sp = jnp.maximum(-lam, 0.0) + jnp.log1p(jnp.exp(-jnp.abs(lam)))
    for n in range(LRU_BLOCKS):
        cs = slice(n * LRU_BLOCK_W, (n + 1) * LRU_BLOCK_W)
        z = jnp.dot(xcb[:, cs], w_ref[d, n], preferred_element_type=F32)
        r = _sigmoid(z[:, :LRU_BLOCK_W] + bias_ref[d, 0:1, cs])
        i = _sigmoid(z[:, LRU_BLOCK_W:] + bias_ref[d, 1:2, cs])
        log_a = -LRU_C * r * sp[:, cs]
        a = jnp.exp(log_a)
        a_scr[:, cs] = a
        v = 1.0 - a * a
        b_scr[:, cs] = v * lax.rsqrt(jnp.maximum(v, SQRT_GUARD)) * (i * xc[:, cs])


def _lru_scan(a_scr, b_scr, h_ref, state_ref, reset, reverse):
    groups = ROW_TILE // SUBLANES
    row = lax.broadcasted_iota(jnp.int32, (SUBLANES, LRU_W), 0)
    carry0 = jnp.where(reset, 0.0, state_ref[...])

    def body(g, carry):
        gi = (groups - 1 - g) if reverse else g
        r0 = pl.multiple_of(gi * SUBLANES, SUBLANES)
        A = a_scr[pl.ds(r0, SUBLANES), :]
        Bv = b_scr[pl.ds(r0, SUBLANES), :]
        for s in (1, 2, 4):
            if reverse:
                sh, m = SUBLANES - s, row < SUBLANES - s
            else:
                sh, m = s, row >= s
            A_sh = pltpu.roll(A, sh, 0)
            B_sh = pltpu.roll(Bv, sh, 0)
            Bv = jnp.where(m, A * B_sh + Bv, Bv)
            A = jnp.where(m, A * A_sh, A)
        h = Bv + A * carry
        h_ref[0, pl.ds(r0, SUBLANES), :] = h
        last = h[0:1] if reverse else h[SUBLANES - 1:SUBLANES]
        return jnp.broadcast_to(last, (SUBLANES, LRU_W))

    state_ref[...] = lax.fori_loop(0, groups, body, carry0)


def _lru_kernel(xf_ref, xfp_ref, xfn_ref, xb_ref, xbp_ref, xbn_ref, cw_ref, cb_ref, w_ref, bias_ref, lam_ref,
                hf_ref, hb_ref, sf_ref, sb_ref, ext_scr, a_scr, b_scr, *, n_lat_tiles):
    t = pl.program_id(1)
    reset = t == 0
    bt = jnp.where(t == 0, 0, n_lat_tiles + 1 - t)
    _lru_coeffs(xf_ref, xfp_ref, xfn_ref, t, 0, cw_ref, cb_ref, w_ref, bias_ref, lam_ref,
                ext_scr, a_scr, b_scr, n_lat_tiles)
    _lru_scan(a_scr, b_scr, hf_ref, sf_ref, reset, False)
    _lru_coeffs(xb_ref, xbp_ref, xbn_ref, bt, 1, cw_ref, cb_ref, w_ref, bias_ref, lam_ref,
                ext_scr, a_scr, b_scr, n_lat_tiles)
    _lru_scan(a_scr, b_scr, hb_ref, sb_ref, reset, True)


def _lru(xr, conv_w, conv_b, w_gates, bias, lam, L):
    B, rows, _ = xr.shape
    nl = L // ROW_TILE
    nt = nl + 1
    per = ROW_TILE // SUBLANES
    n8 = rows // SUBLANES
    ft = lambda b, t: t
    btile = lambda b, t: jnp.where(t == 0, 0, nl + 1 - t)
    main = lambda f: pl.BlockSpec((1, ROW_TILE, LRU_W), lambda b, t: (b, f(b, t), 0))
    prev = lambda f: pl.BlockSpec((1, SUBLANES, LRU_W), lambda b, t: (b, jnp.maximum(f(b, t) * per - 1, 0), 0))
    nxt = lambda f: pl.BlockSpec((1, SUBLANES, LRU_W),
                                 lambda b, t: (b, jnp.minimum((f(b, t) + 1) * per, n8 - 1), 0))
    full = lambda a: pl.BlockSpec(a.shape, lambda b, t: (0,) * a.ndim)
    return pl.pallas_call(
        functools.partial(_lru_kernel, n_lat_tiles=nl),
        grid=(B, nt),
        in_specs=[main(ft), prev(ft), nxt(ft), main(btile), prev(btile), nxt(btile),
                  full(conv_w), full(conv_b), full(w_gates), full(bias), full(lam)],
        out_specs=[
            pl.BlockSpec((1, ROW_TILE, LRU_W), lambda b, t: (b, jnp.maximum(t - 1, 0), 0)),
            pl.BlockSpec((1, ROW_TILE, LRU_W), lambda b, t: (b, nl - jnp.maximum(t, 1), 0)),
        ],
        out_shape=[jax.ShapeDtypeStruct((B, L, LRU_W), F32)] * 2,
        scratch_shapes=[
            pltpu.VMEM((SUBLANES, LRU_W), F32), pltpu.VMEM((SUBLANES, LRU_W), F32),
            pltpu.VMEM((ROW_TILE + 2 * SUBLANES, LRU_W), F32),
            pltpu.VMEM((ROW_TILE, LRU_W), F32), pltpu.VMEM((ROW_TILE, LRU_W), F32),
        ],
        compiler_params=_params(("arbitrary", "arbitrary")),
        name="lru",
    )(xr, xr, xr, xr, xr, xr, conv_w, conv_b, w_gates, bias, lam)


def _route(scores, sel):
    tm = scores.shape[1]
    neg = -jnp.inf
    iota_g = lax.broadcasted_iota(jnp.int32, (GROUP_SIZE, tm), 0)
    grp_score = []
    for g in range(N_GROUPS):
        sg = sel[g * GROUP_SIZE:(g + 1) * GROUP_SIZE]
        m1 = jnp.max(sg, axis=0, keepdims=True)
        first = jnp.min(jnp.where(sg == m1, iota_g, GROUP_SIZE), axis=0, keepdims=True)
        m2 = jnp.max(jnp.where(iota_g == first, neg, sg), axis=0, keepdims=True)
        grp_score.append(m1 + m2)
    masked = []
    for g in range(N_GROUPS):
        rank = jnp.zeros((1, tm), jnp.int32)
        for o in range(N_GROUPS):
            if o == g:
                continue
            ahead = (grp_score[o] > grp_score[g]) if o > g else (grp_score[o] >= grp_score[g])
            rank += ahead.astype(jnp.int32)
        keep = rank < TOPK_GROUPS
        masked.append(jnp.where(keep, sel[g * GROUP_SIZE:(g + 1) * GROUP_SIZE], neg))
    cand = jnp.concatenate(masked, axis=0)
    iota_e = lax.broadcasted_iota(jnp.int32, (N_EXPERTS, tm), 0)
    chosen = jnp.zeros((N_EXPERTS, tm), jnp.bool_)
    picks = []
    for _ in range(TOP_K):
        cur = jnp.where(chosen, neg, cand)
        m = jnp.max(cur, axis=0, keepdims=True)
        idx = jnp.min(jnp.where((cur == m) & jnp.logical_not(chosen), iota_e, N_EXPERTS), axis=0, keepdims=True)
        chosen = chosen | (iota_e == idx)
        picks.append(idx)
    w = jnp.where(chosen, scores, 0.0)
    gates = w / jnp.sum(w, axis=0, keepdims=True) * ROUTED_SCALE
    return gates, chosen, picks


def _tok_scratch(rows):
    return pltpu.VMEM((rows, TOK_PITCH, LANES), F32)


def _to_token_major(y, scr):
    rows = scr.shape[0]
    flat = scr.reshape(rows * TOK_PITCH, LANES)
    for s in range(TOK_ROWS):
        flat[pl.ds(s, rows, stride=TOK_PITCH), :] = y[:, s * LANES:(s + 1) * LANES]


def _from_token_major(scr):
    rows = scr.shape[0]
    flat = scr.reshape(rows * TOK_PITCH, LANES)
    return [flat[pl.ds(s, rows, stride=TOK_PITCH), :] for s in range(TOK_ROWS)]


def _token_tiles(scr):
    return scr.at[:, 0:TOK_ROWS, :]


def _merge_kernel(attn_ref, hf_ref, hb_ref, yg_ref, x_ref, mod_ref, wout_ref, ga_ref, gl_ref, lg_ref, lb_ref,
                  rwh_ref, rwl_ref, rb_ref, tri_ref, xn_ref, u2_ref, up_ref, eidx_ref, wk_ref, pos_ref, cnt_ref,
                  carry_ref, tok_scr):
    b = pl.program_id(0)

    @pl.when((b == 0) & (pl.program_id(1) == 0))
    def _():
        carry_ref[...] = jnp.zeros_like(carry_ref)

    D = D_MODEL
    mrow = lambda i: mod_ref[pl.ds(b, 1), i * D:(i + 1) * D]
    nt = (((1,), (1,)), ((), ()))
    u2_parts, logit_parts = [], []
    sub = ROW_TILE // MERGE_SPLIT
    for p in range(MERGE_SPLIT):
        rs = slice(p * sub, (p + 1) * sub)
        lru_y = (hf_ref[0, rs] + hb_ref[0, rs]) * jax.nn.gelu(yg_ref[0, rs])
        na = _rms_norm(attn_ref[0, rs], ga_ref[...]).astype(BF16)
        nl = _rms_norm(lru_y, gl_ref[...]).astype(BF16)
        y1 = jnp.dot(na, wout_ref[0:ATTN_W], preferred_element_type=F32)
        y1 += jnp.dot(nl, wout_ref[ATTN_W:D], preferred_element_type=F32)
        xn = _layer_norm(DEEPNORM_ALPHA * x_ref[0, rs] + mrow(2) * y1) * lg_ref[...] + lb_ref[...]
        xn_ref[0, rs] = xn
        u2_p = _layer_norm(xn) * (1.0 + mrow(4)) + mrow(3)
        u_hi, u_lo = _split_bf16(u2_p)
        u2_ref[0, rs] = u_hi
        lg_p = lax.dot_general(rwh_ref[...], u_hi, nt, preferred_element_type=F32)
        lg_p += lax.dot_general(rwh_ref[...], u_lo, nt, preferred_element_type=F32)
        lg_p += lax.dot_general(rwl_ref[...], u_hi, nt, preferred_element_type=F32)
        u2_parts.append(u2_p)
        logit_parts.append(lg_p)
    u2 = jnp.concatenate(u2_parts, axis=0)
    logits = jnp.concatenate(logit_parts, axis=1)
    _to_token_major(u2, tok_scr)
    up_ref[...] = _token_tiles(tok_scr)[...].astype(BF16)
    scores = jax.nn.sigmoid(logits)
    gates, chosen, picks = _route(scores, scores + rb_ref[...])
    sel01 = jnp.where(chosen, 1.0, 0.0)
    incl = jnp.dot(sel01.astype(BF16), tri_ref[...], preferred_element_type=F32)
    rank = carry_ref[:, 0:1] + incl - sel01
    carry_ref[...] = carry_ref[...] + incl[:, ROW_TILE - 1:ROW_TILE]
    cnt_ref[...] = carry_ref[...]
    iota_e = lax.broadcasted_iota(jnp.int32, (N_EXPERTS, ROW_TILE), 0)
    for kk, idx in enumerate(picks):
        hit = iota_e == idx
        eidx_ref[kk:kk + 1, :] = idx
        wk_ref[kk:kk + 1, :] = jnp.sum(jnp.where(hit, gates, 0.0), axis=0, keepdims=True)
        pos_ref[kk:kk + 1, :] = jnp.sum(jnp.where(hit, rank, 0.0), axis=0, keepdims=True).astype(jnp.int32)


def _merge(attn, hf, hb, yg, x, mod, w_out_bf, ga, gl, lg, lb, rw_hi, rw_lo, rbias, tri, C):
    B, L, _ = x.shape
    T = B * L
    nt = L // ROW_TILE
    off = C // ROW_TILE
    row = lambda w, o=0: pl.BlockSpec((1, ROW_TILE, w), lambda b, t: (b, t + o, 0))
    full = lambda a: pl.BlockSpec(a.shape, lambda b, t: (0,) * a.ndim)
    tok = pl.BlockSpec((TOP_K, ROW_TILE), lambda b, t: (0, b * nt + t))
    return pl.pallas_call(
        _merge_kernel,
        grid=(B, nt),
        in_specs=[row(ATTN_W), row(LRU_W), row(LRU_W), row(LRU_W, off), row(D_MODEL),
                  full(mod), full(w_out_bf), full(ga), full(gl), full(lg), full(lb),
                  full(rw_hi), full(rw_lo), full(rbias), full(tri)],
        out_specs=[row(D_MODEL), row(D_MODEL),
                   pl.BlockSpec((ROW_TILE, TOK_ROWS, LANES), lambda b, t: (b * nt + t, 0, 0)),
                   tok, tok, tok,
                   pl.BlockSpec((N_EXPERTS, LANES), lambda b, t: (0, 0))],
        out_shape=[jax.ShapeDtypeStruct((B, L, D_MODEL), F32),
                   jax.ShapeDtypeStruct((B, L, D_MODEL), BF16),
                   jax.ShapeDtypeStruct((T, TOK_ROWS, LANES), BF16),
                   jax.ShapeDtypeStruct((TOP_K, T), jnp.int32),
                   jax.ShapeDtypeStruct((TOP_K, T), F32),
                   jax.ShapeDtypeStruct((TOP_K, T), jnp.int32),
                   jax.ShapeDtypeStruct((N_EXPERTS, LANES), F32)],
        scratch_shapes=[pltpu.VMEM((N_EXPERTS, LANES), F32), _tok_scratch(ROW_TILE)],
        compiler_params=_params(("arbitrary", "arbitrary")),
        name="merge",
    )(attn, hf, hb, yg, x, mod, w_out_bf, ga, gl, lg, lb, rw_hi, rw_lo, rbias, tri)


def _swiglu(u, wg, wu):
    hg = jnp.dot(u, wg, preferred_element_type=F32)
    hu = jnp.dot(u, wu, preferred_element_type=F32)
    return hg * jax.nn.sigmoid(hg) * hu


def _dispatch_kernel(zt_ref, dest_ref, up_ref, xs_ref, zero_scr, sem, zsem):
    @pl.when(pl.program_id(0) == 0)
    def _():
        zero_scr[...] = jnp.zeros_like(zero_scr)
        n_tiles = zt_ref.shape[0]

        def zero_tile(z):
            return pltpu.make_async_copy(zero_scr, xs_ref.at[pl.ds(z * EXP_TR, EXP_TR)], zsem)

        def start(z, carry):
            @pl.when(zt_ref[z] != 0)
            def _():
                zero_tile(z).start()
            return carry

        def wait(z, carry):
            @pl.when(zt_ref[z] != 0)
            def _():
                zero_tile(z).wait()
            return carry

        lax.fori_loop(0, n_tiles, start, 0)
        lax.fori_loop(0, n_tiles, wait, 0)

    def issue(j, carry):
        for k in range(TOP_K):
            pltpu.make_async_copy(up_ref.at[j], xs_ref.at[dest_ref[j * TOP_K + k]], sem).start(priority=k % 2)
        return carry

    lax.fori_loop(0, ROW_TILE, issue, 0)
    for k in range(TOP_K):
        pltpu.make_async_copy(up_ref, xs_ref.at[pl.ds(0, ROW_TILE)], sem).wait()


def _dispatch(zero_tiles, slots, up, n_sorted_rows):
    T = up.shape[0]
    return pl.pallas_call(
        _dispatch_kernel,
        grid_spec=pltpu.PrefetchScalarGridSpec(
            num_scalar_prefetch=1,
            grid=(T // ROW_TILE,),
            in_specs=[pl.BlockSpec((ROW_TILE * TOP_K,), lambda i, zt: (i,), memory_space=pltpu.SMEM),
                      pl.BlockSpec((ROW_TILE, TOK_ROWS, LANES), lambda i, zt: (i, 0, 0))],
            out_specs=pl.BlockSpec(memory_space=pl.ANY),
            scratch_shapes=[pltpu.VMEM((EXP_TR, TOK_ROWS, LANES), BF16),
                            pltpu.SemaphoreType.DMA(()), pltpu.SemaphoreType.DMA(())],
        ),
        out_shape=jax.ShapeDtypeStruct((n_sorted_rows, TOK_ROWS, LANES), BF16),
        compiler_params=_params(("arbitrary",)),
        name="dispatch",
    )(zero_tiles, slots, up)


def _expert_kernel(te_ref, nu_ref, nxt_ref, par_ref, xs_ref, wg_hbm, wu_hbm, wd_hbm, ys_ref,
                   wg32, wu32, wd32, wgb_ref, wub_ref, wdb_ref, tok_scr, wsem):
    i = pl.program_id(0)
    n_used = nu_ref[0]

    def weight_copies(e, s):
        return [pltpu.make_async_copy(hbm.at[e], stage.at[s], wsem.at[s])
                for hbm, stage in ((wg_hbm, wg32), (wu_hbm, wu32), (wd_hbm, wd32))]

    @pl.when(i == 0)
    def _():
        for cp in weight_copies(te_ref[0], 0):
            cp.start(priority=1)

    @pl.when(i < n_used)
    def _():
        e = te_ref[i]

        first_of_expert = (i == 0) | (e != te_ref[jnp.maximum(i - 1, 0)])
        for s in range(2):
            @pl.when(first_of_expert & (par_ref[i] == s))
            def _():
                for cp in weight_copies(e, s):
                    cp.wait()
                for src, dst in ((wg32, wgb_ref), (wu32, wub_ref), (wd32, wdb_ref)):
                    rows = dst.shape[0] // CAST_CHUNKS

                    def cast_chunk(r, carry, src=src, dst=dst, rows=rows):
                        sl = pl.ds(pl.multiple_of(r * rows, rows), rows)
                        dst[sl, :] = src[s, sl, :].astype(BF16)
                        return carry

                    lax.fori_loop(0, CAST_CHUNKS, cast_chunk, 0)

                @pl.when(nxt_ref[i] >= 0)
                def _():
                    for cp in weight_copies(nxt_ref[i], 1 - s):
                        cp.start(priority=1)

        _token_tiles(tok_scr)[...] = xs_ref[...].astype(F32)
        x = jnp.concatenate([c.astype(BF16) for c in _from_token_major(tok_scr)], axis=1)
        h = _swiglu(x, wgb_ref[...], wub_ref[...])
        y = jnp.dot(h.astype(BF16), wdb_ref[...], preferred_element_type=F32)
        _to_token_major(y, tok_scr)
        ys_ref[...] = _token_tiles(tok_scr)[...].astype(BF16)

    @pl.when(i >= n_used)
    def _():
        ys_ref[...] = jnp.zeros_like(ys_ref)


def _experts(tile_expert, n_used, next_expert, run_parity, xs, wg, wu, wd):
    n_tiles = xs.shape[0] // EXP_TR
    blk = (EXP_TR, TOK_ROWS, LANES)
    stage = lambda a: pltpu.VMEM((2,) + a.shape[1:], F32)
    cast = lambda a: pltpu.VMEM(a.shape[1:], BF16)
    hbm = pl.BlockSpec(memory_space=pl.ANY)
    return pl.pallas_call(
        _expert_kernel,
        grid_spec=pltpu.PrefetchScalarGridSpec(
            num_scalar_prefetch=4,
            grid=(n_tiles,),
            in_specs=[pl.BlockSpec(blk, lambda i, te, nu, nx, pr: (jnp.minimum(i, nu[0] - 1), 0, 0)),
                      hbm, hbm, hbm],
            out_specs=pl.BlockSpec(blk, lambda i, te, nu, nx, pr: (i, 0, 0)),
            scratch_shapes=[stage(wg), stage(wu), stage(wd), cast(wg), cast(wu), cast(wd),
                            _tok_scratch(EXP_TR), pltpu.SemaphoreType.DMA((2,))],
        ),
        out_shape=jax.ShapeDtypeStruct(xs.shape, BF16),
        compiler_params=_params(("arbitrary",)),
        name="experts",
    )(tile_expert, n_used, next_expert, run_parity, xs, wg, wu, wd)


def _combine_kernel(dcur_ref, dnext_ref, wv_ref, u_ref, xn_ref, g2_ref, sg_ref, su_ref, sd_ref, lg_ref, lb_ref,
                    ys_ref, o_ref, buf_ref, y_scr, sem):
    i = pl.program_id(0)
    n = pl.num_programs(0)
    tm = COMB_TM
    slot = i % 2

    def issue(d_ref, sl, j):
        for k in range(TOP_K):
            pltpu.make_async_copy(ys_ref.at[d_ref[j * TOP_K + k]], buf_ref.at[sl, k, j],
                                  sem.at[sl]).start(priority=k % 2)

    @pl.when(i == 0)
    def _():
        def first(j, carry):
            issue(dcur_ref, 0, j)
            return carry
        lax.fori_loop(0, tm, first, 0)

    for k in range(TOP_K):
        pltpu.make_async_copy(ys_ref.at[pl.ds(0, tm)], buf_ref.at[slot, k], sem.at[slot]).wait()

    half = TOK_ROWS // 2

    def accumulate(j):
        lo = jnp.zeros((half, LANES), F32)
        hi = jnp.zeros((half, LANES), F32)
        for k in range(TOP_K):
            gate = wv_ref[pl.ds(j * TOP_K + k, half, stride=0), :]
            rows = buf_ref[slot, k, j].astype(F32)
            lo += gate * rows[:half]
            hi += gate * rows[half:]
        y_scr[j, 0:half, :] = lo
        y_scr[j, half:TOK_ROWS, :] = hi

    @pl.when(i + 1 < n)
    def _():
        def both(jo, carry):
            for u in range(COMB_UNROLL):
                j = jo * COMB_UNROLL + u
                issue(dnext_ref, 1 - slot, j)
                accumulate(j)
            return carry
        lax.fori_loop(0, tm // COMB_UNROLL, both, 0)

    @pl.when(i + 1 == n)
    def _():
        def last(jo, carry):
            for u in range(COMB_UNROLL):
                accumulate(jo * COMB_UNROLL + u)
            return carry
        lax.fori_loop(0, tm // COMB_UNROLL, last, 0)
    y_routed = jnp.concatenate(_from_token_major(y_scr), axis=1)

    hs = _swiglu(u_ref[...], sg_ref[...], su_ref[...])
    y2 = jnp.dot(hs.astype(BF16), sd_ref[...], preferred_element_type=F32) + y_routed
    z = DEEPNORM_ALPHA * xn_ref[...] + g2_ref[0] * y2
    o_ref[...] = _layer_norm(z) * lg_ref[...] + lb_ref[...]


def _combine(slots, gates, u2, xn, g2, sg, su, sd, lg, lb, ys, tiles_per_batch):
    T = u2.shape[0]
    tm = COMB_TM
    n = T // tm
    row = pl.BlockSpec((tm, D_MODEL), lambda i: (i, 0))
    full = lambda a: pl.BlockSpec(a.shape, lambda i: (0,) * a.ndim)
    tok = lambda f: pl.BlockSpec((tm * TOP_K,), lambda i: (f(i),), memory_space=pltpu.SMEM)
    return pl.pallas_call(
        _combine_kernel,
        grid=(n,),
        in_specs=[tok(lambda i: i), tok(lambda i: jnp.minimum(i + 1, n - 1)),
                  pl.BlockSpec((tm * TOP_K, LANES), lambda i: (i, 0)), row, row,
                  pl.BlockSpec((1, 1, D_MODEL), lambda i: (i // tiles_per_batch, 0, 0)),
                  full(sg), full(su), full(sd), full(lg), full(lb),
                  pl.BlockSpec(memory_space=pl.ANY)],
        out_specs=row,
        out_shape=jax.ShapeDtypeStruct((T, D_MODEL), F32),
        scratch_shapes=[pltpu.VMEM((2, TOP_K, tm, TOK_ROWS, LANES), BF16),
                        _tok_scratch(tm),
                        pltpu.SemaphoreType.DMA((2,))],
        compiler_params=_params(("arbitrary",)),
        name="combine",
    )(slots, slots, gates, u2, xn, g2, sg, su, sd, lg, lb, ys)


def _rope_tables(L, C):
    rows = L // GRID_W
    row = np.repeat(np.arange(rows, dtype=np.float64), GRID_W)
    col = np.tile(np.arange(GRID_W, dtype=np.float64), rows)
    inv_freq = np.exp(-np.log(ROPE_THETA) * np.arange(0, ROPE_AXIS_DIM, 2, dtype=np.float64) / ROPE_AXIS_DIM)
    ang_r = row[:, None] * inv_freq
    ang_c = col[:, None] * inv_freq
    cr, sr, cc, sc = np.cos(ang_r), np.sin(ang_r), np.cos(ang_c), np.sin(ang_c)
    cos_t = np.concatenate([cr, cr, cc, cc], axis=-1)
    sin_t = np.concatenate([-sr, sr, -sc, sc], axis=-1)
    cos_t = np.concatenate([np.ones((C, HEAD_DIM)), cos_t], axis=0)
    sin_t = np.concatenate([np.zeros((C, HEAD_DIM)), sin_t], axis=0)
    return jnp.asarray(cos_t, F32), jnp.asarray(sin_t, F32)


def kernel(x, c, ctx, c_ctx, w_mod, b_mod, w_in, attn_sink, conv_w, conv_b, lru_wa, lru_ba, lru_wx, lru_bx,
           lru_lam, norm_attn_g, norm_lru_g, w_out, ln1_g, ln1_b, router_w, router_bias, exp_w_gate, exp_w_up,
           exp_w_down, sh_w_gate, sh_w_up, sh_w_down, ln2_g, ln2_b):
    B, L, D = x.shape
    C = ctx.shape[1]
    assert w_mod.shape[0] == DEPTH and D == D_MODEL and B + 1 <= SUBLANES
    row2 = lambda a: a.reshape(1, -1)

    cvec = jnp.concatenate([c, c_ctx[None], jnp.zeros((SUBLANES - B - 1, D), F32)], axis=0)
    mod = _mod(cvec, w_mod[0], row2(b_mod[0]))

    cos_t, sin_t = _rope_tables(L, C)
    q, k, v, xr, yg = _inproj(x, ctx, mod, w_in[0].astype(BF16), cos_t, sin_t)

    attn = _attention(attn_sink[0], q, k, v, L, C)

    w_gates = jnp.concatenate([lru_wa[0], lru_wx[0]], axis=-1).astype(BF16)
    bias = jnp.stack([lru_ba[0], lru_bx[0]], axis=1)
    hf, hb = _lru(xr, conv_w[0], row2(conv_b[0]), w_gates, bias, lru_lam[0], L)

    rw_hi, rw_lo = _split_bf16(router_w[0].T)
    tri = jnp.triu(jnp.ones((ROW_TILE, ROW_TILE), BF16))
    xn, u2, up, eidx, wk, pos, cnt = _merge(
        attn, hf, hb, yg, x, mod, w_out[0].astype(BF16), row2(norm_attn_g[0]), row2(norm_lru_g[0]),
        row2(ln1_g[0]), row2(ln1_b[0]), rw_hi, rw_lo, router_bias[0].reshape(-1, 1), tri, C)

    T = B * L
    n_tiles = T * TOP_K // EXP_TR + N_EXPERTS
    counts = cnt[:, 0].astype(jnp.int32)
    tiles_e = (counts + EXP_TR - 1) // EXP_TR
    tile_end = jnp.cumsum(tiles_e)
    n_used = tile_end[-1:]
    row_start = (tile_end - tiles_e) * EXP_TR
    experts = jnp.arange(N_EXPERTS, dtype=jnp.int32)
    start_of = jnp.sum(jnp.where(eidx[None] == experts[:, None, None], row_start[:, None, None], 0), axis=0)
    slots = (start_of + pos).T.reshape(-1)
    gates = jnp.broadcast_to(wk.T.reshape(-1, 1), (T * TOP_K, LANES))
    tile_id = jnp.minimum(jnp.arange(n_tiles, dtype=jnp.int32), n_used - 1)
    tile_expert = jnp.sum((tile_end[None, :] <= tile_id[:, None]).astype(jnp.int32), axis=1)

    all_tiles = jnp.arange(n_tiles, dtype=jnp.int32)
    ends_run = jnp.any((tile_end[None, :] - 1 == all_tiles[:, None]) & (tiles_e[None, :] > 0), axis=1)
    zero_tiles = (ends_run | (all_tiles >= n_used)).astype(jnp.int32)
    xs = _dispatch(zero_tiles, slots, up, n_tiles * EXP_TR)
    has_rows = tiles_e > 0
    run_parity_e = (jnp.cumsum(has_rows) - has_rows) % 2
    later = has_rows[None, :] & (experts[None, :] > experts[:, None])
    next_e = jnp.min(jnp.where(later, experts[None, :], N_EXPERTS), axis=1)
    next_e = jnp.where(next_e == N_EXPERTS, -1, next_e)
    of_tile = tile_expert[:, None] == experts[None, :]
    per_tile = lambda v: jnp.sum(jnp.where(of_tile, v[None, :], 0), axis=1).astype(jnp.int32)
    ys = _experts(tile_expert, n_used, per_tile(next_e), per_tile(run_parity_e), xs,
                  exp_w_gate[0], exp_w_up[0], exp_w_down[0])
    g2 = mod[:B, 5 * D:6 * D].reshape(B, 1, D)
    out = _combine(slots, gates, u2.reshape(T, D), xn.reshape(T, D), g2,
                   sh_w_gate[0].astype(BF16), sh_w_up[0].astype(BF16), sh_w_down[0].astype(BF16),
                   row2(ln2_g[0]), row2(ln2_b[0]), ys, L // COMB_TM)
    return out.reshape(B, L, D)
```

```python
import functools

import jax
import jax.numpy as jnp
import numpy as np
from jax import lax
from jax.experimental import pallas as pl
from jax.experimental.pallas import tpu as pltpu

F32 = jnp.float32
BF16 = jnp.bfloat16

D_MODEL = 2048
GRID_W = 64
N_HEADS = 8
N_KV_HEADS = 2
HEAD_DIM = 128
GQA_GROUP = N_HEADS // N_KV_HEADS
ATTN_W = N_HEADS * HEAD_DIM
KV_W = N_KV_HEADS * HEAD_DIM
ATTN_SCALE = HEAD_DIM ** -0.5
BLOCK = 128
ROPE_THETA = 10000.0
ROPE_AXIS_DIM = HEAD_DIM // 2
LRU_W = D_MODEL - ATTN_W
LRU_BLOCKS = 8
LRU_BLOCK_W = LRU_W // LRU_BLOCKS
LRU_C = 8.0
N_EXPERTS = 64
TOP_K = 8
N_GROUPS = 8
GROUP_SIZE = N_EXPERTS // N_GROUPS
TOPK_GROUPS = 4
EXPERT_FF = 512
ROUTED_SCALE = 2.5
LN_EPS = 1e-6
SQRT_GUARD = 1e-30
DEPTH = 1
DEEPNORM_ALPHA = (2 * DEPTH) ** 0.25

SUBLANES = 8
LANES = 128
VMEM_LIMIT = 56 * 1024 * 1024

ROW_TILE = 256
MOD_TN = 1024
MERGE_SPLIT = 2
TOK_ROWS = D_MODEL // LANES
TOK_PITCH = TOK_ROWS + SUBLANES
EXP_TR = 256
CAST_CHUNKS = 16
DISP_TM = 512
COMB_TM = 256
COMB_UNROLL = 4


def _params(sem, vmem=VMEM_LIMIT):
    return pltpu.CompilerParams(dimension_semantics=sem, vmem_limit_bytes=vmem)


def _layer_norm(x):
    mu = jnp.mean(x, axis=-1, keepdims=True)
    xc = x - mu
    var = jnp.mean(xc * xc, axis=-1, keepdims=True)
    return xc * lax.rsqrt(var + LN_EPS)


def _rms_norm(x, g):
    return x * lax.rsqrt(jnp.mean(x * x, axis=-1, keepdims=True) + LN_EPS) * g


def _sigmoid(x):
    return 0.5 * jnp.tanh(0.5 * x) + 0.5


def _split_bf16(x):
    hi = x.astype(BF16)
    lo = (x - hi.astype(F32)).astype(BF16)
    return hi, lo


def _mod_kernel(c_ref, w_ref, b_ref, o_ref):
    cv = c_ref[...]
    s = cv * jax.nn.sigmoid(cv)
    hi, lo = _split_bf16(s)
    lhs = jnp.concatenate([hi, lo], axis=0)
    r = jnp.dot(lhs, w_ref[...].astype(BF16), preferred_element_type=F32)
    o_ref[...] = r[:SUBLANES] + r[SUBLANES:] + b_ref[...]


def _mod(cvec, w_mod, b_mod):
    n = w_mod.shape[1]
    return pl.pallas_call(
        _mod_kernel,
        grid=(n // MOD_TN,),
        in_specs=[
            pl.BlockSpec((SUBLANES, D_MODEL), lambda j: (0, 0)),
            pl.BlockSpec((D_MODEL, MOD_TN), lambda j: (0, j)),
            pl.BlockSpec((1, MOD_TN), lambda j: (0, j)),
        ],
        out_specs=pl.BlockSpec((SUBLANES, MOD_TN), lambda j: (0, j)),
        out_shape=jax.ShapeDtypeStruct((SUBLANES, n), F32),
        compiler_params=_params(("arbitrary",)),
        name="mod",
    )(cvec, w_mod, b_mod)


def _rope(xh, cos, sin, even_block):
    partner = jnp.where(even_block, pltpu.roll(xh, 96, 1), pltpu.roll(xh, 32, 1))
    return xh * cos + partner * sin


def _inproj_kernel(x_ref, ctx_ref, mod_ref, w_ref, cos_ref, sin_ref,
                   q_ref, k_ref, v_ref, xr_ref, yg_ref):
    b = pl.program_id(0)
    t = pl.program_id(1)
    is_ctx = t == 0
    xin = jnp.where(is_ctx, ctx_ref[0], x_ref[0])
    r = jnp.where(is_ctx, 2, b)
    shift = mod_ref[pl.ds(r, 1), 0:D_MODEL]
    scale = mod_ref[pl.ds(r, 1), D_MODEL:2 * D_MODEL]
    u = (_layer_norm(xin) * (1.0 + scale) + shift).astype(BF16)

    cos = cos_ref[...]
    sin = sin_ref[...]
    lane = lax.broadcasted_iota(jnp.int32, (ROW_TILE, HEAD_DIM), 1)
    even_block = (lane % 64) < 32

    def proj(c0, c1):
        return jnp.dot(u, w_ref[:, c0:c1], preferred_element_type=F32)

    head = lambda y, h: y[:, h * HEAD_DIM:(h + 1) * HEAD_DIM]
    for p in range(N_HEADS // 2):
        qq = proj(2 * p * HEAD_DIM, (2 * p + 2) * HEAD_DIM)
        for h in range(2):
            q_ref[0, :, (2 * p + h) * HEAD_DIM:(2 * p + h + 1) * HEAD_DIM] = (
                _rope(head(qq, h), cos, sin, even_block) * ATTN_SCALE).astype(BF16)
    kk = proj(ATTN_W, ATTN_W + KV_W)
    for h in range(N_KV_HEADS):
        k_ref[0, :, h * HEAD_DIM:(h + 1) * HEAD_DIM] = _rope(head(kk, h), cos, sin, even_block).astype(BF16)
    v_ref[0] = proj(ATTN_W + KV_W, ATTN_W + 2 * KV_W).astype(BF16)
    c0 = ATTN_W + 2 * KV_W
    xr_ref[0] = proj(c0, c0 + LRU_W)
    yg_ref[0] = proj(c0 + LRU_W, c0 + 2 * LRU_W)


def _inproj(x, ctx, mod, w_in_bf, cos_t, sin_t):
    B, L, _ = x.shape
    C = ctx.shape[1]
    assert C == ROW_TILE and L % ROW_TILE == 0
    nt = L // ROW_TILE + 1
    rows = L + C
    d_in = w_in_bf.shape[1]
    out = lambda w, dt: jax.ShapeDtypeStruct((B, rows, w), dt)
    ospec = lambda w: pl.BlockSpec((1, ROW_TILE, w), lambda b, t: (b, t, 0))
    return pl.pallas_call(
        _inproj_kernel,
        grid=(B, nt),
        in_specs=[
            pl.BlockSpec((1, ROW_TILE, D_MODEL), lambda b, t: (b, jnp.maximum(t - 1, 0), 0)),
            pl.BlockSpec((1, ROW_TILE, D_MODEL), lambda b, t: (b, 0, 0)),
            pl.BlockSpec((SUBLANES, 2 * D_MODEL), lambda b, t: (0, 0)),
            pl.BlockSpec((D_MODEL, d_in), lambda b, t: (0, 0)),
            pl.BlockSpec((ROW_TILE, HEAD_DIM), lambda b, t: (t, 0)),
            pl.BlockSpec((ROW_TILE, HEAD_DIM), lambda b, t: (t, 0)),
        ],
        out_specs=[ospec(ATTN_W), ospec(KV_W), ospec(KV_W), ospec(LRU_W), ospec(LRU_W)],
        out_shape=[out(ATTN_W, BF16), out(KV_W, BF16), out(KV_W, BF16), out(LRU_W, F32), out(LRU_W, F32)],
        compiler_params=_params(("arbitrary", "arbitrary")),
        name="inproj",
    )(x, ctx, mod, w_in_bf, cos_t, sin_t)


def _attn_kernel(sink_ref, q_ref, kp_ref, kc_ref, kn_ref, vp_ref, vc_ref, vn_ref, kx_ref, vx_ref, o_ref):
    n = pl.program_id(1)
    nb = pl.num_programs(1)
    rows = GQA_GROUP * BLOCK
    n_ctx = kx_ref.shape[1]
    qi = lax.broadcasted_iota(jnp.int32, (rows, BLOCK), 0) % BLOCK
    kj = lax.broadcasted_iota(jnp.int32, (rows, BLOCK), 1)
    prev_ok = (kj >= qi) & (n > 0)
    next_ok = (kj <= qi) & (n < nb - 1)
    grp = lax.broadcasted_iota(jnp.int32, (rows, 1), 0) // BLOCK
    neg = -jnp.inf
    for h in range(N_KV_HEADS):
        hs = slice(h * HEAD_DIM, (h + 1) * HEAD_DIM)
        q4 = jnp.concatenate(
            [q_ref[0, :, (h * GQA_GROUP + g) * HEAD_DIM:(h * GQA_GROUP + g + 1) * HEAD_DIM]
             for g in range(GQA_GROUP)], axis=0)
        nt = (((1,), (1,)), ((), ()))
        sp = jnp.where(prev_ok, lax.dot_general(q4, kp_ref[0, :, hs], nt, preferred_element_type=F32), neg)
        sc = lax.dot_general(q4, kc_ref[0, :, hs], nt, preferred_element_type=F32)
        sn = jnp.where(next_ok, lax.dot_general(q4, kn_ref[0, :, hs], nt, preferred_element_type=F32), neg)
        sx = lax.dot_general(q4, kx_ref[0, :, hs], nt, preferred_element_type=F32)
        sink = jnp.zeros((rows, 1), F32)
        for g in range(GQA_GROUP):
            sink = jnp.where(grp == g, sink_ref[h * GQA_GROUP + g], sink)
        lane_chunks = lambda a: [a[:, c * BLOCK:(c + 1) * BLOCK] for c in range(a.shape[1] // BLOCK)]
        fold = lambda op, parts: functools.reduce(op, parts)
        m = jnp.max(fold(jnp.maximum, [sp, sc, sn] + lane_chunks(sx)), -1, keepdims=True)
        m = jnp.maximum(m, sink)
        pp = jnp.exp(sp - m)
        pc = jnp.exp(sc - m)
        pn = jnp.exp(sn - m)
        px = jnp.exp(sx - m)
        denom = jnp.sum(fold(jnp.add, [pp, pc, pn] + lane_chunks(px)), -1, keepdims=True) + jnp.exp(sink - m)
        acc = jnp.dot(pp.astype(BF16), vp_ref[0, :, hs], preferred_element_type=F32)
        acc += jnp.dot(pc.astype(BF16), vc_ref[0, :, hs], preferred_element_type=F32)
        acc += jnp.dot(pn.astype(BF16), vn_ref[0, :, hs], preferred_element_type=F32)
        acc += jnp.dot(px.astype(BF16), vx_ref[0, :, hs], preferred_element_type=F32)
        o = acc / denom
        for g in range(GQA_GROUP):
            c0 = (h * GQA_GROUP + g) * HEAD_DIM
            o_ref[0, :, c0:c0 + HEAD_DIM] = o[g * BLOCK:(g + 1) * BLOCK]


def _attention(sink, q, k, v, L, C):
    B = q.shape[0]
    nb = L // BLOCK
    off = C // BLOCK
    cur = lambda b, n: (b, n + off, 0)
    prv = lambda b, n: (b, jnp.maximum(n - 1, 0) + off, 0)
    nxt = lambda b, n: (b, jnp.minimum(n + 1, nb - 1) + off, 0)
    kv = lambda im: pl.BlockSpec((1, BLOCK, KV_W), im)
    cx = pl.BlockSpec((1, C, KV_W), lambda b, n: (b, 0, 0))
    return pl.pallas_call(
        _attn_kernel,
        grid=(B, nb),
        in_specs=[
            pl.BlockSpec(memory_space=pltpu.SMEM),
            pl.BlockSpec((1, BLOCK, ATTN_W), cur),
            kv(prv), kv(cur), kv(nxt), kv(prv), kv(cur), kv(nxt), cx, cx,
        ],
        out_specs=pl.BlockSpec((1, BLOCK, ATTN_W), lambda b, n: (b, n, 0)),
        out_shape=jax.ShapeDtypeStruct((B, L, ATTN_W), F32),
        compiler_params=_params(("arbitrary", "arbitrary")),
        name="attn",
    )(sink, q, k, k, k, v, v, v, k, v)


def _lru_coeffs(x_ref, p_ref, n_ref, tile, d, cw_ref, cb_ref, w_ref, bias_ref, lam_ref,
                ext_scr, a_scr, b_scr, n_lat_tiles):
    tm = ROW_TILE
    prev_ok = tile >= 2
    next_ok = (tile >= 1) & (tile < n_lat_tiles)
    ext_scr[0:SUBLANES] = jnp.where(prev_ok, p_ref[0], 0.0)
    ext_scr[SUBLANES:SUBLANES + tm] = x_ref[0]
    ext_scr[SUBLANES + tm:2 * SUBLANES + tm] = jnp.where(next_ok, n_ref[0], 0.0)
    xc = cb_ref[...] + cw_ref[2:3] * x_ref[0]
    xc += cw_ref[0:1] * ext_scr[SUBLANES - 2:SUBLANES - 2 + tm]
    xc += cw_ref[1:2] * ext_scr[SUBLANES - 1:SUBLANES - 1 + tm]
    xc += cw_ref[3:4] * ext_scr[SUBLANES + 1:SUBLANES + 1 + tm]
    xcb = xc.astype(BF16)
    lam = lam_ref[d:d + 1]
    sp = jnp.maximum(-lam, 0.0) + jnp.log1p(jnp.exp(-jnp.abs(lam)))
    for n in range(LRU_BLOCKS):
        cs = slice(n * LRU_BLOCK_W, (n + 1) * LRU_BLOCK_W)
        z = jnp.dot(xcb[:, cs], w_ref[d, n], preferred_element_type=F32)
        r = _sigmoid(z[:, :LRU_BLOCK_W] + bias_ref[d, 0:1, cs])
        i = _sigmoid(z[:, LRU_BLOCK_W:] + bias_ref[d, 1:2, cs])
        log_a = -LRU_C * r * sp[:, cs]
        a = jnp.exp(log_a)
        a_scr[:, cs] = a
        v = 1.0 - a * a
        b_scr[:, cs] = v * lax.rsqrt(jnp.maximum(v, SQRT_GUARD)) * (i * xc[:, cs])


def _lru_scan(a_scr, b_scr, h_ref, state_ref, reset, reverse):
    groups = ROW_TILE // SUBLANES
    row = lax.broadcasted_iota(jnp.int32, (SUBLANES, LRU_W), 0)
    carry0 = jnp.where(reset, 0.0, state_ref[...])

    def body(g, carry):
        gi = (groups - 1 - g) if reverse else g
        r0 = pl.multiple_of(gi * SUBLANES, SUBLANES)
        A = a_scr[pl.ds(r0, SUBLANES), :]
        Bv = b_scr[pl.ds(r0, SUBLANES), :]
        for s in (1, 2, 4):
            if reverse:
                sh, m = SUBLANES - s, row < SUBLANES - s
            else:
                sh, m = s, row >= s
            A_sh = pltpu.roll(A, sh, 0)
            B_sh = pltpu.roll(Bv, sh, 0)
            Bv = jnp.where(m, A * B_sh + Bv, Bv)
            A = jnp.where(m, A * A_sh, A)
        h = Bv + A * carry
        h_ref[0, pl.ds(r0, SUBLANES), :] = h
        last = h[0:1] if reverse else h[SUBLANES - 1:SUBLANES]
        return jnp.broadcast_to(last, (SUBLANES, LRU_W))

    state_ref[...] = lax.fori_loop(0, groups, body, carry0)


def _lru_kernel(xf_ref, xfp_ref, xfn_ref, xb_ref, xbp_ref, xbn_ref, cw_ref, cb_ref, w_ref, bias_ref, lam_ref,
                hf_ref, hb_ref, sf_ref, sb_ref, ext_scr, a_scr, b_scr, *, n_lat_tiles):
    t = pl.program_id(1)
    reset = t == 0
    bt = jnp.where(t == 0, 0, n_lat_tiles + 1 - t)
    _lru_coeffs(xf_ref, xfp_ref, xfn_ref, t, 0, cw_ref, cb_ref, w_ref, bias_ref, lam_ref,
                ext_scr, a_scr, b_scr, n_lat_tiles)
    _lru_scan(a_scr, b_scr, hf_ref, sf_ref, reset, False)
    _lru_coeffs(xb_ref, xbp_ref, xbn_ref, bt, 1, cw_ref, cb_ref, w_ref, bias_ref, lam_ref,
                ext_scr, a_scr, b_scr, n_lat_tiles)
    _lru_scan(a_scr, b_scr, hb_ref, sb_ref, reset, True)


def _lru(xr, conv_w, conv_b, w_gates, bias, lam, L):
    B, rows, _ = xr.shape
    nl = L // ROW_TILE
    nt = nl + 1
    per = ROW_TILE // SUBLANES
    n8 = rows // SUBLANES
    ft = lambda b, t: t
    btile = lambda b, t: jnp.where(t == 0, 0, nl + 1 - t)
    main = lambda f: pl.BlockSpec((1, ROW_TILE, LRU_W), lambda b, t: (b, f(b, t), 0))
    prev = lambda f: pl.BlockSpec((1, SUBLANES, LRU_W), lambda b, t: (b, jnp.maximum(f(b, t) * per - 1, 0), 0))
    nxt = lambda f: pl.BlockSpec((1, SUBLANES, LRU_W),
                                 lambda b, t: (b, jnp.minimum((f(b, t) + 1) * per, n8 - 1), 0))
    full = lambda a: pl.BlockSpec(a.shape, lambda b, t: (0,) * a.ndim)
    return pl.pallas_call(
        functools.partial(_lru_kernel, n_lat_tiles=nl),
        grid=(B, nt),
        in_specs=[main(ft), prev(ft), nxt(ft), main(btile), prev(btile), nxt(btile),
                  full(conv_w), full(conv_b), full(w_gates), full(bias), full(lam)],
        out_specs=[
            pl.BlockSpec((1, ROW_TILE, LRU_W), lambda b, t: (b, jnp.maximum(t - 1, 0), 0)),
            pl.BlockSpec((1, ROW_TILE, LRU_W), lambda b, t: (b, nl - jnp.maximum(t, 1), 0)),
        ],
        out_shape=[jax.ShapeDtypeStruct((B, L, LRU_W), F32)] * 2,
        scratch_shapes=[
            pltpu.VMEM((SUBLANES, LRU_W), F32), pltpu.VMEM((SUBLANES, LRU_W), F32),
            pltpu.VMEM((ROW_TILE + 2 * SUBLANES, LRU_W), F32),
            pltpu.VMEM((ROW_TILE, LRU_W), F32), pltpu.VMEM((ROW_TILE, LRU_W), F32),
        ],
        compiler_params=_params(("arbitrary", "arbitrary")),
        name="lru",
    )(xr, xr, xr, xr, xr, xr, conv_w, conv_b, w_gates, bias, lam)


def _route(scores, sel):
    tm = scores.shape[1]
    neg = -jnp.inf
    iota_g = lax.broadcasted_iota(jnp.int32, (GROUP_SIZE, tm), 0)
    grp_score = []
    for g in range(N_GROUPS):
        sg = sel[g * GROUP_SIZE:(g + 1) * GROUP_SIZE]
        m1 = jnp.max(sg, axis=0, keepdims=True)
        first = jnp.min(jnp.where(sg == m1, iota_g, GROUP_SIZE), axis=0, keepdims=True)
        m2 = jnp.max(jnp.where(iota_g == first, neg, sg), axis=0, keepdims=True)
        grp_score.append(m1 + m2)
    masked = []
    for g in range(N_GROUPS):
        rank = jnp.zeros((1, tm), jnp.int32)
        for o in range(N_GROUPS):
            if o == g:
                continue
            ahead = (grp_score[o] > grp_score[g]) if o > g else (grp_score[o] >= grp_score[g])
            rank += ahead.astype(jnp.int32)
        keep = rank < TOPK_GROUPS
        masked.append(jnp.where(keep, sel[g * GROUP_SIZE:(g + 1) * GROUP_SIZE], neg))
    cand = jnp.concatenate(masked, axis=0)
    iota_e = lax.broadcasted_iota(jnp.int32, (N_EXPERTS, tm), 0)
    chosen = jnp.zeros((N_EXPERTS, tm), jnp.bool_)
    picks = []
    for _ in range(TOP_K):
        cur = jnp.where(chosen, neg, cand)
        m = jnp.max(cur, axis=0, keepdims=True)
        idx = jnp.min(jnp.where((cur == m) & jnp.logical_not(chosen), iota_e, N_EXPERTS), axis=0, keepdims=True)
        chosen = chosen | (iota_e == idx)
        picks.append(idx)
    w = jnp.where(chosen, scores, 0.0)
    gates = w / jnp.sum(w, axis=0, keepdims=True) * ROUTED_SCALE
    return gates, chosen, picks


def _tok_scratch(rows):
    return pltpu.VMEM((rows, TOK_PITCH, LANES), F32)


def _to_token_major(y, scr):
    rows = scr.shape[0]
    flat = scr.reshape(rows * TOK_PITCH, LANES)
    for s in range(TOK_ROWS):
        flat[pl.ds(s, rows, stride=TOK_PITCH), :] = y[:, s * LANES:(s + 1) * LANES]


def _from_token_major(scr):
    rows = scr.shape[0]
    flat = scr.reshape(rows * TOK_PITCH, LANES)
    return [flat[pl.ds(s, rows, stride=TOK_PITCH), :] for s in range(TOK_ROWS)]


def _token_tiles(scr):
    return scr.at[:, 0:TOK_ROWS, :]


def _merge_kernel(attn_ref, hf_ref, hb_ref, yg_ref, x_ref, mod_ref, wout_ref, ga_ref, gl_ref, lg_ref, lb_ref,
                  rwh_ref, rwl_ref, rb_ref, tri_ref, xn_ref, u2_ref, up_ref, eidx_ref, wk_ref, pos_ref, cnt_ref,
                  carry_ref, tok_scr):
    b = pl.program_id(0)

    @pl.when((b == 0) & (pl.program_id(1) == 0))
    def _():
        carry_ref[...] = jnp.zeros_like(carry_ref)

    D = D_MODEL
    mrow = lambda i: mod_ref[pl.ds(b, 1), i * D:(i + 1) * D]
    nt = (((1,), (1,)), ((), ()))
    u2_parts, logit_parts = [], []
    sub = ROW_TILE // MERGE_SPLIT
    for p in range(MERGE_SPLIT):
        rs = slice(p * sub, (p + 1) * sub)
        lru_y = (hf_ref[0, rs] + hb_ref[0, rs]) * jax.nn.gelu(yg_ref[0, rs])
        na = _rms_norm(attn_ref[0, rs], ga_ref[...]).astype(BF16)
        nl = _rms_norm(lru_y, gl_ref[...]).astype(BF16)
        y1 = jnp.dot(na, wout_ref[0:ATTN_W], preferred_element_type=F32)
        y1 += jnp.dot(nl, wout_ref[ATTN_W:D], preferred_element_type=F32)
        xn = _layer_norm(DEEPNORM_ALPHA * x_ref[0, rs] + mrow(2) * y1) * lg_ref[...] + lb_ref[...]
        xn_ref[0, rs] = xn
        u2_p = _layer_norm(xn) * (1.0 + mrow(4)) + mrow(3)
        u_hi, u_lo = _split_bf16(u2_p)
        u2_ref[0, rs] = u_hi
        lg_p = lax.dot_general(rwh_ref[...], u_hi, nt, preferred_element_type=F32)
        lg_p += lax.dot_general(rwh_ref[...], u_lo, nt, preferred_element_type=F32)
        lg_p += lax.dot_general(rwl_ref[...], u_hi, nt, preferred_element_type=F32)
        u2_parts.append(u2_p)
        logit_parts.append(lg_p)
    u2 = jnp.concatenate(u2_parts, axis=0)
    logits = jnp.concatenate(logit_parts, axis=1)
    _to_token_major(u2, tok_scr)
    up_ref[...] = _token_tiles(tok_scr)[...].astype(BF16)
    scores = jax.nn.sigmoid(logits)
    gates, chosen, picks = _route(scores, scores + rb_ref[...])
    sel01 = jnp.where(chosen, 1.0, 0.0)
    incl = jnp.dot(sel01.astype(BF16), tri_ref[...], preferred_element_type=F32)
    rank = carry_ref[:, 0:1] + incl - sel01
    carry_ref[...] = carry_ref[...] + incl[:, ROW_TILE - 1:ROW_TILE]
    cnt_ref[...] = carry_ref[...]
    iota_e = lax.broadcasted_iota(jnp.int32, (N_EXPERTS, ROW_TILE), 0)
    for kk, idx in enumerate(picks):
        hit = iota_e == idx
        eidx_ref[kk:kk + 1, :] = idx
        wk_ref[kk:kk + 1, :] = jnp.sum(jnp.where(hit, gates, 0.0), axis=0, keepdims=True)
        pos_ref[kk:kk + 1, :] = jnp.sum(jnp.where(hit, rank, 0.0), axis=0, keepdims=True).astype(jnp.int32)


def _merge(attn, hf, hb, yg, x, mod, w_out_bf, ga, gl, lg, lb, rw_hi, rw_lo, rbias, tri, C):
    B, L, _ = x.shape
    T = B * L
    nt = L // ROW_TILE
    off = C // ROW_TILE
    row = lambda w, o=0: pl.BlockSpec((1, ROW_TILE, w), lambda b, t: (b, t + o, 0))
    full = lambda a: pl.BlockSpec(a.shape, lambda b, t: (0,) * a.ndim)
    tok = pl.BlockSpec((TOP_K, ROW_TILE), lambda b, t: (0, b * nt + t))
    return pl.pallas_call(
        _merge_kernel,
        grid=(B, nt),
        in_specs=[row(ATTN_W), row(LRU_W), row(LRU_W), row(LRU_W, off), row(D_MODEL),
                  full(mod), full(w_out_bf), full(ga), full(gl), full(lg), full(lb),
                  full(rw_hi), full(rw_lo), full(rbias), full(tri)],
        out_specs=[row(D_MODEL), row(D_MODEL),
                   pl.BlockSpec((ROW_TILE, TOK_ROWS, LANES), lambda b, t: (b * nt + t, 0, 0)),
                   tok, tok, tok,
                   pl.BlockSpec((N_EXPERTS, LANES), lambda b, t: (0, 0))],
        out_shape=[jax.ShapeDtypeStruct((B, L, D_MODEL), F32),
                   jax.ShapeDtypeStruct((B, L, D_MODEL), BF16),
                   jax.ShapeDtypeStruct((T, TOK_ROWS, LANES), BF16),
                   jax.ShapeDtypeStruct((TOP_K, T), jnp.int32),
                   jax.ShapeDtypeStruct((TOP_K, T), F32),
                   jax.ShapeDtypeStruct((TOP_K, T), jnp.int32),
                   jax.ShapeDtypeStruct((N_EXPERTS, LANES), F32)],
        scratch_shapes=[pltpu.VMEM((N_EXPERTS, LANES), F32), _tok_scratch(ROW_TILE)],
        compiler_params=_params(("arbitrary", "arbitrary")),
        name="merge",
    )(attn, hf, hb, yg, x, mod, w_out_bf, ga, gl, lg, lb, rw_hi, rw_lo, rbias, tri)


def _swiglu(u, wg, wu):
    hg = jnp.dot(u, wg, preferred_element_type=F32)
    hu = jnp.dot(u, wu, preferred_element_type=F32)
    return hg * jax.nn.sigmoid(hg) * hu


def _dispatch_kernel(zt_ref, dest_ref, up_ref, u_ref, sg_ref, su_ref, sd_ref, xs_ref, ysh_ref, zero_scr, sem, zsem):
    @pl.when(pl.program_id(0) == 0)
    def _():
        zero_scr[...] = jnp.zeros_like(zero_scr)
        n_tiles = zt_ref.shape[0]

        def zero_tile(z):
            return pltpu.make_async_copy(zero_scr, xs_ref.at[pl.ds(z * EXP_TR, EXP_TR)], zsem)

        def start(z, carry):
            @pl.when(zt_ref[z] != 0)
            def _():
                zero_tile(z).start()
            return carry

        def wait(z, carry):
            @pl.when(zt_ref[z] != 0)
            def _():
                zero_tile(z).wait()
            return carry

        lax.fori_loop(0, n_tiles, start, 0)
        lax.fori_loop(0, n_tiles, wait, 0)

    def issue(j, carry):
        for k in range(TOP_K):
            pltpu.make_async_copy(up_ref.at[j], xs_ref.at[dest_ref[j * TOP_K + k]], sem).start(priority=k % 2)
        return carry

    lax.fori_loop(0, DISP_TM, issue, 0)

    hs = _swiglu(u_ref[...], sg_ref[...], su_ref[...])
    ysh_ref[...] = jnp.dot(hs.astype(BF16), sd_ref[...], preferred_element_type=F32).astype(BF16)

    for k in range(TOP_K):
        pltpu.make_async_copy(up_ref, xs_ref.at[pl.ds(0, DISP_TM)], sem).wait()


def _dispatch(zero_tiles, slots, up, u2, sg, su, sd, n_sorted_rows):
    T = up.shape[0]
    full = lambda a: pl.BlockSpec(a.shape, lambda i, zt: (0,) * a.ndim)
    rows = pl.BlockSpec((DISP_TM, D_MODEL), lambda i, zt: (i, 0))
    return pl.pallas_call(
        _dispatch_kernel,
        grid_spec=pltpu.PrefetchScalarGridSpec(
            num_scalar_prefetch=1,
            grid=(T // DISP_TM,),
            in_specs=[pl.BlockSpec((DISP_TM * TOP_K,), lambda i, zt: (i,), memory_space=pltpu.SMEM),
                      pl.BlockSpec((DISP_TM, TOK_ROWS, LANES), lambda i, zt: (i, 0, 0)),
                      rows, full(sg), full(su), full(sd)],
            out_specs=[pl.BlockSpec(memory_space=pl.ANY), rows],
            scratch_shapes=[pltpu.VMEM((EXP_TR, TOK_ROWS, LANES), BF16),
                            pltpu.SemaphoreType.DMA(()), pltpu.SemaphoreType.DMA(())],
        ),
        out_shape=[jax.ShapeDtypeStruct((n_sorted_rows, TOK_ROWS, LANES), BF16),
                   jax.ShapeDtypeStruct((T, D_MODEL), BF16)],
        compiler_params=_params(("arbitrary",)),
        name="dispatch",
    )(zero_tiles, slots, up, u2, sg, su, sd)


def _expert_kernel(te_ref, nu_ref, nxt_ref, par_ref, xs_ref, wg_hbm, wu_hbm, wd_hbm, ys_ref,
                   wg32, wu32, wd32, wgb_ref, wub_ref, wdb_ref, tok_scr, wsem):
    i = pl.program_id(0)
    n_used = nu_ref[0]

    def weight_copies(e, s):
        return [pltpu.make_async_copy(hbm.at[e], stage.at[s], wsem.at[s])
                for hbm, stage in ((wg_hbm, wg32), (wu_hbm, wu32), (wd_hbm, wd32))]

    @pl.when(i == 0)
    def _():
        for cp in weight_copies(te_ref[0], 0):
            cp.start(priority=1)

    @pl.when(i < n_used)
    def _():
        e = te_ref[i]

        first_of_expert = (i == 0) | (e != te_ref[jnp.maximum(i - 1, 0)])
        for s in range(2):
            @pl.when(first_of_expert & (par_ref[i] == s))
            def _():
                for cp in weight_copies(e, s):
                    cp.wait()
                for src, dst in ((wg32, wgb_ref), (wu32, wub_ref), (wd32, wdb_ref)):
                    rows = dst.shape[0] // CAST_CHUNKS

                    def cast_chunk(r, carry, src=src, dst=dst, rows=rows):
                        sl = pl.ds(pl.multiple_of(r * rows, rows), rows)
                        dst[sl, :] = src[s, sl, :].astype(BF16)
                        return carry

                    lax.fori_loop(0, CAST_CHUNKS, cast_chunk, 0)

                @pl.when(nxt_ref[i] >= 0)
                def _():
                    for cp in weight_copies(nxt_ref[i], 1 - s):
                        cp.start(priority=1)

        _token_tiles(tok_scr)[...] = xs_ref[...].astype(F32)
        x = jnp.concatenate([c.astype(BF16) for c in _from_token_major(tok_scr)], axis=1)
        h = _swiglu(x, wgb_ref[...], wub_ref[...])
        y = jnp.dot(h.astype(BF16), wdb_ref[...], preferred_element_type=F32)
        _to_token_major(y, tok_scr)
        ys_ref[...] = _token_tiles(tok_scr)[...].astype(BF16)

    @pl.when(i >= n_used)
    def _():
        ys_ref[...] = jnp.zeros_like(ys_ref)


def _experts(tile_expert, n_used, next_expert, run_parity, xs, wg, wu, wd):
    n_tiles = xs.shape[0] // EXP_TR
    blk = (EXP_TR, TOK_ROWS, LANES)
    stage = lambda a: pltpu.VMEM((2,) + a.shape[1:], F32)
    cast = lambda a: pltpu.VMEM(a.shape[1:], BF16)
    hbm = pl.BlockSpec(memory_space=pl.ANY)
    return pl.pallas_call(
        _expert_kernel,
        grid_spec=pltpu.PrefetchScalarGridSpec(
            num_scalar_prefetch=4,
            grid=(n_tiles,),
            in_specs=[pl.BlockSpec(blk, lambda i, te, nu, nx, pr: (jnp.minimum(i, nu[0] - 1), 0, 0)),
                      hbm, hbm, hbm],
            out_specs=pl.BlockSpec(blk, lambda i, te, nu, nx, pr: (i, 0, 0)),
            scratch_shapes=[stage(wg), stage(wu), stage(wd), cast(wg), cast(wu), cast(wd),
                            _tok_scratch(EXP_TR), pltpu.SemaphoreType.DMA((2,))],
        ),
        out_shape=jax.ShapeDtypeStruct(xs.shape, BF16),
        compiler_params=_params(("arbitrary",)),
        name="experts",
    )(tile_expert, n_used, next_expert, run_parity, xs, wg, wu, wd)


def _combine_kernel(dcur_ref, dnext_ref, wv_ref, ysh_ref, xn_ref, g2_ref, lg_ref, lb_ref,
                    ys_ref, o_ref, buf_ref, y_scr, sem):
    i = pl.program_id(0)
    n = pl.num_programs(0)
    tm = COMB_TM
    slot = i % 2

    def issue(d_ref, sl, j):
        for k in range(TOP_K):
            pltpu.make_async_copy(ys_ref.at[d_ref[j * TOP_K + k]], buf_ref.at[sl, k, j],
                                  sem.at[sl]).start(priority=k % 2)

    @pl.when(i == 0)
    def _():
        def first(j, carry):
            issue(dcur_ref, 0, j)
            return carry
        lax.fori_loop(0, tm, first, 0)

    for k in range(TOP_K):
        pltpu.make_async_copy(ys_ref.at[pl.ds(0, tm)], buf_ref.at[slot, k], sem.at[slot]).wait()

    half = TOK_ROWS // 2

    def accumulate(j):
        lo = jnp.zeros((half, LANES), F32)
        hi = jnp.zeros((half, LANES), F32)
        for k in range(TOP_K):
            gate = wv_ref[pl.ds(j * TOP_K + k, half, stride=0), :]
            rows = buf_ref[slot, k, j].astype(F32)
            lo += gate * rows[:half]
            hi += gate * rows[half:]
        y_scr[j, 0:half, :] = lo
        y_scr[j, half:TOK_ROWS, :] = hi

    @pl.when(i + 1 < n)
    def _():
        def both(jo, carry):
            for u in range(COMB_UNROLL):
                j = jo * COMB_UNROLL + u
                issue(dnext_ref, 1 - slot, j)
                accumulate(j)
            return carry
        lax.fori_loop(0, tm // COMB_UNROLL, both, 0)

    @pl.when(i + 1 == n)
    def _():
        def last(jo, carry):
            for u in range(COMB_UNROLL):
                accumulate(jo * COMB_UNROLL + u)
            return carry
        lax.fori_loop(0, tm // COMB_UNROLL, last, 0)
    y_routed = jnp.concatenate(_from_token_major(y_scr), axis=1)

    y2 = ysh_ref[...].astype(F32) + y_routed
    z = DEEPNORM_ALPHA * xn_ref[...] + g2_ref[0] * y2
    o_ref[...] = _layer_norm(z) * lg_ref[...] + lb_ref[...]


def _combine(slots, gates, ysh, xn, g2, lg, lb, ys, tiles_per_batch):
    T = ysh.shape[0]
    tm = COMB_TM
    n = T // tm
    row = pl.BlockSpec((tm, D_MODEL), lambda i: (i, 0))
    full = lambda a: pl.BlockSpec(a.shape, lambda i: (0,) * a.ndim)
    tok = lambda f: pl.BlockSpec((tm * TOP_K,), lambda i: (f(i),), memory_space=pltpu.SMEM)
    return pl.pallas_call(
        _combine_kernel,
        grid=(n,),
        in_specs=[tok(lambda i: i), tok(lambda i: jnp.minimum(i + 1, n - 1)),
                  pl.BlockSpec((tm * TOP_K, LANES), lambda i: (i, 0)), row, row,
                  pl.BlockSpec((1, 1, D_MODEL), lambda i: (i // tiles_per_batch, 0, 0)),
                  full(lg), full(lb),
                  pl.BlockSpec(memory_space=pl.ANY)],
        out_specs=row,
        out_shape=jax.ShapeDtypeStruct((T, D_MODEL), F32),
        scratch_shapes=[pltpu.VMEM((2, TOP_K, tm, TOK_ROWS, LANES), BF16),
                        _tok_scratch(tm),
                        pltpu.SemaphoreType.DMA((2,))],
        compiler_params=_params(("arbitrary",)),
        name="combine",
    )(slots, slots, gates, ysh, xn, g2, lg, lb, ys)


def _rope_tables(L, C):
    rows = L // GRID_W
    row = np.repeat(np.arange(rows, dtype=np.float64), GRID_W)
    col = np.tile(np.arange(GRID_W, dtype=np.float64), rows)
    inv_freq = np.exp(-np.log(ROPE_THETA) * np.arange(0, ROPE_AXIS_DIM, 2, dtype=np.float64) / ROPE_AXIS_DIM)
    ang_r = row[:, None] * inv_freq
    ang_c = col[:, None] * inv_freq
    cr, sr, cc, sc = np.cos(ang_r), np.sin(ang_r), np.cos(ang_c), np.sin(ang_c)
    cos_t = np.concatenate([cr, cr, cc, cc], axis=-1)
    sin_t = np.concatenate([-sr, sr, -sc, sc], axis=-1)
    cos_t = np.concatenate([np.ones((C, HEAD_DIM)), cos_t], axis=0)
    sin_t = np.concatenate([np.zeros((C, HEAD_DIM)), sin_t], axis=0)
    return jnp.asarray(cos_t, F32), jnp.asarray(sin_t, F32)


def kernel(x, c, ctx, c_ctx, w_mod, b_mod, w_in, attn_sink, conv_w, conv_b, lru_wa, lru_ba, lru_wx, lru_bx,
           lru_lam, norm_attn_g, norm_lru_g, w_out, ln1_g, ln1_b, router_w, router_bias, exp_w_gate, exp_w_up,
           exp_w_down, sh_w_gate, sh_w_up, sh_w_down, ln2_g, ln2_b):
    B, L, D = x.shape
    C = ctx.shape[1]
    assert w_mod.shape[0] == DEPTH and D == D_MODEL and B + 1 <= SUBLANES
    row2 = lambda a: a.reshape(1, -1)

    cvec = jnp.concatenate([c, c_ctx[None], jnp.zeros((SUBLANES - B - 1, D), F32)], axis=0)
    mod = _mod(cvec, w_mod[0], row2(b_mod[0]))

    cos_t, sin_t = _rope_tables(L, C)
    q, k, v, xr, yg = _inproj(x, ctx, mod, w_in[0].astype(BF16), cos_t, sin_t)

    attn = _attention(attn_sink[0], q, k, v, L, C)

    w_gates = jnp.concatenate([lru_wa[0], lru_wx[0]], axis=-1).astype(BF16)
    bias = jnp.stack([lru_ba[0], lru_bx[0]], axis=1)
    hf, hb = _lru(xr, conv_w[0], row2(conv_b[0]), w_gates, bias, lru_lam[0], L)

    rw_hi, rw_lo = _split_bf16(router_w[0].T)
    tri = jnp.triu(jnp.ones((ROW_TILE, ROW_TILE), BF16))
    xn, u2, up, eidx, wk, pos, cnt = _merge(
        attn, hf, hb, yg, x, mod, w_out[0].astype(BF16), row2(norm_attn_g[0]), row2(norm_lru_g[0]),
        row2(ln1_g[0]), row2(ln1_b[0]), rw_hi, rw_lo, router_bias[0].reshape(-1, 1), tri, C)

    T = B * L
    n_tiles = T * TOP_K // EXP_TR + N_EXPERTS
    counts = cnt[:, 0].astype(jnp.int32)
    tiles_e = (counts + EXP_TR - 1) // EXP_TR
    tile_end = jnp.cumsum(tiles_e)
    n_used = tile_end[-1:]
    row_start = (tile_end - tiles_e) * EXP_TR
    experts = jnp.arange(N_EXPERTS, dtype=jnp.int32)
    start_of = jnp.sum(jnp.where(eidx[None] == experts[:, None, None], row_start[:, None, None], 0), axis=0)
    slots = (start_of + pos).T.reshape(-1)
    gates = jnp.broadcast_to(wk.T.reshape(-1, 1), (T * TOP_K, LANES))
    tile_id = jnp.minimum(jnp.arange(n_tiles, dtype=jnp.int32), n_used - 1)
    tile_expert = jnp.sum((tile_end[None, :] <= tile_id[:, None]).astype(jnp.int32), axis=1)

    all_tiles = jnp.arange(n_tiles, dtype=jnp.int32)
    ends_run = jnp.any((tile_end[None, :] - 1 == all_tiles[:, None]) & (tiles_e[None, :] > 0), axis=1)
    zero_tiles = (ends_run | (all_tiles >= n_used)).astype(jnp.int32)
    xs, ysh = _dispatch(zero_tiles, slots, up, u2.reshape(T, D), sh_w_gate[0].astype(BF16),
                        sh_w_up[0].astype(BF16), sh_w_down[0].astype(BF16), n_tiles * EXP_TR)
    has_rows = tiles_e > 0
    run_parity_e = (jnp.cumsum(has_rows) - has_rows) % 2
    later = has_rows[None, :] & (experts[None, :] > experts[:, None])
    next_e = jnp.min(jnp.where(later, experts[None, :], N_EXPERTS), axis=1)
    next_e = jnp.where(next_e == N_EXPERTS, -1, next_e)
    of_tile = tile_expert[:, None] == experts[None, :]
    per_tile = lambda v: jnp.sum(jnp.where(of_tile, v[None, :], 0), axis=1).astype(jnp.int32)
    ys = _experts(tile_expert, n_used, per_tile(next_e), per_tile(run_parity_e), xs,
                  exp_w_gate[0], exp_w_up[0], exp_w_down[0])
    g2 = mod[:B, 5 * D:6 * D].reshape(B, 1, D)
    out = _combine(slots, gates, ysh, xn.reshape(T, D), g2, row2(ln2_g[0]), row2(ln2_b[0]), ys, L // COMB_TM)
    return out.reshape(B, L, D)
```

```python
import functools

import jax
import jax.numpy as jnp
import numpy as np
from jax import lax
from jax.experimental import pallas as pl
from jax.experimental.pallas import tpu as pltpu

F32 = jnp.float32
BF16 = jnp.bfloat16

D_MODEL = 2048
GRID_W = 64
N_HEADS = 8
N_KV_HEADS = 2
HEAD_DIM = 128
GQA_GROUP = N_HEADS // N_KV_HEADS
ATTN_W = N_HEADS * HEAD_DIM
KV_W = N_KV_HEADS * HEAD_DIM
ATTN_SCALE = HEAD_DIM ** -0.5
BLOCK = 128
ROPE_THETA = 10000.0
ROPE_AXIS_DIM = HEAD_DIM // 2
LRU_W = D_MODEL - ATTN_W
LRU_BLOCKS = 8
LRU_BLOCK_W = LRU_W // LRU_BLOCKS
LRU_C = 8.0
N_EXPERTS = 64
TOP_K = 8
N_GROUPS = 8
GROUP_SIZE = N_EXPERTS // N_GROUPS
TOPK_GROUPS = 4
EXPERT_FF = 512
ROUTED_SCALE = 2.5
LN_EPS = 1e-6
SQRT_GUARD = 1e-30
DEPTH = 1
DEEPNORM_ALPHA = (2 * DEPTH) ** 0.25

SUBLANES = 8
LANES = 128
VMEM_LIMIT = 56 * 1024 * 1024

ROW_TILE = 256
MOD_TN = 1024
MERGE_SPLIT = 2
ATTN_STACK = GQA_GROUP
TOK_ROWS = D_MODEL // LANES
TOK_PITCH = TOK_ROWS + SUBLANES
EXP_TR = 256
CAST_CHUNKS = 16
DISP_TM = 1024
COMB_TM = 256
COMB_UNROLL = 8


def _params(sem, vmem=VMEM_LIMIT):
    return pltpu.CompilerParams(dimension_semantics=sem, vmem_limit_bytes=vmem)


def _layer_norm(x):
    mu = jnp.mean(x, axis=-1, keepdims=True)
    xc = x - mu
    var = jnp.mean(xc * xc, axis=-1, keepdims=True)
    return xc * lax.rsqrt(var + LN_EPS)


def _rms_norm(x, g):
    return x * lax.rsqrt(jnp.mean(x * x, axis=-1, keepdims=True) + LN_EPS) * g


def _sigmoid(x):
    return 0.5 * jnp.tanh(0.5 * x) + 0.5


def _split_bf16(x):
    hi = x.astype(BF16)
    lo = (x - hi.astype(F32)).astype(BF16)
    return hi, lo


def _mod_kernel(c_ref, w_ref, b_ref, o_ref):
    cv = c_ref[...]
    s = cv * jax.nn.sigmoid(cv)
    hi, lo = _split_bf16(s)
    lhs = jnp.concatenate([hi, lo], axis=0)
    r = jnp.dot(lhs, w_ref[...].astype(BF16), preferred_element_type=F32)
    o_ref[...] = r[:SUBLANES] + r[SUBLANES:] + b_ref[...]


def _mod(cvec, w_mod, b_mod):
    n = w_mod.shape[1]
    return pl.pallas_call(
        _mod_kernel,
        grid=(n // MOD_TN,),
        in_specs=[
            pl.BlockSpec((SUBLANES, D_MODEL), lambda j: (0, 0)),
            pl.BlockSpec((D_MODEL, MOD_TN), lambda j: (0, j)),
            pl.BlockSpec((1, MOD_TN), lambda j: (0, j)),
        ],
        out_specs=pl.BlockSpec((SUBLANES, MOD_TN), lambda j: (0, j)),
        out_shape=jax.ShapeDtypeStruct((SUBLANES, n), F32),
        compiler_params=_params(("arbitrary",)),
        name="mod",
    )(cvec, w_mod, b_mod)


def _rope(xh, cos, sin, even_block):
    partner = jnp.where(even_block, pltpu.roll(xh, 96, 1), pltpu.roll(xh, 32, 1))
    return xh * cos + partner * sin


def _inproj_kernel(x_ref, ctx_ref, mod_ref, w_ref, cos_ref, sin_ref,
                   q_ref, k_ref, v_ref, xr_ref, yg_ref):
    b = pl.program_id(0)
    t = pl.program_id(1)
    is_ctx = t == 0
    xin = jnp.where(is_ctx, ctx_ref[0], x_ref[0])
    r = jnp.where(is_ctx, 2, b)
    shift = mod_ref[pl.ds(r, 1), 0:D_MODEL]
    scale = mod_ref[pl.ds(r, 1), D_MODEL:2 * D_MODEL]
    u = (_layer_norm(xin) * (1.0 + scale) + shift).astype(BF16)

    cos = cos_ref[...]
    sin = sin_ref[...]
    lane = lax.broadcasted_iota(jnp.int32, (ROW_TILE, HEAD_DIM), 1)
    even_block = (lane % 64) < 32

    def proj(c0, c1):
        return jnp.dot(u, w_ref[:, c0:c1], preferred_element_type=F32)

    head = lambda y, h: y[:, h * HEAD_DIM:(h + 1) * HEAD_DIM]
    for p in range(N_HEADS // 2):
        qq = proj(2 * p * HEAD_DIM, (2 * p + 2) * HEAD_DIM)
        for h in range(2):
            q_ref[0, :, (2 * p + h) * HEAD_DIM:(2 * p + h + 1) * HEAD_DIM] = (
                _rope(head(qq, h), cos, sin, even_block) * ATTN_SCALE).astype(BF16)
    kk = proj(ATTN_W, ATTN_W + KV_W)
    for h in range(N_KV_HEADS):
        k_ref[0, :, h * HEAD_DIM:(h + 1) * HEAD_DIM] = _rope(head(kk, h), cos, sin, even_block).astype(BF16)
    v_ref[0] = proj(ATTN_W + KV_W, ATTN_W + 2 * KV_W).astype(BF16)
    c0 = ATTN_W + 2 * KV_W
    xr_ref[0] = proj(c0, c0 + LRU_W)
    yg_ref[0] = proj(c0 + LRU_W, c0 + 2 * LRU_W)


def _inproj(x, ctx, mod, w_in_bf, cos_t, sin_t):
    B, L, _ = x.shape
    C = ctx.shape[1]
    assert C == ROW_TILE and L % ROW_TILE == 0
    nt = L // ROW_TILE + 1
    rows = L + C
    d_in = w_in_bf.shape[1]
    out = lambda w, dt: jax.ShapeDtypeStruct((B, rows, w), dt)
    ospec = lambda w: pl.BlockSpec((1, ROW_TILE, w), lambda b, t: (b, t, 0))
    return pl.pallas_call(
        _inproj_kernel,
        grid=(B, nt),
        in_specs=[
            pl.BlockSpec((1, ROW_TILE, D_MODEL), lambda b, t: (b, jnp.maximum(t - 1, 0), 0)),
            pl.BlockSpec((1, ROW_TILE, D_MODEL), lambda b, t: (b, 0, 0)),
            pl.BlockSpec((SUBLANES, 2 * D_MODEL), lambda b, t: (0, 0)),
            pl.BlockSpec((D_MODEL, d_in), lambda b, t: (0, 0)),
            pl.BlockSpec((ROW_TILE, HEAD_DIM), lambda b, t: (t, 0)),
            pl.BlockSpec((ROW_TILE, HEAD_DIM), lambda b, t: (t, 0)),
        ],
        out_specs=[ospec(ATTN_W), ospec(KV_W), ospec(KV_W), ospec(LRU_W), ospec(LRU_W)],
        out_shape=[out(ATTN_W, BF16), out(KV_W, BF16), out(KV_W, BF16), out(LRU_W, F32), out(LRU_W, F32)],
        compiler_params=_params(("arbitrary", "arbitrary")),
        name="inproj",
    )(x, ctx, mod, w_in_bf, cos_t, sin_t)


def _attn_kernel(sink_ref, q_ref, kp_ref, kc_ref, kn_ref, vp_ref, vc_ref, vn_ref, kx_ref, vx_ref, o_ref):
    n = pl.program_id(1)
    nb = pl.num_programs(1)
    rows = ATTN_STACK * BLOCK
    qi = lax.broadcasted_iota(jnp.int32, (rows, BLOCK), 0) % BLOCK
    kj = lax.broadcasted_iota(jnp.int32, (rows, BLOCK), 1)
    prev_ok = (kj >= qi) & (n > 0)
    next_ok = (kj <= qi) & (n < nb - 1)
    grp = lax.broadcasted_iota(jnp.int32, (rows, 1), 0) // BLOCK
    neg = -jnp.inf
    for q0 in range(0, N_HEADS, ATTN_STACK):
        h = q0 // GQA_GROUP
        hs = slice(h * HEAD_DIM, (h + 1) * HEAD_DIM)
        q4 = jnp.concatenate(
            [q_ref[0, :, (q0 + g) * HEAD_DIM:(q0 + g + 1) * HEAD_DIM] for g in range(ATTN_STACK)], axis=0)
        nt = (((1,), (1,)), ((), ()))
        sp = jnp.where(prev_ok, lax.dot_general(q4, kp_ref[0, :, hs], nt, preferred_element_type=F32), neg)
        sc = lax.dot_general(q4, kc_ref[0, :, hs], nt, preferred_element_type=F32)
        sn = jnp.where(next_ok, lax.dot_general(q4, kn_ref[0, :, hs], nt, preferred_element_type=F32), neg)
        sx = lax.dot_general(q4, kx_ref[0, :, hs], nt, preferred_element_type=F32)
        sink = jnp.zeros((rows, 1), F32)
        for g in range(ATTN_STACK):
            sink = jnp.where(grp == g, sink_ref[q0 + g], sink)
        lane_chunks = lambda a: [a[:, c * BLOCK:(c + 1) * BLOCK] for c in range(a.shape[1] // BLOCK)]
        fold = lambda op, parts: functools.reduce(op, parts)
        m = jnp.max(fold(jnp.maximum, [sp, sc, sn] + lane_chunks(sx)), -1, keepdims=True)
        m = jnp.maximum(m, sink)
        pp = jnp.exp(sp - m)
        pc = jnp.exp(sc - m)
        pn = jnp.exp(sn - m)
        px = jnp.exp(sx - m)
        denom = jnp.sum(fold(jnp.add, [pp, pc, pn] + lane_chunks(px)), -1, keepdims=True) + jnp.exp(sink - m)
        acc = jnp.dot(pp.astype(BF16), vp_ref[0, :, hs], preferred_element_type=F32)
        acc += jnp.dot(pc.astype(BF16), vc_ref[0, :, hs], preferred_element_type=F32)
        acc += jnp.dot(pn.astype(BF16), vn_ref[0, :, hs], preferred_element_type=F32)
        acc += jnp.dot(px.astype(BF16), vx_ref[0, :, hs], preferred_element_type=F32)
        o = acc / denom
        for g in range(ATTN_STACK):
            c0 = (q0 + g) * HEAD_DIM
            o_ref[0, :, c0:c0 + HEAD_DIM] = o[g * BLOCK:(g + 1) * BLOCK]


def _attention(sink, q, k, v, L, C):
    B = q.shape[0]
    nb = L // BLOCK
    off = C // BLOCK
    cur = lambda b, n: (b, n + off, 0)
    prv = lambda b, n: (b, jnp.maximum(n - 1, 0) + off, 0)
    nxt = lambda b, n: (b, jnp.minimum(n + 1, nb - 1) + off, 0)
    kv = lambda im: pl.BlockSpec((1, BLOCK, KV_W), im)
    cx = pl.BlockSpec((1, C, KV_W), lambda b, n: (b, 0, 0))
    return pl.pallas_call(
        _attn_kernel,
        grid=(B, nb),
        in_specs=[
            pl.BlockSpec(memory_space=pltpu.SMEM),
            pl.BlockSpec((1, BLOCK, ATTN_W), cur),
            kv(prv), kv(cur), kv(nxt), kv(prv), kv(cur), kv(nxt), cx, cx,
        ],
        out_specs=pl.BlockSpec((1, BLOCK, ATTN_W), lambda b, n: (b, n, 0)),
        out_shape=jax.ShapeDtypeStruct((B, L, ATTN_W), F32),
        compiler_params=_params(("arbitrary", "arbitrary")),
        name="attn",
    )(sink, q, k, k, k, v, v, v, k, v)


def _lru_coeffs(x_ref, p_ref, n_ref, tile, d, cw_ref, cb_ref, w_ref, bias_ref, lam_ref,
                ext_scr, a_scr, b_scr, n_lat_tiles):
    tm = ROW_TILE
    prev_ok = tile >= 2
    next_ok = (tile >= 1) & (tile < n_lat_tiles)
    ext_scr[0:SUBLANES] = jnp.where(prev_ok, p_ref[0], 0.0)
    ext_scr[SUBLANES:SUBLANES + tm] = x_ref[0]
    ext_scr[SUBLANES + tm:2 * SUBLANES + tm] = jnp.where(next_ok, n_ref[0], 0.0)
    xc = cb_ref[...] + cw_ref[2:3] * x_ref[0]
    xc += cw_ref[0:1] * ext_scr[SUBLANES - 2:SUBLANES - 2 + tm]
    xc += cw_ref[1:2] * ext_scr[SUBLANES - 1:SUBLANES - 1 + tm]
    xc += cw_ref[3:4] * ext_scr[SUBLANES + 1:SUBLANES + 1 + tm]
    xcb = xc.astype(BF16)
    lam = lam_ref[d:d + 1]
    sp = jnp.maximum(-lam, 0.0) + jnp.log1p(jnp.exp(-jnp.abs(lam)))
    for n in range(LRU_BLOCKS):
        cs = slice(n * LRU_BLOCK_W, (n + 1) * LRU_BLOCK_W)
        z = jnp.dot(xcb[:, cs], w_ref[d, n], preferred_element_type=F32)
        r = _sigmoid(z[:, :LRU_BLOCK_W] + bias_ref[d, 0:1, cs])
        i = _sigmoid(z[:, LRU_BLOCK_W:] + bias_ref[d, 1:2, cs])
        log_a = -LRU_C * r * sp[:, cs]
        a = jnp.exp(log_a)
        a_scr[:, cs] = a
        v = 1.0 - a * a
        b_scr[:, cs] = v * lax.rsqrt(jnp.maximum(v, SQRT_GUARD)) * (i * xc[:, cs])


def _lru_scan(a_scr, b_scr, h_ref, state_ref, reset, reverse):
    groups = ROW_TILE // SUBLANES
    row = lax.broadcasted_iota(jnp.int32, (SUBLANES, LRU_W), 0)
    carry0 = jnp.where(reset, 0.0, state_ref[...])

    def body(g, carry):
        gi = (groups - 1 - g) if reverse else g
        r0 = pl.multiple_of(gi * SUBLANES, SUBLANES)
        A = a_scr[pl.ds(r0, SUBLANES), :]
        Bv = b_scr[pl.ds(r0, SUBLANES), :]
        for s in (1, 2, 4):
            if reverse:
                sh, m = SUBLANES - s, row < SUBLANES - s
            else:
                sh, m = s, row >= s
            A_sh = pltpu.roll(A, sh, 0)
            B_sh = pltpu.roll(Bv, sh, 0)
            Bv = jnp.where(m, A * B_sh + Bv, Bv)
            A = jnp.where(m, A * A_sh, A)
        h = Bv + A * carry
        h_ref[0, pl.ds(r0, SUBLANES), :] = h
        last = h[0:1] if reverse else h[SUBLANES - 1:SUBLANES]
        return jnp.broadcast_to(last, (SUBLANES, LRU_W))

    state_ref[...] = lax.fori_loop(0, groups, body, carry0)


def _lru_kernel(xf_ref, xfp_ref, xfn_ref, xb_ref, xbp_ref, xbn_ref, cw_ref, cb_ref, w_ref, bias_ref, lam_ref,
                hf_ref, hb_ref, sf_ref, sb_ref, ext_scr, a_scr, b_scr, *, n_lat_tiles):
    t = pl.program_id(1)
    reset = t == 0
    bt = jnp.where(t == 0, 0, n_lat_tiles + 1 - t)
    _lru_coeffs(xf_ref, xfp_ref, xfn_ref, t, 0, cw_ref, cb_ref, w_ref, bias_ref, lam_ref,
                ext_scr, a_scr, b_scr, n_lat_tiles)
    _lru_scan(a_scr, b_scr, hf_ref, sf_ref, reset, False)
    _lru_coeffs(xb_ref, xbp_ref, xbn_ref, bt, 1, cw_ref, cb_ref, w_ref, bias_ref, lam_ref,
                ext_scr, a_scr, b_scr, n_lat_tiles)
    _lru_scan(a_scr, b_scr, hb_ref, sb_ref, reset, True)


def _lru(xr, conv_w, conv_b, w_gates, bias, lam, L):
    B, rows, _ = xr.shape
    nl = L // ROW_TILE
    nt = nl + 1
    per = ROW_TILE // SUBLANES
    n8 = rows // SUBLANES
    ft = lambda b, t: t
    btile = lambda b, t: jnp.where(t == 0, 0, nl + 1 - t)
    main = lambda f: pl.BlockSpec((1, ROW_TILE, LRU_W), lambda b, t: (b, f(b, t), 0))
    prev = lambda f: pl.BlockSpec((1, SUBLANES, LRU_W), lambda b, t: (b, jnp.maximum(f(b, t) * per - 1, 0), 0))
    nxt = lambda f: pl.BlockSpec((1, SUBLANES, LRU_W),
                                 lambda b, t: (b, jnp.minimum((f(b, t) + 1) * per, n8 - 1), 0))
    full = lambda a: pl.BlockSpec(a.shape, lambda b, t: (0,) * a.ndim)
    return pl.pallas_call(
        functools.partial(_lru_kernel, n_lat_tiles=nl),
        grid=(B, nt),
        in_specs=[main(ft), prev(ft), nxt(ft), main(btile), prev(btile), nxt(btile),
                  full(conv_w), full(conv_b), full(w_gates), full(bias), full(lam)],
        out_specs=[
            pl.BlockSpec((1, ROW_TILE, LRU_W), lambda b, t: (b, jnp.maximum(t - 1, 0), 0)),
            pl.BlockSpec((1, ROW_TILE, LRU_W), lambda b, t: (b, nl - jnp.maximum(t, 1), 0)),
        ],
        out_shape=[jax.ShapeDtypeStruct((B, L, LRU_W), F32)] * 2,
        scratch_shapes=[
            pltpu.VMEM((SUBLANES, LRU_W), F32), pltpu.VMEM((SUBLANES, LRU_W), F32),
            pltpu.VMEM((ROW_TILE + 2 * SUBLANES, LRU_W), F32),
            pltpu.VMEM((ROW_TILE, LRU_W), F32), pltpu.VMEM((ROW_TILE, LRU_W), F32),
        ],
        compiler_params=_params(("arbitrary", "arbitrary")),
        name="lru",
    )(xr, xr, xr, xr, xr, xr, conv_w, conv_b, w_gates, bias, lam)


def _route(scores, sel):
    tm = scores.shape[1]
    neg = -jnp.inf
    iota_g = lax.broadcasted_iota(jnp.int32, (GROUP_SIZE, tm), 0)
    grp_score = []
    for g in range(N_GROUPS):
        sg = sel[g * GROUP_SIZE:(g + 1) * GROUP_SIZE]
        m1 = jnp.max(sg, axis=0, keepdims=True)
        first = jnp.min(jnp.where(sg == m1, iota_g, GROUP_SIZE), axis=0, keepdims=True)
        m2 = jnp.max(jnp.where(iota_g == first, neg, sg), axis=0, keepdims=True)
        grp_score.append(m1 + m2)
    masked = []
    for g in range(N_GROUPS):
        rank = jnp.zeros((1, tm), jnp.int32)
        for o in range(N_GROUPS):
            if o == g:
                continue
            ahead = (grp_score[o] > grp_score[g]) if o > g else (grp_score[o] >= grp_score[g])
            rank += ahead.astype(jnp.int32)
        keep = rank < TOPK_GROUPS
        masked.append(jnp.where(keep, sel[g * GROUP_SIZE:(g + 1) * GROUP_SIZE], neg))
    cand = jnp.concatenate(masked, axis=0)
    iota_e = lax.broadcasted_iota(jnp.int32, (N_EXPERTS, tm), 0)
    chosen = jnp.zeros((N_EXPERTS, tm), jnp.bool_)
    picks = []
    for _ in range(TOP_K):
        cur = jnp.where(chosen, neg, cand)
        m = jnp.max(cur, axis=0, keepdims=True)
        idx = jnp.min(jnp.where((cur == m) & jnp.logical_not(chosen), iota_e, N_EXPERTS), axis=0, keepdims=True)
        chosen = chosen | (iota_e == idx)
        picks.append(idx)
    w = jnp.where(chosen, scores, 0.0)
    gates = w / jnp.sum(w, axis=0, keepdims=True) * ROUTED_SCALE
    return gates, chosen, picks


def _tok_scratch(rows):
    return pltpu.VMEM((rows, TOK_PITCH, LANES), F32)


def _to_token_major(y, scr):
    rows = scr.shape[0]
    flat = scr.reshape(rows * TOK_PITCH, LANES)
    for s in range(TOK_ROWS):
        flat[pl.ds(s, rows, stride=TOK_PITCH), :] = y[:, s * LANES:(s + 1) * LANES]


def _from_token_major(scr):
    rows = scr.shape[0]
    flat = scr.reshape(rows * TOK_PITCH, LANES)
    return [flat[pl.ds(s, rows, stride=TOK_PITCH), :] for s in range(TOK_ROWS)]


def _token_tiles(scr):
    return scr.at[:, 0:TOK_ROWS, :]


def _merge_kernel(attn_ref, hf_ref, hb_ref, yg_ref, x_ref, mod_ref, wout_ref, ga_ref, gl_ref, lg_ref, lb_ref,
                  rwh_ref, rwl_ref, rb_ref, tri_ref, xn_ref, u2_ref, up_ref, eidx_ref, wk_ref, pos_ref, cnt_ref,
                  carry_ref, tok_scr):
    b = pl.program_id(0)

    @pl.when((b == 0) & (pl.program_id(1) == 0))
    def _():
        carry_ref[...] = jnp.zeros_like(carry_ref)

    D = D_MODEL
    mrow = lambda i: mod_ref[pl.ds(b, 1), i * D:(i + 1) * D]
    nt = (((1,), (1,)), ((), ()))
    u2_parts, logit_parts = [], []
    sub = ROW_TILE // MERGE_SPLIT
    for p in range(MERGE_SPLIT):
        rs = slice(p * sub, (p + 1) * sub)
        lru_y = (hf_ref[0, rs] + hb_ref[0, rs]) * jax.nn.gelu(yg_ref[0, rs])
        na = _rms_norm(attn_ref[0, rs], ga_ref[...]).astype(BF16)
        nl = _rms_norm(lru_y, gl_ref[...]).astype(BF16)
        y1 = jnp.dot(na, wout_ref[0:ATTN_W], preferred_element_type=F32)
        y1 += jnp.dot(nl, wout_ref[ATTN_W:D], preferred_element_type=F32)
        xn = _layer_norm(DEEPNORM_ALPHA * x_ref[0, rs] + mrow(2) * y1) * lg_ref[...] + lb_ref[...]
        xn_ref[0, rs] = xn
        u2_p = _layer_norm(xn) * (1.0 + mrow(4)) + mrow(3)
        u_hi, u_lo = _split_bf16(u2_p)
        u2_ref[0, rs] = u_hi
        lg_p = lax.dot_general(rwh_ref[...], u_hi, nt, preferred_element_type=F32)
        lg_p += lax.dot_general(rwh_ref[...], u_lo, nt, preferred_element_type=F32)
        lg_p += lax.dot_general(rwl_ref[...], u_hi, nt, preferred_element_type=F32)
        u2_parts.append(u2_p)
        logit_parts.append(lg_p)
    u2 = jnp.concatenate(u2_parts, axis=0)
    _to_token_major(u2, tok_scr)
    up_ref[...] = _token_tiles(tok_scr)[...].astype(BF16)
    scores = jax.nn.sigmoid(jnp.concatenate(logit_parts, axis=1))
    gates, chosen, picks = _route(scores, scores + rb_ref[...])
    sel01 = jnp.where(chosen, 1.0, 0.0)
    incl = jnp.dot(sel01.astype(BF16), tri_ref[...], preferred_element_type=F32)
    rank = carry_ref[:, 0:1] + incl - sel01
    carry_ref[...] = carry_ref[...] + incl[:, ROW_TILE - 1:ROW_TILE]
    cnt_ref[...] = carry_ref[...]
    iota_e = lax.broadcasted_iota(jnp.int32, (N_EXPERTS, ROW_TILE), 0)
    for kk, idx in enumerate(picks):
        hit = iota_e == idx
        eidx_ref[kk:kk + 1, :] = idx
        wk_ref[kk:kk + 1, :] = jnp.sum(jnp.where(hit, gates, 0.0), axis=0, keepdims=True)
        pos_ref[kk:kk + 1, :] = jnp.sum(jnp.where(hit, rank, 0.0), axis=0, keepdims=True).astype(jnp.int32)


def _merge(attn, hf, hb, yg, x, mod, w_out_bf, ga, gl, lg, lb, rw_hi, rw_lo, rbias, tri, C):
    B, L, _ = x.shape
    T = B * L
    nt = L // ROW_TILE
    off = C // ROW_TILE
    row = lambda w, o=0: pl.BlockSpec((1, ROW_TILE, w), lambda b, t: (b, t + o, 0))
    full = lambda a: pl.BlockSpec(a.shape, lambda b, t: (0,) * a.ndim)
    tok = pl.BlockSpec((TOP_K, ROW_TILE), lambda b, t: (0, b * nt + t))
    return pl.pallas_call(
        _merge_kernel,
        grid=(B, nt),
        in_specs=[row(ATTN_W), row(LRU_W), row(LRU_W), row(LRU_W, off), row(D_MODEL),
                  full(mod), full(w_out_bf), full(ga), full(gl), full(lg), full(lb),
                  full(rw_hi), full(rw_lo), full(rbias), full(tri)],
        out_specs=[row(D_MODEL), row(D_MODEL),
                   pl.BlockSpec((ROW_TILE, TOK_ROWS, LANES), lambda b, t: (b * nt + t, 0, 0)),
                   tok, tok, tok,
                   pl.BlockSpec((N_EXPERTS, LANES), lambda b, t: (0, 0))],
        out_shape=[jax.ShapeDtypeStruct((B, L, D_MODEL), F32),
                   jax.ShapeDtypeStruct((B, L, D_MODEL), BF16),
                   jax.ShapeDtypeStruct((T, TOK_ROWS, LANES), BF16),
                   jax.ShapeDtypeStruct((TOP_K, T), jnp.int32),
                   jax.ShapeDtypeStruct((TOP_K, T), F32),
                   jax.ShapeDtypeStruct((TOP_K, T), jnp.int32),
                   jax.ShapeDtypeStruct((N_EXPERTS, LANES), F32)],
        scratch_shapes=[pltpu.VMEM((N_EXPERTS, LANES), F32), _tok_scratch(ROW_TILE)],
        compiler_params=_params(("arbitrary", "arbitrary")),
        name="merge",
    )(attn, hf, hb, yg, x, mod, w_out_bf, ga, gl, lg, lb, rw_hi, rw_lo, rbias, tri)


def _swiglu(u, wg, wu):
    hg = jnp.dot(u, wg, preferred_element_type=F32)
    hu = jnp.dot(u, wu, preferred_element_type=F32)
    return hg * jax.nn.sigmoid(hg) * hu


def _dispatch_kernel(zt_ref, dest_ref, up_ref, u_ref, sg_ref, su_ref, sd_ref, xs_ref, ysh_ref, zero_scr, sem, zsem):
    @pl.when(pl.program_id(0) == 0)
    def _():
        zero_scr[...] = jnp.zeros_like(zero_scr)
        n_tiles = zt_ref.shape[0]

        def zero_tile(z):
            return pltpu.make_async_copy(zero_scr, xs_ref.at[pl.ds(z * EXP_TR, EXP_TR)], zsem)

        def start(z, carry):
            @pl.when(zt_ref[z] != 0)
            def _():
                zero_tile(z).start()
            return carry

        def wait(z, carry):
            @pl.when(zt_ref[z] != 0)
            def _():
                zero_tile(z).wait()
            return carry

        lax.fori_loop(0, n_tiles, start, 0)
        lax.fori_loop(0, n_tiles, wait, 0)

    def issue(j, carry):
        for k in range(TOP_K):
            pltpu.make_async_copy(up_ref.at[j], xs_ref.at[dest_ref[j * TOP_K + k]], sem).start(priority=k % 2)
        return carry

    lax.fori_loop(0, DISP_TM, issue, 0)

    hs = _swiglu(u_ref[...], sg_ref[...], su_ref[...])
    ysh_ref[...] = jnp.dot(hs.astype(BF16), sd_ref[...], preferred_element_type=F32).astype(BF16)

    for k in range(TOP_K):
        pltpu.make_async_copy(up_ref, xs_ref.at[pl.ds(0, DISP_TM)], sem).wait()


def _dispatch(zero_tiles, slots, up, u2, sg, su, sd, n_sorted_rows):
    T = up.shape[0]
    full = lambda a: pl.BlockSpec(a.shape, lambda i, zt: (0,) * a.ndim)
    rows = pl.BlockSpec((DISP_TM, D_MODEL), lambda i, zt: (i, 0))
    return pl.pallas_call(
        _dispatch_kernel,
        grid_spec=pltpu.PrefetchScalarGridSpec(
            num_scalar_prefetch=1,
            grid=(T // DISP_TM,),
            in_specs=[pl.BlockSpec((DISP_TM * TOP_K,), lambda i, zt: (i,), memory_space=pltpu.SMEM),
                      pl.BlockSpec((DISP_TM, TOK_ROWS, LANES), lambda i, zt: (i, 0, 0)),
                      rows, full(sg), full(su), full(sd)],
            out_specs=[pl.BlockSpec(memory_space=pl.ANY), rows],
            scratch_shapes=[pltpu.VMEM((EXP_TR, TOK_ROWS, LANES), BF16),
                            pltpu.SemaphoreType.DMA(()), pltpu.SemaphoreType.DMA(())],
        ),
        out_shape=[jax.ShapeDtypeStruct((n_sorted_rows, TOK_ROWS, LANES), BF16),
                   jax.ShapeDtypeStruct((T, D_MODEL), BF16)],
        compiler_params=_params(("arbitrary",)),
        name="dispatch",
    )(zero_tiles, slots, up, u2, sg, su, sd)


def _expert_kernel(te_ref, nu_ref, nxt_ref, par_ref, xs_ref, wg_hbm, wu_hbm, wd_hbm, ys_ref,
                   wg32, wu32, wd32, wgb_ref, wub_ref, wdb_ref, tok_scr, wsem):
    i = pl.program_id(0)
    n_used = nu_ref[0]

    def weight_copies(e, s):
        return [pltpu.make_async_copy(hbm.at[e], stage.at[s], wsem.at[s])
                for hbm, stage in ((wg_hbm, wg32), (wu_hbm, wu32), (wd_hbm, wd32))]

    @pl.when(i == 0)
    def _():
        for cp in weight_copies(te_ref[0], 0):
            cp.start(priority=1)

    @pl.when(i < n_used)
    def _():
        e = te_ref[i]

        first_of_expert = (i == 0) | (e != te_ref[jnp.maximum(i - 1, 0)])
        for s in range(2):
            @pl.when(first_of_expert & (par_ref[i] == s))
            def _():
                for cp in weight_copies(e, s):
                    cp.wait()
                for src, dst in ((wg32, wgb_ref), (wu32, wub_ref), (wd32, wdb_ref)):
                    rows = dst.shape[0] // CAST_CHUNKS

                    def cast_chunk(r, carry, src=src, dst=dst, rows=rows):
                        sl = pl.ds(pl.multiple_of(r * rows, rows), rows)
                        dst[sl, :] = src[s, sl, :].astype(BF16)
                        return carry

                    lax.fori_loop(0, CAST_CHUNKS, cast_chunk, 0)

                @pl.when(nxt_ref[i] >= 0)
                def _():
                    for cp in weight_copies(nxt_ref[i], 1 - s):
                        cp.start(priority=1)

        _token_tiles(tok_scr)[...] = xs_ref[...].astype(F32)
        x = jnp.concatenate([c.astype(BF16) for c in _from_token_major(tok_scr)], axis=1)
        h = _swiglu(x, wgb_ref[...], wub_ref[...])
        y = jnp.dot(h.astype(BF16), wdb_ref[...], preferred_element_type=F32)
        _to_token_major(y, tok_scr)
        ys_ref[...] = _token_tiles(tok_scr)[...].astype(BF16)

    @pl.when(i >= n_used)
    def _():
        ys_ref[...] = jnp.zeros_like(ys_ref)


def _experts(tile_expert, n_used, next_expert, run_parity, xs, wg, wu, wd):
    n_tiles = xs.shape[0] // EXP_TR
    blk = (EXP_TR, TOK_ROWS, LANES)
    stage = lambda a: pltpu.VMEM((2,) + a.shape[1:], F32)
    cast = lambda a: pltpu.VMEM(a.shape[1:], BF16)
    hbm = pl.BlockSpec(memory_space=pl.ANY)
    return pl.pallas_call(
        _expert_kernel,
        grid_spec=pltpu.PrefetchScalarGridSpec(
            num_scalar_prefetch=4,
            grid=(n_tiles,),
            in_specs=[pl.BlockSpec(blk, lambda i, te, nu, nx, pr: (jnp.minimum(i, nu[0] - 1), 0, 0)),
                      hbm, hbm, hbm],
            out_specs=pl.BlockSpec(blk, lambda i, te, nu, nx, pr: (i, 0, 0)),
            scratch_shapes=[stage(wg), stage(wu), stage(wd), cast(wg), cast(wu), cast(wd),
                            _tok_scratch(EXP_TR), pltpu.SemaphoreType.DMA((2,))],
        ),
        out_shape=jax.ShapeDtypeStruct(xs.shape, BF16),
        compiler_params=_params(("arbitrary",)),
        name="experts",
    )(tile_expert, n_used, next_expert, run_parity, xs, wg, wu, wd)


def _combine_kernel(dcur_ref, dnext_ref, wv_ref, ysh_ref, xn_ref, g2_ref, lg_ref, lb_ref,
                    ys_ref, o_ref, buf_ref, y_scr, sem):
    i = pl.program_id(0)
    n = pl.num_programs(0)
    tm = COMB_TM
    slot = i % 2

    def issue(d_ref, sl, j):
        for k in range(TOP_K):
            pltpu.make_async_copy(ys_ref.at[d_ref[j * TOP_K + k]], buf_ref.at[sl, k, j],
                                  sem.at[sl]).start(priority=k % 2)

    @pl.when(i == 0)
    def _():
        def first(j, carry):
            issue(dcur_ref, 0, j)
            return carry
        lax.fori_loop(0, tm, first, 0)

    for k in range(TOP_K):
        pltpu.make_async_copy(ys_ref.at[pl.ds(0, tm)], buf_ref.at[slot, k], sem.at[slot]).wait()

    half = TOK_ROWS // 2

    def accumulate(j):
        lo = jnp.zeros((half, LANES), F32)
        hi = jnp.zeros((half, LANES), F32)
        for k in range(TOP_K):
            gate = wv_ref[pl.ds(j * TOP_K + k, half, stride=0), :]
            rows = buf_ref[slot, k, j].astype(F32)
            lo += gate * rows[:half]
            hi += gate * rows[half:]
        y_scr[j, 0:half, :] = lo
        y_scr[j, half:TOK_ROWS, :] = hi

    @pl.when(i + 1 < n)
    def _():
        def both(jo, carry):
            for u in range(COMB_UNROLL):
                j = jo * COMB_UNROLL + u
                issue(dnext_ref, 1 - slot, j)
                accumulate(j)
            return carry
        lax.fori_loop(0, tm // COMB_UNROLL, both, 0)

    @pl.when(i + 1 == n)
    def _():
        def last(jo, carry):
            for u in range(COMB_UNROLL):
                accumulate(jo * COMB_UNROLL + u)
            return carry
        lax.fori_loop(0, tm // COMB_UNROLL, last, 0)
    y_routed = jnp.concatenate(_from_token_major(y_scr), axis=1)

    y2 = ysh_ref[...].astype(F32) + y_routed
    z = DEEPNORM_ALPHA * xn_ref[...] + g2_ref[0] * y2
    o_ref[...] = _layer_norm(z) * lg_ref[...] + lb_ref[...]


def _combine(slots, gates, ysh, xn, g2, lg, lb, ys, tiles_per_batch):
    T = ysh.shape[0]
    tm = COMB_TM
    n = T // tm
    row = pl.BlockSpec((tm, D_MODEL), lambda i: (i, 0))
    full = lambda a: pl.BlockSpec(a.shape, lambda i: (0,) * a.ndim)
    tok = lambda f: pl.BlockSpec((tm * TOP_K,), lambda i: (f(i),), memory_space=pltpu.SMEM)
    return pl.pallas_call(
        _combine_kernel,
        grid=(n,),
        in_specs=[tok(lambda i: i), tok(lambda i: jnp.minimum(i + 1, n - 1)),
                  pl.BlockSpec((tm * TOP_K, LANES), lambda i: (i, 0)), row, row,
                  pl.BlockSpec((1, 1, D_MODEL), lambda i: (i // tiles_per_batch, 0, 0)),
                  full(lg), full(lb),
                  pl.BlockSpec(memory_space=pl.ANY)],
        out_specs=row,
        out_shape=jax.ShapeDtypeStruct((T, D_MODEL), F32),
        scratch_shapes=[pltpu.VMEM((2, TOP_K, tm, TOK_ROWS, LANES), BF16),
                        _tok_scratch(tm),
                        pltpu.SemaphoreType.DMA((2,))],
        compiler_params=_params(("arbitrary",)),
        name="combine",
    )(slots, slots, gates, ysh, xn, g2, lg, lb, ys)


def _rope_tables(L, C):
    rows = L // GRID_W
    row = np.repeat(np.arange(rows, dtype=np.float64), GRID_W)
    col = np.tile(np.arange(GRID_W, dtype=np.float64), rows)
    inv_freq = np.exp(-np.log(ROPE_THETA) * np.arange(0, ROPE_AXIS_DIM, 2, dtype=np.float64) / ROPE_AXIS_DIM)
    ang_r = row[:, None] * inv_freq
    ang_c = col[:, None] * inv_freq
    cr, sr, cc, sc = np.cos(ang_r), np.sin(ang_r), np.cos(ang_c), np.sin(ang_c)
    cos_t = np.concatenate([cr, cr, cc, cc], axis=-1)
    sin_t = np.concatenate([-sr, sr, -sc, sc], axis=-1)
    cos_t = np.concatenate([np.ones((C, HEAD_DIM)), cos_t], axis=0)
    sin_t = np.concatenate([np.zeros((C, HEAD_DIM)), sin_t], axis=0)
    return jnp.asarray(cos_t, F32), jnp.asarray(sin_t, F32)


def kernel(x, c, ctx, c_ctx, w_mod, b_mod, w_in, attn_sink, conv_w, conv_b, lru_wa, lru_ba, lru_wx, lru_bx,
           lru_lam, norm_attn_g, norm_lru_g, w_out, ln1_g, ln1_b, router_w, router_bias, exp_w_gate, exp_w_up,
           exp_w_down, sh_w_gate, sh_w_up, sh_w_down, ln2_g, ln2_b):
    B, L, D = x.shape
    C = ctx.shape[1]
    assert w_mod.shape[0] == DEPTH and D == D_MODEL and B + 1 <= SUBLANES
    row2 = lambda a: a.reshape(1, -1)

    cvec = jnp.concatenate([c, c_ctx[None], jnp.zeros((SUBLANES - B - 1, D), F32)], axis=0)
    mod = _mod(cvec, w_mod[0], row2(b_mod[0]))

    cos_t, sin_t = _rope_tables(L, C)
    q, k, v, xr, yg = _inproj(x, ctx, mod, w_in[0].astype(BF16), cos_t, sin_t)

    attn = _attention(attn_sink[0], q, k, v, L, C)

    w_gates = jnp.concatenate([lru_wa[0], lru_wx[0]], axis=-1).astype(BF16)
    bias = jnp.stack([lru_ba[0], lru_bx[0]], axis=1)
    hf, hb = _lru(xr, conv_w[0], row2(conv_b[0]), w_gates, bias, lru_lam[0], L)

    rw_hi, rw_lo = _split_bf16(router_w[0].T)
    tri = jnp.triu(jnp.ones((ROW_TILE, ROW_TILE), BF16))
    xn, u2, up, eidx, wk, pos, cnt = _merge(
        attn, hf, hb, yg, x, mod, w_out[0].astype(BF16), row2(norm_attn_g[0]), row2(norm_lru_g[0]),
        row2(ln1_g[0]), row2(ln1_b[0]), rw_hi, rw_lo, router_bias[0].reshape(-1, 1), tri, C)

    T = B * L
    n_tiles = T * TOP_K // EXP_TR + N_EXPERTS
    counts = cnt[:, 0].astype(jnp.int32)
    tiles_e = (counts + EXP_TR - 1) // EXP_TR
    tile_end = jnp.cumsum(tiles_e)
    n_used = tile_end[-1:]
    row_start = (tile_end - tiles_e) * EXP_TR
    experts = jnp.arange(N_EXPERTS, dtype=jnp.int32)
    start_of = jnp.sum(jnp.where(eidx[None] == experts[:, None, None], row_start[:, None, None], 0), axis=0)
    slots = (start_of + pos).T.reshape(-1)
    gates = jnp.broadcast_to(wk.T.reshape(-1, 1), (T * TOP_K, LANES))
    tile_id = jnp.minimum(jnp.arange(n_tiles, dtype=jnp.int32), n_used - 1)
    tile_expert = jnp.sum((tile_end[None, :] <= tile_id[:, None]).astype(jnp.int32), axis=1)

    all_tiles = jnp.arange(n_tiles, dtype=jnp.int32)
    ends_run = jnp.any((tile_end[None, :] - 1 == all_tiles[:, None]) & (tiles_e[None, :] > 0), axis=1)
    zero_tiles = (ends_run | (all_tiles >= n_used)).astype(jnp.int32)
    xs, ysh = _dispatch(zero_tiles, slots, up, u2.reshape(T, D), sh_w_gate[0].astype(BF16),
                        sh_w_up[0].astype(BF16), sh_w_down[0].astype(BF16), n_tiles * EXP_TR)
    has_rows = tiles_e > 0
    run_parity_e = (jnp.cumsum(has_rows) - has_rows) % 2
    later = has_rows[None, :] & (experts[None, :] > experts[:, None])
    next_e = jnp.min(jnp.where(later, experts[None, :], N_EXPERTS), axis=1)
    next_e = jnp.where(next_e == N_EXPERTS, -1, next_e)
    of_tile = tile_expert[:, None] == experts[None, :]
    per_tile = lambda v: jnp.sum(jnp.where(of_tile, v[None, :], 0), axis=1).astype(jnp.int32)
    ys = _experts(tile_expert, n_used, per_tile(next_e), per_tile(run_parity_e), xs,
                  exp_w_gate[0], exp_w_up[0], exp_w_down[0])
    g2 = mod[:B, 5 * D:6 * D].reshape(B, 1, D)
    out = _combine(slots, gates, ysh, xn.reshape(T, D), g2, row2(ln2_g[0]), row2(ln2_b[0]), ys, L // COMB_TM)
    return out.reshape(B, L, D)
```

```python
import functools

import jax
import jax.numpy as jnp
import numpy as np
from jax import lax
from jax.experimental import pallas as pl
from jax.experimental.pallas import tpu as pltpu

F32 = jnp.float32
BF16 = jnp.bfloat16

D_MODEL = 2048
GRID_W = 64
N_HEADS = 8
N_KV_HEADS = 2
HEAD_DIM = 128
GQA_GROUP = N_HEADS // N_KV_HEADS
ATTN_W = N_HEADS * HEAD_DIM
KV_W = N_KV_HEADS * HEAD_DIM
ATTN_SCALE = HEAD_DIM ** -0.5
BLOCK = 128
ROPE_THETA = 10000.0
ROPE_AXIS_DIM = HEAD_DIM // 2
LRU_W = D_MODEL - ATTN_W
LRU_BLOCKS = 8
LRU_BLOCK_W = LRU_W // LRU_BLOCKS
LRU_C = 8.0
N_EXPERTS = 64
TOP_K = 8
N_GROUPS = 8
GROUP_SIZE = N_EXPERTS // N_GROUPS
TOPK_GROUPS = 4
EXPERT_FF = 512
ROUTED_SCALE = 2.5
LN_EPS = 1e-6
SQRT_GUARD = 1e-30
DEPTH = 1
DEEPNORM_ALPHA = (2 * DEPTH) ** 0.25

SUBLANES = 8
LANES = 128
VMEM_LIMIT = 56 * 1024 * 1024

ROW_TILE = 256
MOD_TN = 1024
MERGE_SPLIT = 2
ATTN_STACK = GQA_GROUP
TOK_ROWS = D_MODEL // LANES
TOK_PITCH = TOK_ROWS + SUBLANES
EXP_TR = 256
CAST_CHUNKS = 16
DISP_TM = 1024
COMB_TM = 256
COMB_UNROLL = 8


def _params(sem, vmem=VMEM_LIMIT):
    return pltpu.CompilerParams(dimension_semantics=sem, vmem_limit_bytes=vmem)


def _layer_norm(x):
    mu = jnp.mean(x, axis=-1, keepdims=True)
    xc = x - mu
    var = jnp.mean(xc * xc, axis=-1, keepdims=True)
    return xc * lax.rsqrt(var + LN_EPS)


def _rms_norm(x, g):
    return x * lax.rsqrt(jnp.mean(x * x, axis=-1, keepdims=True) + LN_EPS) * g


def _sigmoid(x):
    return 0.5 * jnp.tanh(0.5 * x) + 0.5


def _split_bf16(x):
    hi = x.astype(BF16)
    lo = (x - hi.astype(F32)).astype(BF16)
    return hi, lo


def _mod_kernel(c_ref, w_ref, b_ref, o_ref):
    cv = c_ref[...]
    s = cv * jax.nn.sigmoid(cv)
    hi, lo = _split_bf16(s)
    lhs = jnp.concatenate([hi, lo], axis=0)
    r = jnp.dot(lhs, w_ref[...].astype(BF16), preferred_element_type=F32)
    o_ref[...] = r[:SUBLANES] + r[SUBLANES:] + b_ref[...]


def _mod(cvec, w_mod, b_mod):
    n = w_mod.shape[1]
    return pl.pallas_call(
        _mod_kernel,
        grid=(n // MOD_TN,),
        in_specs=[
            pl.BlockSpec((SUBLANES, D_MODEL), lambda j: (0, 0)),
            pl.BlockSpec((D_MODEL, MOD_TN), lambda j: (0, j)),
            pl.BlockSpec((1, MOD_TN), lambda j: (0, j)),
        ],
        out_specs=pl.BlockSpec((SUBLANES, MOD_TN), lambda j: (0, j)),
        out_shape=jax.ShapeDtypeStruct((SUBLANES, n), F32),
        compiler_params=_params(("arbitrary",)),
        name="mod",
    )(cvec, w_mod, b_mod)


def _rope(xh, cos, sin, even_block):
    partner = jnp.where(even_block, pltpu.roll(xh, 96, 1), pltpu.roll(xh, 32, 1))
    return xh * cos + partner * sin


def _inproj_kernel(x_ref, ctx_ref, mod_ref, w_ref, cos_ref, sin_ref,
                   q_ref, k_ref, v_ref, xr_ref, yg_ref):
    b = pl.program_id(0)
    t = pl.program_id(1)
    is_ctx = t == 0
    xin = jnp.where(is_ctx, ctx_ref[0], x_ref[0])
    r = jnp.where(is_ctx, 2, b)
    shift = mod_ref[pl.ds(r, 1), 0:D_MODEL]
    scale = mod_ref[pl.ds(r, 1), D_MODEL:2 * D_MODEL]
    u = (_layer_norm(xin) * (1.0 + scale) + shift).astype(BF16)

    cos = cos_ref[...]
    sin = sin_ref[...]
    lane = lax.broadcasted_iota(jnp.int32, (ROW_TILE, HEAD_DIM), 1)
    even_block = (lane % 64) < 32

    def proj(c0, c1):
        return jnp.dot(u, w_ref[:, c0:c1], preferred_element_type=F32)

    head = lambda y, h: y[:, h * HEAD_DIM:(h + 1) * HEAD_DIM]
    for p in range(N_HEADS // 2):
        qq = proj(2 * p * HEAD_DIM, (2 * p + 2) * HEAD_DIM)
        for h in range(2):
            q_ref[0, :, (2 * p + h) * HEAD_DIM:(2 * p + h + 1) * HEAD_DIM] = (
                _rope(head(qq, h), cos, sin, even_block) * ATTN_SCALE).astype(BF16)
    kk = proj(ATTN_W, ATTN_W + KV_W)
    for h in range(N_KV_HEADS):
        k_ref[0, :, h * HEAD_DIM:(h + 1) * HEAD_DIM] = _rope(head(kk, h), cos, sin, even_block).astype(BF16)
    v_ref[0] = proj(ATTN_W + KV_W, ATTN_W + 2 * KV_W).astype(BF16)
    c0 = ATTN_W + 2 * KV_W
    xr_ref[0] = proj(c0, c0 + LRU_W)
    yg_ref[0] = proj(c0 + LRU_W, c0 + 2 * LRU_W)


def _inproj(x, ctx, mod, w_in_bf, cos_t, sin_t):
    B, L, _ = x.shape
    C = ctx.shape[1]
    assert C == ROW_TILE and L % ROW_TILE == 0
    nt = L // ROW_TILE + 1
    rows = L + C
    d_in = w_in_bf.shape[1]
    out = lambda w, dt: jax.ShapeDtypeStruct((B, rows, w), dt)
    ospec = lambda w: pl.BlockSpec((1, ROW_TILE, w), lambda b, t: (b, t, 0))
    return pl.pallas_call(
        _inproj_kernel,
        grid=(B, nt),
        in_specs=[
            pl.BlockSpec((1, ROW_TILE, D_MODEL), lambda b, t: (b, jnp.maximum(t - 1, 0), 0)),
            pl.BlockSpec((1, ROW_TILE, D_MODEL), lambda b, t: (b, 0, 0)),
            pl.BlockSpec((SUBLANES, 2 * D_MODEL), lambda b, t: (0, 0)),
            pl.BlockSpec((D_MODEL, d_in), lambda b, t: (0, 0)),
            pl.BlockSpec((ROW_TILE, HEAD_DIM), lambda b, t: (t, 0)),
            pl.BlockSpec((ROW_TILE, HEAD_DIM), lambda b, t: (t, 0)),
        ],
        out_specs=[ospec(ATTN_W), ospec(KV_W), ospec(KV_W), ospec(LRU_W), ospec(LRU_W)],
        out_shape=[out(ATTN_W, BF16), out(KV_W, BF16), out(KV_W, BF16), out(LRU_W, F32), out(LRU_W, F32)],
        compiler_params=_params(("arbitrary", "arbitrary")),
        name="inproj",
    )(x, ctx, mod, w_in_bf, cos_t, sin_t)


def _attn_kernel(sink_ref, q_ref, kp_ref, kc_ref, kn_ref, vp_ref, vc_ref, vn_ref, kx_ref, vx_ref, o_ref):
    n = pl.program_id(1)
    nb = pl.num_programs(1)
    rows = ATTN_STACK * BLOCK
    qi = lax.broadcasted_iota(jnp.int32, (rows, BLOCK), 0) % BLOCK
    kj = lax.broadcasted_iota(jnp.int32, (rows, BLOCK), 1)
    prev_ok = (kj >= qi) & (n > 0)
    next_ok = (kj <= qi) & (n < nb - 1)
    grp = lax.broadcasted_iota(jnp.int32, (rows, 1), 0) // BLOCK
    neg = -jnp.inf
    for q0 in range(0, N_HEADS, ATTN_STACK):
        h = q0 // GQA_GROUP
        hs = slice(h * HEAD_DIM, (h + 1) * HEAD_DIM)
        q4 = jnp.concatenate(
            [q_ref[0, :, (q0 + g) * HEAD_DIM:(q0 + g + 1) * HEAD_DIM] for g in range(ATTN_STACK)], axis=0)
        nt = (((1,), (1,)), ((), ()))
        sp = jnp.where(prev_ok, lax.dot_general(q4, kp_ref[0, :, hs], nt, preferred_element_type=F32), neg)
        sc = lax.dot_general(q4, kc_ref[0, :, hs], nt, preferred_element_type=F32)
        sn = jnp.where(next_ok, lax.dot_general(q4, kn_ref[0, :, hs], nt, preferred_element_type=F32), neg)
        sx = lax.dot_general(q4, kx_ref[0, :, hs], nt, preferred_element_type=F32)
        sink = jnp.zeros((rows, 1), F32)
        for g in range(ATTN_STACK):
            sink = jnp.where(grp == g, sink_ref[q0 + g], sink)
        lane_chunks = lambda a: [a[:, c * BLOCK:(c + 1) * BLOCK] for c in range(a.shape[1] // BLOCK)]
        fold = lambda op, parts: functools.reduce(op, parts)
        m = jnp.max(fold(jnp.maximum, [sp, sc, sn] + lane_chunks(sx)), -1, keepdims=True)
        m = jnp.maximum(m, sink)
        pp = jnp.exp(sp - m)
        pc = jnp.exp(sc - m)
        pn = jnp.exp(sn - m)
        px = jnp.exp(sx - m)
        denom = jnp.sum(fold(jnp.add, [pp, pc, pn] + lane_chunks(px)), -1, keepdims=True) + jnp.exp(sink - m)
        acc = jnp.dot(pp.astype(BF16), vp_ref[0, :, hs], preferred_element_type=F32)
        acc += jnp.dot(pc.astype(BF16), vc_ref[0, :, hs], preferred_element_type=F32)
        acc += jnp.dot(pn.astype(BF16), vn_ref[0, :, hs], preferred_element_type=F32)
        acc += jnp.dot(px.astype(BF16), vx_ref[0, :, hs], preferred_element_type=F32)
        o = acc / denom
        for g in range(ATTN_STACK):
            c0 = (q0 + g) * HEAD_DIM
            o_ref[0, :, c0:c0 + HEAD_DIM] = o[g * BLOCK:(g + 1) * BLOCK]


def _attention(sink, q, k, v, L, C):
    B = q.shape[0]
    nb = L // BLOCK
    off = C // BLOCK
    cur = lambda b, n: (b, n + off, 0)
    prv = lambda b, n: (b, jnp.maximum(n - 1, 0) + off, 0)
    nxt = lambda b, n: (b, jnp.minimum(n + 1, nb - 1) + off, 0)
    kv = lambda im: pl.BlockSpec((1, BLOCK, KV_W), im)
    cx = pl.BlockSpec((1, C, KV_W), lambda b, n: (b, 0, 0))
    return pl.pallas_call(
        _attn_kernel,
        grid=(B, nb),
        in_specs=[
            pl.BlockSpec(memory_space=pltpu.SMEM),
            pl.BlockSpec((1, BLOCK, ATTN_W), cur),
            kv(prv), kv(cur), kv(nxt), kv(prv), kv(cur), kv(nxt), cx, cx,
        ],
        out_specs=pl.BlockSpec((1, BLOCK, ATTN_W), lambda b, n: (b, n, 0)),
        out_shape=jax.ShapeDtypeStruct((B, L, ATTN_W), F32),
        compiler_params=_params(("arbitrary", "arbitrary")),
        name="attn",
    )(sink, q, k, k, k, v, v, v, k, v)


def _lru_coeffs(x_ref, p_ref, n_ref, tile, d, cw_ref, cb_ref, w_ref, bias_ref, lam_ref,
                ext_scr, a_scr, b_scr, n_lat_tiles):
    tm = ROW_TILE
    prev_ok = tile >= 2
    next_ok = (tile >= 1) & (tile < n_lat_tiles)
    ext_scr[0:SUBLANES] = jnp.where(prev_ok, p_ref[0], 0.0)
    ext_scr[SUBLANES:SUBLANES + tm] = x_ref[0]
    ext_scr[SUBLANES + tm:2 * SUBLANES + tm] = jnp.where(next_ok, n_ref[0], 0.0)
    xc = cb_ref[...] + cw_ref[2:3] * x_ref[0]
    xc += cw_ref[0:1] * ext_scr[SUBLANES - 2:SUBLANES - 2 + tm]
    xc += cw_ref[1:2] * ext_scr[SUBLANES - 1:SUBLANES - 1 + tm]
    xc += cw_ref[3:4] * ext_scr[SUBLANES + 1:SUBLANES + 1 + tm]
    xcb = xc.astype(BF16)
    lam = lam_ref[d:d + 1]
    sp = jnp.maximum(-lam, 0.0) + jnp.log1p(jnp.exp(-jnp.abs(lam)))
    for n in range(LRU_BLOCKS):
        cs = slice(n * LRU_BLOCK_W, (n + 1) * LRU_BLOCK_W)
        z = jnp.dot(xcb[:, cs], w_ref[d, n], preferred_element_type=F32)
        r = _sigmoid(z[:, :LRU_BLOCK_W] + bias_ref[d, 0:1, cs])
        i = _sigmoid(z[:, LRU_BLOCK_W:] + bias_ref[d, 1:2, cs])
        log_a = -LRU_C * r * sp[:, cs]
        a = jnp.exp(log_a)
        a_scr[:, cs] = a
        v = 1.0 - a * a
        b_scr[:, cs] = v * lax.rsqrt(jnp.maximum(v, SQRT_GUARD)) * (i * xc[:, cs])


def _lru_scan(a_scr, b_scr, h_ref, state_ref, reset, reverse):
    groups = ROW_TILE // SUBLANES
    row = lax.broadcasted_iota(jnp.int32, (SUBLANES, LRU_W), 0)
    carry0 = jnp.where(reset, 0.0, state_ref[...])

    def body(g, carry):
        gi = (groups - 1 - g) if reverse else g
        r0 = pl.multiple_of(gi * SUBLANES, SUBLANES)
        A = a_scr[pl.ds(r0, SUBLANES), :]
        Bv = b_scr[pl.ds(r0, SUBLANES), :]
        for s in (1, 2, 4):
            if reverse:
                sh, m = SUBLANES - s, row < SUBLANES - s
            else:
                sh, m = s, row >= s
            A_sh = pltpu.roll(A, sh, 0)
            B_sh = pltpu.roll(Bv, sh, 0)
            Bv = jnp.where(m, A * B_sh + Bv, Bv)
            A = jnp.where(m, A * A_sh, A)
        h = Bv + A * carry
        h_ref[0, pl.ds(r0, SUBLANES), :] = h
        last = h[0:1] if reverse else h[SUBLANES - 1:SUBLANES]
        return jnp.broadcast_to(last, (SUBLANES, LRU_W))

    state_ref[...] = lax.fori_loop(0, groups, body, carry0)


def _lru_kernel(xf_ref, xfp_ref, xfn_ref, xb_ref, xbp_ref, xbn_ref, cw_ref, cb_ref, w_ref, bias_ref, lam_ref,
                hf_ref, hb_ref, sf_ref, sb_ref, ext_scr, a_scr, b_scr, *, n_lat_tiles):
    t = pl.program_id(1)
    reset = t == 0
    bt = jnp.where(t == 0, 0, n_lat_tiles + 1 - t)
    _lru_coeffs(xf_ref, xfp_ref, xfn_ref, t, 0, cw_ref, cb_ref, w_ref, bias_ref, lam_ref,
                ext_scr, a_scr, b_scr, n_lat_tiles)
    _lru_scan(a_scr, b_scr, hf_ref, sf_ref, reset, False)
    _lru_coeffs(xb_ref, xbp_ref, xbn_ref, bt, 1, cw_ref, cb_ref, w_ref, bias_ref, lam_ref,
                ext_scr, a_scr, b_scr, n_lat_tiles)
    _lru_scan(a_scr, b_scr, hb_ref, sb_ref, reset, True)


def _lru(xr, conv_w, conv_b, w_gates, bias, lam, L):
    B, rows, _ = xr.shape
    nl = L // ROW_TILE
    nt = nl + 1
    per = ROW_TILE // SUBLANES
    n8 = rows // SUBLANES
    ft = lambda b, t: t
    btile = lambda b, t: jnp.where(t == 0, 0, nl + 1 - t)
    main = lambda f: pl.BlockSpec((1, ROW_TILE, LRU_W), lambda b, t: (b, f(b, t), 0))
    prev = lambda f: pl.BlockSpec((1, SUBLANES, LRU_W), lambda b, t: (b, jnp.maximum(f(b, t) * per - 1, 0), 0))
    nxt = lambda f: pl.BlockSpec((1, SUBLANES, LRU_W),
                                 lambda b, t: (b, jnp.minimum((f(b, t) + 1) * per, n8 - 1), 0))
    full = lambda a: pl.BlockSpec(a.shape, lambda b, t: (0,) * a.ndim)
    return pl.pallas_call(
        functools.partial(_lru_kernel, n_lat_tiles=nl),
        grid=(B, nt),
        in_specs=[main(ft), prev(ft), nxt(ft), main(btile), prev(btile), nxt(btile),
                  full(conv_w), full(conv_b), full(w_gates), full(bias), full(lam)],
        out_specs=[
            pl.BlockSpec((1, ROW_TILE, LRU_W), lambda b, t: (b, jnp.maximum(t - 1, 0), 0)),
            pl.BlockSpec((1, ROW_TILE, LRU_W), lambda b, t: (b, nl - jnp.maximum(t, 1), 0)),
        ],
        out_shape=[jax.ShapeDtypeStruct((B, L, LRU_W), F32)] * 2,
        scratch_shapes=[
            pltpu.VMEM((SUBLANES, LRU_W), F32), pltpu.VMEM((SUBLANES, LRU_W), F32),
            pltpu.VMEM((ROW_TILE + 2 * SUBLANES, LRU_W), F32),
            pltpu.VMEM((ROW_TILE, LRU_W), F32), pltpu.VMEM((ROW_TILE, LRU_W), F32),
        ],
        compiler_params=_params(("arbitrary", "arbitrary")),
        name="lru",
    )(xr, xr, xr, xr, xr, xr, conv_w, conv_b, w_gates, bias, lam)


def _route(scores, sel):
    tm = scores.shape[1]
    neg = -jnp.inf
    iota_g = lax.broadcasted_iota(jnp.int32, (GROUP_SIZE, tm), 0)
    grp_score = []
    for g in range(N_GROUPS):
        sg = sel[g * GROUP_SIZE:(g + 1) * GROUP_SIZE]
        m1 = jnp.max(sg, axis=0, keepdims=True)
        first = jnp.min(jnp.where(sg == m1, iota_g, GROUP_SIZE), axis=0, keepdims=True)
        m2 = jnp.max(jnp.where(iota_g == first, neg, sg), axis=0, keepdims=True)
        grp_score.append(m1 + m2)
    masked = []
    for g in range(N_GROUPS):
        rank = jnp.zeros((1, tm), jnp.int32)
        for o in range(N_GROUPS):
            if o == g:
                continue
            ahead = (grp_score[o] > grp_score[g]) if o > g else (grp_score[o] >= grp_score[g])
            rank += ahead.astype(jnp.int32)
        keep = rank < TOPK_GROUPS
        masked.append(jnp.where(keep, sel[g * GROUP_SIZE:(g + 1) * GROUP_SIZE], neg))
    cand = jnp.concatenate(masked, axis=0)
    iota_e = lax.broadcasted_iota(jnp.int32, (N_EXPERTS, tm), 0)
    chosen = jnp.zeros((N_EXPERTS, tm), jnp.bool_)
    picks = []
    for _ in range(TOP_K):
        cur = jnp.where(chosen, neg, cand)
        m = jnp.max(cur, axis=0, keepdims=True)
        idx = jnp.min(jnp.where((cur == m) & jnp.logical_not(chosen), iota_e, N_EXPERTS), axis=0, keepdims=True)
        chosen = chosen | (iota_e == idx)
        picks.append(idx)
    w = jnp.where(chosen, scores, 0.0)
    gates = w / jnp.sum(w, axis=0, keepdims=True) * ROUTED_SCALE
    return gates, chosen, picks


def _tok_scratch(rows):
    return pltpu.VMEM((rows, TOK_PITCH, LANES), F32)


def _to_token_major(y, scr):
    rows = scr.shape[0]
    flat = scr.reshape(rows * TOK_PITCH, LANES)
    for s in range(TOK_ROWS):
        flat[pl.ds(s, rows, stride=TOK_PITCH), :] = y[:, s * LANES:(s + 1) * LANES]


def _from_token_major(scr):
    rows = scr.shape[0]
    flat = scr.reshape(rows * TOK_PITCH, LANES)
    return [flat[pl.ds(s, rows, stride=TOK_PITCH), :] for s in range(TOK_ROWS)]


def _token_tiles(scr):
    return scr.at[:, 0:TOK_ROWS, :]


def _merge_kernel(attn_ref, hf_ref, hb_ref, yg_ref, x_ref, mod_ref, wout_ref, ga_ref, gl_ref, lg_ref, lb_ref,
                  rwh_ref, rwl_ref, rb_ref, tri_ref, eye_ref, xn_ref, u2_ref, up_ref, eidx_ref, gates_ref, pos_ref,
                  cnt_ref, carry_ref, tok_scr):
    b = pl.program_id(0)

    @pl.when((b == 0) & (pl.program_id(1) == 0))
    def _():
        carry_ref[...] = jnp.zeros_like(carry_ref)

    D = D_MODEL
    mrow = lambda i: mod_ref[pl.ds(b, 1), i * D:(i + 1) * D]
    nt = (((1,), (1,)), ((), ()))
    u2_parts, logit_parts = [], []
    sub = ROW_TILE // MERGE_SPLIT
    for p in range(MERGE_SPLIT):
        rs = slice(p * sub, (p + 1) * sub)
        lru_y = (hf_ref[0, rs] + hb_ref[0, rs]) * jax.nn.gelu(yg_ref[0, rs])
        na = _rms_norm(attn_ref[0, rs], ga_ref[...]).astype(BF16)
        nl = _rms_norm(lru_y, gl_ref[...]).astype(BF16)
        y1 = jnp.dot(na, wout_ref[0:ATTN_W], preferred_element_type=F32)
        y1 += jnp.dot(nl, wout_ref[ATTN_W:D], preferred_element_type=F32)
        xn = _layer_norm(DEEPNORM_ALPHA * x_ref[0, rs] + mrow(2) * y1) * lg_ref[...] + lb_ref[...]
        xn_ref[0, rs] = xn
        u2_p = _layer_norm(xn) * (1.0 + mrow(4)) + mrow(3)
        u_hi, u_lo = _split_bf16(u2_p)
        u2_ref[0, rs] = u_hi
        lg_p = lax.dot_general(rwh_ref[...], u_hi, nt, preferred_element_type=F32)
        lg_p += lax.dot_general(rwh_ref[...], u_lo, nt, preferred_element_type=F32)
        lg_p += lax.dot_general(rwl_ref[...], u_hi, nt, preferred_element_type=F32)
        u2_parts.append(u2_p)
        logit_parts.append(lg_p)
    u2 = jnp.concatenate(u2_parts, axis=0)
    _to_token_major(u2, tok_scr)
    up_ref[...] = _token_tiles(tok_scr)[...].astype(BF16)
    scores = jax.nn.sigmoid(jnp.concatenate(logit_parts, axis=1))
    gates, chosen, picks = _route(scores, scores + rb_ref[...])
    sel01 = jnp.where(chosen, 1.0, 0.0)
    incl = jnp.dot(sel01.astype(BF16), tri_ref[...], preferred_element_type=F32)
    rank = carry_ref[:, 0:1] + incl - sel01
    carry_ref[...] = carry_ref[...] + incl[:, ROW_TILE - 1:ROW_TILE]
    cnt_ref[...] = carry_ref[...]
    iota_e = lax.broadcasted_iota(jnp.int32, (N_EXPERTS, ROW_TILE), 0)
    gate_rows = []
    for kk, idx in enumerate(picks):
        hit = iota_e == idx
        eidx_ref[kk:kk + 1, :] = idx
        gate_rows.append(jnp.sum(jnp.where(hit, gates, 0.0), axis=0, keepdims=True))
        pos_ref[kk:kk + 1, :] = jnp.sum(jnp.where(hit, rank, 0.0), axis=0, keepdims=True).astype(jnp.int32)

    piece = jnp.concatenate(gate_rows, axis=0)
    gate_cols = jnp.zeros((ROW_TILE, TOP_K), F32)
    for _ in range(3):
        part = piece.astype(BF16)
        gate_cols += lax.dot_general(eye_ref[...], part, nt, preferred_element_type=F32)
        piece = piece - part.astype(F32)
    for kk in range(TOP_K):
        gates_ref[pl.ds(kk, ROW_TILE, stride=TOP_K), :] = jnp.broadcast_to(gate_cols[:, kk:kk + 1],
                                                                           (ROW_TILE, LANES))


def _merge(attn, hf, hb, yg, x, mod, w_out_bf, ga, gl, lg, lb, rw_hi, rw_lo, rbias, tri, eye, C):
    B, L, _ = x.shape
    T = B * L
    nt = L // ROW_TILE
    off = C // ROW_TILE
    row = lambda w, o=0: pl.BlockSpec((1, ROW_TILE, w), lambda b, t: (b, t + o, 0))
    full = lambda a: pl.BlockSpec(a.shape, lambda b, t: (0,) * a.ndim)
    tok = pl.BlockSpec((TOP_K, ROW_TILE), lambda b, t: (0, b * nt + t))
    return pl.pallas_call(
        _merge_kernel,
        grid=(B, nt),
        in_specs=[row(ATTN_W), row(LRU_W), row(LRU_W), row(LRU_W, off), row(D_MODEL),
                  full(mod), full(w_out_bf), full(ga), full(gl), full(lg), full(lb),
                  full(rw_hi), full(rw_lo), full(rbias), full(tri), full(eye)],
        out_specs=[row(D_MODEL), row(D_MODEL),
                   pl.BlockSpec((ROW_TILE, TOK_ROWS, LANES), lambda b, t: (b * nt + t, 0, 0)),
                   tok,
                   pl.BlockSpec((ROW_TILE * TOP_K, LANES), lambda b, t: (b * nt + t, 0)),
                   tok,
                   pl.BlockSpec((N_EXPERTS, LANES), lambda b, t: (0, 0))],
        out_shape=[jax.ShapeDtypeStruct((B, L, D_MODEL), F32),
                   jax.ShapeDtypeStruct((B, L, D_MODEL), BF16),
                   jax.ShapeDtypeStruct((T, TOK_ROWS, LANES), BF16),
                   jax.ShapeDtypeStruct((TOP_K, T), jnp.int32),
                   jax.ShapeDtypeStruct((T * TOP_K, LANES), F32),
                   jax.ShapeDtypeStruct((TOP_K, T), jnp.int32),
                   jax.ShapeDtypeStruct((N_EXPERTS, LANES), F32)],
        scratch_shapes=[pltpu.VMEM((N_EXPERTS, LANES), F32), _tok_scratch(ROW_TILE)],
        compiler_params=_params(("arbitrary", "arbitrary")),
        name="merge",
    )(attn, hf, hb, yg, x, mod, w_out_bf, ga, gl, lg, lb, rw_hi, rw_lo, rbias, tri, eye)


def _swiglu(u, wg, wu):
    hg = jnp.dot(u, wg, preferred_element_type=F32)
    hu = jnp.dot(u, wu, preferred_element_type=F32)
    return hg * jax.nn.sigmoid(hg) * hu


def _dispatch_kernel(zt_ref, dest_ref, up_ref, u_ref, sg_ref, su_ref, sd_ref, xs_ref, ysh_ref, zero_scr, sem, zsem):
    @pl.when(pl.program_id(0) == 0)
    def _():
        zero_scr[...] = jnp.zeros_like(zero_scr)
        n_tiles = zt_ref.shape[0]

        def zero_tile(z):
            return pltpu.make_async_copy(zero_scr, xs_ref.at[pl.ds(z * EXP_TR, EXP_TR)], zsem)

        def start(z, carry):
            @pl.when(zt_ref[z] != 0)
            def _():
                zero_tile(z).start()
            return carry

        def wait(z, carry):
            @pl.when(zt_ref[z] != 0)
            def _():
                zero_tile(z).wait()
            return carry

        lax.fori_loop(0, n_tiles, start, 0)
        lax.fori_loop(0, n_tiles, wait, 0)

    def issue(j, carry):
        for k in range(TOP_K):
            pltpu.make_async_copy(up_ref.at[j], xs_ref.at[dest_ref[j * TOP_K + k]], sem).start(priority=k % 2)
        return carry

    lax.fori_loop(0, DISP_TM, issue, 0)

    hs = _swiglu(u_ref[...], sg_ref[...], su_ref[...])
    ysh_ref[...] = jnp.dot(hs.astype(BF16), sd_ref[...], preferred_element_type=F32).astype(BF16)

    for k in range(TOP_K):
        pltpu.make_async_copy(up_ref, xs_ref.at[pl.ds(0, DISP_TM)], sem).wait()


def _dispatch(zero_tiles, slots, up, u2, sg, su, sd, n_sorted_rows):
    T = up.shape[0]
    full = lambda a: pl.BlockSpec(a.shape, lambda i, zt: (0,) * a.ndim)
    rows = pl.BlockSpec((DISP_TM, D_MODEL), lambda i, zt: (i, 0))
    return pl.pallas_call(
        _dispatch_kernel,
        grid_spec=pltpu.PrefetchScalarGridSpec(
            num_scalar_prefetch=1,
            grid=(T // DISP_TM,),
            in_specs=[pl.BlockSpec((DISP_TM * TOP_K,), lambda i, zt: (i,), memory_space=pltpu.SMEM),
                      pl.BlockSpec((DISP_TM, TOK_ROWS, LANES), lambda i, zt: (i, 0, 0)),
                      rows, full(sg), full(su), full(sd)],
            out_specs=[pl.BlockSpec(memory_space=pl.ANY), rows],
            scratch_shapes=[pltpu.VMEM((EXP_TR, TOK_ROWS, LANES), BF16),
                            pltpu.SemaphoreType.DMA(()), pltpu.SemaphoreType.DMA(())],
        ),
        out_shape=[jax.ShapeDtypeStruct((n_sorted_rows, TOK_ROWS, LANES), BF16),
                   jax.ShapeDtypeStruct((T, D_MODEL), BF16)],
        compiler_params=_params(("arbitrary",)),
        name="dispatch",
    )(zero_tiles, slots, up, u2, sg, su, sd)


def _expert_kernel(te_ref, nu_ref, nxt_ref, par_ref, xs_ref, wg_hbm, wu_hbm, wd_hbm, ys_ref,
                   wg32, wu32, wd32, wgb_ref, wub_ref, wdb_ref, tok_scr, wsem):
    i = pl.program_id(0)
    n_used = nu_ref[0]

    def weight_copies(e, s):
        return [pltpu.make_async_copy(hbm.at[e], stage.at[s], wsem.at[s])
                for hbm, stage in ((wg_hbm, wg32), (wu_hbm, wu32), (wd_hbm, wd32))]

    @pl.when(i == 0)
    def _():
        for cp in weight_copies(te_ref[0], 0):
            cp.start(priority=1)

    @pl.when(i < n_used)
    def _():
        e = te_ref[i]

        first_of_expert = (i == 0) | (e != te_ref[jnp.maximum(i - 1, 0)])
        for s in range(2):
            @pl.when(first_of_expert & (par_ref[i] == s))
            def _():
                for cp in weight_copies(e, s):
                    cp.wait()
                for src, dst in ((wg32, wgb_ref), (wu32, wub_ref), (wd32, wdb_ref)):
                    rows = dst.shape[0] // CAST_CHUNKS

                    def cast_chunk(r, carry, src=src, dst=dst, rows=rows):
                        sl = pl.ds(pl.multiple_of(r * rows, rows), rows)
                        dst[sl, :] = src[s, sl, :].astype(BF16)
                        return carry

                    lax.fori_loop(0, CAST_CHUNKS, cast_chunk, 0)

                @pl.when(nxt_ref[i] >= 0)
                def _():
                    for cp in weight_copies(nxt_ref[i], 1 - s):
                        cp.start(priority=1)

        _token_tiles(tok_scr)[...] = xs_ref[...].astype(F32)
        x = jnp.concatenate([c.astype(BF16) for c in _from_token_major(tok_scr)], axis=1)
        h = _swiglu(x, wgb_ref[...], wub_ref[...])
        y = jnp.dot(h.astype(BF16), wdb_ref[...], preferred_element_type=F32)
        _to_token_major(y, tok_scr)
        ys_ref[...] = _token_tiles(tok_scr)[...].astype(BF16)

    @pl.when(i >= n_used)
    def _():
        ys_ref[...] = jnp.zeros_like(ys_ref)


def _experts(tile_expert, n_used, next_expert, run_parity, xs, wg, wu, wd):
    n_tiles = xs.shape[0] // EXP_TR
    blk = (EXP_TR, TOK_ROWS, LANES)
    stage = lambda a: pltpu.VMEM((2,) + a.shape[1:], F32)
    cast = lambda a: pltpu.VMEM(a.shape[1:], BF16)
    hbm = pl.BlockSpec(memory_space=pl.ANY)
    return pl.pallas_call(
        _expert_kernel,
        grid_spec=pltpu.PrefetchScalarGridSpec(
            num_scalar_prefetch=4,
            grid=(n_tiles,),
            in_specs=[pl.BlockSpec(blk, lambda i, te, nu, nx, pr: (jnp.minimum(i, nu[0] - 1), 0, 0)),
                      hbm, hbm, hbm],
            out_specs=pl.BlockSpec(blk, lambda i, te, nu, nx, pr: (i, 0, 0)),
            scratch_shapes=[stage(wg), stage(wu), stage(wd), cast(wg), cast(wu), cast(wd),
                            _tok_scratch(EXP_TR), pltpu.SemaphoreType.DMA((2,))],
        ),
        out_shape=jax.ShapeDtypeStruct(xs.shape, BF16),
        compiler_params=_params(("arbitrary",)),
        name="experts",
    )(tile_expert, n_used, next_expert, run_parity, xs, wg, wu, wd)


def _combine_kernel(dcur_ref, dnext_ref, wv_ref, ysh_ref, xn_ref, g2_ref, lg_ref, lb_ref,
                    ys_ref, o_ref, buf_ref, y_scr, sem):
    i = pl.program_id(0)
    n = pl.num_programs(0)
    tm = COMB_TM
    slot = i % 2

    def issue(d_ref, sl, j):
        for k in range(TOP_K):
            pltpu.make_async_copy(ys_ref.at[d_ref[j * TOP_K + k]], buf_ref.at[sl, k, j],
                                  sem.at[sl]).start(priority=k % 2)

    @pl.when(i == 0)
    def _():
        def first(j, carry):
            issue(dcur_ref, 0, j)
            return carry
        lax.fori_loop(0, tm, first, 0)

    for k in range(TOP_K):
        pltpu.make_async_copy(ys_ref.at[pl.ds(0, tm)], buf_ref.at[slot, k], sem.at[slot]).wait()

    half = TOK_ROWS // 2

    def accumulate(j):
        lo = jnp.zeros((half, LANES), F32)
        hi = jnp.zeros((half, LANES), F32)
        for k in range(TOP_K):
            gate = wv_ref[pl.ds(j * TOP_K + k, half, stride=0), :]
            rows = buf_ref[slot, k, j].astype(F32)
            lo += gate * rows[:half]
            hi += gate * rows[half:]
        y_scr[j, 0:half, :] = lo
        y_scr[j, half:TOK_ROWS, :] = hi

    @pl.when(i + 1 < n)
    def _():
        def both(jo, carry):
            for u in range(COMB_UNROLL):
                j = jo * COMB_UNROLL + u
                issue(dnext_ref, 1 - slot, j)
                accumulate(j)
            return carry
        lax.fori_loop(0, tm // COMB_UNROLL, both, 0)

    @pl.when(i + 1 == n)
    def _():
        def last(jo, carry):
            for u in range(COMB_UNROLL):
                accumulate(jo * COMB_UNROLL + u)
            return carry
        lax.fori_loop(0, tm // COMB_UNROLL, last, 0)
    y_routed = jnp.concatenate(_from_token_major(y_scr), axis=1)

    y2 = ysh_ref[...].astype(F32) + y_routed
    z = DEEPNORM_ALPHA * xn_ref[...] + g2_ref[0] * y2
    o_ref[...] = _layer_norm(z) * lg_ref[...] + lb_ref[...]


def _combine(slots, gates, ysh, xn, g2, lg, lb, ys, tiles_per_batch):
    T = ysh.shape[0]
    tm = COMB_TM
    n = T // tm
    row = pl.BlockSpec((tm, D_MODEL), lambda i: (i, 0))
    full = lambda a: pl.BlockSpec(a.shape, lambda i: (0,) * a.ndim)
    tok = lambda f: pl.BlockSpec((tm * TOP_K,), lambda i: (f(i),), memory_space=pltpu.SMEM)
    return pl.pallas_call(
        _combine_kernel,
        grid=(n,),
        in_specs=[tok(lambda i: i), tok(lambda i: jnp.minimum(i + 1, n - 1)),
                  pl.BlockSpec((tm * TOP_K, LANES), lambda i: (i, 0)), row, row,
                  pl.BlockSpec((1, 1, D_MODEL), lambda i: (i // tiles_per_batch, 0, 0)),
                  full(lg), full(lb),
                  pl.BlockSpec(memory_space=pl.ANY)],
        out_specs=row,
        out_shape=jax.ShapeDtypeStruct((T, D_MODEL), F32),
        scratch_shapes=[pltpu.VMEM((2, TOP_K, tm, TOK_ROWS, LANES), BF16),
                        _tok_scratch(tm),
                        pltpu.SemaphoreType.DMA((2,))],
        compiler_params=_params(("arbitrary",)),
        name="combine",
    )(slots, slots, gates, ysh, xn, g2, lg, lb, ys)


def _rope_tables(L, C):
    rows = L // GRID_W
    row = np.repeat(np.arange(rows, dtype=np.float64), GRID_W)
    col = np.tile(np.arange(GRID_W, dtype=np.float64), rows)
    inv_freq = np.exp(-np.log(ROPE_THETA) * np.arange(0, ROPE_AXIS_DIM, 2, dtype=np.float64) / ROPE_AXIS_DIM)
    ang_r = row[:, None] * inv_freq
    ang_c = col[:, None] * inv_freq
    cr, sr, cc, sc = np.cos(ang_r), np.sin(ang_r), np.cos(ang_c), np.sin(ang_c)
    cos_t = np.concatenate([cr, cr, cc, cc], axis=-1)
    sin_t = np.concatenate([-sr, sr, -sc, sc], axis=-1)
    cos_t = np.concatenate([np.ones((C, HEAD_DIM)), cos_t], axis=0)
    sin_t = np.concatenate([np.zeros((C, HEAD_DIM)), sin_t], axis=0)
    return jnp.asarray(cos_t, F32), jnp.asarray(sin_t, F32)


def kernel(x, c, ctx, c_ctx, w_mod, b_mod, w_in, attn_sink, conv_w, conv_b, lru_wa, lru_ba, lru_wx, lru_bx,
           lru_lam, norm_attn_g, norm_lru_g, w_out, ln1_g, ln1_b, router_w, router_bias, exp_w_gate, exp_w_up,
           exp_w_down, sh_w_gate, sh_w_up, sh_w_down, ln2_g, ln2_b):
    B, L, D = x.shape
    C = ctx.shape[1]
    assert w_mod.shape[0] == DEPTH and D == D_MODEL and B + 1 <= SUBLANES
    row2 = lambda a: a.reshape(1, -1)

    cvec = jnp.concatenate([c, c_ctx[None], jnp.zeros((SUBLANES - B - 1, D), F32)], axis=0)
    mod = _mod(cvec, w_mod[0], row2(b_mod[0]))

    cos_t, sin_t = _rope_tables(L, C)
    q, k, v, xr, yg = _inproj(x, ctx, mod, w_in[0].astype(BF16), cos_t, sin_t)

    attn = _attention(attn_sink[0], q, k, v, L, C)

    w_gates = jnp.concatenate([lru_wa[0], lru_wx[0]], axis=-1).astype(BF16)
    bias = jnp.stack([lru_ba[0], lru_bx[0]], axis=1)
    hf, hb = _lru(xr, conv_w[0], row2(conv_b[0]), w_gates, bias, lru_lam[0], L)

    rw_hi, rw_lo = _split_bf16(router_w[0].T)
    tri = jnp.triu(jnp.ones((ROW_TILE, ROW_TILE), BF16))
    eye = jnp.eye(ROW_TILE, dtype=BF16)
    xn, u2, up, eidx, gates, pos, cnt = _merge(
        attn, hf, hb, yg, x, mod, w_out[0].astype(BF16), row2(norm_attn_g[0]), row2(norm_lru_g[0]),
        row2(ln1_g[0]), row2(ln1_b[0]), rw_hi, rw_lo, router_bias[0].reshape(-1, 1), tri, eye, C)

    T = B * L
    n_tiles = T * TOP_K // EXP_TR + N_EXPERTS
    counts = cnt[:, 0].astype(jnp.int32)
    tiles_e = (counts + EXP_TR - 1) // EXP_TR
    tile_end = jnp.cumsum(tiles_e)
    n_used = tile_end[-1:]
    row_start = (tile_end - tiles_e) * EXP_TR
    experts = jnp.arange(N_EXPERTS, dtype=jnp.int32)
    start_of = jnp.sum(jnp.where(eidx[None] == experts[:, None, None], row_start[:, None, None], 0), axis=0)
    slots = (start_of + pos).T.reshape(-1)
    tile_id = jnp.minimum(jnp.arange(n_tiles, dtype=jnp.int32), n_used - 1)
    tile_expert = jnp.sum((tile_end[None, :] <= tile_id[:, None]).astype(jnp.int32), axis=1)

    all_tiles = jnp.arange(n_tiles, dtype=jnp.int32)
    ends_run = jnp.any((tile_end[None, :] - 1 == all_tiles[:, None]) & (tiles_e[None, :] > 0), axis=1)
    zero_tiles = (ends_run | (all_tiles >= n_used)).astype(jnp.int32)
    xs, ysh = _dispatch(zero_tiles, slots, up, u2.reshape(T, D), sh_w_gate[0].astype(BF16),
                        sh_w_up[0].astype(BF16), sh_w_down[0].astype(BF16), n_tiles * EXP_TR)
    has_rows = tiles_e > 0
    run_parity_e = (jnp.cumsum(has_rows) - has_rows) % 2
    later = has_rows[None, :] & (experts[None, :] > experts[:, None])
    next_e = jnp.min(jnp.where(later, experts[None, :], N_EXPERTS), axis=1)
    next_e = jnp.where(next_e == N_EXPERTS, -1, next_e)
    of_tile = tile_expert[:, None] == experts[None, :]
    per_tile = lambda v: jnp.sum(jnp.where(of_tile, v[None, :], 0), axis=1).astype(jnp.int32)
    ys = _experts(tile_expert, n_used, per_tile(next_e), per_tile(run_parity_e), xs,
                  exp_w_gate[0], exp_w_up[0], exp_w_down[0])
    g2 = mod[:B, 5 * D:6 * D].reshape(B, 1, D)
    out = _combine(slots, gates, ysh, xn.reshape(T, D), g2, row2(ln2_g[0]), row2(ln2_b[0]), ys, L // COMB_TM)
    return out.reshape(B, L, D)
```

```python
import functools

import jax
import jax.numpy as jnp
import numpy as np
from jax import lax
from jax.experimental import pallas as pl
from jax.experimental.pallas import tpu as pltpu

F32 = jnp.float32
BF16 = jnp.bfloat16

D_MODEL = 2048
GRID_W = 64
N_HEADS = 8
N_KV_HEADS = 2
HEAD_DIM = 128
GQA_GROUP = N_HEADS // N_KV_HEADS
ATTN_W = N_HEADS * HEAD_DIM
KV_W = N_KV_HEADS * HEAD_DIM
ATTN_SCALE = HEAD_DIM ** -0.5
BLOCK = 128
ROPE_THETA = 10000.0
ROPE_AXIS_DIM = HEAD_DIM // 2
LRU_W = D_MODEL - ATTN_W
LRU_BLOCKS = 8
LRU_BLOCK_W = LRU_W // LRU_BLOCKS
LRU_C = 8.0
N_EXPERTS = 64
TOP_K = 8
N_GROUPS = 8
GROUP_SIZE = N_EXPERTS // N_GROUPS
TOPK_GROUPS = 4
EXPERT_FF = 512
ROUTED_SCALE = 2.5
LN_EPS = 1e-6
SQRT_GUARD = 1e-30
DEPTH = 1
DEEPNORM_ALPHA = (2 * DEPTH) ** 0.25

SUBLANES = 8
LANES = 128
VMEM_LIMIT = 56 * 1024 * 1024

ROW_TILE = 256
MOD_TN = 1024
MERGE_SPLIT = 2
ATTN_STACK = GQA_GROUP
TOK_ROWS = D_MODEL // LANES
TOK_PITCH = TOK_ROWS + SUBLANES
EXP_TR = 256
CAST_CHUNKS = 16
WEIGHT_CHUNK = 256
CAST_ROWS = 32
DISP_TM = 1024
COMB_TM = 256
COMB_UNROLL = 8


def _params(sem, vmem=VMEM_LIMIT):
    return pltpu.CompilerParams(dimension_semantics=sem, vmem_limit_bytes=vmem)


def _layer_norm(x):
    mu = jnp.mean(x, axis=-1, keepdims=True)
    xc = x - mu
    var = jnp.mean(xc * xc, axis=-1, keepdims=True)
    return xc * lax.rsqrt(var + LN_EPS)


def _rms_norm(x, g):
    return x * lax.rsqrt(jnp.mean(x * x, axis=-1, keepdims=True) + LN_EPS) * g


def _sigmoid(x):
    return 0.5 * jnp.tanh(0.5 * x) + 0.5


def _load_weight_bf16(w_hbm, wbf_ref, stage_ref, sem):
    chunk = stage_ref.shape[1]
    n = w_hbm.shape[0] // chunk
    copy = lambda c: pltpu.make_async_copy(w_hbm.at[pl.ds(c * chunk, chunk)], stage_ref.at[c % 2], sem.at[c % 2])
    copy(0).start()
    for c in range(n):
        copy(c).wait()
        if c + 1 < n:
            copy(c + 1).start()

        def cast(r, carry, c=c):
            src = pl.ds(pl.multiple_of(r * CAST_ROWS, CAST_ROWS), CAST_ROWS)
            dst = pl.ds(pl.multiple_of(c * chunk + r * CAST_ROWS, CAST_ROWS), CAST_ROWS)
            wbf_ref[dst, :] = stage_ref[c % 2, src, :].astype(BF16)
            return carry

        lax.fori_loop(0, chunk // CAST_ROWS, cast, 0)


def _weight_scratch(w):
    return [pltpu.VMEM(w.shape, BF16), pltpu.VMEM((2, WEIGHT_CHUNK, w.shape[1]), F32), pltpu.SemaphoreType.DMA((2,))]


def _split_bf16(x):
    hi = x.astype(BF16)
    lo = (x - hi.astype(F32)).astype(BF16)
    return hi, lo


def _mod_kernel(c_ref, w_ref, b_ref, o_ref):
    cv = c_ref[...]
    s = cv * jax.nn.sigmoid(cv)
    hi, lo = _split_bf16(s)
    lhs = jnp.concatenate([hi, lo], axis=0)
    r = jnp.dot(lhs, w_ref[...].astype(BF16), preferred_element_type=F32)
    o_ref[...] = r[:SUBLANES] + r[SUBLANES:] + b_ref[...]


def _mod(cvec, w_mod, b_mod):
    n = w_mod.shape[1]
    return pl.pallas_call(
        _mod_kernel,
        grid=(n // MOD_TN,),
        in_specs=[
            pl.BlockSpec((SUBLANES, D_MODEL), lambda j: (0, 0)),
            pl.BlockSpec((D_MODEL, MOD_TN), lambda j: (0, j)),
            pl.BlockSpec((1, MOD_TN), lambda j: (0, j)),
        ],
        out_specs=pl.BlockSpec((SUBLANES, MOD_TN), lambda j: (0, j)),
        out_shape=jax.ShapeDtypeStruct((SUBLANES, n), F32),
        compiler_params=_params(("arbitrary",)),
        name="mod",
    )(cvec, w_mod, b_mod)


def _rope(xh, cos, sin, even_block):
    partner = jnp.where(even_block, pltpu.roll(xh, 96, 1), pltpu.roll(xh, 32, 1))
    return xh * cos + partner * sin


def _inproj_kernel(x_ref, ctx_ref, mod_ref, w_hbm, cos_ref, sin_ref,
                   q_ref, k_ref, v_ref, xr_ref, yg_ref, w_ref, w_stage, w_sem):
    b = pl.program_id(0)
    t = pl.program_id(1)

    @pl.when((b == 0) & (t == 0))
    def _():
        _load_weight_bf16(w_hbm, w_ref, w_stage, w_sem)

    is_ctx = t == 0
    xin = jnp.where(is_ctx, ctx_ref[0], x_ref[0])
    r = jnp.where(is_ctx, 2, b)
    shift = mod_ref[pl.ds(r, 1), 0:D_MODEL]
    scale = mod_ref[pl.ds(r, 1), D_MODEL:2 * D_MODEL]
    u = (_layer_norm(xin) * (1.0 + scale) + shift).astype(BF16)

    cos = cos_ref[...]
    sin = sin_ref[...]
    lane = lax.broadcasted_iota(jnp.int32, (ROW_TILE, HEAD_DIM), 1)
    even_block = (lane % 64) < 32

    def proj(c0, c1):
        return jnp.dot(u, w_ref[:, c0:c1], preferred_element_type=F32)

    head = lambda y, h: y[:, h * HEAD_DIM:(h + 1) * HEAD_DIM]
    for p in range(N_HEADS // 2):
        qq = proj(2 * p * HEAD_DIM, (2 * p + 2) * HEAD_DIM)
        for h in range(2):
            q_ref[0, :, (2 * p + h) * HEAD_DIM:(2 * p + h + 1) * HEAD_DIM] = (
                _rope(head(qq, h), cos, sin, even_block) * ATTN_SCALE).astype(BF16)
    kk = proj(ATTN_W, ATTN_W + KV_W)
    for h in range(N_KV_HEADS):
        k_ref[0, :, h * HEAD_DIM:(h + 1) * HEAD_DIM] = _rope(head(kk, h), cos, sin, even_block).astype(BF16)
    v_ref[0] = proj(ATTN_W + KV_W, ATTN_W + 2 * KV_W).astype(BF16)
    c0 = ATTN_W + 2 * KV_W
    xr_ref[0] = proj(c0, c0 + LRU_W)
    yg_ref[0] = proj(c0 + LRU_W, c0 + 2 * LRU_W)


def _inproj(x, ctx, mod, w_in, cos_t, sin_t):
    B, L, _ = x.shape
    C = ctx.shape[1]
    assert C == ROW_TILE and L % ROW_TILE == 0
    nt = L // ROW_TILE + 1
    rows = L + C
    out = lambda w, dt: jax.ShapeDtypeStruct((B, rows, w), dt)
    ospec = lambda w: pl.BlockSpec((1, ROW_TILE, w), lambda b, t: (b, t, 0))
    return pl.pallas_call(
        _inproj_kernel,
        grid=(B, nt),
        in_specs=[
            pl.BlockSpec((1, ROW_TILE, D_MODEL), lambda b, t: (b, jnp.maximum(t - 1, 0), 0)),
            pl.BlockSpec((1, ROW_TILE, D_MODEL), lambda b, t: (b, 0, 0)),
            pl.BlockSpec((SUBLANES, 2 * D_MODEL), lambda b, t: (0, 0)),
            pl.BlockSpec(memory_space=pl.ANY),
            pl.BlockSpec((ROW_TILE, HEAD_DIM), lambda b, t: (t, 0)),
            pl.BlockSpec((ROW_TILE, HEAD_DIM), lambda b, t: (t, 0)),
        ],
        out_specs=[ospec(ATTN_W), ospec(KV_W), ospec(KV_W), ospec(LRU_W), ospec(LRU_W)],
        out_shape=[out(ATTN_W, BF16), out(KV_W, BF16), out(KV_W, BF16), out(LRU_W, F32), out(LRU_W, F32)],
        scratch_shapes=_weight_scratch(w_in),
        compiler_params=_params(("arbitrary", "arbitrary")),
        name="inproj",
    )(x, ctx, mod, w_in, cos_t, sin_t)


def _attn_kernel(sink_ref, q_ref, kp_ref, kc_ref, kn_ref, vp_ref, vc_ref, vn_ref, kx_ref, vx_ref, o_ref):
    n = pl.program_id(1)
    nb = pl.num_programs(1)
    rows = ATTN_STACK * BLOCK
    qi = lax.broadcasted_iota(jnp.int32, (rows, BLOCK), 0) % BLOCK
    kj = lax.broadcasted_iota(jnp.int32, (rows, BLOCK), 1)
    prev_ok = (kj >= qi) & (n > 0)
    next_ok = (kj <= qi) & (n < nb - 1)
    grp = lax.broadcasted_iota(jnp.int32, (rows, 1), 0) // BLOCK
    neg = -jnp.inf
    for q0 in range(0, N_HEADS, ATTN_STACK):
        h = q0 // GQA_GROUP
        hs = slice(h * HEAD_DIM, (h + 1) * HEAD_DIM)
        q4 = jnp.concatenate(
            [q_ref[0, :, (q0 + g) * HEAD_DIM:(q0 + g + 1) * HEAD_DIM] for g in range(ATTN_STACK)], axis=0)
        nt = (((1,), (1,)), ((), ()))
        sp = jnp.where(prev_ok, lax.dot_general(q4, kp_ref[0, :, hs], nt, preferred_element_type=F32), neg)
        sc = lax.dot_general(q4, kc_ref[0, :, hs], nt, preferred_element_type=F32)
        sn = jnp.where(next_ok, lax.dot_general(q4, kn_ref[0, :, hs], nt, preferred_element_type=F32), neg)
        sx = lax.dot_general(q4, kx_ref[0, :, hs], nt, preferred_element_type=F32)
        sink = jnp.zeros((rows, 1), F32)
        for g in range(ATTN_STACK):
            sink = jnp.where(grp == g, sink_ref[q0 + g], sink)
        lane_chunks = lambda a: [a[:, c * BLOCK:(c + 1) * BLOCK] for c in range(a.shape[1] // BLOCK)]
        fold = lambda op, parts: functools.reduce(op, parts)
        m = jnp.max(fold(jnp.maximum, [sp, sc, sn] + lane_chunks(sx)), -1, keepdims=True)
        m = jnp.maximum(m, sink)
        pp = jnp.exp(sp - m)
        pc = jnp.exp(sc - m)
        pn = jnp.exp(sn - m)
        px = jnp.exp(sx - m)
        denom = jnp.sum(fold(jnp.add, [pp, pc, pn] + lane_chunks(px)), -1, keepdims=True) + jnp.exp(sink - m)
        acc = jnp.dot(pp.astype(BF16), vp_ref[0, :, hs], preferred_element_type=F32)
        acc += jnp.dot(pc.astype(BF16), vc_ref[0, :, hs], preferred_element_type=F32)
        acc += jnp.dot(pn.astype(BF16), vn_ref[0, :, hs], preferred_element_type=F32)
        acc += jnp.dot(px.astype(BF16), vx_ref[0, :, hs], preferred_element_type=F32)
        o = acc / denom
        for g in range(ATTN_STACK):
            c0 = (q0 + g) * HEAD_DIM
            o_ref[0, :, c0:c0 + HEAD_DIM] = o[g * BLOCK:(g + 1) * BLOCK]


def _attention(sink, q, k, v, L, C):
    B = q.shape[0]
    nb = L // BLOCK
    off = C // BLOCK
    cur = lambda b, n: (b, n + off, 0)
    prv = lambda b, n: (b, jnp.maximum(n - 1, 0) + off, 0)
    nxt = lambda b, n: (b, jnp.minimum(n + 1, nb - 1) + off, 0)
    kv = lambda im: pl.BlockSpec((1, BLOCK, KV_W), im)
    cx = pl.BlockSpec((1, C, KV_W), lambda b, n: (b, 0, 0))
    return pl.pallas_call(
        _attn_kernel,
        grid=(B, nb),
        in_specs=[
            pl.BlockSpec(memory_space=pltpu.SMEM),
            pl.BlockSpec((1, BLOCK, ATTN_W), cur),
            kv(prv), kv(cur), kv(nxt), kv(prv), kv(cur), kv(nxt), cx, cx,
        ],
        out_specs=pl.BlockSpec((1, BLOCK, ATTN_W), lambda b, n: (b, n, 0)),
        out_shape=jax.ShapeDtypeStruct((B, L, ATTN_W), F32),
        compiler_params=_params(("arbitrary", "arbitrary")),
        name="attn",
    )(sink, q, k, k, k, v, v, v, k, v)


def _lru_coeffs(x_ref, p_ref, n_ref, tile, d, cw_ref, cb_ref, w_ref, bias_ref, lam_ref,
                ext_scr, a_scr, b_scr, n_lat_tiles):
    tm = ROW_TILE
    prev_ok = tile >= 2
    next_ok = (tile >= 1) & (tile < n_lat_tiles)
    ext_scr[0:SUBLANES] = jnp.where(prev_ok, p_ref[0], 0.0)
    ext_scr[SUBLANES:SUBLANES + tm] = x_ref[0]
    ext_scr[SUBLANES + tm:2 * SUBLANES + tm] = jnp.where(next_ok, n_ref[0], 0.0)
    xc = cb_ref[...] + cw_ref[2:3] * x_ref[0]
    xc += cw_ref[0:1] * ext_scr[SUBLANES - 2:SUBLANES - 2 + tm]
    xc += cw_ref[1:2] * ext_scr[SUBLANES - 1:SUBLANES - 1 + tm]
    xc += cw_ref[3:4] * ext_scr[SUBLANES + 1:SUBLANES + 1 + tm]
    xcb = xc.astype(BF16)
    lam = lam_ref[d:d + 1]
    sp = jnp.maximum(-lam, 0.0) + jnp.log1p(jnp.exp(-jnp.abs(lam)))
    for n in range(LRU_BLOCKS):
        cs = slice(n * LRU_BLOCK_W, (n + 1) * LRU_BLOCK_W)
        z = jnp.dot(xcb[:, cs], w_ref[d, n], preferred_element_type=F32)
        r = _sigmoid(z[:, :LRU_BLOCK_W] + bias_ref[d, 0:1, cs])
        i = _sigmoid(z[:, LRU_BLOCK_W:] + bias_ref[d, 1:2, cs])
        log_a = -LRU_C * r * sp[:, cs]
        a = jnp.exp(log_a)
        a_scr[:, cs] = a
        v = 1.0 - a * a
        b_scr[:, cs] = v * lax.rsqrt(jnp.maximum(v, SQRT_GUARD)) * (i * xc[:, cs])


def _lru_scan(a_scr, b_scr, h_ref, state_ref, reset, reverse):
    groups = ROW_TILE // SUBLANES
    row = lax.broadcasted_iota(jnp.int32, (SUBLANES, LRU_W), 0)
    carry0 = jnp.where(reset, 0.0, state_ref[...])

    def body(g, carry):
        gi = (groups - 1 - g) if reverse else g
        r0 = pl.multiple_of(gi * SUBLANES, SUBLANES)
        A = a_scr[pl.ds(r0, SUBLANES), :]
        Bv = b_scr[pl.ds(r0, SUBLANES), :]
        for s in (1, 2, 4):
            if reverse:
                sh, m = SUBLANES - s, row < SUBLANES - s
            else:
                sh, m = s, row >= s
            A_sh = pltpu.roll(A, sh, 0)
            B_sh = pltpu.roll(Bv, sh, 0)
            Bv = jnp.where(m, A * B_sh + Bv, Bv)
            A = jnp.where(m, A * A_sh, A)
        h = Bv + A * carry
        h_ref[0, pl.ds(r0, SUBLANES), :] = h
        last = h[0:1] if reverse else h[SUBLANES - 1:SUBLANES]
        return jnp.broadcast_to(last, (SUBLANES, LRU_W))

    state_ref[...] = lax.fori_loop(0, groups, body, carry0)


def _lru_kernel(xf_ref, xfp_ref, xfn_ref, xb_ref, xbp_ref, xbn_ref, cw_ref, cb_ref, w_ref, bias_ref, lam_ref,
                hf_ref, hb_ref, sf_ref, sb_ref, ext_scr, a_scr, b_scr, *, n_lat_tiles):
    t = pl.program_id(1)
    reset = t == 0
    bt = jnp.where(t == 0, 0, n_lat_tiles + 1 - t)
    _lru_coeffs(xf_ref, xfp_ref, xfn_ref, t, 0, cw_ref, cb_ref, w_ref, bias_ref, lam_ref,
                ext_scr, a_scr, b_scr, n_lat_tiles)
    _lru_scan(a_scr, b_scr, hf_ref, sf_ref, reset, False)
    _lru_coeffs(xb_ref, xbp_ref, xbn_ref, bt, 1, cw_ref, cb_ref, w_ref, bias_ref, lam_ref,
                ext_scr, a_scr, b_scr, n_lat_tiles)
    _lru_scan(a_scr, b_scr, hb_ref, sb_ref, reset, True)


def _lru(xr, conv_w, conv_b, w_gates, bias, lam, L):
    B, rows, _ = xr.shape
    nl = L // ROW_TILE
    nt = nl + 1
    per = ROW_TILE // SUBLANES
    n8 = rows // SUBLANES
    ft = lambda b, t: t
    btile = lambda b, t: jnp.where(t == 0, 0, nl + 1 - t)
    main = lambda f: pl.BlockSpec((1, ROW_TILE, LRU_W), lambda b, t: (b, f(b, t), 0))
    prev = lambda f: pl.BlockSpec((1, SUBLANES, LRU_W), lambda b, t: (b, jnp.maximum(f(b, t) * per - 1, 0), 0))
    nxt = lambda f: pl.BlockSpec((1, SUBLANES, LRU_W),
                                 lambda b, t: (b, jnp.minimum((f(b, t) + 1) * per, n8 - 1), 0))
    full = lambda a: pl.BlockSpec(a.shape, lambda b, t: (0,) * a.ndim)
    return pl.pallas_call(
        functools.partial(_lru_kernel, n_lat_tiles=nl),
        grid=(B, nt),
        in_specs=[main(ft), prev(ft), nxt(ft), main(btile), prev(btile), nxt(btile),
                  full(conv_w), full(conv_b), full(w_gates), full(bias), full(lam)],
        out_specs=[
            pl.BlockSpec((1, ROW_TILE, LRU_W), lambda b, t: (b, jnp.maximum(t - 1, 0), 0)),
            pl.BlockSpec((1, ROW_TILE, LRU_W), lambda b, t: (b, nl - jnp.maximum(t, 1), 0)),
        ],
        out_shape=[jax.ShapeDtypeStruct((B, L, LRU_W), F32)] * 2,
        scratch_shapes=[
            pltpu.VMEM((SUBLANES, LRU_W), F32), pltpu.VMEM((SUBLANES, LRU_W), F32),
            pltpu.VMEM((ROW_TILE + 2 * SUBLANES, LRU_W), F32),
            pltpu.VMEM((ROW_TILE, LRU_W), F32), pltpu.VMEM((ROW_TILE, LRU_W), F32),
        ],
        compiler_params=_params(("arbitrary", "arbitrary")),
        name="lru",
    )(xr, xr, xr, xr, xr, xr, conv_w, conv_b, w_gates, bias, lam)


def _route(scores, sel):
    tm = scores.shape[1]
    neg = -jnp.inf
    iota_g = lax.broadcasted_iota(jnp.int32, (GROUP_SIZE, tm), 0)
    grp_score = []
    for g in range(N_GROUPS):
        sg = sel[g * GROUP_SIZE:(g + 1) * GROUP_SIZE]
        m1 = jnp.max(sg, axis=0, keepdims=True)
        first = jnp.min(jnp.where(sg == m1, iota_g, GROUP_SIZE), axis=0, keepdims=True)
        m2 = jnp.max(jnp.where(iota_g == first, neg, sg), axis=0, keepdims=True)
        grp_score.append(m1 + m2)
    masked = []
    for g in range(N_GROUPS):
        rank = jnp.zeros((1, tm), jnp.int32)
        for o in range(N_GROUPS):
            if o == g:
                continue
            ahead = (grp_score[o] > grp_score[g]) if o > g else (grp_score[o] >= grp_score[g])
            rank += ahead.astype(jnp.int32)
        keep = rank < TOPK_GROUPS
        masked.append(jnp.where(keep, sel[g * GROUP_SIZE:(g + 1) * GROUP_SIZE], neg))
    cand = jnp.concatenate(masked, axis=0)
    iota_e = lax.broadcasted_iota(jnp.int32, (N_EXPERTS, tm), 0)
    chosen = jnp.zeros((N_EXPERTS, tm), jnp.bool_)
    picks = []
    for _ in range(TOP_K):
        cur = jnp.where(chosen, neg, cand)
        m = jnp.max(cur, axis=0, keepdims=True)
        idx = jnp.min(jnp.where((cur == m) & jnp.logical_not(chosen), iota_e, N_EXPERTS), axis=0, keepdims=True)
        chosen = chosen | (iota_e == idx)
        picks.append(idx)
    w = jnp.where(chosen, scores, 0.0)
    gates = w / jnp.sum(w, axis=0, keepdims=True) * ROUTED_SCALE
    return gates, chosen, picks


def _tok_scratch(rows):
    return pltpu.VMEM((rows, TOK_PITCH, LANES), F32)


def _to_token_major(y, scr):
    rows = scr.shape[0]
    flat = scr.reshape(rows * TOK_PITCH, LANES)
    for s in range(TOK_ROWS):
        flat[pl.ds(s, rows, stride=TOK_PITCH), :] = y[:, s * LANES:(s + 1) * LANES]


def _from_token_major(scr):
    rows = scr.shape[0]
    flat = scr.reshape(rows * TOK_PITCH, LANES)
    return [flat[pl.ds(s, rows, stride=TOK_PITCH), :] for s in range(TOK_ROWS)]


def _token_tiles(scr):
    return scr.at[:, 0:TOK_ROWS, :]


def _merge_kernel(attn_ref, hf_ref, hb_ref, yg_ref, x_ref, mod_ref, wout_hbm, ga_ref, gl_ref, lg_ref, lb_ref,
                  rwh_ref, rwl_ref, rb_ref, tri_ref, eye_ref, xn_ref, u2_ref, up_ref, eidx_ref, gates_ref, pos_ref,
                  cnt_ref, carry_ref, tok_scr, wout_ref, wout_stage, wout_sem):
    b = pl.program_id(0)

    @pl.when((b == 0) & (pl.program_id(1) == 0))
    def _():
        carry_ref[...] = jnp.zeros_like(carry_ref)
        _load_weight_bf16(wout_hbm, wout_ref, wout_stage, wout_sem)

    D = D_MODEL
    mrow = lambda i: mod_ref[pl.ds(b, 1), i * D:(i + 1) * D]
    nt = (((1,), (1,)), ((), ()))
    u2_parts, logit_parts = [], []
    sub = ROW_TILE // MERGE_SPLIT
    for p in range(MERGE_SPLIT):
        rs = slice(p * sub, (p + 1) * sub)
        lru_y = (hf_ref[0, rs] + hb_ref[0, rs]) * jax.nn.gelu(yg_ref[0, rs])
        na = _rms_norm(attn_ref[0, rs], ga_ref[...]).astype(BF16)
        nl = _rms_norm(lru_y, gl_ref[...]).astype(BF16)
        y1 = jnp.dot(na, wout_ref[0:ATTN_W], preferred_element_type=F32)
        y1 += jnp.dot(nl, wout_ref[ATTN_W:D], preferred_element_type=F32)
        xn = _layer_norm(DEEPNORM_ALPHA * x_ref[0, rs] + mrow(2) * y1) * lg_ref[...] + lb_ref[...]
        xn_ref[0, rs] = xn
        u2_p = _layer_norm(xn) * (1.0 + mrow(4)) + mrow(3)
        u_hi, u_lo = _split_bf16(u2_p)
        u2_ref[0, rs] = u_hi
        lg_p = lax.dot_general(rwh_ref[...], u_hi, nt, preferred_element_type=F32)
        lg_p += lax.dot_general(rwh_ref[...], u_lo, nt, preferred_element_type=F32)
        lg_p += lax.dot_general(rwl_ref[...], u_hi, nt, preferred_element_type=F32)
        u2_parts.append(u2_p)
        logit_parts.append(lg_p)
    u2 = jnp.concatenate(u2_parts, axis=0)
    _to_token_major(u2, tok_scr)
    up_ref[...] = _token_tiles(tok_scr)[...].astype(BF16)
    scores = jax.nn.sigmoid(jnp.concatenate(logit_parts, axis=1))
    gates, chosen, picks = _route(scores, scores + rb_ref[...])
    sel01 = jnp.where(chosen, 1.0, 0.0)
    incl = jnp.dot(sel01.astype(BF16), tri_ref[...], preferred_element_type=F32)
    rank = carry_ref[:, 0:1] + incl - sel01
    carry_ref[...] = carry_ref[...] + incl[:, ROW_TILE - 1:ROW_TILE]
    cnt_ref[...] = carry_ref[...]
    iota_e = lax.broadcasted_iota(jnp.int32, (N_EXPERTS, ROW_TILE), 0)
    gate_rows = []
    for kk, idx in enumerate(picks):
        hit = iota_e == idx
        eidx_ref[kk:kk + 1, :] = idx
        gate_rows.append(jnp.sum(jnp.where(hit, gates, 0.0), axis=0, keepdims=True))
        pos_ref[kk:kk + 1, :] = jnp.sum(jnp.where(hit, rank, 0.0), axis=0, keepdims=True).astype(jnp.int32)

    piece = jnp.concatenate(gate_rows, axis=0)
    gate_cols = jnp.zeros((ROW_TILE, TOP_K), F32)
    for _ in range(3):
        part = piece.astype(BF16)
        gate_cols += lax.dot_general(eye_ref[...], part, nt, preferred_element_type=F32)
        piece = piece - part.astype(F32)
    for kk in range(TOP_K):
        gates_ref[pl.ds(kk, ROW_TILE, stride=TOP_K), :] = jnp.broadcast_to(gate_cols[:, kk:kk + 1],
                                                                           (ROW_TILE, LANES))


def _merge(attn, hf, hb, yg, x, mod, w_out, ga, gl, lg, lb, rw_hi, rw_lo, rbias, tri, eye, C):
    B, L, _ = x.shape
    T = B * L
    nt = L // ROW_TILE
    off = C // ROW_TILE
    row = lambda w, o=0: pl.BlockSpec((1, ROW_TILE, w), lambda b, t: (b, t + o, 0))
    full = lambda a: pl.BlockSpec(a.shape, lambda b, t: (0,) * a.ndim)
    tok = pl.BlockSpec((TOP_K, ROW_TILE), lambda b, t: (0, b * nt + t))
    return pl.pallas_call(
        _merge_kernel,
        grid=(B, nt),
        in_specs=[row(ATTN_W), row(LRU_W), row(LRU_W), row(LRU_W, off), row(D_MODEL),
                  full(mod), pl.BlockSpec(memory_space=pl.ANY), full(ga), full(gl), full(lg), full(lb),
                  full(rw_hi), full(rw_lo), full(rbias), full(tri), full(eye)],
        out_specs=[row(D_MODEL), row(D_MODEL),
                   pl.BlockSpec((ROW_TILE, TOK_ROWS, LANES), lambda b, t: (b * nt + t, 0, 0)),
                   tok,
                   pl.BlockSpec((ROW_TILE * TOP_K, LANES), lambda b, t: (b * nt + t, 0)),
                   tok,
                   pl.BlockSpec((N_EXPERTS, LANES), lambda b, t: (0, 0))],
        out_shape=[jax.ShapeDtypeStruct((B, L, D_MODEL), F32),
                   jax.ShapeDtypeStruct((B, L, D_MODEL), BF16),
                   jax.ShapeDtypeStruct((T, TOK_ROWS, LANES), BF16),
                   jax.ShapeDtypeStruct((TOP_K, T), jnp.int32),
                   jax.ShapeDtypeStruct((T * TOP_K, LANES), F32),
                   jax.ShapeDtypeStruct((TOP_K, T), jnp.int32),
                   jax.ShapeDtypeStruct((N_EXPERTS, LANES), F32)],
        scratch_shapes=[pltpu.VMEM((N_EXPERTS, LANES), F32), _tok_scratch(ROW_TILE)] + _weight_scratch(w_out),
        compiler_params=_params(("arbitrary", "arbitrary")),
        name="merge",
    )(attn, hf, hb, yg, x, mod, w_out, ga, gl, lg, lb, rw_hi, rw_lo, rbias, tri, eye)


def _swiglu(u, wg, wu):
    hg = jnp.dot(u, wg, preferred_element_type=F32)
    hu = jnp.dot(u, wu, preferred_element_type=F32)
    return hg * jax.nn.sigmoid(hg) * hu


def _dispatch_kernel(zt_ref, dest_ref, up_ref, u_ref, sg_ref, su_ref, sd_ref, xs_ref, ysh_ref, zero_scr, sem, zsem):
    @pl.when(pl.program_id(0) == 0)
    def _():
        zero_scr[...] = jnp.zeros_like(zero_scr)
        n_tiles = zt_ref.shape[0]

        def zero_tile(z):
            return pltpu.make_async_copy(zero_scr, xs_ref.at[pl.ds(z * EXP_TR, EXP_TR)], zsem)

        def start(z, carry):
            @pl.when(zt_ref[z] != 0)
            def _():
                zero_tile(z).start()
            return carry

        def wait(z, carry):
            @pl.when(zt_ref[z] != 0)
            def _():
                zero_tile(z).wait()
            return carry

        lax.fori_loop(0, n_tiles, start, 0)
        lax.fori_loop(0, n_tiles, wait, 0)

    def issue(j, carry):
        for k in range(TOP_K):
            pltpu.make_async_copy(up_ref.at[j], xs_ref.at[dest_ref[j * TOP_K + k]], sem).start(priority=k % 2)
        return carry

    lax.fori_loop(0, DISP_TM, issue, 0)

    hs = _swiglu(u_ref[...], sg_ref[...], su_ref[...])
    ysh_ref[...] = jnp.dot(hs.astype(BF16), sd_ref[...], preferred_element_type=F32).astype(BF16)

    for k in range(TOP_K):
        pltpu.make_async_copy(up_ref, xs_ref.at[pl.ds(0, DISP_TM)], sem).wait()


def _dispatch(zero_tiles, slots, up, u2, sg, su, sd, n_sorted_rows):
    T = up.shape[0]
    full = lambda a: pl.BlockSpec(a.shape, lambda i, zt: (0,) * a.ndim)
    rows = pl.BlockSpec((DISP_TM, D_MODEL), lambda i, zt: (i, 0))
    return pl.pallas_call(
        _dispatch_kernel,
        grid_spec=pltpu.PrefetchScalarGridSpec(
            num_scalar_prefetch=1,
            grid=(T // DISP_TM,),
            in_specs=[pl.BlockSpec((DISP_TM * TOP_K,), lambda i, zt: (i,), memory_space=pltpu.SMEM),
                      pl.BlockSpec((DISP_TM, TOK_ROWS, LANES), lambda i, zt: (i, 0, 0)),
                      rows, full(sg), full(su), full(sd)],
            out_specs=[pl.BlockSpec(memory_space=pl.ANY), rows],
            scratch_shapes=[pltpu.VMEM((EXP_TR, TOK_ROWS, LANES), BF16),
                            pltpu.SemaphoreType.DMA(()), pltpu.SemaphoreType.DMA(())],
        ),
        out_shape=[jax.ShapeDtypeStruct((n_sorted_rows, TOK_ROWS, LANES), BF16),
                   jax.ShapeDtypeStruct((T, D_MODEL), BF16)],
        compiler_params=_params(("arbitrary",)),
        name="dispatch",
    )(zero_tiles, slots, up, u2, sg, su, sd)


def _expert_kernel(te_ref, nu_ref, nxt_ref, par_ref, xs_ref, wg_hbm, wu_hbm, wd_hbm, ys_ref,
                   wg32, wu32, wd32, wgb_ref, wub_ref, wdb_ref, tok_scr, wsem):
    i = pl.program_id(0)
    n_used = nu_ref[0]

    def weight_copies(e, s):
        return [pltpu.make_async_copy(hbm.at[e], stage.at[s], wsem.at[s])
                for hbm, stage in ((wg_hbm, wg32), (wu_hbm, wu32), (wd_hbm, wd32))]

    @pl.when(i == 0)
    def _():
        for cp in weight_copies(te_ref[0], 0):
            cp.start(priority=1)

    @pl.when(i < n_used)
    def _():
        e = te_ref[i]

        first_of_expert = (i == 0) | (e != te_ref[jnp.maximum(i - 1, 0)])
        for s in range(2):
            @pl.when(first_of_expert & (par_ref[i] == s))
            def _():
                for cp in weight_copies(e, s):
                    cp.wait()
                for src, dst in ((wg32, wgb_ref), (wu32, wub_ref), (wd32, wdb_ref)):
                    rows = dst.shape[0] // CAST_CHUNKS

                    def cast_chunk(r, carry, src=src, dst=dst, rows=rows):
                        sl = pl.ds(pl.multiple_of(r * rows, rows), rows)
                        dst[sl, :] = src[s, sl, :].astype(BF16)
                        return carry

                    lax.fori_loop(0, CAST_CHUNKS, cast_chunk, 0)

                @pl.when(nxt_ref[i] >= 0)
                def _():
                    for cp in weight_copies(nxt_ref[i], 1 - s):
                        cp.start(priority=1)

        _token_tiles(tok_scr)[...] = xs_ref[...].astype(F32)
        x = jnp.concatenate([c.astype(BF16) for c in _from_token_major(tok_scr)], axis=1)
        h = _swiglu(x, wgb_ref[...], wub_ref[...])
        y = jnp.dot(h.astype(BF16), wdb_ref[...], preferred_element_type=F32)
        _to_token_major(y, tok_scr)
        ys_ref[...] = _token_tiles(tok_scr)[...].astype(BF16)

    @pl.when(i >= n_used)
    def _():
        ys_ref[...] = jnp.zeros_like(ys_ref)


def _experts(tile_expert, n_used, next_expert, run_parity, xs, wg, wu, wd):
    n_tiles = xs.shape[0] // EXP_TR
    blk = (EXP_TR, TOK_ROWS, LANES)
    stage = lambda a: pltpu.VMEM((2,) + a.shape[1:], F32)
    cast = lambda a: pltpu.VMEM(a.shape[1:], BF16)
    hbm = pl.BlockSpec(memory_space=pl.ANY)
    return pl.pallas_call(
        _expert_kernel,
        grid_spec=pltpu.PrefetchScalarGridSpec(
            num_scalar_prefetch=4,
            grid=(n_tiles,),
            in_specs=[pl.BlockSpec(blk, lambda i, te, nu, nx, pr: (jnp.minimum(i, nu[0] - 1), 0, 0)),
                      hbm, hbm, hbm],
            out_specs=pl.BlockSpec(blk, lambda i, te, nu, nx, pr: (i, 0, 0)),
            scratch_shapes=[stage(wg), stage(wu), stage(wd), cast(wg), cast(wu), cast(wd),
                            _tok_scratch(EXP_TR), pltpu.SemaphoreType.DMA((2,))],
        ),
        out_shape=jax.ShapeDtypeStruct(xs.shape, BF16),
        compiler_params=_params(("arbitrary",)),
        name="experts",
    )(tile_expert, n_used, next_expert, run_parity, xs, wg, wu, wd)


def _combine_kernel(dcur_ref, dnext_ref, wv_ref, ysh_ref, xn_ref, g2_ref, lg_ref, lb_ref,
                    ys_ref, o_ref, buf_ref, y_scr, sem):
    i = pl.program_id(0)
    n = pl.num_programs(0)
    tm = COMB_TM
    slot = i % 2

    def issue(d_ref, sl, j):
        for k in range(TOP_K):
            pltpu.make_async_copy(ys_ref.at[d_ref[j * TOP_K + k]], buf_ref.at[sl, k, j],
                                  sem.at[sl]).start(priority=k % 2)

    @pl.when(i == 0)
    def _():
        def first(j, carry):
            issue(dcur_ref, 0, j)
            return carry
        lax.fori_loop(0, tm, first, 0)

    for k in range(TOP_K):
        pltpu.make_async_copy(ys_ref.at[pl.ds(0, tm)], buf_ref.at[slot, k], sem.at[slot]).wait()

    half = TOK_ROWS // 2

    def accumulate(j):
        lo = jnp.zeros((half, LANES), F32)
        hi = jnp.zeros((half, LANES), F32)
        for k in range(TOP_K):
            gate = wv_ref[pl.ds(j * TOP_K + k, half, stride=0), :]
            rows = buf_ref[slot, k, j].astype(F32)
            lo += gate * rows[:half]
            hi += gate * rows[half:]
        y_scr[j, 0:half, :] = lo
        y_scr[j, half:TOK_ROWS, :] = hi

    @pl.when(i + 1 < n)
    def _():
        def both(jo, carry):
            for u in range(COMB_UNROLL):
                j = jo * COMB_UNROLL + u
                issue(dnext_ref, 1 - slot, j)
                accumulate(j)
            return carry
        lax.fori_loop(0, tm // COMB_UNROLL, both, 0)

    @pl.when(i + 1 == n)
    def _():
        def last(jo, carry):
            for u in range(COMB_UNROLL):
                accumulate(jo * COMB_UNROLL + u)
            return carry
        lax.fori_loop(0, tm // COMB_UNROLL, last, 0)
    y_routed = jnp.concatenate(_from_token_major(y_scr), axis=1)

    y2 = ysh_ref[...].astype(F32) + y_routed
    z = DEEPNORM_ALPHA * xn_ref[...] + g2_ref[0] * y2
    o_ref[...] = _layer_norm(z) * lg_ref[...] + lb_ref[...]


def _combine(slots, gates, ysh, xn, g2, lg, lb, ys, tiles_per_batch):
    T = ysh.shape[0]
    tm = COMB_TM
    n = T // tm
    row = pl.BlockSpec((tm, D_MODEL), lambda i: (i, 0))
    full = lambda a: pl.BlockSpec(a.shape, lambda i: (0,) * a.ndim)
    tok = lambda f: pl.BlockSpec((tm * TOP_K,), lambda i: (f(i),), memory_space=pltpu.SMEM)
    return pl.pallas_call(
        _combine_kernel,
        grid=(n,),
        in_specs=[tok(lambda i: i), tok(lambda i: jnp.minimum(i + 1, n - 1)),
                  pl.BlockSpec((tm * TOP_K, LANES), lambda i: (i, 0)), row, row,
                  pl.BlockSpec((1, 1, D_MODEL), lambda i: (i // tiles_per_batch, 0, 0)),
                  full(lg), full(lb),
                  pl.BlockSpec(memory_space=pl.ANY)],
        out_specs=row,
        out_shape=jax.ShapeDtypeStruct((T, D_MODEL), F32),
        scratch_shapes=[pltpu.VMEM((2, TOP_K, tm, TOK_ROWS, LANES), BF16),
                        _tok_scratch(tm),
                        pltpu.SemaphoreType.DMA((2,))],
        compiler_params=_params(("arbitrary",)),
        name="combine",
    )(slots, slots, gates, ysh, xn, g2, lg, lb, ys)


def _rope_tables(L, C):
    rows = L // GRID_W
    row = np.repeat(np.arange(rows, dtype=np.float64), GRID_W)
    col = np.tile(np.arange(GRID_W, dtype=np.float64), rows)
    inv_freq = np.exp(-np.log(ROPE_THETA) * np.arange(0, ROPE_AXIS_DIM, 2, dtype=np.float64) / ROPE_AXIS_DIM)
    ang_r = row[:, None] * inv_freq
    ang_c = col[:, None] * inv_freq
    cr, sr, cc, sc = np.cos(ang_r), np.sin(ang_r), np.cos(ang_c), np.sin(ang_c)
    cos_t = np.concatenate([cr, cr, cc, cc], axis=-1)
    sin_t = np.concatenate([-sr, sr, -sc, sc], axis=-1)
    cos_t = np.concatenate([np.ones((C, HEAD_DIM)), cos_t], axis=0)
    sin_t = np.concatenate([np.zeros((C, HEAD_DIM)), sin_t], axis=0)
    return jnp.asarray(cos_t, F32), jnp.asarray(sin_t, F32)


def kernel(x, c, ctx, c_ctx, w_mod, b_mod, w_in, attn_sink, conv_w, conv_b, lru_wa, lru_ba, lru_wx, lru_bx,
           lru_lam, norm_attn_g, norm_lru_g, w_out, ln1_g, ln1_b, router_w, router_bias, exp_w_gate, exp_w_up,
           exp_w_down, sh_w_gate, sh_w_up, sh_w_down, ln2_g, ln2_b):
    B, L, D = x.shape
    C = ctx.shape[1]
    assert w_mod.shape[0] == DEPTH and D == D_MODEL and B + 1 <= SUBLANES
    row2 = lambda a: a.reshape(1, -1)

    cvec = jnp.concatenate([c, c_ctx[None], jnp.zeros((SUBLANES - B - 1, D), F32)], axis=0)
    mod = _mod(cvec, w_mod[0], row2(b_mod[0]))

    cos_t, sin_t = _rope_tables(L, C)
    q, k, v, xr, yg = _inproj(x, ctx, mod, w_in[0], cos_t, sin_t)

    attn = _attention(attn_sink[0], q, k, v, L, C)

    w_gates = jnp.concatenate([lru_wa[0], lru_wx[0]], axis=-1).astype(BF16)
    bias = jnp.stack([lru_ba[0], lru_bx[0]], axis=1)
    hf, hb = _lru(xr, conv_w[0], row2(conv_b[0]), w_gates, bias, lru_lam[0], L)

    rw_hi, rw_lo = _split_bf16(router_w[0].T)
    tri = jnp.triu(jnp.ones((ROW_TILE, ROW_TILE), BF16))
    eye = jnp.eye(ROW_TILE, dtype=BF16)
    xn, u2, up, eidx, gates, pos, cnt = _merge(
        attn, hf, hb, yg, x, mod, w_out[0], row2(norm_attn_g[0]), row2(norm_lru_g[0]),
        row2(ln1_g[0]), row2(ln1_b[0]), rw_hi, rw_lo, router_bias[0].reshape(-1, 1), tri, eye, C)

    T = B * L
    n_tiles = T * TOP_K // EXP_TR + N_EXPERTS
    counts = cnt[:, 0].astype(jnp.int32)
    tiles_e = (counts + EXP_TR - 1) // EXP_TR
    tile_end = jnp.cumsum(tiles_e)
    n_used = tile_end[-1:]
    row_start = (tile_end - tiles_e) * EXP_TR
    experts = jnp.arange(N_EXPERTS, dtype=jnp.int32)
    start_of = jnp.sum(jnp.where(eidx[None] == experts[:, None, None], row_start[:, None, None], 0), axis=0)
    slots = (start_of + pos).T.reshape(-1)
    tile_id = jnp.minimum(jnp.arange(n_tiles, dtype=jnp.int32), n_used - 1)
    tile_expert = jnp.sum((tile_end[None, :] <= tile_id[:, None]).astype(jnp.int32), axis=1)

    all_tiles = jnp.arange(n_tiles, dtype=jnp.int32)
    ends_run = jnp.any((tile_end[None, :] - 1 == all_tiles[:, None]) & (tiles_e[None, :] > 0), axis=1)
    zero_tiles = (ends_run | (all_tiles >= n_used)).astype(jnp.int32)
    xs, ysh = _dispatch(zero_tiles, slots, up, u2.reshape(T, D), sh_w_gate[0].astype(BF16),
                        sh_w_up[0].astype(BF16), sh_w_down[0].astype(BF16), n_tiles * EXP_TR)
    has_rows = tiles_e > 0
    run_parity_e = (jnp.cumsum(has_rows) - has_rows) % 2
    later = has_rows[None, :] & (experts[None, :] > experts[:, None])
    next_e = jnp.min(jnp.where(later, experts[None, :], N_EXPERTS), axis=1)
    next_e = jnp.where(next_e == N_EXPERTS, -1, next_e)
    of_tile = tile_expert[:, None] == experts[None, :]
    per_tile = lambda v: jnp.sum(jnp.where(of_tile, v[None, :], 0), axis=1).astype(jnp.int32)
    ys = _experts(tile_expert, n_used, per_tile(next_e), per_tile(run_parity_e), xs,
                  exp_w_gate[0], exp_w_up[0], exp_w_down[0])
    g2 = mod[:B, 5 * D:6 * D].reshape(B, 1, D)
    out = _combine(slots, gates, ysh, xn.reshape(T, D), g2, row2(ln2_g[0]), row2(ln2_b[0]), ys, L // COMB_TM)
    return out.reshape(B, L, D)
```

```python
import functools

import jax
import jax.numpy as jnp
import numpy as np
from jax import lax
from jax.experimental import pallas as pl
from jax.experimental.pallas import tpu as pltpu

F32 = jnp.float32
BF16 = jnp.bfloat16

D_MODEL = 2048
GRID_W = 64
N_HEADS = 8
N_KV_HEADS = 2
HEAD_DIM = 128
GQA_GROUP = N_HEADS // N_KV_HEADS
ATTN_W = N_HEADS * HEAD_DIM
KV_W = N_KV_HEADS * HEAD_DIM
ATTN_SCALE = HEAD_DIM ** -0.5
BLOCK = 128
ROPE_THETA = 10000.0
ROPE_AXIS_DIM = HEAD_DIM // 2
LRU_W = D_MODEL - ATTN_W
LRU_BLOCKS = 8
LRU_BLOCK_W = LRU_W // LRU_BLOCKS
LRU_C = 8.0
N_EXPERTS = 64
TOP_K = 8
N_GROUPS = 8
GROUP_SIZE = N_EXPERTS // N_GROUPS
TOPK_GROUPS = 4
EXPERT_FF = 512
ROUTED_SCALE = 2.5
LN_EPS = 1e-6
SQRT_GUARD = 1e-30
DEPTH = 1
DEEPNORM_ALPHA = (2 * DEPTH) ** 0.25

SUBLANES = 8
LANES = 128
VMEM_LIMIT = 56 * 1024 * 1024

ROW_TILE = 256
MOD_TN = 1024
MERGE_SPLIT = 2
ATTN_STACK = GQA_GROUP
TOK_ROWS = D_MODEL // LANES
TOK_PITCH = TOK_ROWS + SUBLANES
EXP_TR = 256
CAST_CHUNKS = 16
DISP_TM = 1024
COMB_TM = 256
COMB_UNROLL = 8


def _params(sem, vmem=VMEM_LIMIT):
    return pltpu.CompilerParams(dimension_semantics=sem, vmem_limit_bytes=vmem)


def _layer_norm(x):
    mu = jnp.mean(x, axis=-1, keepdims=True)
    xc = x - mu
    var = jnp.mean(xc * xc, axis=-1, keepdims=True)
    return xc * lax.rsqrt(var + LN_EPS)


def _rms_norm(x, g):
    return x * lax.rsqrt(jnp.mean(x * x, axis=-1, keepdims=True) + LN_EPS) * g


def _sigmoid(x):
    return 0.5 * jnp.tanh(0.5 * x) + 0.5


def _split_bf16(x):
    hi = x.astype(BF16)
    lo = (x - hi.astype(F32)).astype(BF16)
    return hi, lo


def _mod_kernel(c_ref, w_ref, b_ref, o_ref):
    cv = c_ref[...]
    s = cv * jax.nn.sigmoid(cv)
    hi, lo = _split_bf16(s)
    lhs = jnp.concatenate([hi, lo], axis=0)
    r = jnp.dot(lhs, w_ref[...].astype(BF16), preferred_element_type=F32)
    o_ref[...] = r[:SUBLANES] + r[SUBLANES:] + b_ref[...]


def _mod(cvec, w_mod, b_mod):
    n = w_mod.shape[1]
    return pl.pallas_call(
        _mod_kernel,
        grid=(n // MOD_TN,),
        in_specs=[
            pl.BlockSpec((SUBLANES, D_MODEL), lambda j: (0, 0)),
            pl.BlockSpec((D_MODEL, MOD_TN), lambda j: (0, j)),
            pl.BlockSpec((1, MOD_TN), lambda j: (0, j)),
        ],
        out_specs=pl.BlockSpec((SUBLANES, MOD_TN), lambda j: (0, j)),
        out_shape=jax.ShapeDtypeStruct((SUBLANES, n), F32),
        compiler_params=_params(("arbitrary",)),
        name="mod",
    )(cvec, w_mod, b_mod)


def _rope(xh, cos, sin, even_block):
    partner = jnp.where(even_block, pltpu.roll(xh, 96, 1), pltpu.roll(xh, 32, 1))
    return xh * cos + partner * sin


def _inproj_kernel(x_ref, ctx_ref, mod_ref, w_ref, cos_ref, sin_ref,
                   q_ref, k_ref, v_ref, xr_ref, yg_ref):
    b = pl.program_id(0)
    t = pl.program_id(1)
    is_ctx = t == 0
    xin = jnp.where(is_ctx, ctx_ref[0], x_ref[0])
    r = jnp.where(is_ctx, 2, b)
    shift = mod_ref[pl.ds(r, 1), 0:D_MODEL]
    scale = mod_ref[pl.ds(r, 1), D_MODEL:2 * D_MODEL]
    u = (_layer_norm(xin) * (1.0 + scale) + shift).astype(BF16)

    cos = cos_ref[...]
    sin = sin_ref[...]
    lane = lax.broadcasted_iota(jnp.int32, (ROW_TILE, HEAD_DIM), 1)
    even_block = (lane % 64) < 32

    def proj(c0, c1):
        return jnp.dot(u, w_ref[:, c0:c1], preferred_element_type=F32)

    head = lambda y, h: y[:, h * HEAD_DIM:(h + 1) * HEAD_DIM]
    for p in range(N_HEADS // 2):
        qq = proj(2 * p * HEAD_DIM, (2 * p + 2) * HEAD_DIM)
        for h in range(2):
            q_ref[0, :, (2 * p + h) * HEAD_DIM:(2 * p + h + 1) * HEAD_DIM] = (
                _rope(head(qq, h), cos, sin, even_block) * ATTN_SCALE).astype(BF16)
    kk = proj(ATTN_W, ATTN_W + KV_W)
    for h in range(N_KV_HEADS):
        k_ref[0, :, h * HEAD_DIM:(h + 1) * HEAD_DIM] = _rope(head(kk, h), cos, sin, even_block).astype(BF16)
    v_ref[0] = proj(ATTN_W + KV_W, ATTN_W + 2 * KV_W).astype(BF16)
    c0 = ATTN_W + 2 * KV_W
    xr_ref[0] = proj(c0, c0 + LRU_W)
    yg_ref[0] = proj(c0 + LRU_W, c0 + 2 * LRU_W)


def _inproj(x, ctx, mod, w_in_bf, cos_t, sin_t):
    B, L, _ = x.shape
    C = ctx.shape[1]
    assert C == ROW_TILE and L % ROW_TILE == 0
    nt = L // ROW_TILE + 1
    rows = L + C
    d_in = w_in_bf.shape[1]
    out = lambda w, dt: jax.ShapeDtypeStruct((B, rows, w), dt)
    ospec = lambda w: pl.BlockSpec((1, ROW_TILE, w), lambda b, t: (b, t, 0))
    return pl.pallas_call(
        _inproj_kernel,
        grid=(B, nt),
        in_specs=[
            pl.BlockSpec((1, ROW_TILE, D_MODEL), lambda b, t: (b, jnp.maximum(t - 1, 0), 0)),
            pl.BlockSpec((1, ROW_TILE, D_MODEL), lambda b, t: (b, 0, 0)),
            pl.BlockSpec((SUBLANES, 2 * D_MODEL), lambda b, t: (0, 0)),
            pl.BlockSpec((D_MODEL, d_in), lambda b, t: (0, 0)),
            pl.BlockSpec((ROW_TILE, HEAD_DIM), lambda b, t: (t, 0)),
            pl.BlockSpec((ROW_TILE, HEAD_DIM), lambda b, t: (t, 0)),
        ],
        out_specs=[ospec(ATTN_W), ospec(KV_W), ospec(KV_W), ospec(LRU_W), ospec(LRU_W)],
        out_shape=[out(ATTN_W, BF16), out(KV_W, BF16), out(KV_W, BF16), out(LRU_W, F32), out(LRU_W, F32)],
        compiler_params=_params(("arbitrary", "arbitrary")),
        name="inproj",
    )(x, ctx, mod, w_in_bf, cos_t, sin_t)


def _attn_kernel(sink_ref, q_ref, kp_ref, kc_ref, kn_ref, vp_ref, vc_ref, vn_ref, kx_ref, vx_ref, o_ref):
    n = pl.program_id(1)
    nb = pl.num_programs(1)
    rows = ATTN_STACK * BLOCK
    qi = lax.broadcasted_iota(jnp.int32, (rows, BLOCK), 0) % BLOCK
    kj = lax.broadcasted_iota(jnp.int32, (rows, BLOCK), 1)
    prev_ok = (kj >= qi) & (n > 0)
    next_ok = (kj <= qi) & (n < nb - 1)
    grp = lax.broadcasted_iota(jnp.int32, (rows, 1), 0) // BLOCK
    neg = -jnp.inf
    for q0 in range(0, N_HEADS, ATTN_STACK):
        h = q0 // GQA_GROUP
        hs = slice(h * HEAD_DIM, (h + 1) * HEAD_DIM)
        q4 = jnp.concatenate(
            [q_ref[0, :, (q0 + g) * HEAD_DIM:(q0 + g + 1) * HEAD_DIM] for g in range(ATTN_STACK)], axis=0)
        nt = (((1,), (1,)), ((), ()))
        sp = jnp.where(prev_ok, lax.dot_general(q4, kp_ref[0, :, hs], nt, preferred_element_type=F32), neg)
        sc = lax.dot_general(q4, kc_ref[0, :, hs], nt, preferred_element_type=F32)
        sn = jnp.where(next_ok, lax.dot_general(q4, kn_ref[0, :, hs], nt, preferred_element_type=F32), neg)
        sx = lax.dot_general(q4, kx_ref[0, :, hs], nt, preferred_element_type=F32)
        sink = jnp.zeros((rows, 1), F32)
        for g in range(ATTN_STACK):
            sink = jnp.where(grp == g, sink_ref[q0 + g], sink)
        lane_chunks = lambda a: [a[:, c * BLOCK:(c + 1) * BLOCK] for c in range(a.shape[1] // BLOCK)]
        fold = lambda op, parts: functools.reduce(op, parts)
        m = jnp.max(fold(jnp.maximum, [sp, sc, sn] + lane_chunks(sx)), -1, keepdims=True)
        m = jnp.maximum(m, sink)
        pp = jnp.exp(sp - m)
        pc = jnp.exp(sc - m)
        pn = jnp.exp(sn - m)
        px = jnp.exp(sx - m)
        denom = jnp.sum(fold(jnp.add, [pp, pc, pn] + lane_chunks(px)), -1, keepdims=True) + jnp.exp(sink - m)
        acc = jnp.dot(pp.astype(BF16), vp_ref[0, :, hs], preferred_element_type=F32)
        acc += jnp.dot(pc.astype(BF16), vc_ref[0, :, hs], preferred_element_type=F32)
        acc += jnp.dot(pn.astype(BF16), vn_ref[0, :, hs], preferred_element_type=F32)
        acc += jnp.dot(px.astype(BF16), vx_ref[0, :, hs], preferred_element_type=F32)
        o = acc / denom
        for g in range(ATTN_STACK):
            c0 = (q0 + g) * HEAD_DIM
            o_ref[0, :, c0:c0 + HEAD_DIM] = o[g * BLOCK:(g + 1) * BLOCK]


def _attention(sink, q, k, v, L, C):
    B = q.shape[0]
    nb = L // BLOCK
    off = C // BLOCK
    cur = lambda b, n: (b, n + off, 0)
    prv = lambda b, n: (b, jnp.maximum(n - 1, 0) + off, 0)
    nxt = lambda b, n: (b, jnp.minimum(n + 1, nb - 1) + off, 0)
    kv = lambda im: pl.BlockSpec((1, BLOCK, KV_W), im)
    cx = pl.BlockSpec((1, C, KV_W), lambda b, n: (b, 0, 0))
    return pl.pallas_call(
        _attn_kernel,
        grid=(B, nb),
        in_specs=[
            pl.BlockSpec(memory_space=pltpu.SMEM),
            pl.BlockSpec((1, BLOCK, ATTN_W), cur),
            kv(prv), kv(cur), kv(nxt), kv(prv), kv(cur), kv(nxt), cx, cx,
        ],
        out_specs=pl.BlockSpec((1, BLOCK, ATTN_W), lambda b, n: (b, n, 0)),
        out_shape=jax.ShapeDtypeStruct((B, L, ATTN_W), F32),
        compiler_params=_params(("arbitrary", "arbitrary")),
        name="attn",
    )(sink, q, k, k, k, v, v, v, k, v)


def _lru_coeffs(x_ref, p_ref, n_ref, tile, d, cw_ref, cb_ref, w_ref, bias_ref, lam_ref,
                ext_scr, a_scr, b_scr, n_lat_tiles):
    tm = ROW_TILE
    prev_ok = tile >= 2
    next_ok = (tile >= 1) & (tile < n_lat_tiles)
    ext_scr[0:SUBLANES] = jnp.where(prev_ok, p_ref[0], 0.0)
    ext_scr[SUBLANES:SUBLANES + tm] = x_ref[0]
    ext_scr[SUBLANES + tm:2 * SUBLANES + tm] = jnp.where(next_ok, n_ref[0], 0.0)
    xc = cb_ref[...] + cw_ref[2:3] * x_ref[0]
    xc += cw_ref[0:1] * ext_scr[SUBLANES - 2:SUBLANES - 2 + tm]
    xc += cw_ref[1:2] * ext_scr[SUBLANES - 1:SUBLANES - 1 + tm]
    xc += cw_ref[3:4] * ext_scr[SUBLANES + 1:SUBLANES + 1 + tm]
    xcb = xc.astype(BF16)
    lam = lam_ref[d:d + 1]
    sp = jnp.maximum(-lam, 0.0) + jnp.log1p(jnp.exp(-jnp.abs(lam)))
    for n in range(LRU_BLOCKS):
        cs = slice(n * LRU_BLOCK_W, (n + 1) * LRU_BLOCK_W)
        z = jnp.dot(xcb[:, cs], w_ref[d, n], preferred_element_type=F32)
        r = _sigmoid(z[:, :LRU_BLOCK_W] + bias_ref[d, 0:1, cs])
        i = _sigmoid(z[:, LRU_BLOCK_W:] + bias_ref[d, 1:2, cs])
        log_a = -LRU_C * r * sp[:, cs]
        a = jnp.exp(log_a)
        a_scr[:, cs] = a
        v = 1.0 - a * a
        b_scr[:, cs] = v * lax.rsqrt(jnp.maximum(v, SQRT_GUARD)) * (i * xc[:, cs])


def _lru_scan(a_scr, b_scr, h_ref, state_ref, reset, reverse):
    groups = ROW_TILE // SUBLANES
    row = lax.broadcasted_iota(jnp.int32, (SUBLANES, LRU_W), 0)
    carry0 = jnp.where(reset, 0.0, state_ref[...])

    def body(g, carry):
        gi = (groups - 1 - g) if reverse else g
        r0 = pl.multiple_of(gi * SUBLANES, SUBLANES)
        A = a_scr[pl.ds(r0, SUBLANES), :]
        Bv = b_scr[pl.ds(r0, SUBLANES), :]
        for s in (1, 2, 4):
            if reverse:
                sh, m = SUBLANES - s, row < SUBLANES - s
            else:
                sh, m = s, row >= s
            A_sh = pltpu.roll(A, sh, 0)
            B_sh = pltpu.roll(Bv, sh, 0)
            Bv = jnp.where(m, A * B_sh + Bv, Bv)
            A = jnp.where(m, A * A_sh, A)
        h = Bv + A * carry
        h_ref[0, pl.ds(r0, SUBLANES), :] = h
        last = h[0:1] if reverse else h[SUBLANES - 1:SUBLANES]
        return jnp.broadcast_to(last, (SUBLANES, LRU_W))

    state_ref[...] = lax.fori_loop(0, groups, body, carry0)


def _lru_kernel(xf_ref, xfp_ref, xfn_ref, xb_ref, xbp_ref, xbn_ref, cw_ref, cb_ref, w_ref, bias_ref, lam_ref,
                hf_ref, hb_ref, sf_ref, sb_ref, ext_scr, a_scr, b_scr, *, n_lat_tiles):
    t = pl.program_id(1)
    reset = t == 0
    bt = jnp.where(t == 0, 0, n_lat_tiles + 1 - t)
    _lru_coeffs(xf_ref, xfp_ref, xfn_ref, t, 0, cw_ref, cb_ref, w_ref, bias_ref, lam_ref,
                ext_scr, a_scr, b_scr, n_lat_tiles)
    _lru_scan(a_scr, b_scr, hf_ref, sf_ref, reset, False)
    _lru_coeffs(xb_ref, xbp_ref, xbn_ref, bt, 1, cw_ref, cb_ref, w_ref, bias_ref, lam_ref,
                ext_scr, a_scr, b_scr, n_lat_tiles)
    _lru_scan(a_scr, b_scr, hb_ref, sb_ref, reset, True)


def _lru(xr, conv_w, conv_b, w_gates, bias, lam, L):
    B, rows, _ = xr.shape
    nl = L // ROW_TILE
    nt = nl + 1
    per = ROW_TILE // SUBLANES
    n8 = rows // SUBLANES
    ft = lambda b, t: t
    btile = lambda b, t: jnp.where(t == 0, 0, nl + 1 - t)
    main = lambda f: pl.BlockSpec((1, ROW_TILE, LRU_W), lambda b, t: (b, f(b, t), 0))
    prev = lambda f: pl.BlockSpec((1, SUBLANES, LRU_W), lambda b, t: (b, jnp.maximum(f(b, t) * per - 1, 0), 0))
    nxt = lambda f: pl.BlockSpec((1, SUBLANES, LRU_W),
                                 lambda b, t: (b, jnp.minimum((f(b, t) + 1) * per, n8 - 1), 0))
    full = lambda a: pl.BlockSpec(a.shape, lambda b, t: (0,) * a.ndim)
    return pl.pallas_call(
        functools.partial(_lru_kernel, n_lat_tiles=nl),
        grid=(B, nt),
        in_specs=[main(ft), prev(ft), nxt(ft), main(btile), prev(btile), nxt(btile),
                  full(conv_w), full(conv_b), full(w_gates), full(bias), full(lam)],
        out_specs=[
            pl.BlockSpec((1, ROW_TILE, LRU_W), lambda b, t: (b, jnp.maximum(t - 1, 0), 0)),
            pl.BlockSpec((1, ROW_TILE, LRU_W), lambda b, t: (b, nl - jnp.maximum(t, 1), 0)),
        ],
        out_shape=[jax.ShapeDtypeStruct((B, L, LRU_W), F32)] * 2,
        scratch_shapes=[
            pltpu.VMEM((SUBLANES, LRU_W), F32), pltpu.VMEM((SUBLANES, LRU_W), F32),
            pltpu.VMEM((ROW_TILE + 2 * SUBLANES, LRU_W), F32),
            pltpu.VMEM((ROW_TILE, LRU_W), F32), pltpu.VMEM((ROW_TILE, LRU_W), F32),
        ],
        compiler_params=_params(("arbitrary", "arbitrary")),
        name="lru",
    )(xr, xr, xr, xr, xr, xr, conv_w, conv_b, w_gates, bias, lam)


def _route(scores, sel):
    tm = scores.shape[1]
    neg = -jnp.inf
    iota_g = lax.broadcasted_iota(jnp.int32, (GROUP_SIZE, tm), 0)
    grp_score = []
    for g in range(N_GROUPS):
        sg = sel[g * GROUP_SIZE:(g + 1) * GROUP_SIZE]
        m1 = jnp.max(sg, axis=0, keepdims=True)
        first = jnp.min(jnp.where(sg == m1, iota_g, GROUP_SIZE), axis=0, keepdims=True)
        m2 = jnp.max(jnp.where(iota_g == first, neg, sg), axis=0, keepdims=True)
        grp_score.append(m1 + m2)
    masked = []
    for g in range(N_GROUPS):
        rank = jnp.zeros((1, tm), jnp.int32)
        for o in range(N_GROUPS):
            if o == g:
                continue
            ahead = (grp_score[o] > grp_score[g]) if o > g else (grp_score[o] >= grp_score[g])
            rank += ahead.astype(jnp.int32)
        keep = rank < TOPK_GROUPS
        masked.append(jnp.where(keep, sel[g * GROUP_SIZE:(g + 1) * GROUP_SIZE], neg))
    cand = jnp.concatenate(masked, axis=0)
    iota_e = lax.broadcasted_iota(jnp.int32, (N_EXPERTS, tm), 0)
    chosen = jnp.zeros((N_EXPERTS, tm), jnp.bool_)
    picks = []
    for _ in range(TOP_K):
        cur = jnp.where(chosen, neg, cand)
        m = jnp.max(cur, axis=0, keepdims=True)
        idx = jnp.min(jnp.where((cur == m) & jnp.logical_not(chosen), iota_e, N_EXPERTS), axis=0, keepdims=True)
        chosen = chosen | (iota_e == idx)
        picks.append(idx)
    w = jnp.where(chosen, scores, 0.0)
    gates = w / jnp.sum(w, axis=0, keepdims=True) * ROUTED_SCALE
    return gates, chosen, picks


def _tok_scratch(rows):
    return pltpu.VMEM((rows, TOK_PITCH, LANES), F32)


def _to_token_major(y, scr):
    rows = scr.shape[0]
    flat = scr.reshape(rows * TOK_PITCH, LANES)
    for s in range(TOK_ROWS):
        flat[pl.ds(s, rows, stride=TOK_PITCH), :] = y[:, s * LANES:(s + 1) * LANES]


def _from_token_major(scr):
    rows = scr.shape[0]
    flat = scr.reshape(rows * TOK_PITCH, LANES)
    return [flat[pl.ds(s, rows, stride=TOK_PITCH), :] for s in range(TOK_ROWS)]


def _token_tiles(scr):
    return scr.at[:, 0:TOK_ROWS, :]


def _merge_kernel(attn_ref, hf_ref, hb_ref, yg_ref, x_ref, mod_ref, wout_ref, ga_ref, gl_ref, lg_ref, lb_ref,
                  rwh_ref, rwl_ref, rb_ref, tri_ref, eye_ref, xn_ref, u2_ref, up_ref, eidx_ref, gates_ref, pos_ref,
                  cnt_ref, carry_ref, tok_scr):
    b = pl.program_id(0)

    @pl.when((b == 0) & (pl.program_id(1) == 0))
    def _():
        carry_ref[...] = jnp.zeros_like(carry_ref)

    D = D_MODEL
    mrow = lambda i: mod_ref[pl.ds(b, 1), i * D:(i + 1) * D]
    nt = (((1,), (1,)), ((), ()))
    u2_parts, logit_parts = [], []
    sub = ROW_TILE // MERGE_SPLIT
    for p in range(MERGE_SPLIT):
        rs = slice(p * sub, (p + 1) * sub)
        lru_y = (hf_ref[0, rs] + hb_ref[0, rs]) * jax.nn.gelu(yg_ref[0, rs])
        na = _rms_norm(attn_ref[0, rs], ga_ref[...]).astype(BF16)
        nl = _rms_norm(lru_y, gl_ref[...]).astype(BF16)
        y1 = jnp.dot(na, wout_ref[0:ATTN_W], preferred_element_type=F32)
        y1 += jnp.dot(nl, wout_ref[ATTN_W:D], preferred_element_type=F32)
        xn = _layer_norm(DEEPNORM_ALPHA * x_ref[0, rs] + mrow(2) * y1) * lg_ref[...] + lb_ref[...]
        xn_ref[0, rs] = xn
        u2_p = _layer_norm(xn) * (1.0 + mrow(4)) + mrow(3)
        u_hi, u_lo = _split_bf16(u2_p)
        u2_ref[0, rs] = u_hi
        lg_p = lax.dot_general(rwh_ref[...], u_hi, nt, preferred_element_type=F32)
        lg_p += lax.dot_general(rwh_ref[...], u_lo, nt, preferred_element_type=F32)
        lg_p += lax.dot_general(rwl_ref[...], u_hi, nt, preferred_element_type=F32)
        u2_parts.append(u2_p)
        logit_parts.append(lg_p)
    u2 = jnp.concatenate(u2_parts, axis=0)
    _to_token_major(u2, tok_scr)
    up_ref[...] = _token_tiles(tok_scr)[...].astype(BF16)
    scores = jax.nn.sigmoid(jnp.concatenate(logit_parts, axis=1))
    gates, chosen, picks = _route(scores, scores + rb_ref[...])
    sel01 = jnp.where(chosen, 1.0, 0.0)
    incl = jnp.dot(sel01.astype(BF16), tri_ref[...], preferred_element_type=F32)
    rank = carry_ref[:, 0:1] + incl - sel01
    carry_ref[...] = carry_ref[...] + incl[:, ROW_TILE - 1:ROW_TILE]
    cnt_ref[...] = carry_ref[...]
    iota_e = lax.broadcasted_iota(jnp.int32, (N_EXPERTS, ROW_TILE), 0)
    gate_rows = []
    for kk, idx in enumerate(picks):
        hit = iota_e == idx
        eidx_ref[kk:kk + 1, :] = idx
        gate_rows.append(jnp.sum(jnp.where(hit, gates, 0.0), axis=0, keepdims=True))
        pos_ref[kk:kk + 1, :] = jnp.sum(jnp.where(hit, rank, 0.0), axis=0, keepdims=True).astype(jnp.int32)

    piece = jnp.concatenate(gate_rows, axis=0)
    gate_cols = jnp.zeros((ROW_TILE, TOP_K), F32)
    for _ in range(3):
        part = piece.astype(BF16)
        gate_cols += lax.dot_general(eye_ref[...], part, nt, preferred_element_type=F32)
        piece = piece - part.astype(F32)
    for kk in range(TOP_K):
        gates_ref[pl.ds(kk, ROW_TILE, stride=TOP_K), :] = jnp.broadcast_to(gate_cols[:, kk:kk + 1],
                                                                           (ROW_TILE, LANES))


def _merge(attn, hf, hb, yg, x, mod, w_out_bf, ga, gl, lg, lb, rw_hi, rw_lo, rbias, tri, eye, C):
    B, L, _ = x.shape
    T = B * L
    nt = L // ROW_TILE
    off = C // ROW_TILE
    row = lambda w, o=0: pl.BlockSpec((1, ROW_TILE, w), lambda b, t: (b, t + o, 0))
    full = lambda a: pl.BlockSpec(a.shape, lambda b, t: (0,) * a.ndim)
    tok = pl.BlockSpec((TOP_K, ROW_TILE), lambda b, t: (0, b * nt + t))
    return pl.pallas_call(
        _merge_kernel,
        grid=(B, nt),
        in_specs=[row(ATTN_W), row(LRU_W), row(LRU_W), row(LRU_W, off), row(D_MODEL),
                  full(mod), full(w_out_bf), full(ga), full(gl), full(lg), full(lb),
                  full(rw_hi), full(rw_lo), full(rbias), full(tri), full(eye)],
        out_specs=[row(D_MODEL), row(D_MODEL),
                   pl.BlockSpec((ROW_TILE, TOK_ROWS, LANES), lambda b, t: (b * nt + t, 0, 0)),
                   tok,
                   pl.BlockSpec((ROW_TILE * TOP_K, LANES), lambda b, t: (b * nt + t, 0)),
                   tok,
                   pl.BlockSpec((N_EXPERTS, LANES), lambda b, t: (0, 0))],
        out_shape=[jax.ShapeDtypeStruct((B, L, D_MODEL), F32),
                   jax.ShapeDtypeStruct((B, L, D_MODEL), BF16),
                   jax.ShapeDtypeStruct((T, TOK_ROWS, LANES), BF16),
                   jax.ShapeDtypeStruct((TOP_K, T), jnp.int32),
                   jax.ShapeDtypeStruct((T * TOP_K, LANES), F32),
                   jax.ShapeDtypeStruct((TOP_K, T), jnp.int32),
                   jax.ShapeDtypeStruct((N_EXPERTS, LANES), F32)],
        scratch_shapes=[pltpu.VMEM((N_EXPERTS, LANES), F32), _tok_scratch(ROW_TILE)],
        compiler_params=_params(("arbitrary", "arbitrary")),
        name="merge",
    )(attn, hf, hb, yg, x, mod, w_out_bf, ga, gl, lg, lb, rw_hi, rw_lo, rbias, tri, eye)


def _swiglu(u, wg, wu):
    hg = jnp.dot(u, wg, preferred_element_type=F32)
    hu = jnp.dot(u, wu, preferred_element_type=F32)
    return hg * jax.nn.sigmoid(hg) * hu


def _dispatch_kernel(zt_ref, dest_ref, up_ref, u_ref, sg_ref, su_ref, sd_ref, xs_ref, ysh_ref, zero_scr, sem, zsem):
    @pl.when(pl.program_id(0) == 0)
    def _():
        zero_scr[...] = jnp.zeros_like(zero_scr)
        n_tiles = zt_ref.shape[0]

        def zero_tile(z):
            return pltpu.make_async_copy(zero_scr, xs_ref.at[pl.ds(z * EXP_TR, EXP_TR)], zsem)

        def start(z, carry):
            @pl.when(zt_ref[z] != 0)
            def _():
                zero_tile(z).start()
            return carry

        def wait(z, carry):
            @pl.when(zt_ref[z] != 0)
            def _():
                zero_tile(z).wait()
            return carry

        lax.fori_loop(0, n_tiles, start, 0)
        lax.fori_loop(0, n_tiles, wait, 0)

    def issue(j, carry):
        for k in range(TOP_K):
            pltpu.make_async_copy(up_ref.at[j], xs_ref.at[dest_ref[j * TOP_K + k]], sem).start(priority=k % 2)
        return carry

    lax.fori_loop(0, DISP_TM, issue, 0)

    hs = _swiglu(u_ref[...], sg_ref[...], su_ref[...])
    ysh_ref[...] = jnp.dot(hs.astype(BF16), sd_ref[...], preferred_element_type=F32).astype(BF16)

    for k in range(TOP_K):
        pltpu.make_async_copy(up_ref, xs_ref.at[pl.ds(0, DISP_TM)], sem).wait()


def _dispatch(zero_tiles, slots, up, u2, sg, su, sd, n_sorted_rows):
    T = up.shape[0]
    full = lambda a: pl.BlockSpec(a.shape, lambda i, zt: (0,) * a.ndim)
    rows = pl.BlockSpec((DISP_TM, D_MODEL), lambda i, zt: (i, 0))
    return pl.pallas_call(
        _dispatch_kernel,
        grid_spec=pltpu.PrefetchScalarGridSpec(
            num_scalar_prefetch=1,
            grid=(T // DISP_TM,),
            in_specs=[pl.BlockSpec((DISP_TM * TOP_K,), lambda i, zt: (i,), memory_space=pltpu.SMEM),
                      pl.BlockSpec((DISP_TM, TOK_ROWS, LANES), lambda i, zt: (i, 0, 0)),
                      rows, full(sg), full(su), full(sd)],
            out_specs=[pl.BlockSpec(memory_space=pl.ANY), rows],
            scratch_shapes=[pltpu.VMEM((EXP_TR, TOK_ROWS, LANES), BF16),
                            pltpu.SemaphoreType.DMA(()), pltpu.SemaphoreType.DMA(())],
        ),
        out_shape=[jax.ShapeDtypeStruct((n_sorted_rows, TOK_ROWS, LANES), BF16),
                   jax.ShapeDtypeStruct((T, D_MODEL), BF16)],
        compiler_params=_params(("arbitrary",)),
        name="dispatch",
    )(zero_tiles, slots, up, u2, sg, su, sd)


def _expert_kernel(te_ref, nu_ref, nxt_ref, par_ref, val_ref, xs_ref, wg_hbm, wu_hbm, wd_hbm, ys_ref,
                   wg32, wu32, wd32, wgb_ref, wub_ref, wdb_ref, tok_scr, wsem):
    i = pl.program_id(0)
    n_used = nu_ref[0]

    def weight_copies(e, s):
        return [pltpu.make_async_copy(hbm.at[e], stage.at[s], wsem.at[s])
                for hbm, stage in ((wg_hbm, wg32), (wu_hbm, wu32), (wd_hbm, wd32))]

    @pl.when(i == 0)
    def _():
        for cp in weight_copies(te_ref[0], 0):
            cp.start(priority=1)

    @pl.when(i < n_used)
    def _():
        e = te_ref[i]

        first_of_expert = (i == 0) | (e != te_ref[jnp.maximum(i - 1, 0)])
        for s in range(2):
            @pl.when(first_of_expert & (par_ref[i] == s))
            def _():
                for cp in weight_copies(e, s):
                    cp.wait()
                for src, dst in ((wg32, wgb_ref), (wu32, wub_ref), (wd32, wdb_ref)):
                    rows = dst.shape[0] // CAST_CHUNKS

                    def cast_chunk(r, carry, src=src, dst=dst, rows=rows):
                        sl = pl.ds(pl.multiple_of(r * rows, rows), rows)
                        dst[sl, :] = src[s, sl, :].astype(BF16)
                        return carry

                    lax.fori_loop(0, CAST_CHUNKS, cast_chunk, 0)

                @pl.when(nxt_ref[i] >= 0)
                def _():
                    for cp in weight_copies(nxt_ref[i], 1 - s):
                        cp.start(priority=1)

        def tile_body(rows):
            scr = tok_scr.at[0:rows]
            _token_tiles(scr)[...] = xs_ref[0:rows].astype(F32)
            x = jnp.concatenate([c.astype(BF16) for c in _from_token_major(scr)], axis=1)
            h = _swiglu(x, wgb_ref[...], wub_ref[...])
            y = jnp.dot(h.astype(BF16), wdb_ref[...], preferred_element_type=F32)
            _to_token_major(y, scr)
            ys_ref[0:rows] = _token_tiles(scr)[...].astype(BF16)
            if rows < EXP_TR:
                ys_ref[rows:EXP_TR] = jnp.zeros((EXP_TR - rows, TOK_ROWS, LANES), BF16)

        @pl.when(val_ref[i] > EXP_TR // 2)
        def _():
            tile_body(EXP_TR)

        @pl.when(val_ref[i] <= EXP_TR // 2)
        def _():
            tile_body(EXP_TR // 2)

    @pl.when(i >= n_used)
    def _():
        ys_ref[...] = jnp.zeros_like(ys_ref)


def _experts(tile_expert, n_used, next_expert, run_parity, tile_valid, xs, wg, wu, wd):
    n_tiles = xs.shape[0] // EXP_TR
    blk = (EXP_TR, TOK_ROWS, LANES)
    stage = lambda a: pltpu.VMEM((2,) + a.shape[1:], F32)
    cast = lambda a: pltpu.VMEM(a.shape[1:], BF16)
    hbm = pl.BlockSpec(memory_space=pl.ANY)
    return pl.pallas_call(
        _expert_kernel,
        grid_spec=pltpu.PrefetchScalarGridSpec(
            num_scalar_prefetch=5,
            grid=(n_tiles,),
            in_specs=[pl.BlockSpec(blk, lambda i, te, nu, nx, pr, vl: (jnp.minimum(i, nu[0] - 1), 0, 0)),
                      hbm, hbm, hbm],
            out_specs=pl.BlockSpec(blk, lambda i, te, nu, nx, pr, vl: (i, 0, 0)),
            scratch_shapes=[stage(wg), stage(wu), stage(wd), cast(wg), cast(wu), cast(wd),
                            _tok_scratch(EXP_TR), pltpu.SemaphoreType.DMA((2,))],
        ),
        out_shape=jax.ShapeDtypeStruct(xs.shape, BF16),
        compiler_params=_params(("arbitrary",)),
        name="experts",
    )(tile_expert, n_used, next_expert, run_parity, tile_valid, xs, wg, wu, wd)


def _combine_kernel(dcur_ref, dnext_ref, wv_ref, ysh_ref, xn_ref, g2_ref, lg_ref, lb_ref,
                    ys_ref, o_ref, buf_ref, y_scr, sem):
    i = pl.program_id(0)
    n = pl.num_programs(0)
    tm = COMB_TM
    slot = i % 2

    def issue(d_ref, sl, j):
        for k in range(TOP_K):
            pltpu.make_async_copy(ys_ref.at[d_ref[j * TOP_K + k]], buf_ref.at[sl, k, j],
                                  sem.at[sl]).start(priority=k % 2)

    @pl.when(i == 0)
    def _():
        def first(j, carry):
            issue(dcur_ref, 0, j)
            return carry
        lax.fori_loop(0, tm, first, 0)

    for k in range(TOP_K):
        pltpu.make_async_copy(ys_ref.at[pl.ds(0, tm)], buf_ref.at[slot, k], sem.at[slot]).wait()

    half = TOK_ROWS // 2

    def accumulate(j):
        lo = jnp.zeros((half, LANES), F32)
        hi = jnp.zeros((half, LANES), F32)
        for k in range(TOP_K):
            gate = wv_ref[pl.ds(j * TOP_K + k, half, stride=0), :]
            rows = buf_ref[slot, k, j].astype(F32)
            lo += gate * rows[:half]
            hi += gate * rows[half:]
        y_scr[j, 0:half, :] = lo
        y_scr[j, half:TOK_ROWS, :] = hi

    @pl.when(i + 1 < n)
    def _():
        def both(jo, carry):
            for u in range(COMB_UNROLL):
                j = jo * COMB_UNROLL + u
                issue(dnext_ref, 1 - slot, j)
                accumulate(j)
            return carry
        lax.fori_loop(0, tm // COMB_UNROLL, both, 0)

    @pl.when(i + 1 == n)
    def _():
        def last(jo, carry):
            for u in range(COMB_UNROLL):
                accumulate(jo * COMB_UNROLL + u)
            return carry
        lax.fori_loop(0, tm // COMB_UNROLL, last, 0)
    y_routed = jnp.concatenate(_from_token_major(y_scr), axis=1)

    y2 = ysh_ref[...].astype(F32) + y_routed
    z = DEEPNORM_ALPHA * xn_ref[...] + g2_ref[0] * y2
    o_ref[...] = _layer_norm(z) * lg_ref[...] + lb_ref[...]


def _combine(slots, gates, ysh, xn, g2, lg, lb, ys, tiles_per_batch):
    T = ysh.shape[0]
    tm = COMB_TM
    n = T // tm
    row = pl.BlockSpec((tm, D_MODEL), lambda i: (i, 0))
    full = lambda a: pl.BlockSpec(a.shape, lambda i: (0,) * a.ndim)
    tok = lambda f: pl.BlockSpec((tm * TOP_K,), lambda i: (f(i),), memory_space=pltpu.SMEM)
    return pl.pallas_call(
        _combine_kernel,
        grid=(n,),
        in_specs=[tok(lambda i: i), tok(lambda i: jnp.minimum(i + 1, n - 1)),
                  pl.BlockSpec((tm * TOP_K, LANES), lambda i: (i, 0)), row, row,
                  pl.BlockSpec((1, 1, D_MODEL), lambda i: (i // tiles_per_batch, 0, 0)),
                  full(lg), full(lb),
                  pl.BlockSpec(memory_space=pl.ANY)],
        out_specs=row,
        out_shape=jax.ShapeDtypeStruct((T, D_MODEL), F32),
        scratch_shapes=[pltpu.VMEM((2, TOP_K, tm, TOK_ROWS, LANES), BF16),
                        _tok_scratch(tm),
                        pltpu.SemaphoreType.DMA((2,))],
        compiler_params=_params(("arbitrary",)),
        name="combine",
    )(slots, slots, gates, ysh, xn, g2, lg, lb, ys)


def _rope_tables(L, C):
    rows = L // GRID_W
    row = np.repeat(np.arange(rows, dtype=np.float64), GRID_W)
    col = np.tile(np.arange(GRID_W, dtype=np.float64), rows)
    inv_freq = np.exp(-np.log(ROPE_THETA) * np.arange(0, ROPE_AXIS_DIM, 2, dtype=np.float64) / ROPE_AXIS_DIM)
    ang_r = row[:, None] * inv_freq
    ang_c = col[:, None] * inv_freq
    cr, sr, cc, sc = np.cos(ang_r), np.sin(ang_r), np.cos(ang_c), np.sin(ang_c)
    cos_t = np.concatenate([cr, cr, cc, cc], axis=-1)
    sin_t = np.concatenate([-sr, sr, -sc, sc], axis=-1)
    cos_t = np.concatenate([np.ones((C, HEAD_DIM)), cos_t], axis=0)
    sin_t = np.concatenate([np.zeros((C, HEAD_DIM)), sin_t], axis=0)
    return jnp.asarray(cos_t, F32), jnp.asarray(sin_t, F32)


def kernel(x, c, ctx, c_ctx, w_mod, b_mod, w_in, attn_sink, conv_w, conv_b, lru_wa, lru_ba, lru_wx, lru_bx,
           lru_lam, norm_attn_g, norm_lru_g, w_out, ln1_g, ln1_b, router_w, router_bias, exp_w_gate, exp_w_up,
           exp_w_down, sh_w_gate, sh_w_up, sh_w_down, ln2_g, ln2_b):
    B, L, D = x.shape
    C = ctx.shape[1]
    assert w_mod.shape[0] == DEPTH and D == D_MODEL and B + 1 <= SUBLANES
    row2 = lambda a: a.reshape(1, -1)

    cvec = jnp.concatenate([c, c_ctx[None], jnp.zeros((SUBLANES - B - 1, D), F32)], axis=0)
    mod = _mod(cvec, w_mod[0], row2(b_mod[0]))

    cos_t, sin_t = _rope_tables(L, C)
    q, k, v, xr, yg = _inproj(x, ctx, mod, w_in[0].astype(BF16), cos_t, sin_t)

    attn = _attention(attn_sink[0], q, k, v, L, C)

    w_gates = jnp.concatenate([lru_wa[0], lru_wx[0]], axis=-1).astype(BF16)
    bias = jnp.stack([lru_ba[0], lru_bx[0]], axis=1)
    hf, hb = _lru(xr, conv_w[0], row2(conv_b[0]), w_gates, bias, lru_lam[0], L)

    rw_hi, rw_lo = _split_bf16(router_w[0].T)
    tri = jnp.triu(jnp.ones((ROW_TILE, ROW_TILE), BF16))
    eye = jnp.eye(ROW_TILE, dtype=BF16)
    xn, u2, up, eidx, gates, pos, cnt = _merge(
        attn, hf, hb, yg, x, mod, w_out[0].astype(BF16), row2(norm_attn_g[0]), row2(norm_lru_g[0]),
        row2(ln1_g[0]), row2(ln1_b[0]), rw_hi, rw_lo, router_bias[0].reshape(-1, 1), tri, eye, C)

    T = B * L
    n_tiles = T * TOP_K // EXP_TR + N_EXPERTS
    counts = cnt[:, 0].astype(jnp.int32)
    tiles_e = (counts + EXP_TR - 1) // EXP_TR
    tile_end = jnp.cumsum(tiles_e)
    n_used = tile_end[-1:]
    row_start = (tile_end - tiles_e) * EXP_TR
    experts = jnp.arange(N_EXPERTS, dtype=jnp.int32)
    start_of = jnp.sum(jnp.where(eidx[None] == experts[:, None, None], row_start[:, None, None], 0), axis=0)
    slots = (start_of + pos).T.reshape(-1)
    tile_id = jnp.minimum(jnp.arange(n_tiles, dtype=jnp.int32), n_used - 1)
    tile_expert = jnp.sum((tile_end[None, :] <= tile_id[:, None]).astype(jnp.int32), axis=1)

    all_tiles = jnp.arange(n_tiles, dtype=jnp.int32)
    ends_run = jnp.any((tile_end[None, :] - 1 == all_tiles[:, None]) & (tiles_e[None, :] > 0), axis=1)
    zero_tiles = (ends_run | (all_tiles >= n_used)).astype(jnp.int32)
    xs, ysh = _dispatch(zero_tiles, slots, up, u2.reshape(T, D), sh_w_gate[0].astype(BF16),
                        sh_w_up[0].astype(BF16), sh_w_down[0].astype(BF16), n_tiles * EXP_TR)
    has_rows = tiles_e > 0
    run_parity_e = (jnp.cumsum(has_rows) - has_rows) % 2
    later = has_rows[None, :] & (experts[None, :] > experts[:, None])
    next_e = jnp.min(jnp.where(later, experts[None, :], N_EXPERTS), axis=1)
    next_e = jnp.where(next_e == N_EXPERTS, -1, next_e)
    of_tile = tile_expert[:, None] == experts[None, :]
    per_tile = lambda v: jnp.sum(jnp.where(of_tile, v[None, :], 0), axis=1).astype(jnp.int32)
    first_tile = per_tile(tile_end - tiles_e)
    tile_valid = jnp.clip(per_tile(counts) - (tile_id - first_tile) * EXP_TR, 0, EXP_TR)
    ys = _experts(tile_expert, n_used, per_tile(next_e), per_tile(run_parity_e), tile_valid, xs,
                  exp_w_gate[0], exp_w_up[0], exp_w_down[0])
    g2 = mod[:B, 5 * D:6 * D].reshape(B, 1, D)
    out = _combine(slots, gates, ysh, xn.reshape(T, D), g2, row2(ln2_g[0]), row2(ln2_b[0]), ys, L // COMB_TM)
    return out.reshape(B, L, D)
```

```python
import functools

import jax
import jax.numpy as jnp
import numpy as np
from jax import lax
from jax.experimental import pallas as pl
from jax.experimental.pallas import tpu as pltpu

F32 = jnp.float32
BF16 = jnp.bfloat16

D_MODEL = 2048
GRID_W = 64
N_HEADS = 8
N_KV_HEADS = 2
HEAD_DIM = 128
GQA_GROUP = N_HEADS // N_KV_HEADS
ATTN_W = N_HEADS * HEAD_DIM
KV_W = N_KV_HEADS * HEAD_DIM
ATTN_SCALE = HEAD_DIM ** -0.5
BLOCK = 128
ROPE_THETA = 10000.0
ROPE_AXIS_DIM = HEAD_DIM // 2
LRU_W = D_MODEL - ATTN_W
LRU_BLOCKS = 8
LRU_BLOCK_W = LRU_W // LRU_BLOCKS
LRU_C = 8.0
N_EXPERTS = 64
TOP_K = 8
N_GROUPS = 8
GROUP_SIZE = N_EXPERTS // N_GROUPS
TOPK_GROUPS = 4
EXPERT_FF = 512
ROUTED_SCALE = 2.5
LN_EPS = 1e-6
SQRT_GUARD = 1e-30
DEPTH = 1
DEEPNORM_ALPHA = (2 * DEPTH) ** 0.25

SUBLANES = 8
LANES = 128
VMEM_LIMIT = 56 * 1024 * 1024

ROW_TILE = 256
MOD_TN = 1024
MERGE_SPLIT = 2
ATTN_STACK = GQA_GROUP
TOK_ROWS = D_MODEL // LANES
TOK_PITCH = TOK_ROWS + SUBLANES
EXP_TR = 256
TILE_FILL_LEVELS = 4
CAST_CHUNKS = 8
DISP_TM = 1024
COMB_TM = 256
COMB_UNROLL = 8


def _params(sem, vmem=VMEM_LIMIT):
    return pltpu.CompilerParams(dimension_semantics=sem, vmem_limit_bytes=vmem)


def _layer_norm(x):
    mu = jnp.mean(x, axis=-1, keepdims=True)
    xc = x - mu
    var = jnp.mean(xc * xc, axis=-1, keepdims=True)
    return xc * lax.rsqrt(var + LN_EPS)


def _rms_norm(x, g):
    return x * lax.rsqrt(jnp.mean(x * x, axis=-1, keepdims=True) + LN_EPS) * g


def _sigmoid(x):
    return 0.5 * jnp.tanh(0.5 * x) + 0.5


def _split_bf16(x):
    hi = x.astype(BF16)
    lo = (x - hi.astype(F32)).astype(BF16)
    return hi, lo


def _mod_kernel(c_ref, w_ref, b_ref, o_ref):
    cv = c_ref[...]
    s = cv * jax.nn.sigmoid(cv)
    hi, lo = _split_bf16(s)
    lhs = jnp.concatenate([hi, lo], axis=0)
    r = jnp.dot(lhs, w_ref[...].astype(BF16), preferred_element_type=F32)
    o_ref[...] = r[:SUBLANES] + r[SUBLANES:] + b_ref[...]


def _mod(cvec, w_mod, b_mod):
    n = w_mod.shape[1]
    return pl.pallas_call(
        _mod_kernel,
        grid=(n // MOD_TN,),
        in_specs=[
            pl.BlockSpec((SUBLANES, D_MODEL), lambda j: (0, 0)),
            pl.BlockSpec((D_MODEL, MOD_TN), lambda j: (0, j)),
            pl.BlockSpec((1, MOD_TN), lambda j: (0, j)),
        ],
        out_specs=pl.BlockSpec((SUBLANES, MOD_TN), lambda j: (0, j)),
        out_shape=jax.ShapeDtypeStruct((SUBLANES, n), F32),
        compiler_params=_params(("arbitrary",)),
        name="mod",
    )(cvec, w_mod, b_mod)


def _rope(xh, cos, sin, even_block):
    partner = jnp.where(even_block, pltpu.roll(xh, 96, 1), pltpu.roll(xh, 32, 1))
    return xh * cos + partner * sin


def _inproj_kernel(x_ref, ctx_ref, mod_ref, w_ref, cos_ref, sin_ref,
                   q_ref, k_ref, v_ref, xr_ref, yg_ref):
    b = pl.program_id(0)
    t = pl.program_id(1)
    is_ctx = t == 0
    xin = jnp.where(is_ctx, ctx_ref[0], x_ref[0])
    r = jnp.where(is_ctx, 2, b)
    shift = mod_ref[pl.ds(r, 1), 0:D_MODEL]
    scale = mod_ref[pl.ds(r, 1), D_MODEL:2 * D_MODEL]
    u = (_layer_norm(xin) * (1.0 + scale) + shift).astype(BF16)

    cos = cos_ref[...]
    sin = sin_ref[...]
    lane = lax.broadcasted_iota(jnp.int32, (ROW_TILE, HEAD_DIM), 1)
    even_block = (lane % 64) < 32

    def proj(c0, c1):
        return jnp.dot(u, w_ref[:, c0:c1], preferred_element_type=F32)

    head = lambda y, h: y[:, h * HEAD_DIM:(h + 1) * HEAD_DIM]
    for p in range(N_HEADS // 2):
        qq = proj(2 * p * HEAD_DIM, (2 * p + 2) * HEAD_DIM)
        for h in range(2):
            q_ref[0, :, (2 * p + h) * HEAD_DIM:(2 * p + h + 1) * HEAD_DIM] = (
                _rope(head(qq, h), cos, sin, even_block) * ATTN_SCALE).astype(BF16)
    kk = proj(ATTN_W, ATTN_W + KV_W)
    for h in range(N_KV_HEADS):
        k_ref[0, :, h * HEAD_DIM:(h + 1) * HEAD_DIM] = _rope(head(kk, h), cos, sin, even_block).astype(BF16)
    v_ref[0] = proj(ATTN_W + KV_W, ATTN_W + 2 * KV_W).astype(BF16)
    c0 = ATTN_W + 2 * KV_W
    xr_ref[0] = proj(c0, c0 + LRU_W)
    yg_ref[0] = proj(c0 + LRU_W, c0 + 2 * LRU_W)


def _inproj(x, ctx, mod, w_in_bf, cos_t, sin_t):
    B, L, _ = x.shape
    C = ctx.shape[1]
    assert C == ROW_TILE and L % ROW_TILE == 0
    nt = L // ROW_TILE + 1
    rows = L + C
    d_in = w_in_bf.shape[1]
    out = lambda w, dt: jax.ShapeDtypeStruct((B, rows, w), dt)
    ospec = lambda w: pl.BlockSpec((1, ROW_TILE, w), lambda b, t: (b, t, 0))
    return pl.pallas_call(
        _inproj_kernel,
        grid=(B, nt),
        in_specs=[
            pl.BlockSpec((1, ROW_TILE, D_MODEL), lambda b, t: (b, jnp.maximum(t - 1, 0), 0)),
            pl.BlockSpec((1, ROW_TILE, D_MODEL), lambda b, t: (b, 0, 0)),
            pl.BlockSpec((SUBLANES, 2 * D_MODEL), lambda b, t: (0, 0)),
            pl.BlockSpec((D_MODEL, d_in), lambda b, t: (0, 0)),
            pl.BlockSpec((ROW_TILE, HEAD_DIM), lambda b, t: (t, 0)),
            pl.BlockSpec((ROW_TILE, HEAD_DIM), lambda b, t: (t, 0)),
        ],
        out_specs=[ospec(ATTN_W), ospec(KV_W), ospec(KV_W), ospec(LRU_W), ospec(LRU_W)],
        out_shape=[out(ATTN_W, BF16), out(KV_W, BF16), out(KV_W, BF16), out(LRU_W, F32), out(LRU_W, F32)],
        compiler_params=_params(("arbitrary", "arbitrary")),
        name="inproj",
    )(x, ctx, mod, w_in_bf, cos_t, sin_t)


def _attn_kernel(sink_ref, q_ref, kp_ref, kc_ref, kn_ref, vp_ref, vc_ref, vn_ref, kx_ref, vx_ref, o_ref):
    n = pl.program_id(1)
    nb = pl.num_programs(1)
    rows = ATTN_STACK * BLOCK
    qi = lax.broadcasted_iota(jnp.int32, (rows, BLOCK), 0) % BLOCK
    kj = lax.broadcasted_iota(jnp.int32, (rows, BLOCK), 1)
    prev_ok = (kj >= qi) & (n > 0)
    next_ok = (kj <= qi) & (n < nb - 1)
    grp = lax.broadcasted_iota(jnp.int32, (rows, 1), 0) // BLOCK
    neg = -jnp.inf
    for q0 in range(0, N_HEADS, ATTN_STACK):
        h = q0 // GQA_GROUP
        hs = slice(h * HEAD_DIM, (h + 1) * HEAD_DIM)
        q4 = jnp.concatenate(
            [q_ref[0, :, (q0 + g) * HEAD_DIM:(q0 + g + 1) * HEAD_DIM] for g in range(ATTN_STACK)], axis=0)
        nt = (((1,), (1,)), ((), ()))
        sp = jnp.where(prev_ok, lax.dot_general(q4, kp_ref[0, :, hs], nt, preferred_element_type=F32), neg)
        sc = lax.dot_general(q4, kc_ref[0, :, hs], nt, preferred_element_type=F32)
        sn = jnp.where(next_ok, lax.dot_general(q4, kn_ref[0, :, hs], nt, preferred_element_type=F32), neg)
        sx = lax.dot_general(q4, kx_ref[0, :, hs], nt, preferred_element_type=F32)
        sink = jnp.zeros((rows, 1), F32)
        for g in range(ATTN_STACK):
            sink = jnp.where(grp == g, sink_ref[q0 + g], sink)
        lane_chunks = lambda a: [a[:, c * BLOCK:(c + 1) * BLOCK] for c in range(a.shape[1] // BLOCK)]
        fold = lambda op, parts: functools.reduce(op, parts)
        m = jnp.max(fold(jnp.maximum, [sp, sc, sn] + lane_chunks(sx)), -1, keepdims=True)
        m = jnp.maximum(m, sink)
        pp = jnp.exp(sp - m)
        pc = jnp.exp(sc - m)
        pn = jnp.exp(sn - m)
        px = jnp.exp(sx - m)
        denom = jnp.sum(fold(jnp.add, [pp, pc, pn] + lane_chunks(px)), -1, keepdims=True) + jnp.exp(sink - m)
        acc = jnp.dot(pp.astype(BF16), vp_ref[0, :, hs], preferred_element_type=F32)
        acc += jnp.dot(pc.astype(BF16), vc_ref[0, :, hs], preferred_element_type=F32)
        acc += jnp.dot(pn.astype(BF16), vn_ref[0, :, hs], preferred_element_type=F32)
        acc += jnp.dot(px.astype(BF16), vx_ref[0, :, hs], preferred_element_type=F32)
        o = acc / denom
        for g in range(ATTN_STACK):
            c0 = (q0 + g) * HEAD_DIM
            o_ref[0, :, c0:c0 + HEAD_DIM] = o[g * BLOCK:(g + 1) * BLOCK]


def _attention(sink, q, k, v, L, C):
    B = q.shape[0]
    nb = L // BLOCK
    off = C // BLOCK
    cur = lambda b, n: (b, n + off, 0)
    prv = lambda b, n: (b, jnp.maximum(n - 1, 0) + off, 0)
    nxt = lambda b, n: (b, jnp.minimum(n + 1, nb - 1) + off, 0)
    kv = lambda im: pl.BlockSpec((1, BLOCK, KV_W), im)
    cx = pl.BlockSpec((1, C, KV_W), lambda b, n: (b, 0, 0))
    return pl.pallas_call(
        _attn_kernel,
        grid=(B, nb),
        in_specs=[
            pl.BlockSpec(memory_space=pltpu.SMEM),
            pl.BlockSpec((1, BLOCK, ATTN_W), cur),
            kv(prv), kv(cur), kv(nxt), kv(prv), kv(cur), kv(nxt), cx, cx,
        ],
        out_specs=pl.BlockSpec((1, BLOCK, ATTN_W), lambda b, n: (b, n, 0)),
        out_shape=jax.ShapeDtypeStruct((B, L, ATTN_W), F32),
        compiler_params=_params(("arbitrary", "arbitrary")),
        name="attn",
    )(sink, q, k, k, k, v, v, v, k, v)


def _lru_coeffs(x_ref, p_ref, n_ref, tile, d, cw_ref, cb_ref, w_ref, bias_ref, lam_ref,
                ext_scr, a_scr, b_scr, n_lat_tiles):
    tm = ROW_TILE
    prev_ok = tile >= 2
    next_ok = (tile >= 1) & (tile < n_lat_tiles)
    ext_scr[0:SUBLANES] = jnp.where(prev_ok, p_ref[0], 0.0)
    ext_scr[SUBLANES:SUBLANES + tm] = x_ref[0]
    ext_scr[SUBLANES + tm:2 * SUBLANES + tm] = jnp.where(next_ok, n_ref[0], 0.0)
    xc = cb_ref[...] + cw_ref[2:3] * x_ref[0]
    xc += cw_ref[0:1] * ext_scr[SUBLANES - 2:SUBLANES - 2 + tm]
    xc += cw_ref[1:2] * ext_scr[SUBLANES - 1:SUBLANES - 1 + tm]
    xc += cw_ref[3:4] * ext_scr[SUBLANES + 1:SUBLANES + 1 + tm]
    xcb = xc.astype(BF16)
    lam = lam_ref[d:d + 1]
    sp = jnp.maximum(-lam, 0.0) + jnp.log1p(jnp.exp(-jnp.abs(lam)))
    for n in range(LRU_BLOCKS):
        cs = slice(n * LRU_BLOCK_W, (n + 1) * LRU_BLOCK_W)
        z = jnp.dot(xcb[:, cs], w_ref[d, n], preferred_element_type=F32)
        r = _sigmoid(z[:, :LRU_BLOCK_W] + bias_ref[d, 0:1, cs])
        i = _sigmoid(z[:, LRU_BLOCK_W:] + bias_ref[d, 1:2, cs])
        log_a = -LRU_C * r * sp[:, cs]
        a = jnp.exp(log_a)
        a_scr[:, cs] = a
        v = 1.0 - a * a
        b_scr[:, cs] = v * lax.rsqrt(jnp.maximum(v, SQRT_GUARD)) * (i * xc[:, cs])


def _lru_scan(a_scr, b_scr, h_ref, state_ref, reset, reverse):
    groups = ROW_TILE // SUBLANES
    row = lax.broadcasted_iota(jnp.int32, (SUBLANES, LRU_W), 0)
    carry0 = jnp.where(reset, 0.0, state_ref[...])

    def body(g, carry):
        gi = (groups - 1 - g) if reverse else g
        r0 = pl.multiple_of(gi * SUBLANES, SUBLANES)
        A = a_scr[pl.ds(r0, SUBLANES), :]
        Bv = b_scr[pl.ds(r0, SUBLANES), :]
        for s in (1, 2, 4):
            if reverse:
                sh, m = SUBLANES - s, row < SUBLANES - s
            else:
                sh, m = s, row >= s
            A_sh = pltpu.roll(A, sh, 0)
            B_sh = pltpu.roll(Bv, sh, 0)
            Bv = jnp.where(m, A * B_sh + Bv, Bv)
            A = jnp.where(m, A * A_sh, A)
        h = Bv + A * carry
        h_ref[0, pl.ds(r0, SUBLANES), :] = h
        last = h[0:1] if reverse else h[SUBLANES - 1:SUBLANES]
        return jnp.broadcast_to(last, (SUBLANES, LRU_W))

    state_ref[...] = lax.fori_loop(0, groups, body, carry0)


def _lru_kernel(xf_ref, xfp_ref, xfn_ref, xb_ref, xbp_ref, xbn_ref, cw_ref, cb_ref, w_ref, bias_ref, lam_ref,
                hf_ref, hb_ref, sf_ref, sb_ref, ext_scr, a_scr, b_scr, *, n_lat_tiles):
    t = pl.program_id(1)
    reset = t == 0
    bt = jnp.where(t == 0, 0, n_lat_tiles + 1 - t)
    _lru_coeffs(xf_ref, xfp_ref, xfn_ref, t, 0, cw_ref, cb_ref, w_ref, bias_ref, lam_ref,
                ext_scr, a_scr, b_scr, n_lat_tiles)
    _lru_scan(a_scr, b_scr, hf_ref, sf_ref, reset, False)
    _lru_coeffs(xb_ref, xbp_ref, xbn_ref, bt, 1, cw_ref, cb_ref, w_ref, bias_ref, lam_ref,
                ext_scr, a_scr, b_scr, n_lat_tiles)
    _lru_scan(a_scr, b_scr, hb_ref, sb_ref, reset, True)


def _lru(xr, conv_w, conv_b, w_gates, bias, lam, L):
    B, rows, _ = xr.shape
    nl = L // ROW_TILE
    nt = nl + 1
    per = ROW_TILE // SUBLANES
    n8 = rows // SUBLANES
    ft = lambda b, t: t
    btile = lambda b, t: jnp.where(t == 0, 0, nl + 1 - t)
    main = lambda f: pl.BlockSpec((1, ROW_TILE, LRU_W), lambda b, t: (b, f(b, t), 0))
    prev = lambda f: pl.BlockSpec((1, SUBLANES, LRU_W), lambda b, t: (b, jnp.maximum(f(b, t) * per - 1, 0), 0))
    nxt = lambda f: pl.BlockSpec((1, SUBLANES, LRU_W),
                                 lambda b, t: (b, jnp.minimum((f(b, t) + 1) * per, n8 - 1), 0))
    full = lambda a: pl.BlockSpec(a.shape, lambda b, t: (0,) * a.ndim)
    return pl.pallas_call(
        functools.partial(_lru_kernel, n_lat_tiles=nl),
        grid=(B, nt),
        in_specs=[main(ft), prev(ft), nxt(ft), main(btile), prev(btile), nxt(btile),
                  full(conv_w), full(conv_b), full(w_gates), full(bias), full(lam)],
        out_specs=[
            pl.BlockSpec((1, ROW_TILE, LRU_W), lambda b, t: (b, jnp.maximum(t - 1, 0), 0)),
            pl.BlockSpec((1, ROW_TILE, LRU_W), lambda b, t: (b, nl - jnp.maximum(t, 1), 0)),
        ],
        out_shape=[jax.ShapeDtypeStruct((B, L, LRU_W), F32)] * 2,
        scratch_shapes=[
            pltpu.VMEM((SUBLANES, LRU_W), F32), pltpu.VMEM((SUBLANES, LRU_W), F32),
            pltpu.VMEM((ROW_TILE + 2 * SUBLANES, LRU_W), F32),
            pltpu.VMEM((ROW_TILE, LRU_W), F32), pltpu.VMEM((ROW_TILE, LRU_W), F32),
        ],
        compiler_params=_params(("arbitrary", "arbitrary")),
        name="lru",
    )(xr, xr, xr, xr, xr, xr, conv_w, conv_b, w_gates, bias, lam)


def _route(scores, sel):
    tm = scores.shape[1]
    neg = -jnp.inf
    iota_g = lax.broadcasted_iota(jnp.int32, (GROUP_SIZE, tm), 0)
    grp_score = []
    for g in range(N_GROUPS):
        sg = sel[g * GROUP_SIZE:(g + 1) * GROUP_SIZE]
        m1 = jnp.max(sg, axis=0, keepdims=True)
        first = jnp.min(jnp.where(sg == m1, iota_g, GROUP_SIZE), axis=0, keepdims=True)
        m2 = jnp.max(jnp.where(iota_g == first, neg, sg), axis=0, keepdims=True)
        grp_score.append(m1 + m2)
    masked = []
    for g in range(N_GROUPS):
        rank = jnp.zeros((1, tm), jnp.int32)
        for o in range(N_GROUPS):
            if o == g:
                continue
            ahead = (grp_score[o] > grp_score[g]) if o > g else (grp_score[o] >= grp_score[g])
            rank += ahead.astype(jnp.int32)
        keep = rank < TOPK_GROUPS
        masked.append(jnp.where(keep, sel[g * GROUP_SIZE:(g + 1) * GROUP_SIZE], neg))
    cand = jnp.concatenate(masked, axis=0)
    iota_e = lax.broadcasted_iota(jnp.int32, (N_EXPERTS, tm), 0)
    chosen = jnp.zeros((N_EXPERTS, tm), jnp.bool_)
    picks = []
    for _ in range(TOP_K):
        cur = jnp.where(chosen, neg, cand)
        m = jnp.max(cur, axis=0, keepdims=True)
        idx = jnp.min(jnp.where((cur == m) & jnp.logical_not(chosen), iota_e, N_EXPERTS), axis=0, keepdims=True)
        chosen = chosen | (iota_e == idx)
        picks.append(idx)
    w = jnp.where(chosen, scores, 0.0)
    gates = w / jnp.sum(w, axis=0, keepdims=True) * ROUTED_SCALE
    return gates, chosen, picks


def _tok_scratch(rows):
    return pltpu.VMEM((rows, TOK_PITCH, LANES), F32)


def _to_token_major(y, scr):
    rows = scr.shape[0]
    flat = scr.reshape(rows * TOK_PITCH, LANES)
    for s in range(TOK_ROWS):
        flat[pl.ds(s, rows, stride=TOK_PITCH), :] = y[:, s * LANES:(s + 1) * LANES]


def _from_token_major(scr):
    rows = scr.shape[0]
    flat = scr.reshape(rows * TOK_PITCH, LANES)
    return [flat[pl.ds(s, rows, stride=TOK_PITCH), :] for s in range(TOK_ROWS)]


def _token_tiles(scr):
    return scr.at[:, 0:TOK_ROWS, :]


def _merge_kernel(attn_ref, hf_ref, hb_ref, yg_ref, x_ref, mod_ref, wout_ref, ga_ref, gl_ref, lg_ref, lb_ref,
                  rwh_ref, rwl_ref, rb_ref, tri_ref, eye_ref, xn_ref, u2_ref, up_ref, eidx_ref, gates_ref, pos_ref,
                  cnt_ref, carry_ref, tok_scr):
    b = pl.program_id(0)

    @pl.when((b == 0) & (pl.program_id(1) == 0))
    def _():
        carry_ref[...] = jnp.zeros_like(carry_ref)

    D = D_MODEL
    mrow = lambda i: mod_ref[pl.ds(b, 1), i * D:(i + 1) * D]
    nt = (((1,), (1,)), ((), ()))
    u2_parts, logit_parts = [], []
    sub = ROW_TILE // MERGE_SPLIT
    for p in range(MERGE_SPLIT):
        rs = slice(p * sub, (p + 1) * sub)
        lru_y = (hf_ref[0, rs] + hb_ref[0, rs]) * jax.nn.gelu(yg_ref[0, rs])
        na = _rms_norm(attn_ref[0, rs], ga_ref[...]).astype(BF16)
        nl = _rms_norm(lru_y, gl_ref[...]).astype(BF16)
        y1 = jnp.dot(na, wout_ref[0:ATTN_W], preferred_element_type=F32)
        y1 += jnp.dot(nl, wout_ref[ATTN_W:D], preferred_element_type=F32)
        xn = _layer_norm(DEEPNORM_ALPHA * x_ref[0, rs] + mrow(2) * y1) * lg_ref[...] + lb_ref[...]
        xn_ref[0, rs] = xn
        u2_p = _layer_norm(xn) * (1.0 + mrow(4)) + mrow(3)
        u_hi, u_lo = _split_bf16(u2_p)
        u2_ref[0, rs] = u_hi
        lg_p = lax.dot_general(rwh_ref[...], u_hi, nt, preferred_element_type=F32)
        lg_p += lax.dot_general(rwh_ref[...], u_lo, nt, preferred_element_type=F32)
        lg_p += lax.dot_general(rwl_ref[...], u_hi, nt, preferred_element_type=F32)
        u2_parts.append(u2_p)
        logit_parts.append(lg_p)
    u2 = jnp.concatenate(u2_parts, axis=0)
    _to_token_major(u2, tok_scr)
    up_ref[...] = _token_tiles(tok_scr)[...].astype(BF16)
    scores = jax.nn.sigmoid(jnp.concatenate(logit_parts, axis=1))
    gates, chosen, picks = _route(scores, scores + rb_ref[...])
    sel01 = jnp.where(chosen, 1.0, 0.0)
    incl = jnp.dot(sel01.astype(BF16), tri_ref[...], preferred_element_type=F32)
    rank = carry_ref[:, 0:1] + incl - sel01
    carry_ref[...] = carry_ref[...] + incl[:, ROW_TILE - 1:ROW_TILE]
    cnt_ref[...] = carry_ref[...]
    iota_e = lax.broadcasted_iota(jnp.int32, (N_EXPERTS, ROW_TILE), 0)
    gate_rows = []
    for kk, idx in enumerate(picks):
        hit = iota_e == idx
        eidx_ref[kk:kk + 1, :] = idx
        gate_rows.append(jnp.sum(jnp.where(hit, gates, 0.0), axis=0, keepdims=True))
        pos_ref[kk:kk + 1, :] = jnp.sum(jnp.where(hit, rank, 0.0), axis=0, keepdims=True).astype(jnp.int32)

    piece = jnp.concatenate(gate_rows, axis=0)
    gate_cols = jnp.zeros((ROW_TILE, TOP_K), F32)
    for _ in range(3):
        part = piece.astype(BF16)
        gate_cols += lax.dot_general(eye_ref[...], part, nt, preferred_element_type=F32)
        piece = piece - part.astype(F32)
    for kk in range(TOP_K):
        gates_ref[pl.ds(kk, ROW_TILE, stride=TOP_K), :] = jnp.broadcast_to(gate_cols[:, kk:kk + 1],
                                                                           (ROW_TILE, LANES))


def _merge(attn, hf, hb, yg, x, mod, w_out_bf, ga, gl, lg, lb, rw_hi, rw_lo, rbias, tri, eye, C):
    B, L, _ = x.shape
    T = B * L
    nt = L // ROW_TILE
    off = C // ROW_TILE
    row = lambda w, o=0: pl.BlockSpec((1, ROW_TILE, w), lambda b, t: (b, t + o, 0))
    full = lambda a: pl.BlockSpec(a.shape, lambda b, t: (0,) * a.ndim)
    tok = pl.BlockSpec((TOP_K, ROW_TILE), lambda b, t: (0, b * nt + t))
    return pl.pallas_call(
        _merge_kernel,
        grid=(B, nt),
        in_specs=[row(ATTN_W), row(LRU_W), row(LRU_W), row(LRU_W, off), row(D_MODEL),
                  full(mod), full(w_out_bf), full(ga), full(gl), full(lg), full(lb),
                  full(rw_hi), full(rw_lo), full(rbias), full(tri), full(eye)],
        out_specs=[row(D_MODEL), row(D_MODEL),
                   pl.BlockSpec((ROW_TILE, TOK_ROWS, LANES), lambda b, t: (b * nt + t, 0, 0)),
                   tok,
                   pl.BlockSpec((ROW_TILE * TOP_K, LANES), lambda b, t: (b * nt + t, 0)),
                   tok,
                   pl.BlockSpec((N_EXPERTS, LANES), lambda b, t: (0, 0))],
        out_shape=[jax.ShapeDtypeStruct((B, L, D_MODEL), F32),
                   jax.ShapeDtypeStruct((B, L, D_MODEL), BF16),
                   jax.ShapeDtypeStruct((T, TOK_ROWS, LANES), BF16),
                   jax.ShapeDtypeStruct((TOP_K, T), jnp.int32),
                   jax.ShapeDtypeStruct((T * TOP_K, LANES), F32),
                   jax.ShapeDtypeStruct((TOP_K, T), jnp.int32),
                   jax.ShapeDtypeStruct((N_EXPERTS, LANES), F32)],
        scratch_shapes=[pltpu.VMEM((N_EXPERTS, LANES), F32), _tok_scratch(ROW_TILE)],
        compiler_params=_params(("arbitrary", "arbitrary")),
        name="merge",
    )(attn, hf, hb, yg, x, mod, w_out_bf, ga, gl, lg, lb, rw_hi, rw_lo, rbias, tri, eye)


def _swiglu(u, wg, wu):
    hg = jnp.dot(u, wg, preferred_element_type=F32)
    hu = jnp.dot(u, wu, preferred_element_type=F32)
    return hg * jax.nn.sigmoid(hg) * hu


def _dispatch_kernel(zt_ref, dest_ref, up_ref, u_ref, sg_ref, su_ref, sd_ref, xs_ref, ysh_ref, zero_scr, sem, zsem):
    @pl.when(pl.program_id(0) == 0)
    def _():
        zero_scr[...] = jnp.zeros_like(zero_scr)
        n_tiles = zt_ref.shape[0]

        def zero_tile(z):
            return pltpu.make_async_copy(zero_scr, xs_ref.at[pl.ds(z * EXP_TR, EXP_TR)], zsem)

        def start(z, carry):
            @pl.when(zt_ref[z] != 0)
            def _():
                zero_tile(z).start()
            return carry

        def wait(z, carry):
            @pl.when(zt_ref[z] != 0)
            def _():
                zero_tile(z).wait()
            return carry

        lax.fori_loop(0, n_tiles, start, 0)
        lax.fori_loop(0, n_tiles, wait, 0)

    def issue(j, carry):
        for k in range(TOP_K):
            pltpu.make_async_copy(up_ref.at[j], xs_ref.at[dest_ref[j * TOP_K + k]], sem).start(priority=k % 2)
        return carry

    lax.fori_loop(0, DISP_TM, issue, 0)

    hs = _swiglu(u_ref[...], sg_ref[...], su_ref[...])
    ysh_ref[...] = jnp.dot(hs.astype(BF16), sd_ref[...], preferred_element_type=F32).astype(BF16)

    for k in range(TOP_K):
        pltpu.make_async_copy(up_ref, xs_ref.at[pl.ds(0, DISP_TM)], sem).wait()


def _dispatch(zero_tiles, slots, up, u2, sg, su, sd, n_sorted_rows):
    T = up.shape[0]
    full = lambda a: pl.BlockSpec(a.shape, lambda i, zt: (0,) * a.ndim)
    rows = pl.BlockSpec((DISP_TM, D_MODEL), lambda i, zt: (i, 0))
    return pl.pallas_call(
        _dispatch_kernel,
        grid_spec=pltpu.PrefetchScalarGridSpec(
            num_scalar_prefetch=1,
            grid=(T // DISP_TM,),
            in_specs=[pl.BlockSpec((DISP_TM * TOP_K,), lambda i, zt: (i,), memory_space=pltpu.SMEM),
                      pl.BlockSpec((DISP_TM, TOK_ROWS, LANES), lambda i, zt: (i, 0, 0)),
                      rows, full(sg), full(su), full(sd)],
            out_specs=[pl.BlockSpec(memory_space=pl.ANY), rows],
            scratch_shapes=[pltpu.VMEM((EXP_TR, TOK_ROWS, LANES), BF16),
                            pltpu.SemaphoreType.DMA(()), pltpu.SemaphoreType.DMA(())],
        ),
        out_shape=[jax.ShapeDtypeStruct((n_sorted_rows, TOK_ROWS, LANES), BF16),
                   jax.ShapeDtypeStruct((T, D_MODEL), BF16)],
        compiler_params=_params(("arbitrary",)),
        name="dispatch",
    )(zero_tiles, slots, up, u2, sg, su, sd)


def _expert_kernel(te_ref, nu_ref, nxt_ref, par_ref, val_ref, xs_ref, wg_hbm, wu_hbm, wd_hbm, ys_ref,
                   wg32, wu32, wd32, wgb_ref, wub_ref, wdb_ref, tok_scr, wsem):
    i = pl.program_id(0)
    n_used = nu_ref[0]

    def weight_copies(e, s):
        return [pltpu.make_async_copy(hbm.at[e], stage.at[s], wsem.at[s])
                for hbm, stage in ((wg_hbm, wg32), (wu_hbm, wu32), (wd_hbm, wd32))]

    @pl.when(i == 0)
    def _():
        for cp in weight_copies(te_ref[0], 0):
            cp.start(priority=1)

    @pl.when(i < n_used)
    def _():
        e = te_ref[i]

        first_of_expert = (i == 0) | (e != te_ref[jnp.maximum(i - 1, 0)])
        for s in range(2):
            @pl.when(first_of_expert & (par_ref[i] == s))
            def _():
                for cp in weight_copies(e, s):
                    cp.wait()
                for src, dst in ((wg32, wgb_ref), (wu32, wub_ref), (wd32, wdb_ref)):
                    rows = dst.shape[0] // CAST_CHUNKS

                    def cast_chunk(r, carry, src=src, dst=dst, rows=rows):
                        sl = pl.ds(pl.multiple_of(r * rows, rows), rows)
                        dst[sl, :] = src[s, sl, :].astype(BF16)
                        return carry

                    lax.fori_loop(0, CAST_CHUNKS, cast_chunk, 0)

                @pl.when(nxt_ref[i] >= 0)
                def _():
                    for cp in weight_copies(nxt_ref[i], 1 - s):
                        cp.start(priority=1)

        def tile_body(rows):
            scr = tok_scr.at[0:rows]
            _token_tiles(scr)[...] = xs_ref[0:rows].astype(F32)
            x = jnp.concatenate([c.astype(BF16) for c in _from_token_major(scr)], axis=1)
            h = _swiglu(x, wgb_ref[...], wub_ref[...])
            y = jnp.dot(h.astype(BF16), wdb_ref[...], preferred_element_type=F32)
            _to_token_major(y, scr)
            ys_ref[0:rows] = _token_tiles(scr)[...].astype(BF16)
            if rows < EXP_TR:
                ys_ref[rows:EXP_TR] = jnp.zeros((EXP_TR - rows, TOK_ROWS, LANES), BF16)

        lo = 0
        for rows in range(EXP_TR // TILE_FILL_LEVELS, EXP_TR + 1, EXP_TR // TILE_FILL_LEVELS):
            @pl.when((val_ref[i] > lo) & (val_ref[i] <= rows))
            def _(rows=rows):
                tile_body(rows)
            lo = rows

    @pl.when(i >= n_used)
    def _():
        ys_ref[...] = jnp.zeros_like(ys_ref)


def _experts(tile_expert, n_used, next_expert, run_parity, tile_valid, xs, wg, wu, wd):
    n_tiles = xs.shape[0] // EXP_TR
    blk = (EXP_TR, TOK_ROWS, LANES)
    stage = lambda a: pltpu.VMEM((2,) + a.shape[1:], F32)
    cast = lambda a: pltpu.VMEM(a.shape[1:], BF16)
    hbm = pl.BlockSpec(memory_space=pl.ANY)
    return pl.pallas_call(
        _expert_kernel,
        grid_spec=pltpu.PrefetchScalarGridSpec(
            num_scalar_prefetch=5,
            grid=(n_tiles,),
            in_specs=[pl.BlockSpec(blk, lambda i, te, nu, nx, pr, vl: (jnp.minimum(i, nu[0] - 1), 0, 0)),
                      hbm, hbm, hbm],
            out_specs=pl.BlockSpec(blk, lambda i, te, nu, nx, pr, vl: (i, 0, 0)),
            scratch_shapes=[stage(wg), stage(wu), stage(wd), cast(wg), cast(wu), cast(wd),
                            _tok_scratch(EXP_TR), pltpu.SemaphoreType.DMA((2,))],
        ),
        out_shape=jax.ShapeDtypeStruct(xs.shape, BF16),
        compiler_params=_params(("arbitrary",)),
        name="experts",
    )(tile_expert, n_used, next_expert, run_parity, tile_valid, xs, wg, wu, wd)


def _combine_kernel(dcur_ref, dnext_ref, wv_ref, ysh_ref, xn_ref, g2_ref, lg_ref, lb_ref,
                    ys_ref, o_ref, buf_ref, y_scr, sem):
    i = pl.program_id(0)
    n = pl.num_programs(0)
    tm = COMB_TM
    slot = i % 2

    def issue(d_ref, sl, j):
        for k in range(TOP_K):
            pltpu.make_async_copy(ys_ref.at[d_ref[j * TOP_K + k]], buf_ref.at[sl, k, j],
                                  sem.at[sl]).start(priority=k % 2)

    @pl.when(i == 0)
    def _():
        def first(j, carry):
            issue(dcur_ref, 0, j)
            return carry
        lax.fori_loop(0, tm, first, 0)

    for k in range(TOP_K):
        pltpu.make_async_copy(ys_ref.at[pl.ds(0, tm)], buf_ref.at[slot, k], sem.at[slot]).wait()

    half = TOK_ROWS // 2

    def accumulate(j):
        lo = jnp.zeros((half, LANES), F32)
        hi = jnp.zeros((half, LANES), F32)
        for k in range(TOP_K):
            gate = wv_ref[pl.ds(j * TOP_K + k, half, stride=0), :]
            rows = buf_ref[slot, k, j].astype(F32)
            lo += gate * rows[:half]
            hi += gate * rows[half:]
        y_scr[j, 0:half, :] = lo
        y_scr[j, half:TOK_ROWS, :] = hi

    @pl.when(i + 1 < n)
    def _():
        def both(jo, carry):
            for u in range(COMB_UNROLL):
                j = jo * COMB_UNROLL + u
                issue(dnext_ref, 1 - slot, j)
                accumulate(j)
            return carry
        lax.fori_loop(0, tm // COMB_UNROLL, both, 0)

    @pl.when(i + 1 == n)
    def _():
        def last(jo, carry):
            for u in range(COMB_UNROLL):
                accumulate(jo * COMB_UNROLL + u)
            return carry
        lax.fori_loop(0, tm // COMB_UNROLL, last, 0)
    y_routed = jnp.concatenate(_from_token_major(y_scr), axis=1)

    y2 = ysh_ref[...].astype(F32) + y_routed
    z = DEEPNORM_ALPHA * xn_ref[...] + g2_ref[0] * y2
    o_ref[...] = _layer_norm(z) * lg_ref[...] + lb_ref[...]


def _combine(slots, gates, ysh, xn, g2, lg, lb, ys, tiles_per_batch):
    T = ysh.shape[0]
    tm = COMB_TM
    n = T // tm
    row = pl.BlockSpec((tm, D_MODEL), lambda i: (i, 0))
    full = lambda a: pl.BlockSpec(a.shape, lambda i: (0,) * a.ndim)
    tok = lambda f: pl.BlockSpec((tm * TOP_K,), lambda i: (f(i),), memory_space=pltpu.SMEM)
    return pl.pallas_call(
        _combine_kernel,
        grid=(n,),
        in_specs=[tok(lambda i: i), tok(lambda i: jnp.minimum(i + 1, n - 1)),
                  pl.BlockSpec((tm * TOP_K, LANES), lambda i: (i, 0)), row, row,
                  pl.BlockSpec((1, 1, D_MODEL), lambda i: (i // tiles_per_batch, 0, 0)),
                  full(lg), full(lb),
                  pl.BlockSpec(memory_space=pl.ANY)],
        out_specs=row,
        out_shape=jax.ShapeDtypeStruct((T, D_MODEL), F32),
        scratch_shapes=[pltpu.VMEM((2, TOP_K, tm, TOK_ROWS, LANES), BF16),
                        _tok_scratch(tm),
                        pltpu.SemaphoreType.DMA((2,))],
        compiler_params=_params(("arbitrary",)),
        name="combine",
    )(slots, slots, gates, ysh, xn, g2, lg, lb, ys)


def _rope_tables(L, C):
    rows = L // GRID_W
    row = np.repeat(np.arange(rows, dtype=np.float64), GRID_W)
    col = np.tile(np.arange(GRID_W, dtype=np.float64), rows)
    inv_freq = np.exp(-np.log(ROPE_THETA) * np.arange(0, ROPE_AXIS_DIM, 2, dtype=np.float64) / ROPE_AXIS_DIM)
    ang_r = row[:, None] * inv_freq
    ang_c = col[:, None] * inv_freq
    cr, sr, cc, sc = np.cos(ang_r), np.sin(ang_r), np.cos(ang_c), np.sin(ang_c)
    cos_t = np.concatenate([cr, cr, cc, cc], axis=-1)
    sin_t = np.concatenate([-sr, sr, -sc, sc], axis=-1)
    cos_t = np.concatenate([np.ones((C, HEAD_DIM)), cos_t], axis=0)
    sin_t = np.concatenate([np.zeros((C, HEAD_DIM)), sin_t], axis=0)
    return jnp.asarray(cos_t, F32), jnp.asarray(sin_t, F32)


def kernel(x, c, ctx, c_ctx, w_mod, b_mod, w_in, attn_sink, conv_w, conv_b, lru_wa, lru_ba, lru_wx, lru_bx,
           lru_lam, norm_attn_g, norm_lru_g, w_out, ln1_g, ln1_b, router_w, router_bias, exp_w_gate, exp_w_up,
           exp_w_down, sh_w_gate, sh_w_up, sh_w_down, ln2_g, ln2_b):
    B, L, D = x.shape
    C = ctx.shape[1]
    assert w_mod.shape[0] == DEPTH and D == D_MODEL and B + 1 <= SUBLANES
    row2 = lambda a: a.reshape(1, -1)

    cvec = jnp.concatenate([c, c_ctx[None], jnp.zeros((SUBLANES - B - 1, D), F32)], axis=0)
    mod = _mod(cvec, w_mod[0], row2(b_mod[0]))

    cos_t, sin_t = _rope_tables(L, C)
    q, k, v, xr, yg = _inproj(x, ctx, mod, w_in[0].astype(BF16), cos_t, sin_t)

    attn = _attention(attn_sink[0], q, k, v, L, C)

    w_gates = jnp.concatenate([lru_wa[0], lru_wx[0]], axis=-1).astype(BF16)
    bias = jnp.stack([lru_ba[0], lru_bx[0]], axis=1)
    hf, hb = _lru(xr, conv_w[0], row2(conv_b[0]), w_gates, bias, lru_lam[0], L)

    rw_hi, rw_lo = _split_bf16(router_w[0].T)
    tri = jnp.triu(jnp.ones((ROW_TILE, ROW_TILE), BF16))
    eye = jnp.eye(ROW_TILE, dtype=BF16)
    xn, u2, up, eidx, gates, pos, cnt = _merge(
        attn, hf, hb, yg, x, mod, w_out[0].astype(BF16), row2(norm_attn_g[0]), row2(norm_lru_g[0]),
        row2(ln1_g[0]), row2(ln1_b[0]), rw_hi, rw_lo, router_bias[0].reshape(-1, 1), tri, eye, C)

    T = B * L
    n_tiles = T * TOP_K // EXP_TR + N_EXPERTS
    counts = cnt[:, 0].astype(jnp.int32)
    tiles_e = (counts + EXP_TR - 1) // EXP_TR
    tile_end = jnp.cumsum(tiles_e)
    n_used = tile_end[-1:]
    row_start = (tile_end - tiles_e) * EXP_TR
    experts = jnp.arange(N_EXPERTS, dtype=jnp.int32)
    start_of = jnp.sum(jnp.where(eidx[None] == experts[:, None, None], row_start[:, None, None], 0), axis=0)
    slots = (start_of + pos).T.reshape(-1)
    tile_id = jnp.minimum(jnp.arange(n_tiles, dtype=jnp.int32), n_used - 1)
    tile_expert = jnp.sum((tile_end[None, :] <= tile_id[:, None]).astype(jnp.int32), axis=1)

    all_tiles = jnp.arange(n_tiles, dtype=jnp.int32)
    ends_run = jnp.any((tile_end[None, :] - 1 == all_tiles[:, None]) & (tiles_e[None, :] > 0), axis=1)
    zero_tiles = (ends_run | (all_tiles >= n_used)).astype(jnp.int32)
    xs, ysh = _dispatch(zero_tiles, slots, up, u2.reshape(T, D), sh_w_gate[0].astype(BF16),
                        sh_w_up[0].astype(BF16), sh_w_down[0].astype(BF16), n_tiles * EXP_TR)
    has_rows = tiles_e > 0
    run_parity_e = (jnp.cumsum(has_rows) - has_rows) % 2
    later = has_rows[None, :] & (experts[None, :] > experts[:, None])
    next_e = jnp.min(jnp.where(later, experts[None, :], N_EXPERTS), axis=1)
    next_e = jnp.where(next_e == N_EXPERTS, -1, next_e)
    of_tile = tile_expert[:, None] == experts[None, :]
    per_tile = lambda v: jnp.sum(jnp.where(of_tile, v[None, :], 0), axis=1).astype(jnp.int32)
    first_tile = per_tile(tile_end - tiles_e)
    tile_valid = jnp.clip(per_tile(counts) - (tile_id - first_tile) * EXP_TR, 0, EXP_TR)
    ys = _experts(tile_expert, n_used, per_tile(next_e), per_tile(run_parity_e), tile_valid, xs,
                  exp_w_gate[0], exp_w_up[0], exp_w_down[0])
    g2 = mod[:B, 5 * D:6 * D].reshape(B, 1, D)
    out = _combine(slots, gates, ysh, xn.reshape(T, D), g2, row2(ln2_g[0]), row2(ln2_b[0]), ys, L // COMB_TM)
    return out.reshape(B, L, D)
```

```python
import functools

import jax
import jax.numpy as jnp
import numpy as np
from jax import lax
from jax.experimental import pallas as pl
from jax.experimental.pallas import tpu as pltpu

F32 = jnp.float32
BF16 = jnp.bfloat16

D_MODEL = 2048
GRID_W = 64
N_HEADS = 8
N_KV_HEADS = 2
HEAD_DIM = 128
GQA_GROUP = N_HEADS // N_KV_HEADS
ATTN_W = N_HEADS * HEAD_DIM
KV_W = N_KV_HEADS * HEAD_DIM
ATTN_SCALE = HEAD_DIM ** -0.5
BLOCK = 128
ROPE_THETA = 10000.0
ROPE_AXIS_DIM = HEAD_DIM // 2
LRU_W = D_MODEL - ATTN_W
LRU_BLOCKS = 8
LRU_BLOCK_W = LRU_W // LRU_BLOCKS
LRU_C = 8.0
N_EXPERTS = 64
TOP_K = 8
N_GROUPS = 8
GROUP_SIZE = N_EXPERTS // N_GROUPS
TOPK_GROUPS = 4
EXPERT_FF = 512
ROUTED_SCALE = 2.5
LN_EPS = 1e-6
SQRT_GUARD = 1e-30
DEPTH = 1
DEEPNORM_ALPHA = (2 * DEPTH) ** 0.25

SUBLANES = 8
LANES = 128
VMEM_LIMIT = 56 * 1024 * 1024

ROW_TILE = 256
MOD_TN = 1024
MERGE_SPLIT = 2
ATTN_STACK = GQA_GROUP
TOK_ROWS = D_MODEL // LANES
TOK_PITCH = TOK_ROWS + SUBLANES
EXP_TR = 256
CAST_CHUNKS = 16
DISP_TM = 1024
COMB_TM = 256
COMB_UNROLL = 8


def _params(sem, vmem=VMEM_LIMIT):
    return pltpu.CompilerParams(dimension_semantics=sem, vmem_limit_bytes=vmem)


def _layer_norm(x):
    mu = jnp.mean(x, axis=-1, keepdims=True)
    xc = x - mu
    var = jnp.mean(xc * xc, axis=-1, keepdims=True)
    return xc * lax.rsqrt(var + LN_EPS)


def _rms_norm(x, g):
    return x * lax.rsqrt(jnp.mean(x * x, axis=-1, keepdims=True) + LN_EPS) * g


def _sigmoid(x):
    return 0.5 * jnp.tanh(0.5 * x) + 0.5


def _split_bf16(x):
    hi = x.astype(BF16)
    lo = (x - hi.astype(F32)).astype(BF16)
    return hi, lo


def _mod_kernel(c_ref, w_ref, b_ref, o_ref):
    cv = c_ref[...]
    s = cv * jax.nn.sigmoid(cv)
    hi, lo = _split_bf16(s)
    lhs = jnp.concatenate([hi, lo], axis=0)
    r = jnp.dot(lhs, w_ref[...].astype(BF16), preferred_element_type=F32)
    o_ref[...] = r[:SUBLANES] + r[SUBLANES:] + b_ref[...]


def _mod(cvec, w_mod, b_mod):
    n = w_mod.shape[1]
    return pl.pallas_call(
        _mod_kernel,
        grid=(n // MOD_TN,),
        in_specs=[
            pl.BlockSpec((SUBLANES, D_MODEL), lambda j: (0, 0)),
            pl.BlockSpec((D_MODEL, MOD_TN), lambda j: (0, j)),
            pl.BlockSpec((1, MOD_TN), lambda j: (0, j)),
        ],
        out_specs=pl.BlockSpec((SUBLANES, MOD_TN), lambda j: (0, j)),
        out_shape=jax.ShapeDtypeStruct((SUBLANES, n), F32),
        compiler_params=_params(("arbitrary",)),
        name="mod",
    )(cvec, w_mod, b_mod)


def _rope(xh, cos, sin, even_block):
    partner = jnp.where(even_block, pltpu.roll(xh, 96, 1), pltpu.roll(xh, 32, 1))
    return xh * cos + partner * sin


def _inproj_kernel(x_ref, ctx_ref, mod_ref, w_ref, cos_ref, sin_ref,
                   q_ref, k_ref, v_ref, xr_ref, yg_ref):
    b = pl.program_id(0)
    t = pl.program_id(1)
    is_ctx = t == 0
    xin = jnp.where(is_ctx, ctx_ref[0], x_ref[0])
    r = jnp.where(is_ctx, 2, b)
    shift = mod_ref[pl.ds(r, 1), 0:D_MODEL]
    scale = mod_ref[pl.ds(r, 1), D_MODEL:2 * D_MODEL]
    u = (_layer_norm(xin) * (1.0 + scale) + shift).astype(BF16)

    cos = cos_ref[...]
    sin = sin_ref[...]
    lane = lax.broadcasted_iota(jnp.int32, (ROW_TILE, HEAD_DIM), 1)
    even_block = (lane % 64) < 32

    def proj(c0, c1):
        return jnp.dot(u, w_ref[:, c0:c1], preferred_element_type=F32)

    head = lambda y, h: y[:, h * HEAD_DIM:(h + 1) * HEAD_DIM]
    for p in range(N_HEADS // 2):
        qq = proj(2 * p * HEAD_DIM, (2 * p + 2) * HEAD_DIM)
        for h in range(2):
            q_ref[0, :, (2 * p + h) * HEAD_DIM:(2 * p + h + 1) * HEAD_DIM] = (
                _rope(head(qq, h), cos, sin, even_block) * ATTN_SCALE).astype(BF16)
    kk = proj(ATTN_W, ATTN_W + KV_W)
    for h in range(N_KV_HEADS):
        k_ref[0, :, h * HEAD_DIM:(h + 1) * HEAD_DIM] = _rope(head(kk, h), cos, sin, even_block).astype(BF16)
    v_ref[0] = proj(ATTN_W + KV_W, ATTN_W + 2 * KV_W).astype(BF16)
    c0 = ATTN_W + 2 * KV_W
    xr_ref[0] = proj(c0, c0 + LRU_W)
    yg_ref[0] = proj(c0 + LRU_W, c0 + 2 * LRU_W)


def _inproj(x, ctx, mod, w_in_bf, cos_t, sin_t):
    B, L, _ = x.shape
    C = ctx.shape[1]
    assert C == ROW_TILE and L % ROW_TILE == 0
    nt = L // ROW_TILE + 1
    rows = L + C
    d_in = w_in_bf.shape[1]
    out = lambda w, dt: jax.ShapeDtypeStruct((B, rows, w), dt)
    ospec = lambda w: pl.BlockSpec((1, ROW_TILE, w), lambda b, t: (b, t, 0))
    return pl.pallas_call(
        _inproj_kernel,
        grid=(B, nt),
        in_specs=[
            pl.BlockSpec((1, ROW_TILE, D_MODEL), lambda b, t: (b, jnp.maximum(t - 1, 0), 0)),
            pl.BlockSpec((1, ROW_TILE, D_MODEL), lambda b, t: (b, 0, 0)),
            pl.BlockSpec((SUBLANES, 2 * D_MODEL), lambda b, t: (0, 0)),
            pl.BlockSpec((D_MODEL, d_in), lambda b, t: (0, 0)),
            pl.BlockSpec((ROW_TILE, HEAD_DIM), lambda b, t: (t, 0)),
            pl.BlockSpec((ROW_TILE, HEAD_DIM), lambda b, t: (t, 0)),
        ],
        out_specs=[ospec(ATTN_W), ospec(KV_W), ospec(KV_W), ospec(LRU_W), ospec(LRU_W)],
        out_shape=[out(ATTN_W, BF16), out(KV_W, BF16), out(KV_W, BF16), out(LRU_W, F32), out(LRU_W, F32)],
        compiler_params=_params(("arbitrary", "arbitrary")),
        name="inproj",
    )(x, ctx, mod, w_in_bf, cos_t, sin_t)


def _attn_kernel(sink_ref, q_ref, kp_ref, kc_ref, kn_ref, vp_ref, vc_ref, vn_ref, kx_ref, vx_ref, o_ref):
    n = pl.program_id(1)
    nb = pl.num_programs(1)
    rows = ATTN_STACK * BLOCK
    qi = lax.broadcasted_iota(jnp.int32, (rows, BLOCK), 0) % BLOCK
    kj = lax.broadcasted_iota(jnp.int32, (rows, BLOCK), 1)
    prev_ok = (kj >= qi) & (n > 0)
    next_ok = (kj <= qi) & (n < nb - 1)
    grp = lax.broadcasted_iota(jnp.int32, (rows, 1), 0) // BLOCK
    neg = -jnp.inf
    for q0 in range(0, N_HEADS, ATTN_STACK):
        h = q0 // GQA_GROUP
        hs = slice(h * HEAD_DIM, (h + 1) * HEAD_DIM)
        q4 = jnp.concatenate(
            [q_ref[0, :, (q0 + g) * HEAD_DIM:(q0 + g + 1) * HEAD_DIM] for g in range(ATTN_STACK)], axis=0)
        nt = (((1,), (1,)), ((), ()))
        sp = jnp.where(prev_ok, lax.dot_general(q4, kp_ref[0, :, hs], nt, preferred_element_type=F32), neg)
        sc = lax.dot_general(q4, kc_ref[0, :, hs], nt, preferred_element_type=F32)
        sn = jnp.where(next_ok, lax.dot_general(q4, kn_ref[0, :, hs], nt, preferred_element_type=F32), neg)
        sx = lax.dot_general(q4, kx_ref[0, :, hs], nt, preferred_element_type=F32)
        sink = jnp.zeros((rows, 1), F32)
        for g in range(ATTN_STACK):
            sink = jnp.where(grp == g, sink_ref[q0 + g], sink)
        lane_chunks = lambda a: [a[:, c * BLOCK:(c + 1) * BLOCK] for c in range(a.shape[1] // BLOCK)]
        fold = lambda op, parts: functools.reduce(op, parts)
        m = jnp.max(fold(jnp.maximum, [sp, sc, sn] + lane_chunks(sx)), -1, keepdims=True)
        m = jnp.maximum(m, sink)
        pp = jnp.exp(sp - m)
        pc = jnp.exp(sc - m)
        pn = jnp.exp(sn - m)
        px = jnp.exp(sx - m)
        denom = jnp.sum(fold(jnp.add, [pp, pc, pn] + lane_chunks(px)), -1, keepdims=True) + jnp.exp(sink - m)
        acc = jnp.dot(pp.astype(BF16), vp_ref[0, :, hs], preferred_element_type=F32)
        acc += jnp.dot(pc.astype(BF16), vc_ref[0, :, hs], preferred_element_type=F32)
        acc += jnp.dot(pn.astype(BF16), vn_ref[0, :, hs], preferred_element_type=F32)
        acc += jnp.dot(px.astype(BF16), vx_ref[0, :, hs], preferred_element_type=F32)
        o = acc / denom
        for g in range(ATTN_STACK):
            c0 = (q0 + g) * HEAD_DIM
            o_ref[0, :, c0:c0 + HEAD_DIM] = o[g * BLOCK:(g + 1) * BLOCK]


def _attention(sink, q, k, v, L, C):
    B = q.shape[0]
    nb = L // BLOCK
    off = C // BLOCK
    cur = lambda b, n: (b, n + off, 0)
    prv = lambda b, n: (b, jnp.maximum(n - 1, 0) + off, 0)
    nxt = lambda b, n: (b, jnp.minimum(n + 1, nb - 1) + off, 0)
    kv = lambda im: pl.BlockSpec((1, BLOCK, KV_W), im)
    cx = pl.BlockSpec((1, C, KV_W), lambda b, n: (b, 0, 0))
    return pl.pallas_call(
        _attn_kernel,
        grid=(B, nb),
        in_specs=[
            pl.BlockSpec(memory_space=pltpu.SMEM),
            pl.BlockSpec((1, BLOCK, ATTN_W), cur),
            kv(prv), kv(cur), kv(nxt), kv(prv), kv(cur), kv(nxt), cx, cx,
        ],
        out_specs=pl.BlockSpec((1, BLOCK, ATTN_W), lambda b, n: (b, n, 0)),
        out_shape=jax.ShapeDtypeStruct((B, L, ATTN_W), F32),
        compiler_params=_params(("arbitrary", "arbitrary")),
        name="attn",
    )(sink, q, k, k, k, v, v, v, k, v)


def _lru_coeffs(x_ref, p_ref, n_ref, tile, d, cw_ref, cb_ref, w_ref, bias_ref, lam_ref,
                ext_scr, a_scr, b_scr, n_lat_tiles):
    tm = ROW_TILE
    prev_ok = tile >= 2
    next_ok = (tile >= 1) & (tile < n_lat_tiles)
    ext_scr[0:SUBLANES] = jnp.where(prev_ok, p_ref[0], 0.0)
    ext_scr[SUBLANES:SUBLANES + tm] = x_ref[0]
    ext_scr[SUBLANES + tm:2 * SUBLANES + tm] = jnp.where(next_ok, n_ref[0], 0.0)
    xc = cb_ref[...] + cw_ref[2:3] * x_ref[0]
    xc += cw_ref[0:1] * ext_scr[SUBLANES - 2:SUBLANES - 2 + tm]
    xc += cw_ref[1:2] * ext_scr[SUBLANES - 1:SUBLANES - 1 + tm]
    xc += cw_ref[3:4] * ext_scr[SUBLANES + 1:SUBLANES + 1 + tm]
    xcb = xc.astype(BF16)
    lam = lam_ref[d:d + 1]
    sp = jnp.maximum(-lam, 0.0) + jnp.log1p(jnp.exp(-jnp.abs(lam)))
    for n in range(LRU_BLOCKS):
        cs = slice(n * LRU_BLOCK_W, (n + 1) * LRU_BLOCK_W)
        z = jnp.dot(xcb[:, cs], w_ref[d, n], preferred_element_type=F32)
        r = _sigmoid(z[:, :LRU_BLOCK_W] + bias_ref[d, 0:1, cs])
        i = _sigmoid(z[:, LRU_BLOCK_W:] + bias_ref[d, 1:2, cs])
        log_a = -LRU_C * r * sp[:, cs]
        a = jnp.exp(log_a)
        a_scr[:, cs] = a
        v = 1.0 - a * a
        b_scr[:, cs] = v * lax.rsqrt(jnp.maximum(v, SQRT_GUARD)) * (i * xc[:, cs])


def _lru_scan(a_scr, b_scr, h_ref, state_ref, reset, reverse):
    groups = ROW_TILE // SUBLANES
    row = lax.broadcasted_iota(jnp.int32, (SUBLANES, LRU_W), 0)
    carry0 = jnp.where(reset, 0.0, state_ref[...])

    def body(g, carry):
        gi = (groups - 1 - g) if reverse else g
        r0 = pl.multiple_of(gi * SUBLANES, SUBLANES)
        A = a_scr[pl.ds(r0, SUBLANES), :]
        Bv = b_scr[pl.ds(r0, SUBLANES), :]
        for s in (1, 2, 4):
            if reverse:
                sh, m = SUBLANES - s, row < SUBLANES - s
            else:
                sh, m = s, row >= s
            A_sh = pltpu.roll(A, sh, 0)
            B_sh = pltpu.roll(Bv, sh, 0)
            Bv = jnp.where(m, A * B_sh + Bv, Bv)
            A = jnp.where(m, A * A_sh, A)
        h = Bv + A * carry
        h_ref[0, pl.ds(r0, SUBLANES), :] = h
        last = h[0:1] if reverse else h[SUBLANES - 1:SUBLANES]
        return jnp.broadcast_to(last, (SUBLANES, LRU_W))

    state_ref[...] = lax.fori_loop(0, groups, body, carry0)


def _lru_kernel(xf_ref, xfp_ref, xfn_ref, xb_ref, xbp_ref, xbn_ref, cw_ref, cb_ref, w_ref, bias_ref, lam_ref,
                hf_ref, hb_ref, sf_ref, sb_ref, ext_scr, a_scr, b_scr, *, n_lat_tiles):
    t = pl.program_id(1)
    reset = t == 0
    bt = jnp.where(t == 0, 0, n_lat_tiles + 1 - t)
    _lru_coeffs(xf_ref, xfp_ref, xfn_ref, t, 0, cw_ref, cb_ref, w_ref, bias_ref, lam_ref,
                ext_scr, a_scr, b_scr, n_lat_tiles)
    _lru_scan(a_scr, b_scr, hf_ref, sf_ref, reset, False)
    _lru_coeffs(xb_ref, xbp_ref, xbn_ref, bt, 1, cw_ref, cb_ref, w_ref, bias_ref, lam_ref,
                ext_scr, a_scr, b_scr, n_lat_tiles)
    _lru_scan(a_scr, b_scr, hb_ref, sb_ref, reset, True)


def _lru(xr, conv_w, conv_b, w_gates, bias, lam, L):
    B, rows, _ = xr.shape
    nl = L // ROW_TILE
    nt = nl + 1
    per = ROW_TILE // SUBLANES
    n8 = rows // SUBLANES
    ft = lambda b, t: t
    btile = lambda b, t: jnp.where(t == 0, 0, nl + 1 - t)
    main = lambda f: pl.BlockSpec((1, ROW_TILE, LRU_W), lambda b, t: (b, f(b, t), 0))
    prev = lambda f: pl.BlockSpec((1, SUBLANES, LRU_W), lambda b, t: (b, jnp.maximum(f(b, t) * per - 1, 0), 0))
    nxt = lambda f: pl.BlockSpec((1, SUBLANES, LRU_W),
                                 lambda b, t: (b, jnp.minimum((f(b, t) + 1) * per, n8 - 1), 0))
    full = lambda a: pl.BlockSpec(a.shape, lambda b, t: (0,) * a.ndim)
    return pl.pallas_call(
        functools.partial(_lru_kernel, n_lat_tiles=nl),
        grid=(B, nt),
        in_specs=[main(ft), prev(ft), nxt(ft), main(btile), prev(btile), nxt(btile),
                  full(conv_w), full(conv_b), full(w_gates), full(bias), full(lam)],
        out_specs=[
            pl.BlockSpec((1, ROW_TILE, LRU_W), lambda b, t: (b, jnp.maximum(t - 1, 0), 0)),
            pl.BlockSpec((1, ROW_TILE, LRU_W), lambda b, t: (b, nl - jnp.maximum(t, 1), 0)),
        ],
        out_shape=[jax.ShapeDtypeStruct((B, L, LRU_W), F32)] * 2,
        scratch_shapes=[
            pltpu.VMEM((SUBLANES, LRU_W), F32), pltpu.VMEM((SUBLANES, LRU_W), F32),
            pltpu.VMEM((ROW_TILE + 2 * SUBLANES, LRU_W), F32),
            pltpu.VMEM((ROW_TILE, LRU_W), F32), pltpu.VMEM((ROW_TILE, LRU_W), F32),
        ],
        compiler_params=_params(("arbitrary", "arbitrary")),
        name="lru",
    )(xr, xr, xr, xr, xr, xr, conv_w, conv_b, w_gates, bias, lam)


def _route(scores, sel):
    tm = scores.shape[1]
    neg = -jnp.inf
    iota_g = lax.broadcasted_iota(jnp.int32, (GROUP_SIZE, tm), 0)
    grp_score = []
    for g in range(N_GROUPS):
        sg = sel[g * GROUP_SIZE:(g + 1) * GROUP_SIZE]
        m1 = jnp.max(sg, axis=0, keepdims=True)
        first = jnp.min(jnp.where(sg == m1, iota_g, GROUP_SIZE), axis=0, keepdims=True)
        m2 = jnp.max(jnp.where(iota_g == first, neg, sg), axis=0, keepdims=True)
        grp_score.append(m1 + m2)
    masked = []
    for g in range(N_GROUPS):
        rank = jnp.zeros((1, tm), jnp.int32)
        for o in range(N_GROUPS):
            if o == g:
                continue
            ahead = (grp_score[o] > grp_score[g]) if o > g else (grp_score[o] >= grp_score[g])
            rank += ahead.astype(jnp.int32)
        keep = rank < TOPK_GROUPS
        masked.append(jnp.where(keep, sel[g * GROUP_SIZE:(g + 1) * GROUP_SIZE], neg))
    cand = jnp.concatenate(masked, axis=0)
    iota_e = lax.broadcasted_iota(jnp.int32, (N_EXPERTS, tm), 0)
    chosen = jnp.zeros((N_EXPERTS, tm), jnp.bool_)
    picks = []
    for _ in range(TOP_K):
        cur = jnp.where(chosen, neg, cand)
        m = jnp.max(cur, axis=0, keepdims=True)
        idx = jnp.min(jnp.where((cur == m) & jnp.logical_not(chosen), iota_e, N_EXPERTS), axis=0, keepdims=True)
        chosen = chosen | (iota_e == idx)
        picks.append(idx)
    w = jnp.where(chosen, scores, 0.0)
    gates = w / jnp.sum(w, axis=0, keepdims=True) * ROUTED_SCALE
    return gates, chosen, picks


def _tok_scratch(rows):
    return pltpu.VMEM((rows, TOK_PITCH, LANES), F32)


def _to_token_major(y, scr):
    rows = scr.shape[0]
    flat = scr.reshape(rows * TOK_PITCH, LANES)
    for s in range(TOK_ROWS):
        flat[pl.ds(s, rows, stride=TOK_PITCH), :] = y[:, s * LANES:(s + 1) * LANES]


def _from_token_major(scr):
    rows = scr.shape[0]
    flat = scr.reshape(rows * TOK_PITCH, LANES)
    return [flat[pl.ds(s, rows, stride=TOK_PITCH), :] for s in range(TOK_ROWS)]


def _token_tiles(scr):
    return scr.at[:, 0:TOK_ROWS, :]


def _merge_kernel(attn_ref, hf_ref, hb_ref, yg_ref, x_ref, mod_ref, wout_ref, ga_ref, gl_ref, lg_ref, lb_ref,
                  rwh_ref, rwl_ref, rb_ref, tri_ref, eye_ref, xn_ref, u2_ref, up_ref, eidx_ref, gates_ref, pos_ref,
                  cnt_ref, carry_ref, tok_scr):
    b = pl.program_id(0)

    @pl.when((b == 0) & (pl.program_id(1) == 0))
    def _():
        carry_ref[...] = jnp.zeros_like(carry_ref)

    D = D_MODEL
    mrow = lambda i: mod_ref[pl.ds(b, 1), i * D:(i + 1) * D]
    nt = (((1,), (1,)), ((), ()))
    u2_parts, logit_parts = [], []
    sub = ROW_TILE // MERGE_SPLIT
    for p in range(MERGE_SPLIT):
        rs = slice(p * sub, (p + 1) * sub)
        lru_y = (hf_ref[0, rs] + hb_ref[0, rs]) * jax.nn.gelu(yg_ref[0, rs])
        na = _rms_norm(attn_ref[0, rs], ga_ref[...]).astype(BF16)
        nl = _rms_norm(lru_y, gl_ref[...]).astype(BF16)
        y1 = jnp.dot(na, wout_ref[0:ATTN_W], preferred_element_type=F32)
        y1 += jnp.dot(nl, wout_ref[ATTN_W:D], preferred_element_type=F32)
        xn = _layer_norm(DEEPNORM_ALPHA * x_ref[0, rs] + mrow(2) * y1) * lg_ref[...] + lb_ref[...]
        xn_ref[0, rs] = xn
        u2_p = _layer_norm(xn) * (1.0 + mrow(4)) + mrow(3)
        u_hi, u_lo = _split_bf16(u2_p)
        u2_ref[0, rs] = u_hi
        lg_p = lax.dot_general(rwh_ref[...], u_hi, nt, preferred_element_type=F32)
        lg_p += lax.dot_general(rwh_ref[...], u_lo, nt, preferred_element_type=F32)
        lg_p += lax.dot_general(rwl_ref[...], u_hi, nt, preferred_element_type=F32)
        u2_parts.append(u2_p)
        logit_parts.append(lg_p)
    u2 = jnp.concatenate(u2_parts, axis=0)
    _to_token_major(u2, tok_scr)
    up_ref[...] = _token_tiles(tok_scr)[...].astype(BF16)
    scores = jax.nn.sigmoid(jnp.concatenate(logit_parts, axis=1))
    gates, chosen, picks = _route(scores, scores + rb_ref[...])
    sel01 = jnp.where(chosen, 1.0, 0.0)
    incl = jnp.dot(sel01.astype(BF16), tri_ref[...], preferred_element_type=F32)
    rank = carry_ref[:, 0:1] + incl - sel01
    carry_ref[...] = carry_ref[...] + incl[:, ROW_TILE - 1:ROW_TILE]
    cnt_ref[...] = carry_ref[...]
    iota_e = lax.broadcasted_iota(jnp.int32, (N_EXPERTS, ROW_TILE), 0)
    gate_rows = []
    for kk, idx in enumerate(picks):
        hit = iota_e == idx
        eidx_ref[kk:kk + 1, :] = idx
        gate_rows.append(jnp.sum(jnp.where(hit, gates, 0.0), axis=0, keepdims=True))
        pos_ref[kk:kk + 1, :] = jnp.sum(jnp.where(hit, rank, 0.0), axis=0, keepdims=True).astype(jnp.int32)

    piece = jnp.concatenate(gate_rows, axis=0)
    gate_cols = jnp.zeros((ROW_TILE, TOP_K), F32)
    for _ in range(3):
        part = piece.astype(BF16)
        gate_cols += lax.dot_general(eye_ref[...], part, nt, preferred_element_type=F32)
        piece = piece - part.astype(F32)
    for kk in range(TOP_K):
        gates_ref[pl.ds(kk, ROW_TILE, stride=TOP_K), :] = jnp.broadcast_to(gate_cols[:, kk:kk + 1],
                                                                           (ROW_TILE, LANES))


def _merge(attn, hf, hb, yg, x, mod, w_out_bf, ga, gl, lg, lb, rw_hi, rw_lo, rbias, tri, eye, C):
    B, L, _ = x.shape
    T = B * L
    nt = L // ROW_TILE
    off = C // ROW_TILE
    row = lambda w, o=0: pl.BlockSpec((1, ROW_TILE, w), lambda b, t: (b, t + o, 0))
    full = lambda a: pl.BlockSpec(a.shape, lambda b, t: (0,) * a.ndim)
    tok = pl.BlockSpec((TOP_K, ROW_TILE), lambda b, t: (0, b * nt + t))
    return pl.pallas_call(
        _merge_kernel,
        grid=(B, nt),
        in_specs=[row(ATTN_W), row(LRU_W), row(LRU_W), row(LRU_W, off), row(D_MODEL),
                  full(mod), full(w_out_bf), full(ga), full(gl), full(lg), full(lb),
                  full(rw_hi), full(rw_lo), full(rbias), full(tri), full(eye)],
        out_specs=[row(D_MODEL), row(D_MODEL),
                   pl.BlockSpec((ROW_TILE, TOK_ROWS, LANES), lambda b, t: (b * nt + t, 0, 0)),
                   tok,
                   pl.BlockSpec((ROW_TILE * TOP_K, LANES), lambda b, t: (b * nt + t, 0)),
                   tok,
                   pl.BlockSpec((N_EXPERTS, LANES), lambda b, t: (0, 0))],
        out_shape=[jax.ShapeDtypeStruct((B, L, D_MODEL), F32),
                   jax.ShapeDtypeStruct((B, L, D_MODEL), BF16),
                   jax.ShapeDtypeStruct((T, TOK_ROWS, LANES), BF16),
                   jax.ShapeDtypeStruct((TOP_K, T), jnp.int32),
                   jax.ShapeDtypeStruct((T * TOP_K, LANES), F32),
                   jax.ShapeDtypeStruct((TOP_K, T), jnp.int32),
                   jax.ShapeDtypeStruct((N_EXPERTS, LANES), F32)],
        scratch_shapes=[pltpu.VMEM((N_EXPERTS, LANES), F32), _tok_scratch(ROW_TILE)],
        compiler_params=_params(("arbitrary", "arbitrary")),
        name="merge",
    )(attn, hf, hb, yg, x, mod, w_out_bf, ga, gl, lg, lb, rw_hi, rw_lo, rbias, tri, eye)


def _swiglu(u, wg, wu):
    hg = jnp.dot(u, wg, preferred_element_type=F32)
    hu = jnp.dot(u, wu, preferred_element_type=F32)
    return hg * jax.nn.sigmoid(hg) * hu


def _dispatch_kernel(zt_ref, dest_ref, up_ref, u_ref, sg_ref, su_ref, sd_ref, xs_ref, ysh_ref, zero_scr, sem, zsem):
    first_step = pl.program_id(0) == 0
    n_tiles = zt_ref.shape[0]

    def zero_tile(z):
        return pltpu.make_async_copy(zero_scr, xs_ref.at[pl.ds(z * EXP_TR, EXP_TR)], zsem)

    def zero_tiles(act):
        def step(z, carry):
            @pl.when(zt_ref[z] != 0)
            def _():
                act(zero_tile(z))
            return carry
        lax.fori_loop(0, n_tiles, step, 0)

    @pl.when(first_step)
    def _():
        zero_scr[...] = jnp.zeros_like(zero_scr)
        zero_tiles(lambda cp: cp.start())

    hs = _swiglu(u_ref[...], sg_ref[...], su_ref[...])
    ysh_ref[...] = jnp.dot(hs.astype(BF16), sd_ref[...], preferred_element_type=F32).astype(BF16)

    @pl.when(first_step)
    def _():
        zero_tiles(lambda cp: cp.wait())

    def issue(j, carry):
        for k in range(TOP_K):
            pltpu.make_async_copy(up_ref.at[j], xs_ref.at[dest_ref[j * TOP_K + k]], sem).start(priority=k % 2)
        return carry

    lax.fori_loop(0, DISP_TM, issue, 0)

    for k in range(TOP_K):
        pltpu.make_async_copy(up_ref, xs_ref.at[pl.ds(0, DISP_TM)], sem).wait()


def _dispatch(zero_tiles, slots, up, u2, sg, su, sd, n_sorted_rows):
    T = up.shape[0]
    full = lambda a: pl.BlockSpec(a.shape, lambda i, zt: (0,) * a.ndim)
    rows = pl.BlockSpec((DISP_TM, D_MODEL), lambda i, zt: (i, 0))
    return pl.pallas_call(
        _dispatch_kernel,
        grid_spec=pltpu.PrefetchScalarGridSpec(
            num_scalar_prefetch=1,
            grid=(T // DISP_TM,),
            in_specs=[pl.BlockSpec((DISP_TM * TOP_K,), lambda i, zt: (i,), memory_space=pltpu.SMEM),
                      pl.BlockSpec((DISP_TM, TOK_ROWS, LANES), lambda i, zt: (i, 0, 0)),
                      rows, full(sg), full(su), full(sd)],
            out_specs=[pl.BlockSpec(memory_space=pl.ANY), rows],
            scratch_shapes=[pltpu.VMEM((EXP_TR, TOK_ROWS, LANES), BF16),
                            pltpu.SemaphoreType.DMA(()), pltpu.SemaphoreType.DMA(())],
        ),
        out_shape=[jax.ShapeDtypeStruct((n_sorted_rows, TOK_ROWS, LANES), BF16),
                   jax.ShapeDtypeStruct((T, D_MODEL), BF16)],
        compiler_params=_params(("arbitrary",)),
        name="dispatch",
    )(zero_tiles, slots, up, u2, sg, su, sd)


def _expert_kernel(te_ref, nu_ref, nxt_ref, par_ref, val_ref, xs_ref, wg_hbm, wu_hbm, wd_hbm, ys_ref,
                   wg32, wu32, wd32, wgb_ref, wub_ref, wdb_ref, tok_scr, wsem):
    i = pl.program_id(0)
    n_used = nu_ref[0]

    def weight_copies(e, s):
        return [pltpu.make_async_copy(hbm.at[e], stage.at[s], wsem.at[s])
                for hbm, stage in ((wg_hbm, wg32), (wu_hbm, wu32), (wd_hbm, wd32))]

    @pl.when(i == 0)
    def _():
        for cp in weight_copies(te_ref[0], 0):
            cp.start(priority=1)

    @pl.when(i < n_used)
    def _():
        e = te_ref[i]

        first_of_expert = (i == 0) | (e != te_ref[jnp.maximum(i - 1, 0)])
        for s in range(2):
            @pl.when(first_of_expert & (par_ref[i] == s))
            def _():
                for cp in weight_copies(e, s):
                    cp.wait()
                for src, dst in ((wg32, wgb_ref), (wu32, wub_ref), (wd32, wdb_ref)):
                    rows = dst.shape[0] // CAST_CHUNKS

                    def cast_chunk(r, carry, src=src, dst=dst, rows=rows):
                        sl = pl.ds(pl.multiple_of(r * rows, rows), rows)
                        dst[sl, :] = src[s, sl, :].astype(BF16)
                        return carry

                    lax.fori_loop(0, CAST_CHUNKS, cast_chunk, 0)

                @pl.when(nxt_ref[i] >= 0)
                def _():
                    for cp in weight_copies(nxt_ref[i], 1 - s):
                        cp.start(priority=1)

        def tile_body(rows):
            scr = tok_scr.at[0:rows]
            _token_tiles(scr)[...] = xs_ref[0:rows].astype(F32)
            x = jnp.concatenate([c.astype(BF16) for c in _from_token_major(scr)], axis=1)
            h = _swiglu(x, wgb_ref[...], wub_ref[...])
            y = jnp.dot(h.astype(BF16), wdb_ref[...], preferred_element_type=F32)
            _to_token_major(y, scr)
            ys_ref[0:rows] = _token_tiles(scr)[...].astype(BF16)
            if rows < EXP_TR:
                ys_ref[rows:EXP_TR] = jnp.zeros((EXP_TR - rows, TOK_ROWS, LANES), BF16)

        @pl.when(val_ref[i] > EXP_TR // 2)
        def _():
            tile_body(EXP_TR)

        @pl.when(val_ref[i] <= EXP_TR // 2)
        def _():
            tile_body(EXP_TR // 2)

    @pl.when(i >= n_used)
    def _():
        ys_ref[...] = jnp.zeros_like(ys_ref)


def _experts(tile_expert, n_used, next_expert, run_parity, tile_valid, xs, wg, wu, wd):
    n_tiles = xs.shape[0] // EXP_TR
    blk = (EXP_TR, TOK_ROWS, LANES)
    stage = lambda a: pltpu.VMEM((2,) + a.shape[1:], F32)
    cast = lambda a: pltpu.VMEM(a.shape[1:], BF16)
    hbm = pl.BlockSpec(memory_space=pl.ANY)
    return pl.pallas_call(
        _expert_kernel,
        grid_spec=pltpu.PrefetchScalarGridSpec(
            num_scalar_prefetch=5,
            grid=(n_tiles,),
            in_specs=[pl.BlockSpec(blk, lambda i, te, nu, nx, pr, vl: (jnp.minimum(i, nu[0] - 1), 0, 0)),
                      hbm, hbm, hbm],
            out_specs=pl.BlockSpec(blk, lambda i, te, nu, nx, pr, vl: (i, 0, 0)),
            scratch_shapes=[stage(wg), stage(wu), stage(wd), cast(wg), cast(wu), cast(wd),
                            _tok_scratch(EXP_TR), pltpu.SemaphoreType.DMA((2,))],
        ),
        out_shape=jax.ShapeDtypeStruct(xs.shape, BF16),
        compiler_params=_params(("arbitrary",)),
        name="experts",
    )(tile_expert, n_used, next_expert, run_parity, tile_valid, xs, wg, wu, wd)


def _combine_kernel(dcur_ref, dnext_ref, wv_ref, ysh_ref, xn_ref, g2_ref, lg_ref, lb_ref,
                    ys_ref, o_ref, buf_ref, y_scr, sem):
    i = pl.program_id(0)
    n = pl.num_programs(0)
    tm = COMB_TM
    slot = i % 2

    def issue(d_ref, sl, j):
        for k in range(TOP_K):
            pltpu.make_async_copy(ys_ref.at[d_ref[j * TOP_K + k]], buf_ref.at[sl, k, j],
                                  sem.at[sl]).start(priority=k % 2)

    @pl.when(i == 0)
    def _():
        def first(j, carry):
            issue(dcur_ref, 0, j)
            return carry
        lax.fori_loop(0, tm, first, 0)

    for k in range(TOP_K):
        pltpu.make_async_copy(ys_ref.at[pl.ds(0, tm)], buf_ref.at[slot, k], sem.at[slot]).wait()

    half = TOK_ROWS // 2

    def accumulate(j):
        lo = jnp.zeros((half, LANES), F32)
        hi = jnp.zeros((half, LANES), F32)
        for k in range(TOP_K):
            gate = wv_ref[pl.ds(j * TOP_K + k, half, stride=0), :]
            rows = buf_ref[slot, k, j].astype(F32)
            lo += gate * rows[:half]
            hi += gate * rows[half:]
        y_scr[j, 0:half, :] = lo
        y_scr[j, half:TOK_ROWS, :] = hi

    @pl.when(i + 1 < n)
    def _():
        def both(jo, carry):
            for u in range(COMB_UNROLL):
                j = jo * COMB_UNROLL + u
                issue(dnext_ref, 1 - slot, j)
                accumulate(j)
            return carry
        lax.fori_loop(0, tm // COMB_UNROLL, both, 0)

    @pl.when(i + 1 == n)
    def _():
        def last(jo, carry):
            for u in range(COMB_UNROLL):
                accumulate(jo * COMB_UNROLL + u)
            return carry
        lax.fori_loop(0, tm // COMB_UNROLL, last, 0)
    y_routed = jnp.concatenate(_from_token_major(y_scr), axis=1)

    y2 = ysh_ref[...].astype(F32) + y_routed
    z = DEEPNORM_ALPHA * xn_ref[...] + g2_ref[0] * y2
    o_ref[...] = _layer_norm(z) * lg_ref[...] + lb_ref[...]


def _combine(slots, gates, ysh, xn, g2, lg, lb, ys, tiles_per_batch):
    T = ysh.shape[0]
    tm = COMB_TM
    n = T // tm
    row = pl.BlockSpec((tm, D_MODEL), lambda i: (i, 0))
    full = lambda a: pl.BlockSpec(a.shape, lambda i: (0,) * a.ndim)
    tok = lambda f: pl.BlockSpec((tm * TOP_K,), lambda i: (f(i),), memory_space=pltpu.SMEM)
    return pl.pallas_call(
        _combine_kernel,
        grid=(n,),
        in_specs=[tok(lambda i: i), tok(lambda i: jnp.minimum(i + 1, n - 1)),
                  pl.BlockSpec((tm * TOP_K, LANES), lambda i: (i, 0)), row, row,
                  pl.BlockSpec((1, 1, D_MODEL), lambda i: (i // tiles_per_batch, 0, 0)),
                  full(lg), full(lb),
                  pl.BlockSpec(memory_space=pl.ANY)],
        out_specs=row,
        out_shape=jax.ShapeDtypeStruct((T, D_MODEL), F32),
        scratch_shapes=[pltpu.VMEM((2, TOP_K, tm, TOK_ROWS, LANES), BF16),
                        _tok_scratch(tm),
                        pltpu.SemaphoreType.DMA((2,))],
        compiler_params=_params(("arbitrary",)),
        name="combine",
    )(slots, slots, gates, ysh, xn, g2, lg, lb, ys)


def _rope_tables(L, C):
    rows = L // GRID_W
    row = np.repeat(np.arange(rows, dtype=np.float64), GRID_W)
    col = np.tile(np.arange(GRID_W, dtype=np.float64), rows)
    inv_freq = np.exp(-np.log(ROPE_THETA) * np.arange(0, ROPE_AXIS_DIM, 2, dtype=np.float64) / ROPE_AXIS_DIM)
    ang_r = row[:, None] * inv_freq
    ang_c = col[:, None] * inv_freq
    cr, sr, cc, sc = np.cos(ang_r), np.sin(ang_r), np.cos(ang_c), np.sin(ang_c)
    cos_t = np.concatenate([cr, cr, cc, cc], axis=-1)
    sin_t = np.concatenate([-sr, sr, -sc, sc], axis=-1)
    cos_t = np.concatenate([np.ones((C, HEAD_DIM)), cos_t], axis=0)
    sin_t = np.concatenate([np.zeros((C, HEAD_DIM)), sin_t], axis=0)
    return jnp.asarray(cos_t, F32), jnp.asarray(sin_t, F32)


def kernel(x, c, ctx, c_ctx, w_mod, b_mod, w_in, attn_sink, conv_w, conv_b, lru_wa, lru_ba, lru_wx, lru_bx,
           lru_lam, norm_attn_g, norm_lru_g, w_out, ln1_g, ln1_b, router_w, router_bias, exp_w_gate, exp_w_up,
           exp_w_down, sh_w_gate, sh_w_up, sh_w_down, ln2_g, ln2_b):
    B, L, D = x.shape
    C = ctx.shape[1]
    assert w_mod.shape[0] == DEPTH and D == D_MODEL and B + 1 <= SUBLANES
    row2 = lambda a: a.reshape(1, -1)

    cvec = jnp.concatenate([c, c_ctx[None], jnp.zeros((SUBLANES - B - 1, D), F32)], axis=0)
    mod = _mod(cvec, w_mod[0], row2(b_mod[0]))

    cos_t, sin_t = _rope_tables(L, C)
    q, k, v, xr, yg = _inproj(x, ctx, mod, w_in[0].astype(BF16), cos_t, sin_t)

    attn = _attention(attn_sink[0], q, k, v, L, C)

    w_gates = jnp.concatenate([lru_wa[0], lru_wx[0]], axis=-1).astype(BF16)
    bias = jnp.stack([lru_ba[0], lru_bx[0]], axis=1)
    hf, hb = _lru(xr, conv_w[0], row2(conv_b[0]), w_gates, bias, lru_lam[0], L)

    rw_hi, rw_lo = _split_bf16(router_w[0].T)
    tri = jnp.triu(jnp.ones((ROW_TILE, ROW_TILE), BF16))
    eye = jnp.eye(ROW_TILE, dtype=BF16)
    xn, u2, up, eidx, gates, pos, cnt = _merge(
        attn, hf, hb, yg, x, mod, w_out[0].astype(BF16), row2(norm_attn_g[0]), row2(norm_lru_g[0]),
        row2(ln1_g[0]), row2(ln1_b[0]), rw_hi, rw_lo, router_bias[0].reshape(-1, 1), tri, eye, C)

    T = B * L
    n_tiles = T * TOP_K // EXP_TR + N_EXPERTS
    counts = cnt[:, 0].astype(jnp.int32)
    tiles_e = (counts + EXP_TR - 1) // EXP_TR
    tile_end = jnp.cumsum(tiles_e)
    n_used = tile_end[-1:]
    row_start = (tile_end - tiles_e) * EXP_TR
    experts = jnp.arange(N_EXPERTS, dtype=jnp.int32)
    start_of = jnp.sum(jnp.where(eidx[None] == experts[:, None, None], row_start[:, None, None], 0), axis=0)
    slots = (start_of + pos).T.reshape(-1)
    tile_id = jnp.minimum(jnp.arange(n_tiles, dtype=jnp.int32), n_used - 1)
    tile_expert = jnp.sum((tile_end[None, :] <= tile_id[:, None]).astype(jnp.int32), axis=1)

    all_tiles = jnp.arange(n_tiles, dtype=jnp.int32)
    ends_run = jnp.any((tile_end[None, :] - 1 == all_tiles[:, None]) & (tiles_e[None, :] > 0), axis=1)
    zero_tiles = (ends_run | (all_tiles >= n_used)).astype(jnp.int32)
    xs, ysh = _dispatch(zero_tiles, slots, up, u2.reshape(T, D), sh_w_gate[0].astype(BF16),
                        sh_w_up[0].astype(BF16), sh_w_down[0].astype(BF16), n_tiles * EXP_TR)
    has_rows = tiles_e > 0
    run_parity_e = (jnp.cumsum(has_rows) - has_rows) % 2
    later = has_rows[None, :] & (experts[None, :] > experts[:, None])
    next_e = jnp.min(jnp.where(later, experts[None, :], N_EXPERTS), axis=1)
    next_e = jnp.where(next_e == N_EXPERTS, -1, next_e)
    of_tile = tile_expert[:, None] == experts[None, :]
    per_tile = lambda v: jnp.sum(jnp.where(of_tile, v[None, :], 0), axis=1).astype(jnp.int32)
    first_tile = per_tile(tile_end - tiles_e)
    tile_valid = jnp.clip(per_tile(counts) - (tile_id - first_tile) * EXP_TR, 0, EXP_TR)
    ys = _experts(tile_expert, n_used, per_tile(next_e), per_tile(run_parity_e), tile_valid, xs,
                  exp_w_gate[0], exp_w_up[0], exp_w_down[0])
    g2 = mod[:B, 5 * D:6 * D].reshape(B, 1, D)
    out = _combine(slots, gates, ysh, xn.reshape(T, D), g2, row2(ln2_g[0]), row2(ln2_b[0]), ys, L // COMB_TM)
    return out.reshape(B, L, D)
```
